```python
import jax, jax.numpy as jnp
from jax import lax
import numpy as np

D_MODEL = 1024
BATCH = 32
SEQ = 2048
DEPTH = 1
DEC_BATCH = 32
DEC_SEQ = 16
PAST_LEN = 4096

CHUNK = 64
EPS = 1e-6
SSM_EXPAND = 2
SSM_INNER = SSM_EXPAND * D_MODEL
SSM_HEADDIM = 64
SSM_HEADS = SSM_INNER // SSM_HEADDIM
SSM_GROUPS = 4
SSM_STATE = 128
CONV_WIDTH = 4
CONV_DIM = SSM_INNER + 2 * SSM_GROUPS * SSM_STATE
ATTN_HEADS = 16
ATTN_KV_HEADS = 2
HEAD_DIM = 64
ATTN_REP = ATTN_HEADS // ATTN_KV_HEADS
ATTN_WIDTH = ATTN_HEADS * HEAD_DIM
IDX_HEADS = 8
IDX_DIM = 64
IDX_SCALE = (IDX_HEADS * IDX_DIM) ** -0.5
TOPK_MAX = 256
Q_BLOCK = 128
MOE_GROUPS = 4
EXPERTS_PER_GROUP = 8
N_EXPERTS = MOE_GROUPS * EXPERTS_PER_GROUP
EXPERT_FF = 256
TOPK_IN_GROUP = 2
N_BRANCHES = 2
PROJ_SIZES = (SSM_INNER, CONV_DIM, SSM_HEADS,
              ATTN_WIDTH, ATTN_KV_HEADS * HEAD_DIM, ATTN_KV_HEADS * HEAD_DIM,
              IDX_HEADS * IDX_DIM, IDX_DIM, IDX_HEADS,
              N_BRANCHES * D_MODEL)
PROJ_WIDTH = 9064

kernel_name = "hybrid_ssd_dsa_hiermoe_stream_step"


def rmsnorm(x, g):
    xf = x.astype(jnp.float32)
    y = xf * lax.rsqrt(jnp.mean(xf * xf, axis=-1, keepdims=True) + EPS)
    return (y * g.astype(jnp.float32)).astype(x.dtype)


def causal_conv(xbc, conv_state, w, b):
    T = xbc.shape[1]
    xp = jnp.concatenate([conv_state.astype(xbc.dtype), xbc], axis=1)
    y = b + xp[:, 0:T] * w[0]
    for tap in range(1, CONV_WIDTH):
        y = y + xp[:, tap:tap + T] * w[tap]
    return y, xp[:, -(CONV_WIDTH - 1):]


def ssd_scan(x, dt, A, Bm, Cm, state0, chunk):
    b, T, H, P = x.shape
    G, N = Bm.shape[2], Bm.shape[3]
    R = H // G
    nc = T // chunk
    f32 = jnp.float32
    xc = (x.astype(f32) * dt[..., None]).reshape(b, nc, chunk, G, R, P)
    ac = (dt * A).reshape(b, nc, chunk, G, R)
    Bc = Bm.astype(f32).reshape(b, nc, chunk, G, N)
    Cc = Cm.astype(f32).reshape(b, nc, chunk, G, N)
    causal = jnp.tril(jnp.ones((chunk, chunk), bool))[None, :, :, None, None]

    def chunk_step(state, inp):
        xk, ak, bk, ck = inp
        a_cum = jnp.cumsum(ak, axis=1)
        seg = a_cum[:, :, None] - a_cum[:, None, :]
        decay = jnp.where(causal, jnp.exp(jnp.where(causal, seg, 0.0)), 0.0)
        cb = jnp.einsum('blgn,bsgn->blsg', ck, bk)
        y = jnp.einsum('blsg,blsgr,bsgrp->blgrp', cb, decay, xk)
        y = y + jnp.einsum('blgn,bgrpn,blgr->blgrp', ck, state, jnp.exp(a_cum))
        to_end = jnp.exp(a_cum[:, -1:] - a_cum)
        state = (state * jnp.exp(a_cum[:, -1])[..., None, None]
                 + jnp.einsum('blgn,blgr,blgrp->bgrpn', bk, to_end, xk))
        return state, y

    s0 = state0.astype(f32).reshape(b, G, R, P, N)
    inputs = (jnp.moveaxis(xc, 1, 0), jnp.moveaxis(ac, 1, 0),
              jnp.moveaxis(Bc, 1, 0), jnp.moveaxis(Cc, 1, 0))
    final, ys = lax.scan(chunk_step, s0, inputs)
    y = jnp.moveaxis(ys, 0, 1).reshape(b, T, H, P)
    return y, final.reshape(b, H, P, N)


def dsa_attention(q, k_all, v_all, qi, kidx_all, wi, past_len):
    b, T = q.shape[0], q.shape[1]
    L = k_all.shape[1]
    topk = min(TOPK_MAX, L // 4)
    qblk = min(Q_BLOCK, T)
    nblk = T // qblk
    f32 = jnp.float32
    key_chunk = jnp.arange(L) // CHUNK
    kidx_f = kidx_all.astype(f32)

    def block(args):
        qb, qib, wib, start = args
        q_chunk = (past_len + start + jnp.arange(qblk)) // CHUNK
        adm = key_chunk[None, :] <= q_chunk[:, None]
        dots = jnp.einsum('bqhd,bsd->bqhs', qib.astype(f32), kidx_f)
        iscore = jnp.einsum('bqh,bqhs->bqs', wib.astype(f32) * IDX_SCALE, jax.nn.relu(dots))
        iscore = jnp.where(adm[None], iscore, -jnp.inf)
        _, sel = lax.top_k(iscore, topk)
        valid = (sel // CHUNK) <= q_chunk[None, :, None]
        k_sel = jax.vmap(lambda kk, ii: kk[ii])(k_all, sel)
        v_sel = jax.vmap(lambda vv, ii: vv[ii])(v_all, sel)
        qg = qb.reshape(b, qblk, ATTN_KV_HEADS, ATTN_REP, HEAD_DIM).astype(f32)
        s = jnp.einsum('bqgrd,bqkgd->bqgrk', qg, k_sel.astype(f32)) * (HEAD_DIM ** -0.5)
        s = jnp.where(valid[:, :, None, None, :], s, -jnp.inf)
        p = jax.nn.softmax(s, axis=-1)
        o = jnp.einsum('bqgrk,bqkgd->bqgrd', p, v_sel.astype(f32))
        return o.reshape(b, qblk, ATTN_WIDTH).astype(qb.dtype)

    def to_blocks(a):
        return jnp.moveaxis(a.reshape((b, nblk, qblk) + a.shape[2:]), 1, 0)

    out = lax.map(block, (to_blocks(q), to_blocks(qi), to_blocks(wi), jnp.arange(nblk) * qblk))
    return jnp.moveaxis(out, 0, 1).reshape(b, T, ATTN_WIDTH)


def mixer_sublayer(x, conv_state, ssm_state, k_past, v_past, kidx_past,
                   norm_g, w_in, conv_w, conv_b, dt_bias, A_log, D_skip,
                   ssm_norm_g, w_ssm_out, w_attn_out, w_o):
    b, T, _ = x.shape
    past_len = k_past.shape[1]
    f32 = jnp.float32
    h = rmsnorm(x, norm_g)
    proj = h @ w_in
    split_points = tuple(int(s) for s in np.cumsum(PROJ_SIZES)[:-1])
    z, xbc, dt_raw, q, k, v, qi, ki, wi, g = jnp.split(proj, split_points, axis=-1)

    xbc, conv_new = causal_conv(xbc, conv_state, conv_w, conv_b)
    xbc = jax.nn.silu(xbc)
    xs, Bm, Cm = jnp.split(xbc, (SSM_INNER, SSM_INNER + SSM_GROUPS * SSM_STATE), axis=-1)
    xs = xs.reshape(b, T, SSM_HEADS, SSM_HEADDIM)
    dt = jax.nn.softplus(dt_raw.astype(f32) + dt_bias.astype(f32))
    A = -jnp.exp(A_log.astype(f32))
    y_ssd, ssm_new = ssd_scan(xs, dt, A,
                              Bm.reshape(b, T, SSM_GROUPS, SSM_STATE),
                              Cm.reshape(b, T, SSM_GROUPS, SSM_STATE),
                              ssm_state, min(CHUNK, T))
    y_ssd = y_ssd + D_skip.astype(f32)[:, None] * xs.astype(f32)
    y_ssd = y_ssd.reshape(b, T, SSM_INNER).astype(x.dtype)
    y_ssd = rmsnorm(y_ssd * jax.nn.silu(z), ssm_norm_g)
    branch_a = y_ssd @ w_ssm_out

    k = k.reshape(b, T, ATTN_KV_HEADS, HEAD_DIM)
    v = v.reshape(b, T, ATTN_KV_HEADS, HEAD_DIM)
    k_all = jnp.concatenate([k_past.astype(k.dtype), k], axis=1)
    v_all = jnp.concatenate([v_past.astype(v.dtype), v], axis=1)
    kidx_all = jnp.concatenate([kidx_past.astype(ki.dtype), ki], axis=1)
    o = dsa_attention(q.reshape(b, T, ATTN_HEADS, HEAD_DIM), k_all, v_all,
                      qi.reshape(b, T, IDX_HEADS, IDX_DIM), kidx_all, wi, past_len)
    branch_b = o @ w_attn_out

    gates = jax.nn.sigmoid(g.astype(f32)).reshape(b, T, N_BRANCHES, D_MODEL)
    merged = (gates[:, :, 0] * branch_a.astype(f32)
              + gates[:, :, 1] * branch_b.astype(f32)).astype(x.dtype)
    return x + merged @ w_o, k, v, ki, conv_new, ssm_new


def moe_sublayer(x, norm_g, w_router_group, b_router_group, w_router_expert,
                 b_router_expert, w1, w3, w2):
    f32 = jnp.float32
    h = rmsnorm(x, norm_g)
    glog = (h @ w_router_group).astype(f32) + b_router_group.astype(f32)
    gprob = jax.nn.softmax(glog, axis=-1)
    gsel = jnp.argmax(glog, axis=-1)
    p_group = jnp.take_along_axis(gprob, gsel[:, None], axis=-1)
    elog = ((h @ w_router_expert).astype(f32) + b_router_expert.astype(f32)).reshape(
        -1, MOE_GROUPS, EXPERTS_PER_GROUP)
    elog_g = jnp.take_along_axis(elog, gsel[:, None, None], axis=1)[:, 0]
    top_v, top_i = lax.top_k(elog_g, TOPK_IN_GROUP)
    top_w = jax.nn.softmax(top_v, axis=-1) * p_group
    eid = gsel[:, None] * EXPERTS_PER_GROUP + top_i
    combine = jnp.sum(jax.nn.one_hot(eid, N_EXPERTS, dtype=f32) * top_w[..., None], axis=1)
    y = jnp.zeros(x.shape, f32)
    for e in range(N_EXPERTS):
        he = jax.nn.silu(h @ w1[e]) * (h @ w3[e])
        y = y + combine[:, e:e + 1] * (he @ w2[e]).astype(f32)
    return x + y.astype(x.dtype)


def setup_inputs(seed: int = 0) -> dict:
    key = jax.random.key(seed)
    ks = jax.random.split(key, 32)
    f32 = jnp.float32
    nrm = lambda k, shape, s: jax.random.normal(k, shape, f32) * s
    dt_init = jnp.exp(jax.random.uniform(ks[10], (SSM_HEADS,), f32, np.log(1e-3), np.log(1e-1)))
    return {
        "x_prompt": nrm(ks[0], (BATCH, SEQ, D_MODEL), 1.0),
        "x_sample": nrm(ks[1], (DEC_BATCH, DEC_SEQ, D_MODEL), 1.0),
        "cache_k": nrm(ks[2], (DEC_BATCH, PAST_LEN, ATTN_KV_HEADS, HEAD_DIM), 1.0),
        "cache_v": nrm(ks[3], (DEC_BATCH, PAST_LEN, ATTN_KV_HEADS, HEAD_DIM), 1.0),
        "cache_kidx": nrm(ks[4], (DEC_BATCH, PAST_LEN, IDX_DIM), 1.0),
        "state_conv": nrm(ks[5], (DEC_BATCH, CONV_WIDTH - 1, CONV_DIM), 1.0),
        "state_ssm": nrm(ks[6], (DEC_BATCH, SSM_HEADS, SSM_HEADDIM, SSM_STATE), 0.3),
        "norm_mix_g": 1.0 + nrm(ks[7], (D_MODEL,), 0.02),
        "w_in": nrm(ks[8], (D_MODEL, PROJ_WIDTH), D_MODEL ** -0.5),
        "conv_w": nrm(ks[9], (CONV_WIDTH, CONV_DIM), CONV_WIDTH ** -0.5),
        "conv_b": nrm(ks[11], (CONV_DIM,), 0.01),
        "dt_bias": dt_init + jnp.log(-jnp.expm1(-dt_init)),
        "A_log": jnp.log(jax.random.uniform(ks[12], (SSM_HEADS,), f32, 1.0, 16.0)),
        "D_skip": 1.0 + nrm(ks[13], (SSM_HEADS,), 0.02),
        "ssm_norm_g": 1.0 + nrm(ks[14], (SSM_INNER,), 0.02),
        "w_ssm_out": nrm(ks[15], (SSM_INNER, D_MODEL), SSM_INNER ** -0.5),
        "w_attn_out": nrm(ks[16], (ATTN_WIDTH, D_MODEL), ATTN_WIDTH ** -0.5),
        "w_o": nrm(ks[17], (D_MODEL, D_MODEL), D_MODEL ** -0.5),
        "norm_moe_g": 1.0 + nrm(ks[18], (D_MODEL,), 0.02),
        "w_router_group": nrm(ks[19], (D_MODEL, MOE_GROUPS), D_MODEL ** -0.5),
        "b_router_group": nrm(ks[20], (MOE_GROUPS,), 0.01),
        "w_router_expert": nrm(ks[21], (D_MODEL, N_EXPERTS), D_MODEL ** -0.5),
        "b_router_expert": nrm(ks[22], (N_EXPERTS,), 0.01),
        "w1": nrm(ks[23], (N_EXPERTS, D_MODEL, EXPERT_FF), D_MODEL ** -0.5),
        "w3": nrm(ks[24], (N_EXPERTS, D_MODEL, EXPERT_FF), D_MODEL ** -0.5),
        "w2": nrm(ks[25], (N_EXPERTS, EXPERT_FF, D_MODEL), EXPERT_FF ** -0.5),
        "norm_final_g": 1.0 + nrm(ks[26], (D_MODEL,), 0.02),
    }


def reference(x_prompt, x_sample, cache_k, cache_v, cache_kidx, state_conv, state_ssm,
              norm_mix_g, w_in, conv_w, conv_b, dt_bias, A_log, D_skip, ssm_norm_g,
              w_ssm_out, w_attn_out, w_o, norm_moe_g, w_router_group, b_router_group,
              w_router_expert, b_router_expert, w1, w3, w2, norm_final_g):
    b, T, D = x_prompt.shape
    db, ts = x_sample.shape[0], x_sample.shape[1]
    dtype = x_prompt.dtype
    zero_conv = jnp.zeros((b, CONV_WIDTH - 1, CONV_DIM), dtype)
    zero_ssm = jnp.zeros((b, SSM_HEADS, SSM_HEADDIM, SSM_STATE), jnp.float32)
    empty_kv = jnp.zeros((b, 0, ATTN_KV_HEADS, HEAD_DIM), dtype)
    empty_kidx = jnp.zeros((b, 0, IDX_DIM), dtype)
    mix_w = (norm_mix_g, w_in, conv_w, conv_b, dt_bias, A_log, D_skip,
             ssm_norm_g, w_ssm_out, w_attn_out, w_o)
    xp, xs = x_prompt, x_sample
    for layer in range(DEPTH):
        xp, kp, vp, kip, cp, sp = mixer_sublayer(xp, zero_conv, zero_ssm, empty_kv, empty_kv,
                                                 empty_kidx, *mix_w)
        xs, ksn, vsn, kisn, csn, ssn = mixer_sublayer(xs, state_conv, state_ssm, cache_k,
                                                      cache_v, cache_kidx, *mix_w)
        tokens = jnp.concatenate([xp.reshape(b * T, D), xs.reshape(db * ts, D)], axis=0)
        tokens = moe_sublayer(tokens, norm_moe_g, w_router_group, b_router_group,
                              w_router_expert, b_router_expert, w1, w3, w2)
        xp = tokens[:b * T].reshape(b, T, D)
        xs = tokens[b * T:].reshape(db, ts, D)
    y_prompt = rmsnorm(xp, norm_final_g)
    y_sample = rmsnorm(xs, norm_final_g)
    return (y_prompt, y_sample, kp, vp, kip, cp, sp, ksn, vsn, kisn, csn, ssn)
```

```python
import functools

import numpy as np
import jax
import jax.numpy as jnp
from jax import lax
from jax.experimental import pallas as pl
from jax.experimental.pallas import tpu as pltpu

F32 = jnp.float32
BF16 = jnp.bfloat16
I32 = jnp.int32

D_MODEL = 1024
CHUNK = 64
CHUNK_SHIFT = 6
EPS = 1e-6
SSM_INNER = 2048
SSM_HEADDIM = 64
SSM_HEADS = 32
SSM_GROUPS = 4
SSM_HEADS_PER_GROUP = SSM_HEADS // SSM_GROUPS
SSM_STATE = 128
CONV_WIDTH = 4
CONV_DIM = SSM_INNER + 2 * SSM_GROUPS * SSM_STATE
ATTN_HEADS = 16
ATTN_KV_HEADS = 2
HEAD_DIM = 64
ATTN_REP = ATTN_HEADS // ATTN_KV_HEADS
ATTN_WIDTH = ATTN_HEADS * HEAD_DIM
IDX_HEADS = 8
IDX_DIM = 64
IDX_SCALE = (IDX_HEADS * IDX_DIM) ** -0.5
TOPK_MAX = 256
MOE_GROUPS = 4
EXPERTS_PER_GROUP = 8
EXPERT_GROUP_SHIFT = 3
N_EXPERTS = 32
EXPERT_FF = 256
N_BRANCHES = 2

LANE = 128
VMEM_LIMIT = 56 * 1024 * 1024
PROJ_TILE = 1024
PROJ_TILES = 9
Q_BLOCK = 128
KEY_TILE = 256
EXPERTS_PER_STEP = 4
MERGE_TILE = 512
INT_MIN = np.int32(-2 ** 31)
NEG_BIG = -1e30

NT_DIMS = (((1,), (1,)), ((), ()))
TN_DIMS = (((0,), (0,)), ((), ()))


def _sigmoid(x):
    return 1.0 / (1.0 + jnp.exp(-x))


def _split3(x):
    hi = x.astype(BF16)
    r1 = x - hi.astype(F32)
    mid = r1.astype(BF16)
    r2 = r1 - mid.astype(F32)
    lo = r2.astype(BF16)
    return jnp.concatenate([hi, mid, lo], axis=1)


def _inproj_kernel(x_ref, g_ref, w_ref, z_ref, xbc_ref, q_ref, gate_ref, k_ref, v_ref,
                   qi_ref, ki_ref, dtw_ref, h_ref):
    j = pl.program_id(1)

    @pl.when(j == 0)
    def _():
        x = x_ref[...]
        ms = jnp.mean(x * x, axis=-1, keepdims=True)
        h_ref[...] = ((x * lax.rsqrt(ms + EPS)) * g_ref[...]).astype(BF16)

    def tile():
        return jnp.dot(h_ref[...], w_ref[...], preferred_element_type=F32)

    @pl.when(j < 2)
    def _():
        z_ref[...] = tile().astype(BF16)

    @pl.when((j >= 2) & (j < 5))
    def _():
        xbc_ref[...] = tile().astype(BF16)

    @pl.when(j == 5)
    def _():
        q_ref[...] = tile().astype(BF16)

    @pl.when((j >= 6) & (j < 8))
    def _():
        gate_ref[...] = tile().astype(BF16)

    @pl.when(j == 8)
    def _():
        t = tile()
        k_ref[...] = t[:, 0:128]
        v_ref[...] = t[:, 128:256]
        qi_ref[...] = t[:, 256:768].astype(BF16)
        ki_ref[...] = t[:, 768:832]
        dtw_ref[...] = t[:, 896:1024]


def _pack_w_in(w_in):
    sizes = (SSM_INNER, CONV_DIM, SSM_HEADS, ATTN_WIDTH, ATTN_KV_HEADS * HEAD_DIM,
             ATTN_KV_HEADS * HEAD_DIM, IDX_HEADS * IDX_DIM, IDX_DIM, IDX_HEADS,
             N_BRANCHES * D_MODEL)
    offs = np.concatenate([[0], np.cumsum(sizes)])
    z, xbc, dt, q, k, v, qi, ki, wi, g = [w_in[:, offs[i]:offs[i + 1]] for i in range(10)]
    zeros = lambda n: jnp.zeros((D_MODEL, n), w_in.dtype)
    cols = [z, xbc, q, g, k, v, qi, ki, zeros(64), dt, wi, zeros(LANE - SSM_HEADS - IDX_HEADS)]
    return jnp.concatenate(cols, axis=1).astype(BF16)


def _in_proj(x2d, norm_g, w_packed, tm):
    n = x2d.shape[0]
    assert n % tm == 0
    bf = lambda c: jax.ShapeDtypeStruct((n, c), BF16)
    ff = lambda c: jax.ShapeDtypeStruct((n, c), F32)
    clip = lambda j, lo, hi: jnp.clip(j - lo, 0, hi - lo - 1)
    return pl.pallas_call(
        _inproj_kernel,
        out_shape=(bf(SSM_INNER), bf(CONV_DIM), bf(ATTN_WIDTH), bf(2 * D_MODEL),
                   ff(128), ff(128), bf(512), ff(IDX_DIM), ff(128)),
        grid=(n // tm, PROJ_TILES),
        in_specs=[
            pl.BlockSpec((tm, D_MODEL), lambda i, j: (i, 0)),
            pl.BlockSpec((1, D_MODEL), lambda i, j: (0, 0)),
            pl.BlockSpec((D_MODEL, PROJ_TILE), lambda i, j: (0, j)),
        ],
        out_specs=(
            pl.BlockSpec((tm, PROJ_TILE), lambda i, j: (i, clip(j, 0, 2))),
            pl.BlockSpec((tm, PROJ_TILE), lambda i, j: (i, clip(j, 2, 5))),
            pl.BlockSpec((tm, PROJ_TILE), lambda i, j: (i, 0)),
            pl.BlockSpec((tm, PROJ_TILE), lambda i, j: (i, clip(j, 6, 8))),
            pl.BlockSpec((tm, 128), lambda i, j: (i, 0)),
            pl.BlockSpec((tm, 128), lambda i, j: (i, 0)),
            pl.BlockSpec((tm, 512), lambda i, j: (i, 0)),
            pl.BlockSpec((tm, IDX_DIM), lambda i, j: (i, 0)),
            pl.BlockSpec((tm, 128), lambda i, j: (i, 0)),
        ),
        scratch_shapes=[pltpu.VMEM((tm, D_MODEL), BF16)],
        compiler_params=pltpu.CompilerParams(
            dimension_semantics=("arbitrary", "arbitrary"), vmem_limit_bytes=VMEM_LIMIT),
        name="in_proj",
    )(x2d, norm_g.reshape(1, D_MODEL), w_packed)


def _ssd_kernel(*refs, L, nc, has_state):
    if has_state:
        (xbc_ref, z_ref, dtw_ref, cs_ref, s0_ref, convw_ref, convb_ref, dtb_ref, a_ref, dx_ref,
         ng_ref, e3p_ref, e3l_ref, tri_ref, bd_ref, yn_ref, cnew_ref, sout_ref, xpad, st) = refs
    else:
        (xbc_ref, z_ref, dtw_ref, convw_ref, convb_ref, dtb_ref, a_ref, dx_ref,
         ng_ref, e3p_ref, e3l_ref, tri_ref, bd_ref, yn_ref, cnew_ref, sout_ref, xpad, st) = refs
    c = pl.program_id(1)
    G, R, N, P = SSM_GROUPS, SSM_HEADS_PER_GROUP, SSM_STATE, SSM_HEADDIM
    GW = R * P
    HL = SSM_HEADS * L

    @pl.when(c == 0)
    def _():
        xpad[0:8, :] = jnp.zeros((8, CONV_DIM), F32)
        if has_state:
            xpad[5:8, :] = cs_ref[0]
            st[...] = s0_ref[0].T
        else:
            st[...] = jnp.zeros(st.shape, F32)

    xpad[8:8 + L, :] = xbc_ref[0].astype(F32)
    w = convw_ref[...]
    conv = convb_ref[...] + xpad[5:5 + L, :] * w[0:1, :]
    conv = conv + xpad[6:6 + L, :] * w[1:2, :]
    conv = conv + xpad[7:7 + L, :] * w[2:3, :]
    conv = conv + xpad[8:8 + L, :] * w[3:4, :]
    act = conv * _sigmoid(conv)
    xs = act[:, :SSM_INNER]
    bmat = act[:, SSM_INNER:SSM_INNER + G * N].astype(BF16)
    cmat = act[:, SSM_INNER + G * N:].astype(BF16)

    @pl.when(c == nc - 1)
    def _():
        cnew_ref[0] = xpad[8 + L - 3:8 + L, :]

    xpad[5:8, :] = xpad[L + 5:L + 8, :]

    u = dtw_ref[0] + dtb_ref[...]
    dt = jnp.maximum(u, 0.0) + jnp.log1p(jnp.exp(-jnp.abs(u)))
    a = dt * a_ref[...]
    ac3 = jnp.dot(tri_ref[...], _split3(a), preferred_element_type=F32)
    a_cum = ac3[:, 0:LANE] + ac3[:, LANE:2 * LANE] + ac3[:, 2 * LANE:]

    ex = jnp.dot(_split3(jnp.concatenate([a_cum, dt], axis=0)), e3p_ref[...],
                 preferred_element_type=F32)
    acx = ex[:L]
    dtx = ex[L:]
    if L == P:
        acl = acx
    else:
        acl = jnp.dot(_split3(a_cum), e3l_ref[...], preferred_element_type=F32)

    row = lax.broadcasted_iota(I32, (L, HL), 0)
    scol = lax.broadcasted_iota(I32, (L, HL), 1) & (L - 1)
    acs = jnp.sum(jnp.where(scol == row, acl, 0.0), axis=0, keepdims=True)
    decay = jnp.where(scol <= row, jnp.exp(acl - acs), 0.0)

    cb = []
    for g in range(G):
        bt = jnp.concatenate([bmat[:, g * N:(g + 1) * N]] * R, axis=0)
        cb.append(lax.dot_general(cmat[:, g * N:(g + 1) * N], bt, NT_DIMS,
                                  preferred_element_type=F32))
    m = (jnp.concatenate(cb, axis=1) * decay).astype(BF16)

    xdt = xs * dtx
    xdt_b = xdt.astype(BF16)
    eacx = jnp.exp(acx)
    alast = acx[L - 1:L, :]
    xw = (xdt * jnp.exp(alast - acx)).astype(BF16)
    ealast = jnp.exp(alast)

    ys = []
    for g in range(G):
        sl = slice(g * GW, (g + 1) * GW)
        xbd = jnp.concatenate([xdt_b[:, sl]] * R, axis=0) * bd_ref[...]
        y_in = jnp.dot(m[:, g * R * L:(g + 1) * R * L], xbd, preferred_element_type=F32)
        st_g = st[:, sl]
        y_st = jnp.dot(cmat[:, g * N:(g + 1) * N], st_g.astype(BF16),
                       preferred_element_type=F32)
        ys.append(y_in + eacx[:, sl] * y_st)
        upd = lax.dot_general(bmat[:, g * N:(g + 1) * N], xw[:, sl], TN_DIMS,
                              preferred_element_type=F32)
        st[:, sl] = st_g * ealast[:, sl] + upd
    y = jnp.concatenate(ys, axis=1) + dx_ref[...] * xs

    zf = z_ref[0].astype(F32)
    yz = y * (zf * _sigmoid(zf))
    ms = jnp.mean(yz * yz, axis=-1, keepdims=True)
    yn_ref[0] = ((yz * lax.rsqrt(ms + EPS)) * ng_ref[...]).astype(BF16)

    @pl.when(c == nc - 1)
    def _():
        sout_ref[0] = st[...].T


def _expand_matrix(width_per_head):
    r = np.arange(3 * LANE)[:, None] % LANE
    c = np.arange(SSM_HEADS * width_per_head)[None, :] // width_per_head
    return jnp.asarray((r == c).astype(np.float32), BF16)


def _ssd(xbc, z, dtw, conv_state, ssm_state, conv_w, conv_b, dt_bias, a_log, d_skip, ssm_norm_g):
    b, t, _ = xbc.shape
    L = min(CHUNK, t)
    nc = t // L
    assert t % L == 0 and L & (L - 1) == 0
    has_state = conv_state is not None
    pad = lambda v: jnp.pad(v.astype(F32), (0, LANE - SSM_HEADS)).reshape(1, LANE)
    a_neg = pad(-jnp.exp(a_log.astype(F32)))
    tri = jnp.asarray(np.tril(np.ones((L, L), np.float32)), BF16)
    rl = SSM_HEADS_PER_GROUP * L
    gw = SSM_HEADS_PER_GROUP * SSM_HEADDIM
    bd = jnp.asarray((np.arange(rl)[:, None] // L == np.arange(gw)[None, :] // SSM_HEADDIM)
                     .astype(np.float32), BF16)
    const = lambda shape: pl.BlockSpec(shape, lambda i, j: (0,) * len(shape))
    tok = lambda w: pl.BlockSpec((1, L, w), lambda i, j: (i, j, 0))
    per_b = lambda s: pl.BlockSpec((1,) + s, lambda i, j: (i, 0, 0))
    ins = [xbc, z, dtw]
    specs = [tok(CONV_DIM), tok(SSM_INNER), tok(LANE)]
    if has_state:
        ins += [conv_state, ssm_state.reshape(b, SSM_INNER, SSM_STATE)]
        specs += [per_b((CONV_WIDTH - 1, CONV_DIM)), per_b((SSM_INNER, SSM_STATE))]
    ins += [conv_w, conv_b.reshape(1, CONV_DIM), pad(dt_bias), a_neg,
            jnp.repeat(d_skip.astype(F32), SSM_HEADDIM).reshape(1, SSM_INNER),
            ssm_norm_g.reshape(1, SSM_INNER), _expand_matrix(SSM_HEADDIM), _expand_matrix(L),
            tri, bd]
    specs += [const((CONV_WIDTH, CONV_DIM)), const((1, CONV_DIM)), const((1, LANE)),
              const((1, LANE)), const((1, SSM_INNER)), const((1, SSM_INNER)),
              const((3 * LANE, SSM_INNER)), const((3 * LANE, SSM_HEADS * L)),
              const((L, L)), const((rl, gw))]
    yn, cnew, sout = pl.pallas_call(
        functools.partial(_ssd_kernel, L=L, nc=nc, has_state=has_state),
        out_shape=(jax.ShapeDtypeStruct((b, t, SSM_INNER), BF16),
                   jax.ShapeDtypeStruct((b, CONV_WIDTH - 1, CONV_DIM), F32),
                   jax.ShapeDtypeStruct((b, SSM_INNER, SSM_STATE), F32)),
        grid=(b, nc),
        in_specs=specs,
        out_specs=(tok(SSM_INNER), per_b((CONV_WIDTH - 1, CONV_DIM)),
                   per_b((SSM_INNER, SSM_STATE))),
        scratch_shapes=[pltpu.VMEM((L + 8, CONV_DIM), F32),
                        pltpu.VMEM((SSM_STATE, SSM_INNER), F32)],
        compiler_params=pltpu.CompilerParams(
            dimension_semantics=("arbitrary", "arbitrary"), vmem_limit_bytes=VMEM_LIMIT),
        name="ssd",
    )(*ins)
    return yn, cnew, sout.reshape(b, SSM_HEADS, SSM_HEADDIM, SSM_STATE)


def _dsa_kernel(q_ref, qi_ref, dtw_ref, kidx_ref, kh_ref, vh_ref, o_ref,
                qs_ref, key_ref, bias_ref, m_ref, l_ref, acc_ref, mlim_ref,
                *, past_len, l_valid, topk):
    qb = pl.program_id(1)
    TQ, TK = Q_BLOCK, KEY_TILE

    pos = past_len + qb * TQ + lax.broadcasted_iota(I32, (1, TQ), 1)
    n_adm = jnp.minimum(((pos >> CHUNK_SHIFT) + 1) * CHUNK, l_valid)
    k_eff = jnp.minimum(n_adm, topk)
    last_pos = past_len + qb * TQ + TQ - 1
    n_max = jnp.minimum(((last_pos >> CHUNK_SHIFT) + 1) * CHUNK, l_valid)
    nt = (n_max + TK - 1) // TK

    for h in range(IDX_HEADS):
        qs_ref[h * TQ:(h + 1) * TQ, :] = qi_ref[0, :, h * IDX_DIM:(h + 1) * IDX_DIM]
    w_t = dtw_ref[0].T[SSM_HEADS:SSM_HEADS + IDX_HEADS, :] * IDX_SCALE

    def score_tile(t, carry):
        s0 = pl.multiple_of(t * TK, TK)
        d = lax.dot_general(kidx_ref[0, pl.ds(s0, TK), :], qs_ref[...], NT_DIMS,
                            preferred_element_type=F32)
        sc = w_t[0:1, :] * jnp.maximum(d[:, 0:TQ], 0.0)
        for h in range(1, IDX_HEADS):
            sc = sc + w_t[h:h + 1, :] * jnp.maximum(d[:, h * TQ:(h + 1) * TQ], 0.0)
        sc = sc + 0.0
        bits = pltpu.bitcast(sc, I32)
        key = jnp.where(bits < 0, bits ^ np.int32(0x7FFFFFFF), bits)
        s_idx = s0 + lax.broadcasted_iota(I32, (TK, TQ), 0)
        key_ref[pl.ds(s0, TK), :] = jnp.where(s_idx < n_adm, key, INT_MIN)
        return carry

    lax.fori_loop(0, nt, score_tile, 0)

    def count(pred):
        def body(t, acc):
            s0 = pl.multiple_of(t * TK, TK)
            blk = key_ref[pl.ds(s0, TK), :]
            s_idx = s0 + lax.broadcasted_iota(I32, (TK, TQ), 0)
            hit = jnp.where(pred(blk, s_idx), 1, 0).astype(I32)
            return acc + jnp.sum(hit.reshape(TK // 8, 8, TQ), axis=0)
        acc = lax.fori_loop(0, nt, body, jnp.zeros((8, TQ), I32))
        return jnp.sum(acc, axis=0, keepdims=True)

    def bit_step(i, prefix):
        cand = prefix | jnp.left_shift(jnp.int32(1), 31 - i)
        cand_s = cand ^ INT_MIN
        cnt = count(lambda blk, s_idx: blk >= cand_s)
        return jnp.where(cnt >= k_eff, cand, prefix)

    thr = lax.fori_loop(0, 32, bit_step, jnp.zeros((1, TQ), I32)) ^ INT_MIN

    n_gt = count(lambda blk, s_idx: blk > thr)
    n_eq = count(lambda blk, s_idx: blk == thr)
    need = k_eff - n_gt
    mlim_ref[...] = jnp.full((1, TQ), 2 ** 30, I32)

    @pl.when(jnp.max(n_eq - need) > 0)
    def _():
        def idx_step(i, prefix):
            cand = prefix | jnp.left_shift(jnp.int32(1), 14 - i)
            cnt = count(lambda blk, s_idx: (blk == thr) & (s_idx < cand))
            return jnp.where(cnt < need, cand, prefix)
        mlim_ref[...] = lax.fori_loop(0, 15, idx_step, jnp.zeros((1, TQ), I32))

    mlim = mlim_ref[...]

    def bias_tile(t, carry):
        s0 = pl.multiple_of(t * TK, TK)
        blk = key_ref[pl.ds(s0, TK), :]
        s_idx = s0 + lax.broadcasted_iota(I32, (TK, TQ), 0)
        sel = (blk > thr) | ((blk == thr) & (s_idx <= mlim))
        bias_ref[:, pl.ds(s0, TK)] = jnp.where(sel, 0.0, NEG_BIG).astype(F32).T
        return carry

    lax.fori_loop(0, nt, bias_tile, 0)

    for g in range(ATTN_KV_HEADS):
        for r in range(ATTN_REP):
            hq = g * ATTN_REP + r
            qs_ref[r * TQ:(r + 1) * TQ, :] = (
                q_ref[0, :, hq * HEAD_DIM:(hq + 1) * HEAD_DIM] * (HEAD_DIM ** -0.5))
        m_ref[...] = jnp.full(m_ref.shape, -jnp.inf, F32)
        l_ref[...] = jnp.zeros(l_ref.shape, F32)
        acc_ref[...] = jnp.zeros(acc_ref.shape, F32)

        def attn_tile(t, carry):
            s0 = pl.multiple_of(t * TK, TK)
            s = lax.dot_general(qs_ref[...], kh_ref[0, g, pl.ds(s0, TK), :], NT_DIMS,
                                preferred_element_type=F32)
            s = (s.reshape(ATTN_REP, TQ, TK) + bias_ref[:, pl.ds(s0, TK)][None]
                 ).reshape(ATTN_REP * TQ, TK)
            m_prev = m_ref[...]
            m_new = jnp.maximum(m_prev, jnp.max(s, axis=1, keepdims=True))
            p = jnp.exp(s - m_new)
            alpha = jnp.exp(m_prev - m_new)
            l_ref[...] = alpha * l_ref[...] + jnp.sum(p, axis=1, keepdims=True)
            acc_ref[...] = alpha * acc_ref[...] + jnp.dot(
                p.astype(BF16), vh_ref[0, g, pl.ds(s0, TK), :], preferred_element_type=F32)
            m_ref[...] = m_new
            return carry

        lax.fori_loop(0, nt, attn_tile, 0)
        o_g = acc_ref[...] / l_ref[...]
        for r in range(ATTN_REP):
            hq = g * ATTN_REP + r
            o_ref[0, :, hq * HEAD_DIM:(hq + 1) * HEAD_DIM] = (
                o_g[r * TQ:(r + 1) * TQ, :].astype(BF16))


def _dsa(q, qi, dtw, kidx_all, k_heads, v_heads, past_len, l_valid):
    b, tq, _ = q.shape
    lp = kidx_all.shape[1]
    assert tq % Q_BLOCK == 0 and lp % KEY_TILE == 0 and lp < 2 ** 15
    topk = min(TOPK_MAX, l_valid // 4)
    qblk = lambda w: pl.BlockSpec((1, Q_BLOCK, w), lambda i, j: (i, j, 0))
    return pl.pallas_call(
        functools.partial(_dsa_kernel, past_len=past_len, l_valid=l_valid, topk=topk),
        out_shape=jax.ShapeDtypeStruct((b, tq, ATTN_WIDTH), BF16),
        grid=(b, tq // Q_BLOCK),
        in_specs=[qblk(ATTN_WIDTH), qblk(IDX_HEADS * IDX_DIM), qblk(LANE),
                  pl.BlockSpec((1, lp, IDX_DIM), lambda i, j: (i, 0, 0)),
                  pl.BlockSpec((1, ATTN_KV_HEADS, lp, HEAD_DIM), lambda i, j: (i, 0, 0, 0)),
                  pl.BlockSpec((1, ATTN_KV_HEADS, lp, HEAD_DIM), lambda i, j: (i, 0, 0, 0))],
        out_specs=qblk(ATTN_WIDTH),
        scratch_shapes=[pltpu.VMEM((ATTN_REP * Q_BLOCK, HEAD_DIM), BF16),
                        pltpu.VMEM((lp, Q_BLOCK), I32),
                        pltpu.VMEM((Q_BLOCK, lp), F32),
                        pltpu.VMEM((ATTN_REP * Q_BLOCK, 1), F32),
                        pltpu.VMEM((ATTN_REP * Q_BLOCK, 1), F32),
                        pltpu.VMEM((ATTN_REP * Q_BLOCK, HEAD_DIM), F32),
                        pltpu.VMEM((1, Q_BLOCK), I32)],
        compiler_params=pltpu.CompilerParams(
            dimension_semantics=("arbitrary", "arbitrary"), vmem_limit_bytes=VMEM_LIMIT),
        name="dsa",
    )(q, qi, dtw, kidx_all, k_heads, v_heads)


def _merge_kernel(x_ref, yn_ref, o_ref, gate_ref, wa_ref, wb_ref, wo_ref, out_ref):
    br_a = jnp.dot(yn_ref[...], wa_ref[...], preferred_element_type=F32)
    br_b = jnp.dot(o_ref[...], wb_ref[...], preferred_element_type=F32)
    gates = gate_ref[...].astype(F32)
    merged = _sigmoid(gates[:, :D_MODEL]) * br_a + _sigmoid(gates[:, D_MODEL:]) * br_b
    out_ref[...] = x_ref[...] + jnp.dot(merged.astype(BF16), wo_ref[...],
                                        preferred_element_type=F32)


def _merge(x2d, yn, o, gate, w_ssm_out, w_attn_out, w_o, tm):
    n = x2d.shape[0]
    assert n % tm == 0
    row = lambda w: pl.BlockSpec((tm, w), lambda i: (i, 0))
    full = lambda r, c: pl.BlockSpec((r, c), lambda i: (0, 0))
    return pl.pallas_call(
        _merge_kernel,
        out_shape=jax.ShapeDtypeStruct((n, D_MODEL), F32),
        grid=(n // tm,),
        in_specs=[row(D_MODEL), row(SSM_INNER), row(ATTN_WIDTH), row(2 * D_MODEL),
                  full(SSM_INNER, D_MODEL), full(ATTN_WIDTH, D_MODEL), full(D_MODEL, D_MODEL)],
        out_specs=row(D_MODEL),
        compiler_params=pltpu.CompilerParams(
            dimension_semantics=("arbitrary",), vmem_limit_bytes=VMEM_LIMIT),
        name="merge",
    )(x2d, yn, o, gate, w_ssm_out.astype(BF16), w_attn_out.astype(BF16), w_o.astype(BF16))


def _moe_kernel(x_ref, ng_ref, wr_ref, br_ref, w1_ref, w3_ref, w2_ref, fg_ref, out_ref,
                h_ref, comb_ref, acc_ref, *, n_steps):
    e_step = pl.program_id(1)
    tm = x_ref.shape[0]

    @pl.when(e_step == 0)
    def _():
        x = x_ref[...]
        ms = jnp.mean(x * x, axis=-1, keepdims=True)
        h = ((x * lax.rsqrt(ms + EPS)) * ng_ref[...]).astype(BF16)
        h_ref[...] = h
        logits = jnp.dot(h, wr_ref[...], preferred_element_type=F32) + br_ref[...]
        lane_i = lax.broadcasted_iota(I32, (tm, LANE), 1)
        lane = lane_i.astype(F32)
        first = lambda hit: jnp.min(jnp.where(hit, lane, float(LANE)), axis=1, keepdims=True)
        is_g = (lane_i >= N_EXPERTS) & (lane_i < N_EXPERTS + MOE_GROUPS)
        glog = jnp.where(is_g, logits, -jnp.inf)
        gmax = jnp.max(glog, axis=1, keepdims=True)
        gsel = first(glog == gmax) - float(N_EXPERTS)
        p_group = 1.0 / jnp.sum(jnp.exp(glog - gmax), axis=1, keepdims=True)
        grp = (lane_i >> EXPERT_GROUP_SHIFT).astype(F32)
        in_grp = (lane_i < N_EXPERTS) & (grp == gsel)
        el = jnp.where(in_grp, logits, -jnp.inf)
        v1 = jnp.max(el, axis=1, keepdims=True)
        i1 = first(el == v1)
        el2 = jnp.where(lane == i1, -jnp.inf, el)
        v2 = jnp.max(el2, axis=1, keepdims=True)
        i2 = first(el2 == v2)
        e21 = jnp.exp(v2 - v1)
        den = 1.0 + e21
        comb_ref[...] = (jnp.where(lane == i1, (1.0 / den) * p_group, 0.0)
                         + jnp.where(lane == i2, (e21 / den) * p_group, 0.0))
        acc_ref[...] = jnp.zeros(acc_ref.shape, F32)

    h = h_ref[...]
    comb = comb_ref[...]
    lane = lax.broadcasted_iota(I32, (tm, LANE), 1)
    for k in range(EXPERTS_PER_STEP):
        e = e_step * EXPERTS_PER_STEP + k
        a1 = jnp.dot(h, w1_ref[k], preferred_element_type=F32)
        a3 = jnp.dot(h, w3_ref[k], preferred_element_type=F32)
        he = ((a1 * _sigmoid(a1)) * a3).astype(BF16)
        ye = jnp.dot(he, w2_ref[k], preferred_element_type=F32)
        wcol = jnp.sum(jnp.where(lane == e, comb, 0.0), axis=1, keepdims=True)
        acc_ref[...] += wcol * ye

    @pl.when(e_step == n_steps - 1)
    def _():
        y = x_ref[...] + acc_ref[...]
        ms = jnp.mean(y * y, axis=-1, keepdims=True)
        out_ref[...] = (y * lax.rsqrt(ms + EPS)) * fg_ref[...]


def _moe(x2d, norm_g, w_router_group, b_router_group, w_router_expert, b_router_expert,
         w1, w3, w2, norm_final_g, tm):
    n = x2d.shape[0]
    assert n % tm == 0
    n_steps = N_EXPERTS // EXPERTS_PER_STEP
    padw = LANE - N_EXPERTS - MOE_GROUPS
    wr = jnp.concatenate([w_router_expert, w_router_group,
                          jnp.zeros((D_MODEL, padw), F32)], axis=1).astype(BF16)
    br = jnp.concatenate([b_router_expert, b_router_group,
                          jnp.zeros((padw,), F32)]).astype(F32).reshape(1, LANE)
    row = pl.BlockSpec((tm, D_MODEL), lambda i, e: (i, 0))
    const = lambda r, c: pl.BlockSpec((r, c), lambda i, e: (0, 0))
    return pl.pallas_call(
        functools.partial(_moe_kernel, n_steps=n_steps),
        out_shape=jax.ShapeDtypeStruct((n, D_MODEL), F32),
        grid=(n // tm, n_steps),
        in_specs=[row, const(1, D_MODEL), const(D_MODEL, LANE), const(1, LANE),
                  pl.BlockSpec((EXPERTS_PER_STEP, D_MODEL, EXPERT_FF), lambda i, e: (e, 0, 0)),
                  pl.BlockSpec((EXPERTS_PER_STEP, D_MODEL, EXPERT_FF), lambda i, e: (e, 0, 0)),
                  pl.BlockSpec((EXPERTS_PER_STEP, EXPERT_FF, D_MODEL), lambda i, e: (e, 0, 0)),
                  const(1, D_MODEL)],
        out_specs=row,
        scratch_shapes=[pltpu.VMEM((tm, D_MODEL), BF16), pltpu.VMEM((tm, LANE), F32),
                        pltpu.VMEM((tm, D_MODEL), F32)],
        compiler_params=pltpu.CompilerParams(
            dimension_semantics=("arbitrary", "arbitrary"), vmem_limit_bytes=VMEM_LIMIT),
        name="moe",
    )(x2d, norm_g.reshape(1, D_MODEL), wr, br, w1.astype(BF16), w3.astype(BF16),
      w2.astype(BF16), norm_final_g.reshape(1, D_MODEL))


def _token_tile(n):
    for tm in (1024, 512, 256, 128, 64, 32, 16, 8):
        if n % tm == 0:
            return tm
    raise ValueError(f"token count {n} is not a multiple of 8")


def _pad_axis(a, axis, size):
    if a.shape[axis] == size:
        return a
    widths = [(0, 0)] * a.ndim
    widths[axis] = (0, size - a.shape[axis])
    return jnp.pad(a, widths)


def _round_up(n, m):
    return (n + m - 1) // m * m


def _stream_step(x, conv_state, ssm_state, k_past, v_past, kidx_past, w_packed, p):
    b, t, d = x.shape
    n = b * t
    tm = _token_tile(n)
    x2d = x.reshape(n, d)
    z, xbc, q, gate, k, v, qi, ki, dtw = _in_proj(x2d, p["norm_mix_g"], w_packed, tm)

    yn, conv_new, ssm_new = _ssd(
        xbc.reshape(b, t, CONV_DIM), z.reshape(b, t, SSM_INNER), dtw.reshape(b, t, LANE),
        conv_state, ssm_state, p["conv_w"], p["conv_b"], p["dt_bias"], p["A_log"],
        p["D_skip"], p["ssm_norm_g"])

    k4 = k.reshape(b, t, ATTN_KV_HEADS, HEAD_DIM)
    v4 = v.reshape(b, t, ATTN_KV_HEADS, HEAD_DIM)
    ki3 = ki.reshape(b, t, IDX_DIM)
    if k_past is None:
        past_len, k_all, v_all, kidx_all = 0, k4, v4, ki3
    else:
        past_len = k_past.shape[1]
        k_all = jnp.concatenate([k_past, k4], axis=1)
        v_all = jnp.concatenate([v_past, v4], axis=1)
        kidx_all = jnp.concatenate([kidx_past, ki3], axis=1)
    l_valid = past_len + t
    lp = _round_up(l_valid, KEY_TILE)
    tq = _round_up(t, Q_BLOCK)
    heads = lambda a: _pad_axis(a.astype(BF16).transpose(0, 2, 1, 3), 2, lp)
    o = _dsa(_pad_axis(q.reshape(b, t, ATTN_WIDTH), 1, tq),
             _pad_axis(qi.reshape(b, t, IDX_HEADS * IDX_DIM), 1, tq),
             _pad_axis(dtw.reshape(b, t, LANE), 1, tq),
             _pad_axis(kidx_all.astype(BF16), 1, lp), heads(k_all), heads(v_all),
             past_len, l_valid)
    o = o[:, :t].reshape(n, ATTN_WIDTH)

    x1 = _merge(x2d, yn.reshape(n, SSM_INNER), o, gate, p["w_ssm_out"], p["w_attn_out"],
                p["w_o"], min(tm, MERGE_TILE))
    y = _moe(x1, p["norm_moe_g"], p["w_router_group"], p["b_router_group"],
             p["w_router_expert"], p["b_router_expert"], p["w1"], p["w3"], p["w2"],
             p["norm_final_g"], tm)
    return y.reshape(b, t, d), k4, v4, ki3, conv_new, ssm_new


def kernel(x_prompt, x_sample, cache_k, cache_v, cache_kidx, state_conv, state_ssm, norm_mix_g, w_in, conv_w, conv_b, dt_bias, A_log, D_skip, ssm_norm_g, w_ssm_out, w_attn_out, w_o, norm_moe_g, w_router_group, b_router_group, w_router_expert, b_router_expert, w1, w3, w2, norm_final_g):
    p = dict(norm_mix_g=norm_mix_g, conv_w=conv_w, conv_b=conv_b, dt_bias=dt_bias, A_log=A_log,
             D_skip=D_skip, ssm_norm_g=ssm_norm_g, w_ssm_out=w_ssm_out, w_attn_out=w_attn_out,
             w_o=w_o, norm_moe_g=norm_moe_g, w_router_group=w_router_group,
             b_router_group=b_router_group, w_router_expert=w_router_expert,
             b_router_expert=b_router_expert, w1=w1, w3=w3, w2=w2, norm_final_g=norm_final_g)
    w_packed = _pack_w_in(w_in)
    yp, kp, vp, kip, cp, sp = _stream_step(x_prompt, None, None, None, None, None, w_packed, p)
    ys, ksn, vsn, kisn, csn, ssn = _stream_step(x_sample, state_conv, state_ssm, cache_k,
                                                cache_v, cache_kidx, w_packed, p)
    return (yp, ys, kp, vp, kip, cp, sp, ksn, vsn, kisn, csn, ssn)
```

```python
import functools

import numpy as np
import jax
import jax.numpy as jnp
from jax import lax
from jax.experimental import pallas as pl
from jax.experimental.pallas import tpu as pltpu

F32 = jnp.float32
BF16 = jnp.bfloat16
I32 = jnp.int32

D_MODEL = 1024
CHUNK = 64
CHUNK_SHIFT = 6
EPS = 1e-6
SSM_INNER = 2048
SSM_HEADDIM = 64
SSM_HEADS = 32
SSM_GROUPS = 4
SSM_HEADS_PER_GROUP = SSM_HEADS // SSM_GROUPS
SSM_STATE = 128
CONV_WIDTH = 4
CONV_DIM = SSM_INNER + 2 * SSM_GROUPS * SSM_STATE
ATTN_HEADS = 16
ATTN_KV_HEADS = 2
HEAD_DIM = 64
ATTN_REP = ATTN_HEADS // ATTN_KV_HEADS
ATTN_WIDTH = ATTN_HEADS * HEAD_DIM
IDX_HEADS = 8
IDX_DIM = 64
IDX_SCALE = (IDX_HEADS * IDX_DIM) ** -0.5
TOPK_MAX = 256
MOE_GROUPS = 4
EXPERTS_PER_GROUP = 8
EXPERT_GROUP_SHIFT = 3
N_EXPERTS = 32
EXPERT_FF = 256
N_BRANCHES = 2

LANE = 128
VMEM_LIMIT = 56 * 1024 * 1024
PROJ_TILE = 1024
PROJ_TILES = 9
Q_BLOCK = 128
KEY_TILE = 256
EXPERTS_PER_STEP = 4
MERGE_TILE = 512
INT_MIN = np.int32(-2 ** 31)
NEG_BIG = -1e30

NT_DIMS = (((1,), (1,)), ((), ()))
TN_DIMS = (((0,), (0,)), ((), ()))


def _sigmoid(x):
    return 1.0 / (1.0 + jnp.exp(-x))


def _split3(x):
    hi = x.astype(BF16)
    r1 = x - hi.astype(F32)
    mid = r1.astype(BF16)
    r2 = r1 - mid.astype(F32)
    lo = r2.astype(BF16)
    return jnp.concatenate([hi, mid, lo], axis=1)


def _inproj_kernel(x_ref, g_ref, w_ref, z_ref, xbc_ref, q_ref, gate_ref, k_ref, v_ref,
                   qi_ref, ki_ref, dtw_ref, h_ref):
    j = pl.program_id(1)

    @pl.when(j == 0)
    def _():
        x = x_ref[...]
        ms = jnp.mean(x * x, axis=-1, keepdims=True)
        h_ref[...] = ((x * lax.rsqrt(ms + EPS)) * g_ref[...]).astype(BF16)

    def tile():
        return jnp.dot(h_ref[...], w_ref[...], preferred_element_type=F32)

    @pl.when(j < 2)
    def _():
        z_ref[...] = tile().astype(BF16)

    @pl.when((j >= 2) & (j < 5))
    def _():
        xbc_ref[...] = tile().astype(BF16)

    @pl.when(j == 5)
    def _():
        q_ref[...] = tile().astype(BF16)

    @pl.when((j >= 6) & (j < 8))
    def _():
        gate_ref[...] = tile().astype(BF16)

    @pl.when(j == 8)
    def _():
        t = tile()
        k_ref[...] = t[:, 0:128]
        v_ref[...] = t[:, 128:256]
        qi_ref[...] = t[:, 256:768].astype(BF16)
        ki_ref[...] = t[:, 768:832]
        dtw_ref[...] = t[:, 896:1024]


def _pack_w_in(w_in):
    sizes = (SSM_INNER, CONV_DIM, SSM_HEADS, ATTN_WIDTH, ATTN_KV_HEADS * HEAD_DIM,
             ATTN_KV_HEADS * HEAD_DIM, IDX_HEADS * IDX_DIM, IDX_DIM, IDX_HEADS,
             N_BRANCHES * D_MODEL)
    offs = np.concatenate([[0], np.cumsum(sizes)])
    z, xbc, dt, q, k, v, qi, ki, wi, g = [w_in[:, offs[i]:offs[i + 1]] for i in range(10)]
    zeros = lambda n: jnp.zeros((D_MODEL, n), w_in.dtype)
    cols = [z, xbc, q, g, k, v, qi, ki, zeros(64), dt, wi, zeros(LANE - SSM_HEADS - IDX_HEADS)]
    return jnp.concatenate(cols, axis=1).astype(BF16)


def _in_proj(x2d, norm_g, w_packed, tm):
    n = x2d.shape[0]
    assert n % tm == 0
    bf = lambda c: jax.ShapeDtypeStruct((n, c), BF16)
    ff = lambda c: jax.ShapeDtypeStruct((n, c), F32)
    clip = lambda j, lo, hi: jnp.clip(j - lo, 0, hi - lo - 1)
    return pl.pallas_call(
        _inproj_kernel,
        out_shape=(bf(SSM_INNER), bf(CONV_DIM), bf(ATTN_WIDTH), bf(2 * D_MODEL),
                   ff(128), ff(128), bf(512), ff(IDX_DIM), ff(128)),
        grid=(n // tm, PROJ_TILES),
        in_specs=[
            pl.BlockSpec((tm, D_MODEL), lambda i, j: (i, 0)),
            pl.BlockSpec((1, D_MODEL), lambda i, j: (0, 0)),
            pl.BlockSpec((D_MODEL, PROJ_TILE), lambda i, j: (0, j)),
        ],
        out_specs=(
            pl.BlockSpec((tm, PROJ_TILE), lambda i, j: (i, clip(j, 0, 2))),
            pl.BlockSpec((tm, PROJ_TILE), lambda i, j: (i, clip(j, 2, 5))),
            pl.BlockSpec((tm, PROJ_TILE), lambda i, j: (i, 0)),
            pl.BlockSpec((tm, PROJ_TILE), lambda i, j: (i, clip(j, 6, 8))),
            pl.BlockSpec((tm, 128), lambda i, j: (i, 0)),
            pl.BlockSpec((tm, 128), lambda i, j: (i, 0)),
            pl.BlockSpec((tm, 512), lambda i, j: (i, 0)),
            pl.BlockSpec((tm, IDX_DIM), lambda i, j: (i, 0)),
            pl.BlockSpec((tm, 128), lambda i, j: (i, 0)),
        ),
        scratch_shapes=[pltpu.VMEM((tm, D_MODEL), BF16)],
        compiler_params=pltpu.CompilerParams(
            dimension_semantics=("arbitrary", "arbitrary"), vmem_limit_bytes=VMEM_LIMIT),
        name="in_proj",
    )(x2d, norm_g.reshape(1, D_MODEL), w_packed)


def _ssd_kernel(*refs, L, nc, has_state):
    if has_state:
        (xbc_ref, z_ref, dtw_ref, cs_ref, s0_ref, convw_ref, convb_ref, dtb_ref, a_ref, dx_ref,
         ng_ref, e3p_ref, e3l_ref, tri_ref, bd_ref, yn_ref, cnew_ref, sout_ref, xpad, st) = refs
    else:
        (xbc_ref, z_ref, dtw_ref, convw_ref, convb_ref, dtb_ref, a_ref, dx_ref,
         ng_ref, e3p_ref, e3l_ref, tri_ref, bd_ref, yn_ref, cnew_ref, sout_ref, xpad, st) = refs
    c = pl.program_id(1)
    G, R, N, P = SSM_GROUPS, SSM_HEADS_PER_GROUP, SSM_STATE, SSM_HEADDIM
    GW = R * P
    HL = SSM_HEADS * L

    @pl.when(c == 0)
    def _():
        xpad[0:8, :] = jnp.zeros((8, CONV_DIM), F32)
        if has_state:
            xpad[5:8, :] = cs_ref[0]
            st[...] = s0_ref[0].T
        else:
            st[...] = jnp.zeros(st.shape, F32)

    xpad[8:8 + L, :] = xbc_ref[0].astype(F32)
    w = convw_ref[...]
    conv = convb_ref[...] + xpad[5:5 + L, :] * w[0:1, :]
    conv = conv + xpad[6:6 + L, :] * w[1:2, :]
    conv = conv + xpad[7:7 + L, :] * w[2:3, :]
    conv = conv + xpad[8:8 + L, :] * w[3:4, :]
    act = conv * _sigmoid(conv)
    xs = act[:, :SSM_INNER]
    bmat = act[:, SSM_INNER:SSM_INNER + G * N].astype(BF16)
    cmat = act[:, SSM_INNER + G * N:].astype(BF16)

    @pl.when(c == nc - 1)
    def _():
        cnew_ref[0] = xpad[8 + L - 3:8 + L, :]

    xpad[5:8, :] = xpad[L + 5:L + 8, :]

    u = dtw_ref[0] + dtb_ref[...]
    dt = jnp.maximum(u, 0.0) + jnp.log1p(jnp.exp(-jnp.abs(u)))
    a = dt * a_ref[...]
    ac3 = jnp.dot(tri_ref[...], _split3(a), preferred_element_type=F32)
    a_cum = ac3[:, 0:LANE] + ac3[:, LANE:2 * LANE] + ac3[:, 2 * LANE:]

    ex = jnp.dot(_split3(jnp.concatenate([a_cum, dt], axis=0)), e3p_ref[...],
                 preferred_element_type=F32)
    acx = ex[:L]
    dtx = ex[L:]
    if L == P:
        acl = acx
    else:
        acl = jnp.dot(_split3(a_cum), e3l_ref[...], preferred_element_type=F32)

    row = lax.broadcasted_iota(I32, (L, HL), 0)
    scol = lax.broadcasted_iota(I32, (L, HL), 1) & (L - 1)
    acs = jnp.sum(jnp.where(scol == row, acl, 0.0), axis=0, keepdims=True)
    decay = jnp.where(scol <= row, jnp.exp(acl - acs), 0.0)

    cb = []
    for g in range(G):
        bt = jnp.concatenate([bmat[:, g * N:(g + 1) * N]] * R, axis=0)
        cb.append(lax.dot_general(cmat[:, g * N:(g + 1) * N], bt, NT_DIMS,
                                  preferred_element_type=F32))
    m = (jnp.concatenate(cb, axis=1) * decay).astype(BF16)

    xdt = xs * dtx
    xdt_b = xdt.astype(BF16)
    eacx = jnp.exp(acx)
    alast = acx[L - 1:L, :]
    xw = (xdt * jnp.exp(alast - acx)).astype(BF16)
    ealast = jnp.exp(alast)

    ys = []
    for g in range(G):
        sl = slice(g * GW, (g + 1) * GW)
        xbd = jnp.concatenate([xdt_b[:, sl]] * R, axis=0) * bd_ref[...]
        y_in = jnp.dot(m[:, g * R * L:(g + 1) * R * L], xbd, preferred_element_type=F32)
        st_g = st[:, sl]
        y_st = jnp.dot(cmat[:, g * N:(g + 1) * N], st_g.astype(BF16),
                       preferred_element_type=F32)
        ys.append(y_in + eacx[:, sl] * y_st)
        upd = lax.dot_general(bmat[:, g * N:(g + 1) * N], xw[:, sl], TN_DIMS,
                              preferred_element_type=F32)
        st[:, sl] = st_g * ealast[:, sl] + upd
    y = jnp.concatenate(ys, axis=1) + dx_ref[...] * xs

    zf = z_ref[0].astype(F32)
    yz = y * (zf * _sigmoid(zf))
    ms = jnp.mean(yz * yz, axis=-1, keepdims=True)
    yn_ref[0] = ((yz * lax.rsqrt(ms + EPS)) * ng_ref[...]).astype(BF16)

    @pl.when(c == nc - 1)
    def _():
        sout_ref[0] = st[...].T


def _expand_matrix(width_per_head):
    r = np.arange(3 * LANE)[:, None] % LANE
    c = np.arange(SSM_HEADS * width_per_head)[None, :] // width_per_head
    return jnp.asarray((r == c).astype(np.float32), BF16)


def _ssd(xbc, z, dtw, conv_state, ssm_state, conv_w, conv_b, dt_bias, a_log, d_skip, ssm_norm_g):
    b, t, _ = xbc.shape
    L = min(CHUNK, t)
    nc = t // L
    assert t % L == 0 and L & (L - 1) == 0
    has_state = conv_state is not None
    pad = lambda v: jnp.pad(v.astype(F32), (0, LANE - SSM_HEADS)).reshape(1, LANE)
    a_neg = pad(-jnp.exp(a_log.astype(F32)))
    tri = jnp.asarray(np.tril(np.ones((L, L), np.float32)), BF16)
    rl = SSM_HEADS_PER_GROUP * L
    gw = SSM_HEADS_PER_GROUP * SSM_HEADDIM
    bd = jnp.asarray((np.arange(rl)[:, None] // L == np.arange(gw)[None, :] // SSM_HEADDIM)
                     .astype(np.float32), BF16)
    const = lambda shape: pl.BlockSpec(shape, lambda i, j: (0,) * len(shape))
    tok = lambda w: pl.BlockSpec((1, L, w), lambda i, j: (i, j, 0))
    per_b = lambda s: pl.BlockSpec((1,) + s, lambda i, j: (i, 0, 0))
    ins = [xbc, z, dtw]
    specs = [tok(CONV_DIM), tok(SSM_INNER), tok(LANE)]
    if has_state:
        ins += [conv_state, ssm_state.reshape(b, SSM_INNER, SSM_STATE)]
        specs += [per_b((CONV_WIDTH - 1, CONV_DIM)), per_b((SSM_INNER, SSM_STATE))]
    ins += [conv_w, conv_b.reshape(1, CONV_DIM), pad(dt_bias), a_neg,
            jnp.repeat(d_skip.astype(F32), SSM_HEADDIM).reshape(1, SSM_INNER),
            ssm_norm_g.reshape(1, SSM_INNER), _expand_matrix(SSM_HEADDIM), _expand_matrix(L),
            tri, bd]
    specs += [const((CONV_WIDTH, CONV_DIM)), const((1, CONV_DIM)), const((1, LANE)),
              const((1, LANE)), const((1, SSM_INNER)), const((1, SSM_INNER)),
              const((3 * LANE, SSM_INNER)), const((3 * LANE, SSM_HEADS * L)),
              const((L, L)), const((rl, gw))]
    yn, cnew, sout = pl.pallas_call(
        functools.partial(_ssd_kernel, L=L, nc=nc, has_state=has_state),
        out_shape=(jax.ShapeDtypeStruct((b, t, SSM_INNER), BF16),
                   jax.ShapeDtypeStruct((b, CONV_WIDTH - 1, CONV_DIM), F32),
                   jax.ShapeDtypeStruct((b, SSM_INNER, SSM_STATE), F32)),
        grid=(b, nc),
        in_specs=specs,
        out_specs=(tok(SSM_INNER), per_b((CONV_WIDTH - 1, CONV_DIM)),
                   per_b((SSM_INNER, SSM_STATE))),
        scratch_shapes=[pltpu.VMEM((L + 8, CONV_DIM), F32),
                        pltpu.VMEM((SSM_STATE, SSM_INNER), F32)],
        compiler_params=pltpu.CompilerParams(
            dimension_semantics=("arbitrary", "arbitrary"), vmem_limit_bytes=VMEM_LIMIT),
        name="ssd",
    )(*ins)
    return yn, cnew, sout.reshape(b, SSM_HEADS, SSM_HEADDIM, SSM_STATE)


def _dsa_kernel(q_ref, qi_ref, dtw_ref, kidx_ref, kh_ref, vt_ref, o_ref,
                qs_ref, key_ref, bias_ref, m_ref, l_ref, acc_ref, mlim_ref, s_ref,
                *, past_len, l_valid, topk):
    qb = pl.program_id(1)
    TQ, TK = Q_BLOCK, KEY_TILE

    pos = past_len + qb * TQ + lax.broadcasted_iota(I32, (1, TQ), 1)
    n_adm = jnp.minimum(((pos >> CHUNK_SHIFT) + 1) * CHUNK, l_valid)
    k_eff = jnp.minimum(n_adm, topk)
    last_pos = past_len + qb * TQ + TQ - 1
    n_max = jnp.minimum(((last_pos >> CHUNK_SHIFT) + 1) * CHUNK, l_valid)
    nt = (n_max + TK - 1) // TK

    for h in range(IDX_HEADS):
        qs_ref[h * TQ:(h + 1) * TQ, :] = qi_ref[0, :, h * IDX_DIM:(h + 1) * IDX_DIM]
    w_t = dtw_ref[0].T[SSM_HEADS:SSM_HEADS + IDX_HEADS, :] * IDX_SCALE

    def score_tile(t, carry):
        s0 = pl.multiple_of(t * TK, TK)
        d = lax.dot_general(kidx_ref[0, pl.ds(s0, TK), :], qs_ref[...], NT_DIMS,
                            preferred_element_type=F32)
        sc = w_t[0:1, :] * jnp.maximum(d[:, 0:TQ], 0.0)
        for h in range(1, IDX_HEADS):
            sc = sc + w_t[h:h + 1, :] * jnp.maximum(d[:, h * TQ:(h + 1) * TQ], 0.0)
        sc = sc + 0.0
        bits = pltpu.bitcast(sc, I32)
        key = jnp.where(bits < 0, bits ^ np.int32(0x7FFFFFFF), bits)
        s_idx = s0 + lax.broadcasted_iota(I32, (TK, TQ), 0)
        key_ref[pl.ds(s0, TK), :] = jnp.where(s_idx < n_adm, key, INT_MIN)
        return carry

    lax.fori_loop(0, nt, score_tile, 0)

    def count(pred):
        def body(t, acc):
            s0 = pl.multiple_of(t * TK, TK)
            blk = key_ref[pl.ds(s0, TK), :]
            s_idx = s0 + lax.broadcasted_iota(I32, (TK, TQ), 0)
            hit = jnp.where(pred(blk, s_idx), 1, 0).astype(I32)
            return acc + jnp.sum(hit.reshape(TK // 8, 8, TQ), axis=0)
        acc = lax.fori_loop(0, nt, body, jnp.zeros((8, TQ), I32))
        return jnp.sum(acc, axis=0, keepdims=True)

    def bit_step(i, prefix):
        cand = prefix | jnp.left_shift(jnp.int32(1), 31 - i)
        cand_s = cand ^ INT_MIN
        cnt = count(lambda blk, s_idx: blk >= cand_s)
        return jnp.where(cnt >= k_eff, cand, prefix)

    thr = lax.fori_loop(0, 32, bit_step, jnp.zeros((1, TQ), I32)) ^ INT_MIN

    n_gt = count(lambda blk, s_idx: blk > thr)
    n_eq = count(lambda blk, s_idx: blk == thr)
    need = k_eff - n_gt
    mlim_ref[...] = jnp.full((1, TQ), 2 ** 30, I32)

    @pl.when(jnp.max(n_eq - need) > 0)
    def _():
        def idx_step(i, prefix):
            cand = prefix | jnp.left_shift(jnp.int32(1), 14 - i)
            cnt = count(lambda blk, s_idx: (blk == thr) & (s_idx < cand))
            return jnp.where(cnt < need, cand, prefix)
        mlim_ref[...] = lax.fori_loop(0, 15, idx_step, jnp.zeros((1, TQ), I32))

    mlim = mlim_ref[...]

    def bias_tile(t, carry):
        s0 = pl.multiple_of(t * TK, TK)
        blk = key_ref[pl.ds(s0, TK), :]
        s_idx = s0 + lax.broadcasted_iota(I32, (TK, TQ), 0)
        sel = (blk > thr) | ((blk == thr) & (s_idx <= mlim))
        bias_ref[pl.ds(s0, TK), :] = jnp.where(sel, 0.0, NEG_BIG).astype(F32)
        return carry

    lax.fori_loop(0, nt, bias_tile, 0)

    for g in range(ATTN_KV_HEADS):
        for r in range(ATTN_REP):
            hq = g * ATTN_REP + r
            qs_ref[r * TQ:(r + 1) * TQ, :] = (
                q_ref[0, :, hq * HEAD_DIM:(hq + 1) * HEAD_DIM] * (HEAD_DIM ** -0.5))
        m_ref[...] = jnp.full(m_ref.shape, -jnp.inf, F32)
        l_ref[...] = jnp.zeros(l_ref.shape, F32)
        acc_ref[...] = jnp.zeros(acc_ref.shape, F32)

        def scores(t, slot):
            tc = jnp.minimum(t, nt - 1)
            s0 = pl.multiple_of(tc * TK, TK)
            s = lax.dot_general(kh_ref[0, g, pl.ds(s0, TK), :], qs_ref[...], NT_DIMS,
                                preferred_element_type=F32)
            bias = bias_ref[pl.ds(s0, TK), :] + jnp.where(t < nt, 0.0, NEG_BIG)
            s_ref[slot] = s + jnp.concatenate([bias] * ATTN_REP, axis=1)

        def accumulate(t, slot):
            tc = jnp.minimum(t, nt - 1)
            s0 = pl.multiple_of(tc * TK, TK)
            s = s_ref[slot]
            m_prev = m_ref[...]
            m_new = jnp.maximum(m_prev, jnp.max(s, axis=0, keepdims=True))
            p = jnp.exp(s - m_new)
            alpha = jnp.exp(m_prev - m_new)
            l_ref[...] = alpha * l_ref[...] + jnp.sum(p, axis=0, keepdims=True)
            acc_ref[...] = alpha * acc_ref[...] + jnp.dot(
                vt_ref[0, g, :, pl.ds(s0, TK)], p.astype(BF16), preferred_element_type=F32)
            m_ref[...] = m_new

        scores(0, 0)

        def attn_pair(u, carry):
            scores(2 * u + 1, 1)
            accumulate(2 * u, 0)
            scores(2 * u + 2, 0)
            accumulate(2 * u + 1, 1)
            return carry

        lax.fori_loop(0, (nt + 1) // 2, attn_pair, 0)
        o_t = acc_ref[...] * (1.0 / l_ref[...])
        for r in range(0, ATTN_REP, 2):
            hq = g * ATTN_REP + r
            pair = jnp.concatenate([o_t[:, r * TQ:(r + 1) * TQ],
                                    o_t[:, (r + 1) * TQ:(r + 2) * TQ]], axis=0)
            o_ref[0, :, hq * HEAD_DIM:(hq + 2) * HEAD_DIM] = pair.T.astype(BF16)


def _dsa(q, qi, dtw, kidx_all, k_heads, v_heads_t, past_len, l_valid):
    b, tq, _ = q.shape
    lp = kidx_all.shape[1]
    assert tq % Q_BLOCK == 0 and lp % KEY_TILE == 0 and lp < 2 ** 15
    topk = min(TOPK_MAX, l_valid // 4)
    qblk = lambda w: pl.BlockSpec((1, Q_BLOCK, w), lambda i, j: (i, j, 0))
    return pl.pallas_call(
        functools.partial(_dsa_kernel, past_len=past_len, l_valid=l_valid, topk=topk),
        out_shape=jax.ShapeDtypeStruct((b, tq, ATTN_WIDTH), BF16),
        grid=(b, tq // Q_BLOCK),
        in_specs=[qblk(ATTN_WIDTH), qblk(IDX_HEADS * IDX_DIM), qblk(LANE),
                  pl.BlockSpec((1, lp, IDX_DIM), lambda i, j: (i, 0, 0)),
                  pl.BlockSpec((1, ATTN_KV_HEADS, lp, HEAD_DIM), lambda i, j: (i, 0, 0, 0)),
                  pl.BlockSpec((1, ATTN_KV_HEADS, HEAD_DIM, lp), lambda i, j: (i, 0, 0, 0))],
        out_specs=qblk(ATTN_WIDTH),
        scratch_shapes=[pltpu.VMEM((ATTN_REP * Q_BLOCK, HEAD_DIM), BF16),
                        pltpu.VMEM((lp, Q_BLOCK), I32),
                        pltpu.VMEM((lp, Q_BLOCK), F32),
                        pltpu.VMEM((1, ATTN_REP * Q_BLOCK), F32),
                        pltpu.VMEM((1, ATTN_REP * Q_BLOCK), F32),
                        pltpu.VMEM((HEAD_DIM, ATTN_REP * Q_BLOCK), F32),
                        pltpu.VMEM((1, Q_BLOCK), I32),
                        pltpu.VMEM((2, KEY_TILE, ATTN_REP * Q_BLOCK), F32)],
        compiler_params=pltpu.CompilerParams(
            dimension_semantics=("arbitrary", "arbitrary"), vmem_limit_bytes=VMEM_LIMIT),
        name="dsa",
    )(q, qi, dtw, kidx_all, k_heads, v_heads_t)


def _merge_kernel(x_ref, yn_ref, o_ref, gate_ref, wa_ref, wb_ref, wo_ref, out_ref):
    br_a = jnp.dot(yn_ref[...], wa_ref[...], preferred_element_type=F32)
    br_b = jnp.dot(o_ref[...], wb_ref[...], preferred_element_type=F32)
    gates = gate_ref[...].astype(F32)
    merged = _sigmoid(gates[:, :D_MODEL]) * br_a + _sigmoid(gates[:, D_MODEL:]) * br_b
    out_ref[...] = x_ref[...] + jnp.dot(merged.astype(BF16), wo_ref[...],
                                        preferred_element_type=F32)


def _merge(x2d, yn, o, gate, w_ssm_out, w_attn_out, w_o, tm):
    n = x2d.shape[0]
    assert n % tm == 0
    row = lambda w: pl.BlockSpec((tm, w), lambda i: (i, 0))
    full = lambda r, c: pl.BlockSpec((r, c), lambda i: (0, 0))
    return pl.pallas_call(
        _merge_kernel,
        out_shape=jax.ShapeDtypeStruct((n, D_MODEL), F32),
        grid=(n // tm,),
        in_specs=[row(D_MODEL), row(SSM_INNER), row(ATTN_WIDTH), row(2 * D_MODEL),
                  full(SSM_INNER, D_MODEL), full(ATTN_WIDTH, D_MODEL), full(D_MODEL, D_MODEL)],
        out_specs=row(D_MODEL),
        compiler_params=pltpu.CompilerParams(
            dimension_semantics=("arbitrary",), vmem_limit_bytes=VMEM_LIMIT),
        name="merge",
    )(x2d, yn, o, gate, w_ssm_out.astype(BF16), w_attn_out.astype(BF16), w_o.astype(BF16))


def _moe_kernel(x_ref, ng_ref, wr_ref, br_ref, w1_ref, w3_ref, w2_ref, fg_ref, out_ref,
                h_ref, comb_ref, acc_ref, *, n_steps):
    e_step = pl.program_id(1)
    tm = x_ref.shape[0]

    @pl.when(e_step == 0)
    def _():
        x = x_ref[...]
        ms = jnp.mean(x * x, axis=-1, keepdims=True)
        h = ((x * lax.rsqrt(ms + EPS)) * ng_ref[...]).astype(BF16)
        h_ref[...] = h
        logits = jnp.dot(h, wr_ref[...], preferred_element_type=F32) + br_ref[...]
        lane_i = lax.broadcasted_iota(I32, (tm, LANE), 1)
        lane = lane_i.astype(F32)
        first = lambda hit: jnp.min(jnp.where(hit, lane, float(LANE)), axis=1, keepdims=True)
        is_g = (lane_i >= N_EXPERTS) & (lane_i < N_EXPERTS + MOE_GROUPS)
        glog = jnp.where(is_g, logits, -jnp.inf)
        gmax = jnp.max(glog, axis=1, keepdims=True)
        gsel = first(glog == gmax) - float(N_EXPERTS)
        p_group = 1.0 / jnp.sum(jnp.exp(glog - gmax), axis=1, keepdims=True)
        grp = (lane_i >> EXPERT_GROUP_SHIFT).astype(F32)
        in_grp = (lane_i < N_EXPERTS) & (grp == gsel)
        el = jnp.where(in_grp, logits, -jnp.inf)
        v1 = jnp.max(el, axis=1, keepdims=True)
        i1 = first(el == v1)
        el2 = jnp.where(lane == i1, -jnp.inf, el)
        v2 = jnp.max(el2, axis=1, keepdims=True)
        i2 = first(el2 == v2)
        e21 = jnp.exp(v2 - v1)
        den = 1.0 + e21
        comb_ref[...] = (jnp.where(lane == i1, (1.0 / den) * p_group, 0.0)
                         + jnp.where(lane == i2, (e21 / den) * p_group, 0.0))
        acc_ref[...] = jnp.zeros(acc_ref.shape, F32)

    h = h_ref[...]
    comb = comb_ref[...]
    lane = lax.broadcasted_iota(I32, (tm, LANE), 1)
    for k in range(EXPERTS_PER_STEP):
        e = e_step * EXPERTS_PER_STEP + k
        a1 = jnp.dot(h, w1_ref[k], preferred_element_type=F32)
        a3 = jnp.dot(h, w3_ref[k], preferred_element_type=F32)
        he = ((a1 * _sigmoid(a1)) * a3).astype(BF16)
        ye = jnp.dot(he, w2_ref[k], preferred_element_type=F32)
        wcol = jnp.sum(jnp.where(lane == e, comb, 0.0), axis=1, keepdims=True)
        acc_ref[...] += wcol * ye

    @pl.when(e_step == n_steps - 1)
    def _():
        y = x_ref[...] + acc_ref[...]
        ms = jnp.mean(y * y, axis=-1, keepdims=True)
        out_ref[...] = (y * lax.rsqrt(ms + EPS)) * fg_ref[...]


def _moe(x2d, norm_g, w_router_group, b_router_group, w_router_expert, b_router_expert,
         w1, w3, w2, norm_final_g, tm):
    n = x2d.shape[0]
    assert n % tm == 0
    n_steps = N_EXPERTS // EXPERTS_PER_STEP
    padw = LANE - N_EXPERTS - MOE_GROUPS
    wr = jnp.concatenate([w_router_expert, w_router_group,
                          jnp.zeros((D_MODEL, padw), F32)], axis=1).astype(BF16)
    br = jnp.concatenate([b_router_expert, b_router_group,
                          jnp.zeros((padw,), F32)]).astype(F32).reshape(1, LANE)
    row = pl.BlockSpec((tm, D_MODEL), lambda i, e: (i, 0))
    const = lambda r, c: pl.BlockSpec((r, c), lambda i, e: (0, 0))
    return pl.pallas_call(
        functools.partial(_moe_kernel, n_steps=n_steps),
        out_shape=jax.ShapeDtypeStruct((n, D_MODEL), F32),
        grid=(n // tm, n_steps),
        in_specs=[row, const(1, D_MODEL), const(D_MODEL, LANE), const(1, LANE),
                  pl.BlockSpec((EXPERTS_PER_STEP, D_MODEL, EXPERT_FF), lambda i, e: (e, 0, 0)),
                  pl.BlockSpec((EXPERTS_PER_STEP, D_MODEL, EXPERT_FF), lambda i, e: (e, 0, 0)),
                  pl.BlockSpec((EXPERTS_PER_STEP, EXPERT_FF, D_MODEL), lambda i, e: (e, 0, 0)),
                  const(1, D_MODEL)],
        out_specs=row,
        scratch_shapes=[pltpu.VMEM((tm, D_MODEL), BF16), pltpu.VMEM((tm, LANE), F32),
                        pltpu.VMEM((tm, D_MODEL), F32)],
        compiler_params=pltpu.CompilerParams(
            dimension_semantics=("arbitrary", "arbitrary"), vmem_limit_bytes=VMEM_LIMIT),
        name="moe",
    )(x2d, norm_g.reshape(1, D_MODEL), wr, br, w1.astype(BF16), w3.astype(BF16),
      w2.astype(BF16), norm_final_g.reshape(1, D_MODEL))


def _token_tile(n):
    for tm in (1024, 512, 256, 128, 64, 32, 16, 8):
        if n % tm == 0:
            return tm
    raise ValueError(f"token count {n} is not a multiple of 8")


def _pad_axis(a, axis, size):
    if a.shape[axis] == size:
        return a
    widths = [(0, 0)] * a.ndim
    widths[axis] = (0, size - a.shape[axis])
    return jnp.pad(a, widths)


def _round_up(n, m):
    return (n + m - 1) // m * m


def _stream_step(x, conv_state, ssm_state, k_past, v_past, kidx_past, w_packed, p):
    b, t, d = x.shape
    n = b * t
    tm = _token_tile(n)
    x2d = x.reshape(n, d)
    z, xbc, q, gate, k, v, qi, ki, dtw = _in_proj(x2d, p["norm_mix_g"], w_packed, tm)

    yn, conv_new, ssm_new = _ssd(
        xbc.reshape(b, t, CONV_DIM), z.reshape(b, t, SSM_INNER), dtw.reshape(b, t, LANE),
        conv_state, ssm_state, p["conv_w"], p["conv_b"], p["dt_bias"], p["A_log"],
        p["D_skip"], p["ssm_norm_g"])

    k4 = k.reshape(b, t, ATTN_KV_HEADS, HEAD_DIM)
    v4 = v.reshape(b, t, ATTN_KV_HEADS, HEAD_DIM)
    ki3 = ki.reshape(b, t, IDX_DIM)
    if k_past is None:
        past_len, k_all, v_all, kidx_all = 0, k4, v4, ki3
    else:
        past_len = k_past.shape[1]
        k_all = jnp.concatenate([k_past, k4], axis=1)
        v_all = jnp.concatenate([v_past, v4], axis=1)
        kidx_all = jnp.concatenate([kidx_past, ki3], axis=1)
    l_valid = past_len + t
    lp = _round_up(l_valid, KEY_TILE)
    tq = _round_up(t, Q_BLOCK)
    heads = lambda a: _pad_axis(a.astype(BF16).transpose(0, 2, 1, 3), 2, lp)
    heads_t = lambda a: _pad_axis(a.astype(BF16).transpose(0, 2, 3, 1), 3, lp)
    o = _dsa(_pad_axis(q.reshape(b, t, ATTN_WIDTH), 1, tq),
             _pad_axis(qi.reshape(b, t, IDX_HEADS * IDX_DIM), 1, tq),
             _pad_axis(dtw.reshape(b, t, LANE), 1, tq),
             _pad_axis(kidx_all.astype(BF16), 1, lp), heads(k_all), heads_t(v_all),
             past_len, l_valid)
    o = o[:, :t].reshape(n, ATTN_WIDTH)

    x1 = _merge(x2d, yn.reshape(n, SSM_INNER), o, gate, p["w_ssm_out"], p["w_attn_out"],
                p["w_o"], min(tm, MERGE_TILE))
    y = _moe(x1, p["norm_moe_g"], p["w_router_group"], p["b_router_group"],
             p["w_router_expert"], p["b_router_expert"], p["w1"], p["w3"], p["w2"],
             p["norm_final_g"], tm)
    return y.reshape(b, t, d), k4, v4, ki3, conv_new, ssm_new


def kernel(x_prompt, x_sample, cache_k, cache_v, cache_kidx, state_conv, state_ssm, norm_mix_g, w_in, conv_w, conv_b, dt_bias, A_log, D_skip, ssm_norm_g, w_ssm_out, w_attn_out, w_o, norm_moe_g, w_router_group, b_router_group, w_router_expert, b_router_expert, w1, w3, w2, norm_final_g):
    p = dict(norm_mix_g=norm_mix_g, conv_w=conv_w, conv_b=conv_b, dt_bias=dt_bias, A_log=A_log,
             D_skip=D_skip, ssm_norm_g=ssm_norm_g, w_ssm_out=w_ssm_out, w_attn_out=w_attn_out,
             w_o=w_o, norm_moe_g=norm_moe_g, w_router_group=w_router_group,
             b_router_group=b_router_group, w_router_expert=w_router_expert,
             b_router_expert=b_router_expert, w1=w1, w3=w3, w2=w2, norm_final_g=norm_final_g)
    w_packed = _pack_w_in(w_in)
    yp, kp, vp, kip, cp, sp = _stream_step(x_prompt, None, None, None, None, None, w_packed, p)
    ys, ksn, vsn, kisn, csn, ssn = _stream_step(x_sample, state_conv, state_ssm, cache_k,
                                                cache_v, cache_kidx, w_packed, p)
    return (yp, ys, kp, vp, kip, cp, sp, ksn, vsn, kisn, csn, ssn)
```

```python
import functools

import numpy as np
import jax
import jax.numpy as jnp
from jax import lax
from jax.experimental import pallas as pl
from jax.experimental.pallas import tpu as pltpu

F32 = jnp.float32
BF16 = jnp.bfloat16
I32 = jnp.int32

D_MODEL = 1024
CHUNK = 64
CHUNK_SHIFT = 6
EPS = 1e-6
SSM_INNER = 2048
SSM_HEADDIM = 64
SSM_HEADS = 32
SSM_GROUPS = 4
SSM_HEADS_PER_GROUP = SSM_HEADS // SSM_GROUPS
SSM_STATE = 128
CONV_WIDTH = 4
CONV_DIM = SSM_INNER + 2 * SSM_GROUPS * SSM_STATE
ATTN_HEADS = 16
ATTN_KV_HEADS = 2
HEAD_DIM = 64
ATTN_REP = ATTN_HEADS // ATTN_KV_HEADS
ATTN_WIDTH = ATTN_HEADS * HEAD_DIM
IDX_HEADS = 8
IDX_DIM = 64
IDX_SCALE = (IDX_HEADS * IDX_DIM) ** -0.5
TOPK_MAX = 256
MOE_GROUPS = 4
EXPERTS_PER_GROUP = 8
EXPERT_GROUP_SHIFT = 3
N_EXPERTS = 32
EXPERT_FF = 256
N_BRANCHES = 2

LANE = 128
VMEM_LIMIT = 56 * 1024 * 1024
PROJ_TILE = 1024
PROJ_TILES = 9
Q_BLOCK = 128
SMALL_Q_BLOCK = 16
KEY_TILE = 256
EXPERTS_PER_STEP = 4
MERGE_TILE = 512
INT_MIN = np.int32(-2 ** 31)
NEG_BIG = -1e30

NT_DIMS = (((1,), (1,)), ((), ()))
TN_DIMS = (((0,), (0,)), ((), ()))


def _sigmoid(x):
    return 1.0 / (1.0 + jnp.exp(-x))


def _split3(x):
    hi = x.astype(BF16)
    r1 = x - hi.astype(F32)
    mid = r1.astype(BF16)
    r2 = r1 - mid.astype(F32)
    lo = r2.astype(BF16)
    return jnp.concatenate([hi, mid, lo], axis=1)


def _inproj_kernel(x_ref, g_ref, w_ref, z_ref, xbc_ref, q_ref, gate_ref, k_ref, v_ref,
                   qi_ref, ki_ref, dtw_ref, h_ref):
    j = pl.program_id(1)

    @pl.when(j == 0)
    def _():
        x = x_ref[...]
        ms = jnp.mean(x * x, axis=-1, keepdims=True)
        h_ref[...] = ((x * lax.rsqrt(ms + EPS)) * g_ref[...]).astype(BF16)

    def tile():
        return jnp.dot(h_ref[...], w_ref[...], preferred_element_type=F32)

    @pl.when(j < 2)
    def _():
        z_ref[...] = tile().astype(BF16)

    @pl.when((j >= 2) & (j < 5))
    def _():
        xbc_ref[...] = tile().astype(BF16)

    @pl.when(j == 5)
    def _():
        q_ref[...] = tile().astype(BF16)

    @pl.when((j >= 6) & (j < 8))
    def _():
        gate_ref[...] = tile().astype(BF16)

    @pl.when(j == 8)
    def _():
        t = tile()
        k_ref[...] = t[:, 0:128]
        v_ref[...] = t[:, 128:256]
        qi_ref[...] = t[:, 256:768].astype(BF16)
        ki_ref[...] = t[:, 768:832]
        dtw_ref[...] = t[:, 896:1024]


def _pack_w_in(w_in):
    sizes = (SSM_INNER, CONV_DIM, SSM_HEADS, ATTN_WIDTH, ATTN_KV_HEADS * HEAD_DIM,
             ATTN_KV_HEADS * HEAD_DIM, IDX_HEADS * IDX_DIM, IDX_DIM, IDX_HEADS,
             N_BRANCHES * D_MODEL)
    offs = np.concatenate([[0], np.cumsum(sizes)])
    z, xbc, dt, q, k, v, qi, ki, wi, g = [w_in[:, offs[i]:offs[i + 1]] for i in range(10)]
    zeros = lambda n: jnp.zeros((D_MODEL, n), w_in.dtype)
    cols = [z, xbc, q, g, k, v, qi, ki, zeros(64), dt, wi, zeros(LANE - SSM_HEADS - IDX_HEADS)]
    return jnp.concatenate(cols, axis=1).astype(BF16)


def _in_proj(x2d, norm_g, w_packed, tm):
    n = x2d.shape[0]
    assert n % tm == 0
    bf = lambda c: jax.ShapeDtypeStruct((n, c), BF16)
    ff = lambda c: jax.ShapeDtypeStruct((n, c), F32)
    clip = lambda j, lo, hi: jnp.clip(j - lo, 0, hi - lo - 1)
    return pl.pallas_call(
        _inproj_kernel,
        out_shape=(bf(SSM_INNER), bf(CONV_DIM), bf(ATTN_WIDTH), bf(2 * D_MODEL),
                   ff(128), ff(128), bf(512), ff(IDX_DIM), ff(128)),
        grid=(n // tm, PROJ_TILES),
        in_specs=[
            pl.BlockSpec((tm, D_MODEL), lambda i, j: (i, 0)),
            pl.BlockSpec((1, D_MODEL), lambda i, j: (0, 0)),
            pl.BlockSpec((D_MODEL, PROJ_TILE), lambda i, j: (0, j)),
        ],
        out_specs=(
            pl.BlockSpec((tm, PROJ_TILE), lambda i, j: (i, clip(j, 0, 2))),
            pl.BlockSpec((tm, PROJ_TILE), lambda i, j: (i, clip(j, 2, 5))),
            pl.BlockSpec((tm, PROJ_TILE), lambda i, j: (i, 0)),
            pl.BlockSpec((tm, PROJ_TILE), lambda i, j: (i, clip(j, 6, 8))),
            pl.BlockSpec((tm, 128), lambda i, j: (i, 0)),
            pl.BlockSpec((tm, 128), lambda i, j: (i, 0)),
            pl.BlockSpec((tm, 512), lambda i, j: (i, 0)),
            pl.BlockSpec((tm, IDX_DIM), lambda i, j: (i, 0)),
            pl.BlockSpec((tm, 128), lambda i, j: (i, 0)),
        ),
        scratch_shapes=[pltpu.VMEM((tm, D_MODEL), BF16)],
        compiler_params=pltpu.CompilerParams(
            dimension_semantics=("arbitrary", "arbitrary"), vmem_limit_bytes=VMEM_LIMIT),
        name="in_proj",
    )(x2d, norm_g.reshape(1, D_MODEL), w_packed)


def _ssd_kernel(*refs, L, nc, has_state):
    if has_state:
        (xbc_ref, z_ref, dtw_ref, cs_ref, s0_ref, convw_ref, convb_ref, dtb_ref, a_ref, dx_ref,
         ng_ref, e3p_ref, e3l_ref, tri_ref, bd_ref, yn_ref, cnew_ref, sout_ref, xpad, st) = refs
    else:
        (xbc_ref, z_ref, dtw_ref, convw_ref, convb_ref, dtb_ref, a_ref, dx_ref,
         ng_ref, e3p_ref, e3l_ref, tri_ref, bd_ref, yn_ref, cnew_ref, sout_ref, xpad, st) = refs
    c = pl.program_id(1)
    G, R, N, P = SSM_GROUPS, SSM_HEADS_PER_GROUP, SSM_STATE, SSM_HEADDIM
    GW = R * P
    HL = SSM_HEADS * L

    @pl.when(c == 0)
    def _():
        xpad[0:8, :] = jnp.zeros((8, CONV_DIM), F32)
        if has_state:
            xpad[5:8, :] = cs_ref[0]
            st[...] = s0_ref[0].T
        else:
            st[...] = jnp.zeros(st.shape, F32)

    xpad[8:8 + L, :] = xbc_ref[0].astype(F32)
    w = convw_ref[...]
    conv = convb_ref[...] + xpad[5:5 + L, :] * w[0:1, :]
    conv = conv + xpad[6:6 + L, :] * w[1:2, :]
    conv = conv + xpad[7:7 + L, :] * w[2:3, :]
    conv = conv + xpad[8:8 + L, :] * w[3:4, :]
    act = conv * _sigmoid(conv)
    xs = act[:, :SSM_INNER]
    bmat = act[:, SSM_INNER:SSM_INNER + G * N].astype(BF16)
    cmat = act[:, SSM_INNER + G * N:].astype(BF16)

    @pl.when(c == nc - 1)
    def _():
        cnew_ref[0] = xpad[8 + L - 3:8 + L, :]

    xpad[5:8, :] = xpad[L + 5:L + 8, :]

    u = dtw_ref[0] + dtb_ref[...]
    dt = jnp.maximum(u, 0.0) + jnp.log1p(jnp.exp(-jnp.abs(u)))
    a = dt * a_ref[...]
    ac3 = jnp.dot(tri_ref[...], _split3(a), preferred_element_type=F32)
    a_cum = ac3[:, 0:LANE] + ac3[:, LANE:2 * LANE] + ac3[:, 2 * LANE:]

    ex = jnp.dot(_split3(jnp.concatenate([a_cum, dt], axis=0)), e3p_ref[...],
                 preferred_element_type=F32)
    acx = ex[:L]
    dtx = ex[L:]
    if L == P:
        acl = acx
    else:
        acl = jnp.dot(_split3(a_cum), e3l_ref[...], preferred_element_type=F32)

    row = lax.broadcasted_iota(I32, (L, HL), 0)
    scol = lax.broadcasted_iota(I32, (L, HL), 1) & (L - 1)
    acs = jnp.sum(jnp.where(scol == row, acl, 0.0), axis=0, keepdims=True)
    decay = jnp.where(scol <= row, jnp.exp(acl - acs), 0.0)

    cb = []
    for g in range(G):
        bt = jnp.concatenate([bmat[:, g * N:(g + 1) * N]] * R, axis=0)
        cb.append(lax.dot_general(cmat[:, g * N:(g + 1) * N], bt, NT_DIMS,
                                  preferred_element_type=F32))
    m = (jnp.concatenate(cb, axis=1) * decay).astype(BF16)

    xdt = xs * dtx
    xdt_b = xdt.astype(BF16)
    eacx = jnp.exp(acx)
    alast = acx[L - 1:L, :]
    xw = (xdt * jnp.exp(alast - acx)).astype(BF16)
    ealast = jnp.exp(alast)

    ys = []
    for g in range(G):
        sl = slice(g * GW, (g + 1) * GW)
        xbd = jnp.concatenate([xdt_b[:, sl]] * R, axis=0) * bd_ref[...]
        y_in = jnp.dot(m[:, g * R * L:(g + 1) * R * L], xbd, preferred_element_type=F32)
        st_g = st[:, sl]
        y_st = jnp.dot(cmat[:, g * N:(g + 1) * N], st_g.astype(BF16),
                       preferred_element_type=F32)
        ys.append(y_in + eacx[:, sl] * y_st)
        upd = lax.dot_general(bmat[:, g * N:(g + 1) * N], xw[:, sl], TN_DIMS,
                              preferred_element_type=F32)
        st[:, sl] = st_g * ealast[:, sl] + upd
    y = jnp.concatenate(ys, axis=1) + dx_ref[...] * xs

    zf = z_ref[0].astype(F32)
    yz = y * (zf * _sigmoid(zf))
    ms = jnp.mean(yz * yz, axis=-1, keepdims=True)
    yn_ref[0] = ((yz * lax.rsqrt(ms + EPS)) * ng_ref[...]).astype(BF16)

    @pl.when(c == nc - 1)
    def _():
        sout_ref[0] = st[...].T


def _expand_matrix(width_per_head):
    r = np.arange(3 * LANE)[:, None] % LANE
    c = np.arange(SSM_HEADS * width_per_head)[None, :] // width_per_head
    return jnp.asarray((r == c).astype(np.float32), BF16)


def _ssd(xbc, z, dtw, conv_state, ssm_state, conv_w, conv_b, dt_bias, a_log, d_skip, ssm_norm_g):
    b, t, _ = xbc.shape
    L = min(CHUNK, t)
    nc = t // L
    assert t % L == 0 and L & (L - 1) == 0
    has_state = conv_state is not None
    pad = lambda v: jnp.pad(v.astype(F32), (0, LANE - SSM_HEADS)).reshape(1, LANE)
    a_neg = pad(-jnp.exp(a_log.astype(F32)))
    tri = jnp.asarray(np.tril(np.ones((L, L), np.float32)), BF16)
    rl = SSM_HEADS_PER_GROUP * L
    gw = SSM_HEADS_PER_GROUP * SSM_HEADDIM
    bd = jnp.asarray((np.arange(rl)[:, None] // L == np.arange(gw)[None, :] // SSM_HEADDIM)
                     .astype(np.float32), BF16)
    const = lambda shape: pl.BlockSpec(shape, lambda i, j: (0,) * len(shape))
    tok = lambda w: pl.BlockSpec((1, L, w), lambda i, j: (i, j, 0))
    per_b = lambda s: pl.BlockSpec((1,) + s, lambda i, j: (i, 0, 0))
    ins = [xbc, z, dtw]
    specs = [tok(CONV_DIM), tok(SSM_INNER), tok(LANE)]
    if has_state:
        ins += [conv_state, ssm_state.reshape(b, SSM_INNER, SSM_STATE)]
        specs += [per_b((CONV_WIDTH - 1, CONV_DIM)), per_b((SSM_INNER, SSM_STATE))]
    ins += [conv_w, conv_b.reshape(1, CONV_DIM), pad(dt_bias), a_neg,
            jnp.repeat(d_skip.astype(F32), SSM_HEADDIM).reshape(1, SSM_INNER),
            ssm_norm_g.reshape(1, SSM_INNER), _expand_matrix(SSM_HEADDIM), _expand_matrix(L),
            tri, bd]
    specs += [const((CONV_WIDTH, CONV_DIM)), const((1, CONV_DIM)), const((1, LANE)),
              const((1, LANE)), const((1, SSM_INNER)), const((1, SSM_INNER)),
              const((3 * LANE, SSM_INNER)), const((3 * LANE, SSM_HEADS * L)),
              const((L, L)), const((rl, gw))]
    yn, cnew, sout = pl.pallas_call(
        functools.partial(_ssd_kernel, L=L, nc=nc, has_state=has_state),
        out_shape=(jax.ShapeDtypeStruct((b, t, SSM_INNER), BF16),
                   jax.ShapeDtypeStruct((b, CONV_WIDTH - 1, CONV_DIM), F32),
                   jax.ShapeDtypeStruct((b, SSM_INNER, SSM_STATE), F32)),
        grid=(b, nc),
        in_specs=specs,
        out_specs=(tok(SSM_INNER), per_b((CONV_WIDTH - 1, CONV_DIM)),
                   per_b((SSM_INNER, SSM_STATE))),
        scratch_shapes=[pltpu.VMEM((L + 8, CONV_DIM), F32),
                        pltpu.VMEM((SSM_STATE, SSM_INNER), F32)],
        compiler_params=pltpu.CompilerParams(
            dimension_semantics=("arbitrary", "arbitrary"), vmem_limit_bytes=VMEM_LIMIT),
        name="ssd",
    )(*ins)
    return yn, cnew, sout.reshape(b, SSM_HEADS, SSM_HEADDIM, SSM_STATE)


def _dsa_kernel(q_ref, qi_ref, w_ref, kidx_ref, kh_ref, vt_ref, o_ref,
                qs_ref, key_ref, half_ref, bias_ref, m_ref, l_ref, acc_ref, mlim_ref, s_ref,
                *, past_len, l_valid, topk, nq):
    qb = pl.program_id(1)
    TQ, TK, T16 = LANE, KEY_TILE, 2 * KEY_TILE
    W = ATTN_REP * nq
    wide = nq == LANE

    lane = lax.broadcasted_iota(I32, (1, TQ), 1)
    pos = past_len + qb * nq + (lane & (nq - 1))
    n_adm = jnp.minimum(((pos >> CHUNK_SHIFT) + 1) * CHUNK, l_valid)
    k_eff = jnp.minimum(n_adm, topk)
    last_pos = past_len + qb * nq + nq - 1
    n_max = jnp.minimum(((last_pos >> CHUNK_SHIFT) + 1) * CHUNK, l_valid)
    nt = (n_max + TK - 1) // TK
    nt16 = (nt + 1) // 2

    if wide:
        for h in range(IDX_HEADS):
            qs_ref[h * nq:(h + 1) * nq, :] = qi_ref[0, :, h * IDX_DIM:(h + 1) * IDX_DIM]
        w_t = w_ref[0].T[SSM_HEADS:SSM_HEADS + IDX_HEADS, :] * IDX_SCALE
    else:
        qs_ref[...] = qi_ref[0]
        w_row = w_ref[0] * IDX_SCALE

    def score_tile(t, carry):
        s0 = pl.multiple_of(t * TK, TK)
        d = lax.dot_general(kidx_ref[0, pl.ds(s0, TK), :], qs_ref[0:IDX_HEADS * nq, :],
                            NT_DIMS, preferred_element_type=F32)
        if wide:
            sc = w_t[0:1, :] * jnp.maximum(d[:, 0:nq], 0.0)
            for h in range(1, IDX_HEADS):
                sc = sc + w_t[h:h + 1, :] * jnp.maximum(d[:, h * nq:(h + 1) * nq], 0.0)
        else:
            sc = w_row * jnp.maximum(d, 0.0)
            for shift in (64, 32, 16):
                sc = sc + pltpu.roll(sc, shift, axis=1)
        sc = sc + 0.0
        bits = pltpu.bitcast(sc, I32)
        key = jnp.where(bits < 0, bits ^ np.int32(0x7FFFFFFF), bits)
        s_idx = s0 + lax.broadcasted_iota(I32, (TK, TQ), 0)
        key = jnp.where(s_idx < n_adm, key, INT_MIN)
        key_ref[pl.ds(s0, TK), :] = key
        half_ref[0, pl.ds(s0, TK), :] = (key >> 16).astype(jnp.int16)
        half_ref[1, pl.ds(s0, TK), :] = ((key & 0xFFFF) - 32768).astype(jnp.int16)
        return carry

    lax.fori_loop(0, nt, score_tile, 0)

    @pl.when(nt < 2 * nt16)
    def _():
        s0 = pl.multiple_of(nt * TK, TK)
        for part in range(2):
            half_ref[part, pl.ds(s0, TK), :] = jnp.full((TK, TQ), -32768, jnp.int16)

    def count(pred):
        def body(t, acc):
            s0 = pl.multiple_of(t * TK, TK)
            blk = key_ref[pl.ds(s0, TK), :]
            s_idx = s0 + lax.broadcasted_iota(I32, (TK, TQ), 0)
            hit = jnp.where(pred(blk, s_idx), 1, 0).astype(I32)
            return acc + jnp.sum(hit.reshape(TK // 8, 8, TQ), axis=0)
        acc = lax.fori_loop(0, nt, body, jnp.zeros((8, TQ), I32))
        return jnp.sum(acc, axis=0, keepdims=True)

    ones16 = jnp.ones((T16, TQ), BF16)
    zeros16 = jnp.zeros((T16, TQ), BF16)

    def count16(hits):
        def body(t, acc):
            hit = jnp.where(hits(t), ones16, zeros16).reshape(T16 // 16, 16, TQ)
            parts = [hit[r] for r in range(T16 // 16)]
            while len(parts) > 1:
                parts = [parts[i] + parts[i + 1] for i in range(0, len(parts), 2)]
            return acc + parts[0].astype(F32)
        acc = lax.fori_loop(0, nt16, body, jnp.zeros((16, TQ), F32))
        return jnp.sum(acc, axis=0, keepdims=True)

    def tile16(part, t):
        return half_ref[part, pl.ds(pl.multiple_of(t * T16, T16), T16), :]

    def search16(part, k_f):
        def bit_step(i, prefix):
            cand = prefix | jnp.left_shift(jnp.int32(1), 15 - i)
            cand_s = (cand - 32768).astype(jnp.int16)
            cnt = count16(lambda t: tile16(part, t) >= cand_s)
            return jnp.where(cnt >= k_f, cand, prefix)
        return lax.fori_loop(0, 16, bit_step, jnp.zeros((1, TQ), I32)) - 32768

    k_f = k_eff.astype(F32)
    p_hi = search16(0, k_f)
    p_hi_s = p_hi.astype(jnp.int16)
    n_above = count16(lambda t: tile16(0, t) > p_hi_s)

    def bucket_tile(t, carry):
        rows = pl.ds(pl.multiple_of(t * T16, T16), T16)
        half_ref[1, rows, :] = jnp.where(half_ref[0, rows, :] == p_hi_s, half_ref[1, rows, :],
                                         jnp.int16(-32768))
        return carry

    lax.fori_loop(0, nt16, bucket_tile, 0)
    p_lo = search16(1, k_f - n_above)
    thr = (p_hi << 16) | ((p_lo + 32768) & 0xFFFF)

    n_gt = count(lambda blk, s_idx: blk > thr)
    n_eq = count(lambda blk, s_idx: blk == thr)
    need = k_eff - n_gt
    mlim_ref[...] = jnp.full((1, TQ), 2 ** 30, I32)

    @pl.when(jnp.max(n_eq - need) > 0)
    def _():
        def idx_step(i, prefix):
            cand = prefix | jnp.left_shift(jnp.int32(1), 14 - i)
            cnt = count(lambda blk, s_idx: (blk == thr) & (s_idx < cand))
            return jnp.where(cnt < need, cand, prefix)
        mlim_ref[...] = lax.fori_loop(0, 15, idx_step, jnp.zeros((1, TQ), I32))

    mlim = mlim_ref[...]

    def bias_tile(t, carry):
        s0 = pl.multiple_of(t * TK, TK)
        blk = key_ref[pl.ds(s0, TK), :]
        s_idx = s0 + lax.broadcasted_iota(I32, (TK, TQ), 0)
        sel = (blk > thr) | ((blk == thr) & (s_idx <= mlim))
        bias_ref[pl.ds(s0, TK), :] = jnp.where(sel, 0.0, NEG_BIG).astype(F32)
        return carry

    lax.fori_loop(0, nt, bias_tile, 0)

    for g in range(ATTN_KV_HEADS):
        if wide:
            for r in range(ATTN_REP):
                hq = g * ATTN_REP + r
                qs_ref[r * nq:(r + 1) * nq, :] = (
                    q_ref[0, :, hq * HEAD_DIM:(hq + 1) * HEAD_DIM] * (HEAD_DIM ** -0.5))
        else:
            qs_ref[...] = q_ref[0, g] * (HEAD_DIM ** -0.5)
        m_ref[...] = jnp.full(m_ref.shape, -jnp.inf, F32)
        l_ref[...] = jnp.zeros(l_ref.shape, F32)
        acc_ref[...] = jnp.zeros(acc_ref.shape, F32)

        def scores(t, slot):
            tc = jnp.minimum(t, nt - 1)
            s0 = pl.multiple_of(tc * TK, TK)
            s = lax.dot_general(kh_ref[0, g, pl.ds(s0, TK), :], qs_ref[0:W, :], NT_DIMS,
                                preferred_element_type=F32)
            bias = bias_ref[pl.ds(s0, TK), :] + jnp.where(t < nt, 0.0, NEG_BIG)
            s_ref[slot] = s + jnp.concatenate([bias] * (W // TQ), axis=1)

        def accumulate(t, slot):
            tc = jnp.minimum(t, nt - 1)
            s0 = pl.multiple_of(tc * TK, TK)
            s = s_ref[slot]
            m_prev = m_ref[...]
            m_new = jnp.maximum(m_prev, jnp.max(s, axis=0, keepdims=True))
            p = jnp.exp(s - m_new)
            alpha = jnp.exp(m_prev - m_new)
            l_ref[...] = alpha * l_ref[...] + jnp.sum(p, axis=0, keepdims=True)
            acc_ref[...] = alpha * acc_ref[...] + jnp.dot(
                vt_ref[0, g, :, pl.ds(s0, TK)], p.astype(BF16), preferred_element_type=F32)
            m_ref[...] = m_new

        scores(0, 0)

        def attn_pair(u, carry):
            scores(2 * u + 1, 1)
            accumulate(2 * u, 0)
            scores(2 * u + 2, 0)
            accumulate(2 * u + 1, 1)
            return carry

        lax.fori_loop(0, (nt + 1) // 2, attn_pair, 0)
        o_t = acc_ref[...] * (1.0 / l_ref[...])
        if wide:
            for r in range(0, ATTN_REP, 2):
                hq = g * ATTN_REP + r
                pair = jnp.concatenate([o_t[:, r * nq:(r + 1) * nq],
                                        o_t[:, (r + 1) * nq:(r + 2) * nq]], axis=0)
                o_ref[0, :, hq * HEAD_DIM:(hq + 2) * HEAD_DIM] = pair.T.astype(BF16)
        else:
            o_rq = jnp.concatenate([o_t, o_t], axis=0).T[:, 0:HEAD_DIM].astype(BF16)
            for r in range(ATTN_REP):
                hq = g * ATTN_REP + r
                o_ref[0, :, hq * HEAD_DIM:(hq + 1) * HEAD_DIM] = o_rq[r * nq:(r + 1) * nq, :]


def _dsa(q, qi, w, kidx_all, k_heads, v_heads_t, past_len, l_valid, t):
    b = kidx_all.shape[0]
    lp = kidx_all.shape[1]
    nq = Q_BLOCK if t % Q_BLOCK == 0 else SMALL_Q_BLOCK
    assert t % nq == 0 and lp % (2 * KEY_TILE) == 0 and lp < 2 ** 15
    topk = min(TOPK_MAX, l_valid // 4)
    width = ATTN_REP * nq
    if nq == Q_BLOCK:
        qblk = lambda c: pl.BlockSpec((1, nq, c), lambda i, j: (i, j, 0))
        q_specs = [qblk(ATTN_WIDTH), qblk(IDX_HEADS * IDX_DIM), qblk(LANE)]
    else:
        assert t == nq
        q_specs = [pl.BlockSpec((1, ATTN_KV_HEADS, width, HEAD_DIM), lambda i, j: (i, 0, 0, 0)),
                   pl.BlockSpec((1, IDX_HEADS * nq, IDX_DIM), lambda i, j: (i, 0, 0)),
                   pl.BlockSpec((1, 1, LANE), lambda i, j: (i, 0, 0))]
    return pl.pallas_call(
        functools.partial(_dsa_kernel, past_len=past_len, l_valid=l_valid, topk=topk, nq=nq),
        out_shape=jax.ShapeDtypeStruct((b, t, ATTN_WIDTH), BF16),
        grid=(b, t // nq),
        in_specs=q_specs + [
            pl.BlockSpec((1, lp, IDX_DIM), lambda i, j: (i, 0, 0)),
            pl.BlockSpec((1, ATTN_KV_HEADS, lp, HEAD_DIM), lambda i, j: (i, 0, 0, 0)),
            pl.BlockSpec((1, ATTN_KV_HEADS, HEAD_DIM, lp), lambda i, j: (i, 0, 0, 0))],
        out_specs=pl.BlockSpec((1, nq, ATTN_WIDTH), lambda i, j: (i, j, 0)),
        scratch_shapes=[pltpu.VMEM((width, HEAD_DIM), BF16),
                        pltpu.VMEM((lp, LANE), I32),
                        pltpu.VMEM((2, lp, LANE), jnp.int16),
                        pltpu.VMEM((lp, LANE), F32),
                        pltpu.VMEM((1, width), F32),
                        pltpu.VMEM((1, width), F32),
                        pltpu.VMEM((HEAD_DIM, width), F32),
                        pltpu.VMEM((1, LANE), I32),
                        pltpu.VMEM((2, KEY_TILE, width), F32)],
        compiler_params=pltpu.CompilerParams(
            dimension_semantics=("arbitrary", "arbitrary"), vmem_limit_bytes=VMEM_LIMIT),
        name="dsa",
    )(q, qi, w, kidx_all, k_heads, v_heads_t)


def _merge_kernel(x_ref, yn_ref, o_ref, gate_ref, wa_ref, wb_ref, wo_ref, out_ref):
    br_a = jnp.dot(yn_ref[...], wa_ref[...], preferred_element_type=F32)
    br_b = jnp.dot(o_ref[...], wb_ref[...], preferred_element_type=F32)
    gates = gate_ref[...].astype(F32)
    merged = _sigmoid(gates[:, :D_MODEL]) * br_a + _sigmoid(gates[:, D_MODEL:]) * br_b
    out_ref[...] = x_ref[...] + jnp.dot(merged.astype(BF16), wo_ref[...],
                                        preferred_element_type=F32)


def _merge(x2d, yn, o, gate, w_ssm_out, w_attn_out, w_o, tm):
    n = x2d.shape[0]
    assert n % tm == 0
    row = lambda w: pl.BlockSpec((tm, w), lambda i: (i, 0))
    full = lambda r, c: pl.BlockSpec((r, c), lambda i: (0, 0))
    return pl.pallas_call(
        _merge_kernel,
        out_shape=jax.ShapeDtypeStruct((n, D_MODEL), F32),
        grid=(n // tm,),
        in_specs=[row(D_MODEL), row(SSM_INNER), row(ATTN_WIDTH), row(2 * D_MODEL),
                  full(SSM_INNER, D_MODEL), full(ATTN_WIDTH, D_MODEL), full(D_MODEL, D_MODEL)],
        out_specs=row(D_MODEL),
        compiler_params=pltpu.CompilerParams(
            dimension_semantics=("arbitrary",), vmem_limit_bytes=VMEM_LIMIT),
        name="merge",
    )(x2d, yn, o, gate, w_ssm_out.astype(BF16), w_attn_out.astype(BF16), w_o.astype(BF16))


def _moe_kernel(x_ref, ng_ref, wr_ref, br_ref, w1_ref, w3_ref, w2_ref, fg_ref, out_ref,
                h_ref, comb_ref, acc_ref, *, n_steps):
    e_step = pl.program_id(1)
    tm = x_ref.shape[0]

    @pl.when(e_step == 0)
    def _():
        x = x_ref[...]
        ms = jnp.mean(x * x, axis=-1, keepdims=True)
        h = ((x * lax.rsqrt(ms + EPS)) * ng_ref[...]).astype(BF16)
        h_ref[...] = h
        logits = jnp.dot(h, wr_ref[...], preferred_element_type=F32) + br_ref[...]
        lane_i = lax.broadcasted_iota(I32, (tm, LANE), 1)
        lane = lane_i.astype(F32)
        first = lambda hit: jnp.min(jnp.where(hit, lane, float(LANE)), axis=1, keepdims=True)
        is_g = (lane_i >= N_EXPERTS) & (lane_i < N_EXPERTS + MOE_GROUPS)
        glog = jnp.where(is_g, logits, -jnp.inf)
        gmax = jnp.max(glog, axis=1, keepdims=True)
        gsel = first(glog == gmax) - float(N_EXPERTS)
        p_group = 1.0 / jnp.sum(jnp.exp(glog - gmax), axis=1, keepdims=True)
        grp = (lane_i >> EXPERT_GROUP_SHIFT).astype(F32)
        in_grp = (lane_i < N_EXPERTS) & (grp == gsel)
        el = jnp.where(in_grp, logits, -jnp.inf)
        v1 = jnp.max(el, axis=1, keepdims=True)
        i1 = first(el == v1)
        el2 = jnp.where(lane == i1, -jnp.inf, el)
        v2 = jnp.max(el2, axis=1, keepdims=True)
        i2 = first(el2 == v2)
        e21 = jnp.exp(v2 - v1)
        den = 1.0 + e21
        comb_ref[...] = (jnp.where(lane == i1, (1.0 / den) * p_group, 0.0)
                         + jnp.where(lane == i2, (e21 / den) * p_group, 0.0))
        acc_ref[...] = jnp.zeros(acc_ref.shape, F32)

    h = h_ref[...]
    comb = comb_ref[...]
    lane = lax.broadcasted_iota(I32, (tm, LANE), 1)
    for k in range(EXPERTS_PER_STEP):
        e = e_step * EXPERTS_PER_STEP + k
        a1 = jnp.dot(h, w1_ref[k], preferred_element_type=F32)
        a3 = jnp.dot(h, w3_ref[k], preferred_element_type=F32)
        he = ((a1 * _sigmoid(a1)) * a3).astype(BF16)
        ye = jnp.dot(he, w2_ref[k], preferred_element_type=F32)
        wcol = jnp.sum(jnp.where(lane == e, comb, 0.0), axis=1, keepdims=True)
        acc_ref[...] += wcol * ye

    @pl.when(e_step == n_steps - 1)
    def _():
        y = x_ref[...] + acc_ref[...]
        ms = jnp.mean(y * y, axis=-1, keepdims=True)
        out_ref[...] = (y * lax.rsqrt(ms + EPS)) * fg_ref[...]


def _moe(x2d, norm_g, w_router_group, b_router_group, w_router_expert, b_router_expert,
         w1, w3, w2, norm_final_g, tm):
    n = x2d.shape[0]
    assert n % tm == 0
    n_steps = N_EXPERTS // EXPERTS_PER_STEP
    padw = LANE - N_EXPERTS - MOE_GROUPS
    wr = jnp.concatenate([w_router_expert, w_router_group,
                          jnp.zeros((D_MODEL, padw), F32)], axis=1).astype(BF16)
    br = jnp.concatenate([b_router_expert, b_router_group,
                          jnp.zeros((padw,), F32)]).astype(F32).reshape(1, LANE)
    row = pl.BlockSpec((tm, D_MODEL), lambda i, e: (i, 0))
    const = lambda r, c: pl.BlockSpec((r, c), lambda i, e: (0, 0))
    return pl.pallas_call(
        functools.partial(_moe_kernel, n_steps=n_steps),
        out_shape=jax.ShapeDtypeStruct((n, D_MODEL), F32),
        grid=(n // tm, n_steps),
        in_specs=[row, const(1, D_MODEL), const(D_MODEL, LANE), const(1, LANE),
                  pl.BlockSpec((EXPERTS_PER_STEP, D_MODEL, EXPERT_FF), lambda i, e: (e, 0, 0)),
                  pl.BlockSpec((EXPERTS_PER_STEP, D_MODEL, EXPERT_FF), lambda i, e: (e, 0, 0)),
                  pl.BlockSpec((EXPERTS_PER_STEP, EXPERT_FF, D_MODEL), lambda i, e: (e, 0, 0)),
                  const(1, D_MODEL)],
        out_specs=row,
        scratch_shapes=[pltpu.VMEM((tm, D_MODEL), BF16), pltpu.VMEM((tm, LANE), F32),
                        pltpu.VMEM((tm, D_MODEL), F32)],
        compiler_params=pltpu.CompilerParams(
            dimension_semantics=("arbitrary", "arbitrary"), vmem_limit_bytes=VMEM_LIMIT),
        name="moe",
    )(x2d, norm_g.reshape(1, D_MODEL), wr, br, w1.astype(BF16), w3.astype(BF16),
      w2.astype(BF16), norm_final_g.reshape(1, D_MODEL))


def _token_tile(n):
    for tm in (1024, 512, 256, 128, 64, 32, 16, 8):
        if n % tm == 0:
            return tm
    raise ValueError(f"token count {n} is not a multiple of 8")


def _pad_axis(a, axis, size):
    if a.shape[axis] == size:
        return a
    widths = [(0, 0)] * a.ndim
    widths[axis] = (0, size - a.shape[axis])
    return jnp.pad(a, widths)


def _round_up(n, m):
    return (n + m - 1) // m * m


def _stream_step(x, conv_state, ssm_state, k_past, v_past, kidx_past, w_packed, p):
    b, t, d = x.shape
    n = b * t
    tm = _token_tile(n)
    x2d = x.reshape(n, d)
    z, xbc, q, gate, k, v, qi, ki, dtw = _in_proj(x2d, p["norm_mix_g"], w_packed, tm)

    yn, conv_new, ssm_new = _ssd(
        xbc.reshape(b, t, CONV_DIM), z.reshape(b, t, SSM_INNER), dtw.reshape(b, t, LANE),
        conv_state, ssm_state, p["conv_w"], p["conv_b"], p["dt_bias"], p["A_log"],
        p["D_skip"], p["ssm_norm_g"])

    k4 = k.reshape(b, t, ATTN_KV_HEADS, HEAD_DIM)
    v4 = v.reshape(b, t, ATTN_KV_HEADS, HEAD_DIM)
    ki3 = ki.reshape(b, t, IDX_DIM)
    if k_past is None:
        past_len, k_all, v_all, kidx_all = 0, k4, v4, ki3
    else:
        past_len = k_past.shape[1]
        k_all = jnp.concatenate([k_past, k4], axis=1)
        v_all = jnp.concatenate([v_past, v4], axis=1)
        kidx_all = jnp.concatenate([kidx_past, ki3], axis=1)
    l_valid = past_len + t
    lp = _round_up(l_valid, 2 * KEY_TILE)
    heads = lambda a: _pad_axis(a.astype(BF16).transpose(0, 2, 1, 3), 2, lp)
    heads_t = lambda a: _pad_axis(a.astype(BF16).transpose(0, 2, 3, 1), 3, lp)
    if t % Q_BLOCK == 0:
        q_in = (q.reshape(b, t, ATTN_WIDTH), qi.reshape(b, t, IDX_HEADS * IDX_DIM),
                dtw.reshape(b, t, LANE))
    else:
        q_in = (q.reshape(b, t, ATTN_KV_HEADS, ATTN_REP, HEAD_DIM).transpose(0, 2, 3, 1, 4)
                .reshape(b, ATTN_KV_HEADS, ATTN_REP * t, HEAD_DIM),
                qi.reshape(b, t, IDX_HEADS, IDX_DIM).transpose(0, 2, 1, 3)
                .reshape(b, IDX_HEADS * t, IDX_DIM),
                dtw.reshape(b, t, LANE)[:, :, SSM_HEADS:SSM_HEADS + IDX_HEADS]
                .transpose(0, 2, 1).reshape(b, 1, IDX_HEADS * t))
    o = _dsa(*q_in, _pad_axis(kidx_all.astype(BF16), 1, lp), heads(k_all), heads_t(v_all),
             past_len, l_valid, t)
    o = o.reshape(n, ATTN_WIDTH)

    x1 = _merge(x2d, yn.reshape(n, SSM_INNER), o, gate, p["w_ssm_out"], p["w_attn_out"],
                p["w_o"], min(tm, MERGE_TILE))
    y = _moe(x1, p["norm_moe_g"], p["w_router_group"], p["b_router_group"],
             p["w_router_expert"], p["b_router_expert"], p["w1"], p["w3"], p["w2"],
             p["norm_final_g"], tm)
    return y.reshape(b, t, d), k4, v4, ki3, conv_new, ssm_new


def kernel(x_prompt, x_sample, cache_k, cache_v, cache_kidx, state_conv, state_ssm, norm_mix_g, w_in, conv_w, conv_b, dt_bias, A_log, D_skip, ssm_norm_g, w_ssm_out, w_attn_out, w_o, norm_moe_g, w_router_group, b_router_group, w_router_expert, b_router_expert, w1, w3, w2, norm_final_g):
    p = dict(norm_mix_g=norm_mix_g, conv_w=conv_w, conv_b=conv_b, dt_bias=dt_bias, A_log=A_log,
             D_skip=D_skip, ssm_norm_g=ssm_norm_g, w_ssm_out=w_ssm_out, w_attn_out=w_attn_out,
             w_o=w_o, norm_moe_g=norm_moe_g, w_router_group=w_router_group,
             b_router_group=b_router_group, w_router_expert=w_router_expert,
             b_router_expert=b_router_expert, w1=w1, w3=w3, w2=w2, norm_final_g=norm_final_g)
    w_packed = _pack_w_in(w_in)
    yp, kp, vp, kip, cp, sp = _stream_step(x_prompt, None, None, None, None, None, w_packed, p)
    ys, ksn, vsn, kisn, csn, ssn = _stream_step(x_sample, state_conv, state_ssm, cache_k,
                                                cache_v, cache_kidx, w_packed, p)
    return (yp, ys, kp, vp, kip, cp, sp, ksn, vsn, kisn, csn, ssn)
```

```python
import functools

import numpy as np
import jax
import jax.numpy as jnp
from jax import lax
from jax.experimental import pallas as pl
from jax.experimental.pallas import tpu as pltpu

F32 = jnp.float32
BF16 = jnp.bfloat16
I32 = jnp.int32

D_MODEL = 1024
CHUNK = 64
CHUNK_SHIFT = 6
EPS = 1e-6
SSM_INNER = 2048
SSM_HEADDIM = 64
SSM_HEADS = 32
SSM_GROUPS = 4
SSM_HEADS_PER_GROUP = SSM_HEADS // SSM_GROUPS
SSM_STATE = 128
CONV_WIDTH = 4
CONV_DIM = SSM_INNER + 2 * SSM_GROUPS * SSM_STATE
CONV_CARRY = 16
SSD_CHUNKS_PER_STEP = 2
ATTN_HEADS = 16
ATTN_KV_HEADS = 2
HEAD_DIM = 64
ATTN_REP = ATTN_HEADS // ATTN_KV_HEADS
ATTN_WIDTH = ATTN_HEADS * HEAD_DIM
IDX_HEADS = 8
IDX_DIM = 64
IDX_SCALE = (IDX_HEADS * IDX_DIM) ** -0.5
TOPK_MAX = 256
MOE_GROUPS = 4
EXPERTS_PER_GROUP = 8
EXPERT_GROUP_SHIFT = 3
N_EXPERTS = 32
EXPERT_FF = 256
N_BRANCHES = 2

LANE = 128
VMEM_LIMIT = 56 * 1024 * 1024
PROJ_TILE = 1024
PROJ_TILES = 9
Q_BLOCK = 128
SMALL_Q_BLOCK = 16
KEY_TILE = 256
SEARCH_GROUP = 4
EXPERTS_PER_STEP = 4
MERGE_TILE = 512
INT_MIN = np.int32(-2 ** 31)
NEG_BIG = -1e30

NT_DIMS = (((1,), (1,)), ((), ()))
TN_DIMS = (((0,), (0,)), ((), ()))


def _sigmoid(x):
    return 1.0 / (1.0 + jnp.exp(-x))


def _silu(x):
    h = 0.5 * x
    return h + h * jnp.tanh(h)


def _split3(x):
    hi = x.astype(BF16)
    r1 = x - hi.astype(F32)
    mid = r1.astype(BF16)
    r2 = r1 - mid.astype(F32)
    lo = r2.astype(BF16)
    return jnp.concatenate([hi, mid, lo], axis=1)


def _inproj_kernel(x_ref, g_ref, w_ref, z_ref, xbc_ref, q_ref, gate_ref, k_ref, v_ref,
                   qi_ref, ki_ref, dtw_ref, h_ref):
    j = pl.program_id(1)

    @pl.when(j == 0)
    def _():
        x = x_ref[...]
        ms = jnp.mean(x * x, axis=-1, keepdims=True)
        h_ref[...] = ((x * lax.rsqrt(ms + EPS)) * g_ref[...]).astype(BF16)

    def tile():
        return jnp.dot(h_ref[...], w_ref[...], preferred_element_type=F32)

    @pl.when(j < 2)
    def _():
        z_ref[...] = tile().astype(BF16)

    @pl.when((j >= 2) & (j < 5))
    def _():
        xbc_ref[...] = tile().astype(BF16)

    @pl.when(j == 5)
    def _():
        q_ref[...] = tile().astype(BF16)

    @pl.when((j >= 6) & (j < 8))
    def _():
        gate_ref[...] = tile().astype(BF16)

    @pl.when(j == 8)
    def _():
        t = tile()
        k_ref[...] = t[:, 0:128]
        v_ref[...] = t[:, 128:256]
        qi_ref[...] = t[:, 256:768].astype(BF16)
        ki_ref[...] = t[:, 768:832]
        dtw_ref[...] = t[:, 896:1024]


def _pack_w_in(w_in):
    sizes = (SSM_INNER, CONV_DIM, SSM_HEADS, ATTN_WIDTH, ATTN_KV_HEADS * HEAD_DIM,
             ATTN_KV_HEADS * HEAD_DIM, IDX_HEADS * IDX_DIM, IDX_DIM, IDX_HEADS,
             N_BRANCHES * D_MODEL)
    offs = np.concatenate([[0], np.cumsum(sizes)])
    z, xbc, dt, q, k, v, qi, ki, wi, g = [w_in[:, offs[i]:offs[i + 1]] for i in range(10)]
    zeros = lambda n: jnp.zeros((D_MODEL, n), w_in.dtype)
    cols = [z, xbc, q, g, k, v, qi, ki, zeros(64), dt, wi, zeros(LANE - SSM_HEADS - IDX_HEADS)]
    return jnp.concatenate(cols, axis=1).astype(BF16)


def _in_proj(x2d, norm_g, w_packed, tm):
    n = x2d.shape[0]
    assert n % tm == 0
    bf = lambda c: jax.ShapeDtypeStruct((n, c), BF16)
    ff = lambda c: jax.ShapeDtypeStruct((n, c), F32)
    clip = lambda j, lo, hi: jnp.clip(j - lo, 0, hi - lo - 1)
    return pl.pallas_call(
        _inproj_kernel,
        out_shape=(bf(SSM_INNER), bf(CONV_DIM), bf(ATTN_WIDTH), bf(2 * D_MODEL),
                   ff(128), ff(128), bf(512), ff(IDX_DIM), ff(128)),
        grid=(n // tm, PROJ_TILES),
        in_specs=[
            pl.BlockSpec((tm, D_MODEL), lambda i, j: (i, 0)),
            pl.BlockSpec((1, D_MODEL), lambda i, j: (0, 0)),
            pl.BlockSpec((D_MODEL, PROJ_TILE), lambda i, j: (0, j)),
        ],
        out_specs=(
            pl.BlockSpec((tm, PROJ_TILE), lambda i, j: (i, clip(j, 0, 2))),
            pl.BlockSpec((tm, PROJ_TILE), lambda i, j: (i, clip(j, 2, 5))),
            pl.BlockSpec((tm, PROJ_TILE), lambda i, j: (i, 0)),
            pl.BlockSpec((tm, PROJ_TILE), lambda i, j: (i, clip(j, 6, 8))),
            pl.BlockSpec((tm, 128), lambda i, j: (i, 0)),
            pl.BlockSpec((tm, 128), lambda i, j: (i, 0)),
            pl.BlockSpec((tm, 512), lambda i, j: (i, 0)),
            pl.BlockSpec((tm, IDX_DIM), lambda i, j: (i, 0)),
            pl.BlockSpec((tm, 128), lambda i, j: (i, 0)),
        ),
        scratch_shapes=[pltpu.VMEM((tm, D_MODEL), BF16)],
        compiler_params=pltpu.CompilerParams(
            dimension_semantics=("arbitrary", "arbitrary"), vmem_limit_bytes=VMEM_LIMIT),
        name="in_proj",
    )(x2d, norm_g.reshape(1, D_MODEL), w_packed)


def _ssd_kernel(*refs, L, cps, nsteps, has_state):
    if has_state:
        (xbc_ref, z_ref, dtw_ref, cs_ref, s0_ref, shift_ref, convw_ref, convb_ref, dtb_ref, a_ref,
         dx_ref, ng_ref, e3p_ref, e3l_ref, tri_ref, bd_ref, yn_ref, cnew_ref, sout_ref,
         xe_ref, st) = refs
    else:
        (xbc_ref, z_ref, dtw_ref, shift_ref, convw_ref, convb_ref, dtb_ref, a_ref,
         dx_ref, ng_ref, e3p_ref, e3l_ref, tri_ref, bd_ref, yn_ref, cnew_ref, sout_ref,
         xe_ref, st) = refs
    c = pl.program_id(1)
    G, R, N, P = SSM_GROUPS, SSM_HEADS_PER_GROUP, SSM_STATE, SSM_HEADDIM
    GW = R * P
    RL = R * L
    LB = cps * L

    @pl.when(c == 0)
    def _():
        xe_ref[LB:, :] = jnp.zeros((xe_ref.shape[0] - LB, CONV_DIM), BF16)
        if has_state:
            xe_ref[LB:LB + CONV_CARRY, :] = cs_ref[0]
            st[...] = s0_ref[0].T
        else:
            st[...] = jnp.zeros(st.shape, F32)

    xe_ref[0:LB, :] = xbc_ref[0]
    shifted = jnp.dot(shift_ref[0], xe_ref[...], preferred_element_type=F32)
    w = convw_ref[...]
    conv = convb_ref[...] + shifted[2 * LB:3 * LB, :] * w[0:1, :]
    conv = conv + shifted[LB:2 * LB, :] * w[1:2, :]
    conv = conv + shifted[0:LB, :] * w[2:3, :]
    conv = conv + xbc_ref[0].astype(F32) * w[3:4, :]
    act = _silu(conv)

    @pl.when(c == nsteps - 1)
    def _():
        cnew_ref[0] = xbc_ref[0, LB - CONV_CARRY:LB, :].astype(F32)[
            CONV_CARRY - (CONV_WIDTH - 1):CONV_CARRY, :]

    if nsteps > 1:
        xe_ref[LB:LB + CONV_CARRY, :] = xe_ref[LB - CONV_CARRY:LB, :]

    u = dtw_ref[0] + dtb_ref[...]
    dt = jnp.maximum(u, 0.0) + jnp.log1p(jnp.exp(-jnp.abs(u)))
    a = dt * a_ref[...]
    ac3 = jnp.dot(tri_ref[...], _split3(a), preferred_element_type=F32)
    a_cum = ac3[:, 0:LANE] + ac3[:, LANE:2 * LANE] + ac3[:, 2 * LANE:]

    ex = jnp.dot(_split3(jnp.concatenate([a_cum, dt], axis=0)), e3p_ref[...],
                 preferred_element_type=F32)
    if L == P:
        acl_all = ex[:LB]
    else:
        acl_all = jnp.dot(_split3(a_cum), e3l_ref[...], preferred_element_type=F32)
    zs = _silu(z_ref[0].astype(F32))
    row = lax.broadcasted_iota(I32, (L, RL), 0)
    scol = lax.broadcasted_iota(I32, (L, RL), 1) & (L - 1)

    for sub in range(cps):
        rows = slice(sub * L, (sub + 1) * L)
        xs = act[rows, :SSM_INNER]
        bmat = act[rows, SSM_INNER:SSM_INNER + G * N].astype(BF16)
        cmat = act[rows, SSM_INNER + G * N:].astype(BF16)
        acx = ex[rows]
        dtx = ex[LB + sub * L:LB + (sub + 1) * L]
        acl = acl_all[rows]
        xdt = xs * dtx
        xdt_b = xdt.astype(BF16)
        eacx = jnp.exp(acx)
        alast = acx[L - 1:L, :]
        xw = (xdt * jnp.exp(alast - acx)).astype(BF16)
        ealast = jnp.exp(alast)

        ys = []
        for g in range(G):
            sl = slice(g * GW, (g + 1) * GW)
            b_g = bmat[:, g * N:(g + 1) * N]
            c_g = cmat[:, g * N:(g + 1) * N]
            acl_g = acl[:, g * RL:(g + 1) * RL]
            acs_g = jnp.sum(jnp.where(scol == row, acl_g, 0.0), axis=0, keepdims=True)
            decay = jnp.where(scol <= row, jnp.exp(acl_g - acs_g), 0.0)
            cb = lax.dot_general(c_g, jnp.concatenate([b_g] * R, axis=0), NT_DIMS,
                                 preferred_element_type=F32)
            m_g = (cb * decay).astype(BF16)
            xbd = jnp.concatenate([xdt_b[:, sl]] * R, axis=0) * bd_ref[...]
            y_in = jnp.dot(m_g, xbd, preferred_element_type=F32)
            st_g = st[:, sl]
            y_st = jnp.dot(c_g, st_g.astype(BF16), preferred_element_type=F32)
            ys.append(y_in + eacx[:, sl] * y_st)
            upd = lax.dot_general(b_g, xw[:, sl], TN_DIMS, preferred_element_type=F32)
            st[:, sl] = st_g * ealast[:, sl] + upd
        y = jnp.concatenate(ys, axis=1) + dx_ref[...] * xs

        yz = y * zs[rows]
        ms = jnp.mean(yz * yz, axis=-1, keepdims=True)
        yn_ref[0, rows, :] = ((yz * lax.rsqrt(ms + EPS)) * ng_ref[...]).astype(BF16)

    @pl.when(c == nsteps - 1)
    def _():
        sout_ref[0] = st[...].T


def _shift_matrices(L, conv_rows):
    out = np.zeros((2, 3 * L, conv_rows), np.float32)
    for k in range(CONV_WIDTH - 1):
        for t in range(L):
            src = t - (k + 1)
            if src >= 0:
                out[:, k * L + t, src] = 1.0
            else:
                out[1, k * L + t, L + CONV_CARRY + src] = 1.0
                j = CONV_WIDTH - 1 + src
                for piece in range(3):
                    out[0, k * L + t, L + 3 * piece + j] = 1.0
    return jnp.asarray(out, BF16)


def _expand_matrix(width_per_head):
    r = np.arange(3 * LANE)[:, None] % LANE
    c = np.arange(SSM_HEADS * width_per_head)[None, :] // width_per_head
    return jnp.asarray((r == c).astype(np.float32), BF16)


def _ssd(xbc, z, dtw, conv_state, ssm_state, conv_w, conv_b, dt_bias, a_log, d_skip, ssm_norm_g):
    b, t, _ = xbc.shape
    L = min(CHUNK, t)
    nc = t // L
    assert t % L == 0 and L & (L - 1) == 0
    has_state = conv_state is not None
    pad = lambda v: jnp.pad(v.astype(F32), (0, LANE - SSM_HEADS)).reshape(1, LANE)
    a_neg = pad(-jnp.exp(a_log.astype(F32)))
    cps = SSD_CHUNKS_PER_STEP if nc % SSD_CHUNKS_PER_STEP == 0 else 1
    nsteps = nc // cps
    lb = cps * L
    conv_rows = _round_up(lb + CONV_CARRY, LANE)
    tri = jnp.asarray(np.kron(np.eye(cps), np.tril(np.ones((L, L)))).astype(np.float32), BF16)
    rl = SSM_HEADS_PER_GROUP * L
    gw = SSM_HEADS_PER_GROUP * SSM_HEADDIM
    bd = jnp.asarray((np.arange(rl)[:, None] // L == np.arange(gw)[None, :] // SSM_HEADDIM)
                     .astype(np.float32), BF16)
    const = lambda shape: pl.BlockSpec(shape, lambda i, j: (0,) * len(shape))
    tok = lambda w: pl.BlockSpec((1, lb, w), lambda i, j: (i, j, 0))
    per_b = lambda s: pl.BlockSpec((1,) + s, lambda i, j: (i, 0, 0))
    ins = [xbc, z, dtw]
    specs = [tok(CONV_DIM), tok(SSM_INNER), tok(LANE)]
    if has_state:
        triples = _split3(conv_state.astype(F32))
        ins += [_pad_axis(triples, 1, CONV_CARRY), ssm_state.reshape(b, SSM_INNER, SSM_STATE)]
        specs += [per_b((CONV_CARRY, CONV_DIM)), per_b((SSM_INNER, SSM_STATE))]
    first = 0 if has_state else 1
    ins += [_shift_matrices(lb, conv_rows)]
    specs += [pl.BlockSpec((1, 3 * lb, conv_rows),
                           lambda i, j: (jnp.where(j == 0, first, 1), 0, 0))]
    ins += [conv_w, conv_b.reshape(1, CONV_DIM), pad(dt_bias), a_neg,
            jnp.repeat(d_skip.astype(F32), SSM_HEADDIM).reshape(1, SSM_INNER),
            ssm_norm_g.reshape(1, SSM_INNER), _expand_matrix(SSM_HEADDIM), _expand_matrix(L),
            tri, bd]
    specs += [const((CONV_WIDTH, CONV_DIM)), const((1, CONV_DIM)), const((1, LANE)),
              const((1, LANE)), const((1, SSM_INNER)), const((1, SSM_INNER)),
              const((3 * LANE, SSM_INNER)), const((3 * LANE, SSM_HEADS * L)),
              const((lb, lb)), const((rl, gw))]
    yn, cnew, sout = pl.pallas_call(
        functools.partial(_ssd_kernel, L=L, cps=cps, nsteps=nsteps, has_state=has_state),
        out_shape=(jax.ShapeDtypeStruct((b, t, SSM_INNER), BF16),
                   jax.ShapeDtypeStruct((b, CONV_WIDTH - 1, CONV_DIM), F32),
                   jax.ShapeDtypeStruct((b, SSM_INNER, SSM_STATE), F32)),
        grid=(b, nsteps),
        in_specs=specs,
        out_specs=(tok(SSM_INNER), per_b((CONV_WIDTH - 1, CONV_DIM)),
                   per_b((SSM_INNER, SSM_STATE))),
        scratch_shapes=[pltpu.VMEM((conv_rows, CONV_DIM), BF16),
                        pltpu.VMEM((SSM_STATE, SSM_INNER), F32)],
        compiler_params=pltpu.CompilerParams(
            dimension_semantics=("arbitrary", "arbitrary"), vmem_limit_bytes=VMEM_LIMIT),
        name="ssd",
    )(*ins)
    return yn, cnew, sout.reshape(b, SSM_HEADS, SSM_HEADDIM, SSM_STATE)


def _dsa_kernel(q_ref, qi_ref, w_ref, kidx_ref, kh_ref, vt_ref, o_ref,
                qs_ref, key_ref, bias_ref, m_ref, l_ref, acc_ref, mlim_ref, s_ref,
                *, past_len, l_valid, topk, nq):
    qb = pl.program_id(1)
    TQ, TK = LANE, KEY_TILE
    W = ATTN_REP * nq
    wide = nq == LANE

    lane = lax.broadcasted_iota(I32, (1, TQ), 1)
    pos = past_len + qb * nq + (lane & (nq - 1))
    n_adm = jnp.minimum(((pos >> CHUNK_SHIFT) + 1) * CHUNK, l_valid)
    k_eff = jnp.minimum(n_adm, topk)
    last_pos = past_len + qb * nq + nq - 1
    n_max = jnp.minimum(((last_pos >> CHUNK_SHIFT) + 1) * CHUNK, l_valid)
    nt = (n_max + TK - 1) // TK

    if wide:
        for h in range(IDX_HEADS):
            qs_ref[h * nq:(h + 1) * nq, :] = qi_ref[0, :, h * IDX_DIM:(h + 1) * IDX_DIM]
        w_t = w_ref[0].T[SSM_HEADS:SSM_HEADS + IDX_HEADS, :] * IDX_SCALE
    else:
        qs_ref[...] = qi_ref[0]
        w_row = w_ref[0] * IDX_SCALE

    def score_tile(t, carry):
        s0 = pl.multiple_of(t * TK, TK)
        d = lax.dot_general(kidx_ref[0, pl.ds(s0, TK), :], qs_ref[0:IDX_HEADS * nq, :],
                            NT_DIMS, preferred_element_type=F32)
        if wide:
            sc = w_t[0:1, :] * jnp.maximum(d[:, 0:nq], 0.0)
            for h in range(1, IDX_HEADS):
                sc = sc + w_t[h:h + 1, :] * jnp.maximum(d[:, h * nq:(h + 1) * nq], 0.0)
        else:
            sc = w_row * jnp.maximum(d, 0.0)
            for shift in (64, 32, 16):
                sc = sc + pltpu.roll(sc, shift, axis=1)
        sc = sc + 0.0
        bits = pltpu.bitcast(sc, I32)
        key = jnp.where(bits < 0, bits ^ np.int32(0x7FFFFFFF), bits)
        s_idx = s0 + lax.broadcasted_iota(I32, (TK, TQ), 0)
        key_ref[pl.ds(s0, TK), :] = jnp.where(s_idx < n_adm, key, INT_MIN)
        return carry

    lax.fori_loop(0, nt, score_tile, 0)

    def count(pred):
        def body(t, acc):
            s0 = pl.multiple_of(t * TK, TK)
            blk = key_ref[pl.ds(s0, TK), :]
            s_idx = s0 + lax.broadcasted_iota(I32, (TK, TQ), 0)
            hit = jnp.where(pred(blk, s_idx), 1, 0).astype(I32)
            return acc + jnp.sum(hit.reshape(TK // 8, 8, TQ), axis=0)
        acc = lax.fori_loop(0, nt, body, jnp.zeros((8, TQ), I32))
        return jnp.sum(acc, axis=0, keepdims=True)

    def bit_step(i, state):
        prefix, done = state
        cand = prefix | jnp.left_shift(jnp.int32(1), 31 - i)
        cnt = count(lambda blk, s_idx: blk >= (cand ^ INT_MIN))
        prefix = jnp.where((done == 0) & (cnt >= k_eff), cand, prefix)
        return prefix, done | jnp.where(cnt == k_eff, 1, 0)

    def pass_group(state):
        grp, prefix, done, _ = state
        prefix, done = lax.fori_loop(grp * SEARCH_GROUP, (grp + 1) * SEARCH_GROUP, bit_step,
                                     (prefix, done))
        return grp + 1, prefix, done, jnp.min(done)

    zero_row = jnp.zeros((1, TQ), I32)
    _, prefix, _, _ = lax.while_loop(
        lambda state: (state[0] < 32 // SEARCH_GROUP) & (state[3] == 0),
        pass_group, (jnp.int32(0), zero_row, zero_row, jnp.int32(0)))
    thr = prefix ^ INT_MIN

    n_gt = count(lambda blk, s_idx: blk > thr)
    n_eq = count(lambda blk, s_idx: blk == thr)
    need = k_eff - n_gt
    mlim_ref[...] = jnp.full((1, TQ), 2 ** 30, I32)

    @pl.when(jnp.max(n_eq - need) > 0)
    def _():
        def idx_step(i, prefix):
            cand = prefix | jnp.left_shift(jnp.int32(1), 14 - i)
            cnt = count(lambda blk, s_idx: (blk == thr) & (s_idx < cand))
            return jnp.where(cnt < need, cand, prefix)
        mlim_ref[...] = lax.fori_loop(0, 15, idx_step, jnp.zeros((1, TQ), I32))

    mlim = mlim_ref[...]

    def bias_tile(t, carry):
        s0 = pl.multiple_of(t * TK, TK)
        blk = key_ref[pl.ds(s0, TK), :]
        s_idx = s0 + lax.broadcasted_iota(I32, (TK, TQ), 0)
        sel = (blk > thr) | ((blk == thr) & (s_idx <= mlim))
        bias_ref[pl.ds(s0, TK), :] = jnp.where(sel, 0.0, NEG_BIG).astype(F32)
        return carry

    lax.fori_loop(0, nt, bias_tile, 0)

    for g in range(ATTN_KV_HEADS):
        if wide:
            for r in range(ATTN_REP):
                hq = g * ATTN_REP + r
                qs_ref[r * nq:(r + 1) * nq, :] = (
                    q_ref[0, :, hq * HEAD_DIM:(hq + 1) * HEAD_DIM] * (HEAD_DIM ** -0.5))
        else:
            qs_ref[...] = q_ref[0, g] * (HEAD_DIM ** -0.5)
        m_ref[...] = jnp.full(m_ref.shape, -jnp.inf, F32)
        l_ref[...] = jnp.zeros(l_ref.shape, F32)
        acc_ref[...] = jnp.zeros(acc_ref.shape, F32)

        def scores(t, slot):
            tc = jnp.minimum(t, nt - 1)
            s0 = pl.multiple_of(tc * TK, TK)
            s = lax.dot_general(kh_ref[0, g, pl.ds(s0, TK), :], qs_ref[0:W, :], NT_DIMS,
                                preferred_element_type=F32)
            bias = bias_ref[pl.ds(s0, TK), :] + jnp.where(t < nt, 0.0, NEG_BIG)
            s_ref[slot] = s + jnp.concatenate([bias] * (W // TQ), axis=1)

        def accumulate(t, slot):
            tc = jnp.minimum(t, nt - 1)
            s0 = pl.multiple_of(tc * TK, TK)
            s = s_ref[slot]
            m_prev = m_ref[...]
            m_new = jnp.maximum(m_prev, jnp.max(s, axis=0, keepdims=True))
            p = jnp.exp(s - m_new)
            alpha = jnp.exp(m_prev - m_new)
            l_ref[...] = alpha * l_ref[...] + jnp.sum(p, axis=0, keepdims=True)
            acc_ref[...] = alpha * acc_ref[...] + jnp.dot(
                vt_ref[0, g, :, pl.ds(s0, TK)], p.astype(BF16), preferred_element_type=F32)
            m_ref[...] = m_new

        scores(0, 0)

        def attn_pair(u, carry):
            scores(2 * u + 1, 1)
            accumulate(2 * u, 0)
            scores(2 * u + 2, 0)
            accumulate(2 * u + 1, 1)
            return carry

        lax.fori_loop(0, (nt + 1) // 2, attn_pair, 0)
        o_t = acc_ref[...] * (1.0 / l_ref[...])
        if wide:
            for r in range(0, ATTN_REP, 2):
                hq = g * ATTN_REP + r
                pair = jnp.concatenate([o_t[:, r * nq:(r + 1) * nq],
                                        o_t[:, (r + 1) * nq:(r + 2) * nq]], axis=0)
                o_ref[0, :, hq * HEAD_DIM:(hq + 2) * HEAD_DIM] = pair.T.astype(BF16)
        else:
            o_rq = jnp.concatenate([o_t, o_t], axis=0).T[:, 0:HEAD_DIM].astype(BF16)
            for r in range(ATTN_REP):
                hq = g * ATTN_REP + r
                o_ref[0, :, hq * HEAD_DIM:(hq + 1) * HEAD_DIM] = o_rq[r * nq:(r + 1) * nq, :]


def _dsa(q, qi, w, kidx_all, k_heads, v_heads_t, past_len, l_valid, t):
    b = kidx_all.shape[0]
    lp = kidx_all.shape[1]
    nq = Q_BLOCK if t % Q_BLOCK == 0 else SMALL_Q_BLOCK
    assert t % nq == 0 and lp % KEY_TILE == 0 and lp < 2 ** 15
    topk = min(TOPK_MAX, l_valid // 4)
    width = ATTN_REP * nq
    if nq == Q_BLOCK:
        qblk = lambda c: pl.BlockSpec((1, nq, c), lambda i, j: (i, j, 0))
        q_specs = [qblk(ATTN_WIDTH), qblk(IDX_HEADS * IDX_DIM), qblk(LANE)]
    else:
        assert t == nq
        q_specs = [pl.BlockSpec((1, ATTN_KV_HEADS, width, HEAD_DIM), lambda i, j: (i, 0, 0, 0)),
                   pl.BlockSpec((1, IDX_HEADS * nq, IDX_DIM), lambda i, j: (i, 0, 0)),
                   pl.BlockSpec((1, 1, LANE), lambda i, j: (i, 0, 0))]
    return pl.pallas_call(
        functools.partial(_dsa_kernel, past_len=past_len, l_valid=l_valid, topk=topk, nq=nq),
        out_shape=jax.ShapeDtypeStruct((b, t, ATTN_WIDTH), BF16),
        grid=(b, t // nq),
        in_specs=q_specs + [
            pl.BlockSpec((1, lp, IDX_DIM), lambda i, j: (i, 0, 0)),
            pl.BlockSpec((1, ATTN_KV_HEADS, lp, HEAD_DIM), lambda i, j: (i, 0, 0, 0)),
            pl.BlockSpec((1, ATTN_KV_HEADS, HEAD_DIM, lp), lambda i, j: (i, 0, 0, 0))],
        out_specs=pl.BlockSpec((1, nq, ATTN_WIDTH), lambda i, j: (i, j, 0)),
        scratch_shapes=[pltpu.VMEM((width, HEAD_DIM), BF16),
                        pltpu.VMEM((lp, LANE), I32),
                        pltpu.VMEM((lp, LANE), F32),
                        pltpu.VMEM((1, width), F32),
                        pltpu.VMEM((1, width), F32),
                        pltpu.VMEM((HEAD_DIM, width), F32),
                        pltpu.VMEM((1, LANE), I32),
                        pltpu.VMEM((2, KEY_TILE, width), F32)],
        compiler_params=pltpu.CompilerParams(
            dimension_semantics=("arbitrary", "arbitrary"), vmem_limit_bytes=VMEM_LIMIT),
        name="dsa",
    )(q, qi, w, kidx_all, k_heads, v_heads_t)


def _merge_kernel(x_ref, yn_ref, o_ref, gate_ref, wa_ref, wb_ref, wo_ref, out_ref):
    br_a = jnp.dot(yn_ref[...], wa_ref[...], preferred_element_type=F32)
    br_b = jnp.dot(o_ref[...], wb_ref[...], preferred_element_type=F32)
    gates = gate_ref[...].astype(F32)
    merged = _sigmoid(gates[:, :D_MODEL]) * br_a + _sigmoid(gates[:, D_MODEL:]) * br_b
    out_ref[...] = x_ref[...] + jnp.dot(merged.astype(BF16), wo_ref[...],
                                        preferred_element_type=F32)


def _merge(x2d, yn, o, gate, w_ssm_out, w_attn_out, w_o, tm):
    n = x2d.shape[0]
    assert n % tm == 0
    row = lambda w: pl.BlockSpec((tm, w), lambda i: (i, 0))
    full = lambda r, c: pl.BlockSpec((r, c), lambda i: (0, 0))
    return pl.pallas_call(
        _merge_kernel,
        out_shape=jax.ShapeDtypeStruct((n, D_MODEL), F32),
        grid=(n // tm,),
        in_specs=[row(D_MODEL), row(SSM_INNER), row(ATTN_WIDTH), row(2 * D_MODEL),
                  full(SSM_INNER, D_MODEL), full(ATTN_WIDTH, D_MODEL), full(D_MODEL, D_MODEL)],
        out_specs=row(D_MODEL),
        compiler_params=pltpu.CompilerParams(
            dimension_semantics=("arbitrary",), vmem_limit_bytes=VMEM_LIMIT),
        name="merge",
    )(x2d, yn, o, gate, w_ssm_out.astype(BF16), w_attn_out.astype(BF16), w_o.astype(BF16))


def _moe_kernel(x_ref, ng_ref, wr_ref, br_ref, w1_ref, w3_ref, w2_ref, fg_ref, out_ref,
                h_ref, comb_ref, acc_ref, *, n_steps):
    e_step = pl.program_id(1)
    tm = x_ref.shape[0]

    @pl.when(e_step == 0)
    def _():
        x = x_ref[...]
        ms = jnp.mean(x * x, axis=-1, keepdims=True)
        h = ((x * lax.rsqrt(ms + EPS)) * ng_ref[...]).astype(BF16)
        h_ref[...] = h
        logits = jnp.dot(h, wr_ref[...], preferred_element_type=F32) + br_ref[...]
        lane_i = lax.broadcasted_iota(I32, (tm, LANE), 1)
        lane = lane_i.astype(F32)
        first = lambda hit: jnp.min(jnp.where(hit, lane, float(LANE)), axis=1, keepdims=True)
        is_g = (lane_i >= N_EXPERTS) & (lane_i < N_EXPERTS + MOE_GROUPS)
        glog = jnp.where(is_g, logits, -jnp.inf)
        gmax = jnp.max(glog, axis=1, keepdims=True)
        gsel = first(glog == gmax) - float(N_EXPERTS)
        p_group = 1.0 / jnp.sum(jnp.exp(glog - gmax), axis=1, keepdims=True)
        grp = (lane_i >> EXPERT_GROUP_SHIFT).astype(F32)
        in_grp = (lane_i < N_EXPERTS) & (grp == gsel)
        el = jnp.where(in_grp, logits, -jnp.inf)
        v1 = jnp.max(el, axis=1, keepdims=True)
        i1 = first(el == v1)
        el2 = jnp.where(lane == i1, -jnp.inf, el)
        v2 = jnp.max(el2, axis=1, keepdims=True)
        i2 = first(el2 == v2)
        e21 = jnp.exp(v2 - v1)
        den = 1.0 + e21
        comb_ref[...] = (jnp.where(lane == i1, (1.0 / den) * p_group, 0.0)
                         + jnp.where(lane == i2, (e21 / den) * p_group, 0.0))
        acc_ref[...] = jnp.zeros(acc_ref.shape, F32)

    h = h_ref[...]
    comb = comb_ref[...]
    lane = lax.broadcasted_iota(I32, (tm, LANE), 1)
    for k in range(EXPERTS_PER_STEP):
        e = e_step * EXPERTS_PER_STEP + k
        a1 = jnp.dot(h, w1_ref[k], preferred_element_type=F32)
        a3 = jnp.dot(h, w3_ref[k], preferred_element_type=F32)
        he = (_silu(a1) * a3).astype(BF16)
        ye = jnp.dot(he, w2_ref[k], preferred_element_type=F32)
        wcol = jnp.sum(jnp.where(lane == e, comb, 0.0), axis=1, keepdims=True)
        acc_ref[...] += wcol * ye

    @pl.when(e_step == n_steps - 1)
    def _():
        y = x_ref[...] + acc_ref[...]
        ms = jnp.mean(y * y, axis=-1, keepdims=True)
        out_ref[...] = (y * lax.rsqrt(ms + EPS)) * fg_ref[...]


def _moe(x2d, norm_g, w_router_group, b_router_group, w_router_expert, b_router_expert,
         w1, w3, w2, norm_final_g, tm):
    n = x2d.shape[0]
    assert n % tm == 0
    n_steps = N_EXPERTS // EXPERTS_PER_STEP
    padw = LANE - N_EXPERTS - MOE_GROUPS
    wr = jnp.concatenate([w_router_expert, w_router_group,
                          jnp.zeros((D_MODEL, padw), F32)], axis=1).astype(BF16)
    br = jnp.concatenate([b_router_expert, b_router_group,
                          jnp.zeros((padw,), F32)]).astype(F32).reshape(1, LANE)
    row = pl.BlockSpec((tm, D_MODEL), lambda i, e: (i, 0))
    const = lambda r, c: pl.BlockSpec((r, c), lambda i, e: (0, 0))
    return pl.pallas_call(
        functools.partial(_moe_kernel, n_steps=n_steps),
        out_shape=jax.ShapeDtypeStruct((n, D_MODEL), F32),
        grid=(n // tm, n_steps),
        in_specs=[row, const(1, D_MODEL), const(D_MODEL, LANE), const(1, LANE),
                  pl.BlockSpec((EXPERTS_PER_STEP, D_MODEL, EXPERT_FF), lambda i, e: (e, 0, 0)),
                  pl.BlockSpec((EXPERTS_PER_STEP, D_MODEL, EXPERT_FF), lambda i, e: (e, 0, 0)),
                  pl.BlockSpec((EXPERTS_PER_STEP, EXPERT_FF, D_MODEL), lambda i, e: (e, 0, 0)),
                  const(1, D_MODEL)],
        out_specs=row,
        scratch_shapes=[pltpu.VMEM((tm, D_MODEL), BF16), pltpu.VMEM((tm, LANE), F32),
                        pltpu.VMEM((tm, D_MODEL), F32)],
        compiler_params=pltpu.CompilerParams(
            dimension_semantics=("arbitrary", "arbitrary"), vmem_limit_bytes=VMEM_LIMIT),
        name="moe",
    )(x2d, norm_g.reshape(1, D_MODEL), wr, br, w1.astype(BF16), w3.astype(BF16),
      w2.astype(BF16), norm_final_g.reshape(1, D_MODEL))


def _token_tile(n):
    for tm in (1024, 512, 256, 128, 64, 32, 16, 8):
        if n % tm == 0:
            return tm
    raise ValueError(f"token count {n} is not a multiple of 8")


def _pad_axis(a, axis, size):
    if a.shape[axis] == size:
        return a
    widths = [(0, 0)] * a.ndim
    widths[axis] = (0, size - a.shape[axis])
    return jnp.pad(a, widths)


def _round_up(n, m):
    return (n + m - 1) // m * m


def _stream_step(x, conv_state, ssm_state, k_past, v_past, kidx_past, w_packed, p):
    b, t, d = x.shape
    n = b * t
    tm = _token_tile(n)
    x2d = x.reshape(n, d)
    z, xbc, q, gate, k, v, qi, ki, dtw = _in_proj(x2d, p["norm_mix_g"], w_packed, tm)

    yn, conv_new, ssm_new = _ssd(
        xbc.reshape(b, t, CONV_DIM), z.reshape(b, t, SSM_INNER), dtw.reshape(b, t, LANE),
        conv_state, ssm_state, p["conv_w"], p["conv_b"], p["dt_bias"], p["A_log"],
        p["D_skip"], p["ssm_norm_g"])

    k4 = k.reshape(b, t, ATTN_KV_HEADS, HEAD_DIM)
    v4 = v.reshape(b, t, ATTN_KV_HEADS, HEAD_DIM)
    ki3 = ki.reshape(b, t, IDX_DIM)
    if k_past is None:
        past_len, k_all, v_all, kidx_all = 0, k4, v4, ki3
    else:
        past_len = k_past.shape[1]
        k_all = jnp.concatenate([k_past, k4], axis=1)
        v_all = jnp.concatenate([v_past, v4], axis=1)
        kidx_all = jnp.concatenate([kidx_past, ki3], axis=1)
    l_valid = past_len + t
    lp = _round_up(l_valid, KEY_TILE)
    heads = lambda a: _pad_axis(a.astype(BF16).transpose(0, 2, 1, 3), 2, lp)
    heads_t = lambda a: _pad_axis(a.astype(BF16).transpose(0, 2, 3, 1), 3, lp)
    if t % Q_BLOCK == 0:
        q_in = (q.reshape(b, t, ATTN_WIDTH), qi.reshape(b, t, IDX_HEADS * IDX_DIM),
                dtw.reshape(b, t, LANE))
    else:
        q_in = (q.reshape(b, t, ATTN_KV_HEADS, ATTN_REP, HEAD_DIM).transpose(0, 2, 3, 1, 4)
                .reshape(b, ATTN_KV_HEADS, ATTN_REP * t, HEAD_DIM),
                qi.reshape(b, t, IDX_HEADS, IDX_DIM).transpose(0, 2, 1, 3)
                .reshape(b, IDX_HEADS * t, IDX_DIM),
                dtw.reshape(b, t, LANE)[:, :, SSM_HEADS:SSM_HEADS + IDX_HEADS]
                .transpose(0, 2, 1).reshape(b, 1, IDX_HEADS * t))
    o = _dsa(*q_in, _pad_axis(kidx_all.astype(BF16), 1, lp), heads(k_all), heads_t(v_all),
             past_len, l_valid, t)
    o = o.reshape(n, ATTN_WIDTH)

    x1 = _merge(x2d, yn.reshape(n, SSM_INNER), o, gate, p["w_ssm_out"], p["w_attn_out"],
                p["w_o"], min(tm, MERGE_TILE))
    y = _moe(x1, p["norm_moe_g"], p["w_router_group"], p["b_router_group"],
             p["w_router_expert"], p["b_router_expert"], p["w1"], p["w3"], p["w2"],
             p["norm_final_g"], tm)
    return y.reshape(b, t, d), k4, v4, ki3, conv_new, ssm_new


def kernel(x_prompt, x_sample, cache_k, cache_v, cache_kidx, state_conv, state_ssm, norm_mix_g, w_in, conv_w, conv_b, dt_bias, A_log, D_skip, ssm_norm_g, w_ssm_out, w_attn_out, w_o, norm_moe_g, w_router_group, b_router_group, w_router_expert, b_router_expert, w1, w3, w2, norm_final_g):
    p = dict(norm_mix_g=norm_mix_g, conv_w=conv_w, conv_b=conv_b, dt_bias=dt_bias, A_log=A_log,
             D_skip=D_skip, ssm_norm_g=ssm_norm_g, w_ssm_out=w_ssm_out, w_attn_out=w_attn_out,
             w_o=w_o, norm_moe_g=norm_moe_g, w_router_group=w_router_group,
             b_router_group=b_router_group, w_router_expert=w_router_expert,
             b_router_expert=b_router_expert, w1=w1, w3=w3, w2=w2, norm_final_g=norm_final_g)
    w_packed = _pack_w_in(w_in)
    yp, kp, vp, kip, cp, sp = _stream_step(x_prompt, None, None, None, None, None, w_packed, p)
    ys, ksn, vsn, kisn, csn, ssn = _stream_step(x_sample, state_conv, state_ssm, cache_k,
                                                cache_v, cache_kidx, w_packed, p)
    return (yp, ys, kp, vp, kip, cp, sp, ksn, vsn, kisn, csn, ssn)
```

```python
import functools

import numpy as np
import jax
import jax.numpy as jnp
from jax import lax
from jax.experimental import pallas as pl
from jax.experimental.pallas import tpu as pltpu

F32 = jnp.float32
BF16 = jnp.bfloat16
I32 = jnp.int32

D_MODEL = 1024
CHUNK = 64
CHUNK_SHIFT = 6
EPS = 1e-6
SSM_INNER = 2048
SSM_HEADDIM = 64
SSM_HEADS = 32
SSM_GROUPS = 4
SSM_HEADS_PER_GROUP = SSM_HEADS // SSM_GROUPS
SSM_STATE = 128
CONV_WIDTH = 4
CONV_DIM = SSM_INNER + 2 * SSM_GROUPS * SSM_STATE
CONV_CARRY = 16
SSD_CHUNKS_PER_STEP = 2
ATTN_HEADS = 16
ATTN_KV_HEADS = 2
HEAD_DIM = 64
ATTN_REP = ATTN_HEADS // ATTN_KV_HEADS
ATTN_WIDTH = ATTN_HEADS * HEAD_DIM
IDX_HEADS = 8
IDX_DIM = 64
IDX_SCALE = (IDX_HEADS * IDX_DIM) ** -0.5
TOPK_MAX = 256
MOE_GROUPS = 4
EXPERTS_PER_GROUP = 8
EXPERT_GROUP_SHIFT = 3
N_EXPERTS = 32
EXPERT_FF = 256
N_BRANCHES = 2

LANE = 128
VMEM_LIMIT = 56 * 1024 * 1024
PROJ_TILE = 1024
PROJ_TILES = 9
Q_BLOCK = 128
SMALL_Q_BLOCK = 16
KEY_TILE = 256
EXPERTS_PER_STEP = 4
MERGE_TILE = 512
INT_MIN = np.int32(-2 ** 31)
NEG_BIG = -1e30
Q_SCALE = HEAD_DIM ** -0.5 * float(np.log2(np.e))
PV_ROWS = 80

NT_DIMS = (((1,), (1,)), ((), ()))
TN_DIMS = (((0,), (0,)), ((), ()))


def _sigmoid(x):
    return 1.0 / (1.0 + jnp.exp(-x))


def _silu(x):
    h = 0.5 * x
    return h + h * jnp.tanh(h)


def _split3(x):
    hi = x.astype(BF16)
    r1 = x - hi.astype(F32)
    mid = r1.astype(BF16)
    r2 = r1 - mid.astype(F32)
    lo = r2.astype(BF16)
    return jnp.concatenate([hi, mid, lo], axis=1)


def _inproj_kernel(x_ref, g_ref, w_ref, z_ref, xbc_ref, q_ref, gate_ref, k_ref, v_ref,
                   qi_ref, ki_ref, dtw_ref, kb_ref, vb_ref, kib_ref, h_ref):
    j = pl.program_id(1)

    @pl.when(j == 0)
    def _():
        x = x_ref[...]
        ms = jnp.mean(x * x, axis=-1, keepdims=True)
        h_ref[...] = ((x * lax.rsqrt(ms + EPS)) * g_ref[...]).astype(BF16)

    def tile():
        return jnp.dot(h_ref[...], w_ref[...], preferred_element_type=F32)

    @pl.when(j < 2)
    def _():
        z_ref[...] = tile().astype(BF16)

    @pl.when((j >= 2) & (j < 5))
    def _():
        xbc_ref[...] = tile().astype(BF16)

    @pl.when(j == 5)
    def _():
        q_ref[...] = (tile() * Q_SCALE).astype(BF16)

    @pl.when((j >= 6) & (j < 8))
    def _():
        gate_ref[...] = tile().astype(BF16)

    @pl.when(j == 8)
    def _():
        t = tile()
        k_ref[...] = t[:, 0:128]
        v_ref[...] = t[:, 128:256]
        qi_ref[...] = t[:, 256:768].astype(BF16)
        ki_ref[...] = t[:, 768:832]
        dtw_ref[...] = t[:, 896:1024]
        kb_ref[...] = t[:, 0:128].astype(BF16)
        vb_ref[...] = t[:, 128:256].astype(BF16)
        kib_ref[...] = t[:, 768:832].astype(BF16)


def _pack_w_in(w_in):
    sizes = (SSM_INNER, CONV_DIM, SSM_HEADS, ATTN_WIDTH, ATTN_KV_HEADS * HEAD_DIM,
             ATTN_KV_HEADS * HEAD_DIM, IDX_HEADS * IDX_DIM, IDX_DIM, IDX_HEADS,
             N_BRANCHES * D_MODEL)
    offs = np.concatenate([[0], np.cumsum(sizes)])
    z, xbc, dt, q, k, v, qi, ki, wi, g = [w_in[:, offs[i]:offs[i + 1]] for i in range(10)]
    zeros = lambda n: jnp.zeros((D_MODEL, n), w_in.dtype)
    cols = [z, xbc, q, g, k, v, qi, ki, zeros(64), dt, wi, zeros(LANE - SSM_HEADS - IDX_HEADS)]
    return jnp.concatenate(cols, axis=1).astype(BF16)


def _in_proj(x2d, norm_g, w_packed, tm):
    n = x2d.shape[0]
    assert n % tm == 0
    bf = lambda c: jax.ShapeDtypeStruct((n, c), BF16)
    ff = lambda c: jax.ShapeDtypeStruct((n, c), F32)
    clip = lambda j, lo, hi: jnp.clip(j - lo, 0, hi - lo - 1)
    return pl.pallas_call(
        _inproj_kernel,
        out_shape=(bf(SSM_INNER), bf(CONV_DIM), bf(ATTN_WIDTH), bf(2 * D_MODEL),
                   ff(128), ff(128), bf(512), ff(IDX_DIM), ff(128), bf(128), bf(128), bf(IDX_DIM)),
        grid=(n // tm, PROJ_TILES),
        in_specs=[
            pl.BlockSpec((tm, D_MODEL), lambda i, j: (i, 0)),
            pl.BlockSpec((1, D_MODEL), lambda i, j: (0, 0)),
            pl.BlockSpec((D_MODEL, PROJ_TILE), lambda i, j: (0, j)),
        ],
        out_specs=(
            pl.BlockSpec((tm, PROJ_TILE), lambda i, j: (i, clip(j, 0, 2))),
            pl.BlockSpec((tm, PROJ_TILE), lambda i, j: (i, clip(j, 2, 5))),
            pl.BlockSpec((tm, PROJ_TILE), lambda i, j: (i, 0)),
            pl.BlockSpec((tm, PROJ_TILE), lambda i, j: (i, clip(j, 6, 8))),
            pl.BlockSpec((tm, 128), lambda i, j: (i, 0)),
            pl.BlockSpec((tm, 128), lambda i, j: (i, 0)),
            pl.BlockSpec((tm, 512), lambda i, j: (i, 0)),
            pl.BlockSpec((tm, IDX_DIM), lambda i, j: (i, 0)),
            pl.BlockSpec((tm, 128), lambda i, j: (i, 0)),
            pl.BlockSpec((tm, 128), lambda i, j: (i, 0)),
            pl.BlockSpec((tm, 128), lambda i, j: (i, 0)),
            pl.BlockSpec((tm, IDX_DIM), lambda i, j: (i, 0)),
        ),
        scratch_shapes=[pltpu.VMEM((tm, D_MODEL), BF16)],
        compiler_params=pltpu.CompilerParams(
            dimension_semantics=("arbitrary", "arbitrary"), vmem_limit_bytes=VMEM_LIMIT),
        name="in_proj",
    )(x2d, norm_g.reshape(1, D_MODEL), w_packed)


def _ssd_kernel(*refs, L, cps, nsteps, has_state):
    if has_state:
        (xbc_ref, z_ref, dtw_ref, cs_ref, s0_ref, shift_ref, convw_ref, convb_ref, dtb_ref, a_ref,
         dx_ref, ng_ref, e3p_ref, e3l_ref, tri_ref, bd_ref, yn_ref, cnew_ref, sout_ref,
         xe_ref, st) = refs
    else:
        (xbc_ref, z_ref, dtw_ref, shift_ref, convw_ref, convb_ref, dtb_ref, a_ref,
         dx_ref, ng_ref, e3p_ref, e3l_ref, tri_ref, bd_ref, yn_ref, cnew_ref, sout_ref,
         xe_ref, st) = refs
    c = pl.program_id(1)
    G, R, N, P = SSM_GROUPS, SSM_HEADS_PER_GROUP, SSM_STATE, SSM_HEADDIM
    GW = R * P
    RL = R * L
    LB = cps * L

    @pl.when(c == 0)
    def _():
        xe_ref[LB:, :] = jnp.zeros((xe_ref.shape[0] - LB, CONV_DIM), BF16)
        if has_state:
            xe_ref[LB:LB + CONV_CARRY, :] = cs_ref[0]
            st[...] = s0_ref[0].T
        else:
            st[...] = jnp.zeros(st.shape, F32)

    xe_ref[0:LB, :] = xbc_ref[0]
    shifted = jnp.dot(shift_ref[0], xe_ref[...], preferred_element_type=F32)
    w = convw_ref[...]
    conv = convb_ref[...] + shifted[2 * LB:3 * LB, :] * w[0:1, :]
    conv = conv + shifted[LB:2 * LB, :] * w[1:2, :]
    conv = conv + shifted[0:LB, :] * w[2:3, :]
    conv = conv + xbc_ref[0].astype(F32) * w[3:4, :]
    act = _silu(conv)

    @pl.when(c == nsteps - 1)
    def _():
        cnew_ref[0] = xbc_ref[0, LB - CONV_CARRY:LB, :].astype(F32)[
            CONV_CARRY - (CONV_WIDTH - 1):CONV_CARRY, :]

    if nsteps > 1:
        xe_ref[LB:LB + CONV_CARRY, :] = xe_ref[LB - CONV_CARRY:LB, :]

    u = dtw_ref[0] + dtb_ref[...]
    dt = jnp.maximum(u, 0.0) + jnp.log1p(jnp.exp(-jnp.abs(u)))
    a = dt * a_ref[...]
    ac3 = jnp.dot(tri_ref[...], _split3(a), preferred_element_type=F32)
    a_cum = ac3[:, 0:LANE] + ac3[:, LANE:2 * LANE] + ac3[:, 2 * LANE:]

    ex = jnp.dot(_split3(jnp.concatenate([a_cum, dt], axis=0)), e3p_ref[...],
                 preferred_element_type=F32)
    if L == P:
        acl_all = ex[:LB]
    else:
        acl_all = jnp.dot(_split3(a_cum), e3l_ref[...], preferred_element_type=F32)
    zs = _silu(z_ref[0].astype(F32))
    row = lax.broadcasted_iota(I32, (L, RL), 0)
    scol = lax.broadcasted_iota(I32, (L, RL), 1) & (L - 1)

    for sub in range(cps):
        rows = slice(sub * L, (sub + 1) * L)
        xs = act[rows, :SSM_INNER]
        bmat = act[rows, SSM_INNER:SSM_INNER + G * N].astype(BF16)
        cmat = act[rows, SSM_INNER + G * N:].astype(BF16)
        acx = ex[rows]
        dtx = ex[LB + sub * L:LB + (sub + 1) * L]
        acl = acl_all[rows]
        xdt = xs * dtx
        xdt_b = xdt.astype(BF16)
        eacx = jnp.exp(acx)
        alast = acx[L - 1:L, :]
        xw = (xdt * jnp.exp(alast - acx)).astype(BF16)
        ealast = jnp.exp(alast)

        ys = []
        for g in range(G):
            sl = slice(g * GW, (g + 1) * GW)
            b_g = bmat[:, g * N:(g + 1) * N]
            c_g = cmat[:, g * N:(g + 1) * N]
            acl_g = acl[:, g * RL:(g + 1) * RL]
            acs_g = jnp.sum(jnp.where(scol == row, acl_g, 0.0), axis=0, keepdims=True)
            decay = jnp.where(scol <= row, jnp.exp(acl_g - acs_g), 0.0)
            cb = lax.dot_general(c_g, jnp.concatenate([b_g] * R, axis=0), NT_DIMS,
                                 preferred_element_type=F32)
            m_g = (cb * decay).astype(BF16)
            xbd = jnp.concatenate([xdt_b[:, sl]] * R, axis=0) * bd_ref[...]
            y_in = jnp.dot(m_g, xbd, preferred_element_type=F32)
            st_g = st[:, sl]
            y_st = jnp.dot(c_g, st_g.astype(BF16), preferred_element_type=F32)
            ys.append(y_in + eacx[:, sl] * y_st)
            upd = lax.dot_general(b_g, xw[:, sl], TN_DIMS, preferred_element_type=F32)
            st[:, sl] = st_g * ealast[:, sl] + upd
        y = jnp.concatenate(ys, axis=1) + dx_ref[...] * xs

        yz = y * zs[rows]
        ms = jnp.mean(yz * yz, axis=-1, keepdims=True)
        yn_ref[0, rows, :] = ((yz * lax.rsqrt(ms + EPS)) * ng_ref[...]).astype(BF16)

    @pl.when(c == nsteps - 1)
    def _():
        sout_ref[0] = st[...].T


def _shift_matrices(L, conv_rows):
    out = np.zeros((2, 3 * L, conv_rows), np.float32)
    for k in range(CONV_WIDTH - 1):
        for t in range(L):
            src = t - (k + 1)
            if src >= 0:
                out[:, k * L + t, src] = 1.0
            else:
                out[1, k * L + t, L + CONV_CARRY + src] = 1.0
                j = CONV_WIDTH - 1 + src
                for piece in range(3):
                    out[0, k * L + t, L + 3 * piece + j] = 1.0
    return jnp.asarray(out, BF16)


def _expand_matrix(width_per_head):
    r = np.arange(3 * LANE)[:, None] % LANE
    c = np.arange(SSM_HEADS * width_per_head)[None, :] // width_per_head
    return jnp.asarray((r == c).astype(np.float32), BF16)


def _ssd(xbc, z, dtw, conv_state, ssm_state, conv_w, conv_b, dt_bias, a_log, d_skip, ssm_norm_g):
    b, t, _ = xbc.shape
    L = min(CHUNK, t)
    nc = t // L
    assert t % L == 0 and L & (L - 1) == 0
    has_state = conv_state is not None
    pad = lambda v: jnp.pad(v.astype(F32), (0, LANE - SSM_HEADS)).reshape(1, LANE)
    a_neg = pad(-jnp.exp(a_log.astype(F32)))
    cps = SSD_CHUNKS_PER_STEP if nc % SSD_CHUNKS_PER_STEP == 0 else 1
    nsteps = nc // cps
    lb = cps * L
    conv_rows = _round_up(lb + CONV_CARRY, LANE)
    tri = jnp.asarray(np.kron(np.eye(cps), np.tril(np.ones((L, L)))).astype(np.float32), BF16)
    rl = SSM_HEADS_PER_GROUP * L
    gw = SSM_HEADS_PER_GROUP * SSM_HEADDIM
    bd = jnp.asarray((np.arange(rl)[:, None] // L == np.arange(gw)[None, :] // SSM_HEADDIM)
                     .astype(np.float32), BF16)
    const = lambda shape: pl.BlockSpec(shape, lambda i, j: (0,) * len(shape))
    tok = lambda w: pl.BlockSpec((1, lb, w), lambda i, j: (i, j, 0))
    per_b = lambda s: pl.BlockSpec((1,) + s, lambda i, j: (i, 0, 0))
    ins = [xbc, z, dtw]
    specs = [tok(CONV_DIM), tok(SSM_INNER), tok(LANE)]
    if has_state:
        triples = _split3(conv_state.astype(F32))
        ins += [_pad_axis(triples, 1, CONV_CARRY), ssm_state.reshape(b, SSM_INNER, SSM_STATE)]
        specs += [per_b((CONV_CARRY, CONV_DIM)), per_b((SSM_INNER, SSM_STATE))]
    first = 0 if has_state else 1
    ins += [_shift_matrices(lb, conv_rows)]
    specs += [pl.BlockSpec((1, 3 * lb, conv_rows),
                           lambda i, j: (jnp.where(j == 0, first, 1), 0, 0))]
    ins += [conv_w, conv_b.reshape(1, CONV_DIM), pad(dt_bias), a_neg,
            jnp.repeat(d_skip.astype(F32), SSM_HEADDIM).reshape(1, SSM_INNER),
            ssm_norm_g.reshape(1, SSM_INNER), _expand_matrix(SSM_HEADDIM), _expand_matrix(L),
            tri, bd]
    specs += [const((CONV_WIDTH, CONV_DIM)), const((1, CONV_DIM)), const((1, LANE)),
              const((1, LANE)), const((1, SSM_INNER)), const((1, SSM_INNER)),
              const((3 * LANE, SSM_INNER)), const((3 * LANE, SSM_HEADS * L)),
              const((lb, lb)), const((rl, gw))]
    yn, cnew, sout = pl.pallas_call(
        functools.partial(_ssd_kernel, L=L, cps=cps, nsteps=nsteps, has_state=has_state),
        out_shape=(jax.ShapeDtypeStruct((b, t, SSM_INNER), BF16),
                   jax.ShapeDtypeStruct((b, CONV_WIDTH - 1, CONV_DIM), F32),
                   jax.ShapeDtypeStruct((b, SSM_INNER, SSM_STATE), F32)),
        grid=(b, nsteps),
        in_specs=specs,
        out_specs=(tok(SSM_INNER), per_b((CONV_WIDTH - 1, CONV_DIM)),
                   per_b((SSM_INNER, SSM_STATE))),
        scratch_shapes=[pltpu.VMEM((conv_rows, CONV_DIM), BF16),
                        pltpu.VMEM((SSM_STATE, SSM_INNER), F32)],
        compiler_params=pltpu.CompilerParams(
            dimension_semantics=("arbitrary", "arbitrary"), vmem_limit_bytes=VMEM_LIMIT),
        name="ssd",
    )(*ins)
    return yn, cnew, sout.reshape(b, SSM_HEADS, SSM_HEADDIM, SSM_STATE)


def _dsa_kernel(q_ref, qi_ref, w_ref, eq_ref, kidx_ref, k_ref, v_ref, o_ref,
                qs_ref, qx_ref, key_ref, kx_ref, vx_ref, m_ref, acc_ref, mlim_ref, s_ref,
                *, past_len, l_valid, topk, nq):
    qb = pl.program_id(1)
    TQ, TK = LANE, KEY_TILE
    W = ATTN_REP * nq
    wide = nq == LANE

    lane = lax.broadcasted_iota(I32, (1, TQ), 1)
    pos = past_len + qb * nq + (lane & (nq - 1))
    n_adm = jnp.minimum(((pos >> CHUNK_SHIFT) + 1) * CHUNK, l_valid)
    k_eff = jnp.minimum(n_adm, topk)
    last_pos = past_len + qb * nq + nq - 1
    n_max = jnp.minimum(((last_pos >> CHUNK_SHIFT) + 1) * CHUNK, l_valid)
    nt = (n_max + TK - 1) // TK
    ntp = 2 * ((nt + 1) // 2)
    lp = key_ref.shape[0]

    @pl.when(qb == 0)
    def _():
        kx_ref[:, LANE:] = k_ref[0]

        def v_tile(t, carry):
            s0 = pl.multiple_of(t * TK, TK)
            v_t = v_ref[0, pl.ds(s0, TK), :].astype(F32).T
            for g in range(ATTN_KV_HEADS):
                vx_ref[g, 0:HEAD_DIM, pl.ds(s0, TK)] = (
                    v_t[g * HEAD_DIM:(g + 1) * HEAD_DIM, :].astype(BF16))
            return carry

        lax.fori_loop(0, lp // TK, v_tile, 0)
        tail_row = lax.broadcasted_iota(I32, (PV_ROWS - HEAD_DIM, lp), 0)
        for g in range(ATTN_KV_HEADS):
            vx_ref[g, HEAD_DIM:, :] = jnp.where(tail_row == 0, 1.0, 0.0).astype(BF16)

    if wide:
        for h in range(IDX_HEADS):
            qs_ref[h * nq:(h + 1) * nq, :] = qi_ref[0, :, h * IDX_DIM:(h + 1) * IDX_DIM]
        w_t = w_ref[0].T[SSM_HEADS:SSM_HEADS + IDX_HEADS, :] * IDX_SCALE
    else:
        qs_ref[...] = qi_ref[0]
        w_row = w_ref[0] * IDX_SCALE

    def score_tile(t, carry):
        s0 = pl.multiple_of(t * TK, TK)
        d = lax.dot_general(kidx_ref[0, pl.ds(s0, TK), :], qs_ref[0:IDX_HEADS * nq, :],
                            NT_DIMS, preferred_element_type=F32)
        if wide:
            sc = w_t[0:1, :] * jnp.maximum(d[:, 0:nq], 0.0)
            for h in range(1, IDX_HEADS):
                sc = sc + w_t[h:h + 1, :] * jnp.maximum(d[:, h * nq:(h + 1) * nq], 0.0)
        else:
            sc = w_row * jnp.maximum(d, 0.0)
            for shift in (64, 32, 16):
                sc = sc + pltpu.roll(sc, shift, axis=1)
        sc = sc + 0.0
        bits = pltpu.bitcast(sc, I32)
        key = jnp.where(bits < 0, bits ^ np.int32(0x7FFFFFFF), bits)
        s_idx = s0 + lax.broadcasted_iota(I32, (TK, TQ), 0)
        key_ref[pl.ds(s0, TK), :] = jnp.where(s_idx < n_adm, key, INT_MIN)
        return carry

    lax.fori_loop(0, nt, score_tile, 0)

    def count(pred):
        def body(t, acc):
            s0 = pl.multiple_of(t * TK, TK)
            blk = key_ref[pl.ds(s0, TK), :]
            s_idx = s0 + lax.broadcasted_iota(I32, (TK, TQ), 0)
            hit = jnp.where(pred(blk, s_idx), 1, 0).astype(I32)
            return acc + jnp.sum(hit.reshape(TK // 8, 8, TQ), axis=0)
        acc = lax.fori_loop(0, nt, body, jnp.zeros((8, TQ), I32))
        return jnp.sum(acc, axis=0, keepdims=True)

    def bit_step(i, prefix):
        cand = prefix | jnp.left_shift(jnp.int32(1), 31 - i)
        cand_s = cand ^ INT_MIN
        cnt = count(lambda blk, s_idx: blk >= cand_s)
        return jnp.where(cnt >= k_eff, cand, prefix)

    thr = lax.fori_loop(0, 32, bit_step, jnp.zeros((1, TQ), I32)) ^ INT_MIN

    n_gt = count(lambda blk, s_idx: blk > thr)
    n_eq = count(lambda blk, s_idx: blk == thr)
    need = k_eff - n_gt
    mlim_ref[...] = jnp.full((1, TQ), 2 ** 30, I32)

    @pl.when(jnp.max(n_eq - need) > 0)
    def _():
        def idx_step(i, prefix):
            cand = prefix | jnp.left_shift(jnp.int32(1), 14 - i)
            cnt = count(lambda blk, s_idx: (blk == thr) & (s_idx < cand))
            return jnp.where(cnt < need, cand, prefix)
        mlim_ref[...] = lax.fori_loop(0, 15, idx_step, jnp.zeros((1, TQ), I32))

    mlim = mlim_ref[...]

    def bias_tile(t, carry):
        s0 = pl.multiple_of(t * TK, TK)
        blk = key_ref[pl.ds(s0, TK), :]
        s_idx = s0 + lax.broadcasted_iota(I32, (TK, TQ), 0)
        sel = (blk > thr) | ((blk == thr) & (s_idx <= mlim))
        kx_ref[pl.ds(s0, TK), 0:LANE] = jnp.where(sel, 0.0, NEG_BIG).astype(BF16)
        return carry

    lax.fori_loop(0, nt, bias_tile, 0)

    @pl.when(nt < ntp)
    def _():
        kx_ref[pl.ds(pl.multiple_of(nt * TK, TK), TK), 0:LANE] = jnp.full((TK, LANE), NEG_BIG,
                                                                           BF16)

    for g in range(ATTN_KV_HEADS):
        lanes = slice(LANE + g * HEAD_DIM, LANE + (g + 1) * HEAD_DIM)
        qx_ref[g, :, 0:LANE] = eq_ref[...]
        qx_ref[g, :, LANE:] = jnp.zeros((W, LANE), BF16)
        if wide:
            for r in range(ATTN_REP):
                hq = g * ATTN_REP + r
                qx_ref[g, r * nq:(r + 1) * nq, lanes] = (
                    q_ref[0, :, hq * HEAD_DIM:(hq + 1) * HEAD_DIM])
        else:
            qx_ref[g, :, lanes] = q_ref[0, g]
    m_ref[...] = jnp.full(m_ref.shape, -jnp.inf, F32)
    acc_ref[...] = jnp.zeros(acc_ref.shape, F32)

    def scores(t, slot):
        s0 = pl.multiple_of(jnp.minimum(t, ntp - 1) * TK, TK)
        k_tile = kx_ref[pl.ds(s0, TK), :]
        for g in range(ATTN_KV_HEADS):
            s_ref[g, slot] = lax.dot_general(k_tile, qx_ref[g], NT_DIMS,
                                             preferred_element_type=F32)

    def accumulate(t, slot):
        s0 = pl.multiple_of(t * TK, TK)
        for g in range(ATTN_KV_HEADS):
            s = s_ref[g, slot]
            m_prev = m_ref[g]
            m_new = jnp.maximum(m_prev, jnp.max(s, axis=0, keepdims=True))
            p = jnp.exp2(s - m_new)
            acc_ref[g] = jnp.exp2(m_prev - m_new) * acc_ref[g] + jnp.dot(
                vx_ref[g, :, pl.ds(s0, TK)], p.astype(BF16), preferred_element_type=F32)
            m_ref[g] = m_new

    scores(0, 0)

    def attn_pair(u, carry):
        scores(2 * u + 1, 1)
        accumulate(2 * u, 0)
        scores(2 * u + 2, 0)
        accumulate(2 * u + 1, 1)
        return carry

    lax.fori_loop(0, ntp // 2, attn_pair, 0)
    for g in range(ATTN_KV_HEADS):
        acc = acc_ref[g]
        o_t = acc[0:HEAD_DIM, :] * (1.0 / acc[HEAD_DIM:HEAD_DIM + 1, :])
        if wide:
            for r in range(0, ATTN_REP, 2):
                hq = g * ATTN_REP + r
                pair = jnp.concatenate([o_t[:, r * nq:(r + 1) * nq],
                                        o_t[:, (r + 1) * nq:(r + 2) * nq]], axis=0)
                o_ref[0, :, hq * HEAD_DIM:(hq + 2) * HEAD_DIM] = pair.T.astype(BF16)
        else:
            o_rq = jnp.concatenate([o_t, o_t], axis=0).T[:, 0:HEAD_DIM].astype(BF16)
            for r in range(ATTN_REP):
                hq = g * ATTN_REP + r
                o_ref[0, :, hq * HEAD_DIM:(hq + 1) * HEAD_DIM] = o_rq[r * nq:(r + 1) * nq, :]


def _dsa(q, qi, w, kidx_all, k_all, v_all, past_len, l_valid, t):
    b = kidx_all.shape[0]
    lp = kidx_all.shape[1]
    nq = Q_BLOCK if t % Q_BLOCK == 0 else SMALL_Q_BLOCK
    assert t % nq == 0 and lp % (2 * KEY_TILE) == 0 and lp < 2 ** 15
    topk = min(TOPK_MAX, l_valid // 4)
    width = ATTN_REP * nq
    if nq == Q_BLOCK:
        qblk = lambda c: pl.BlockSpec((1, nq, c), lambda i, j: (i, j, 0))
        q_specs = [qblk(ATTN_WIDTH), qblk(IDX_HEADS * IDX_DIM), qblk(LANE)]
    else:
        assert t == nq
        q_specs = [pl.BlockSpec((1, ATTN_KV_HEADS, width, HEAD_DIM), lambda i, j: (i, 0, 0, 0)),
                   pl.BlockSpec((1, IDX_HEADS * nq, IDX_DIM), lambda i, j: (i, 0, 0)),
                   pl.BlockSpec((1, 1, LANE), lambda i, j: (i, 0, 0))]
    slot_onehot = jnp.asarray(
        (np.arange(width)[:, None] % nq == np.arange(LANE)[None, :]).astype(np.float32), BF16)
    per_stream = lambda c: pl.BlockSpec((1, lp, c), lambda i, j: (i, 0, 0))
    return pl.pallas_call(
        functools.partial(_dsa_kernel, past_len=past_len, l_valid=l_valid, topk=topk, nq=nq),
        out_shape=jax.ShapeDtypeStruct((b, t, ATTN_WIDTH), BF16),
        grid=(b, t // nq),
        in_specs=q_specs + [pl.BlockSpec((width, LANE), lambda i, j: (0, 0)),
                            per_stream(IDX_DIM), per_stream(LANE), per_stream(LANE)],
        out_specs=pl.BlockSpec((1, nq, ATTN_WIDTH), lambda i, j: (i, j, 0)),
        scratch_shapes=[pltpu.VMEM((width, IDX_DIM), BF16),
                        pltpu.VMEM((ATTN_KV_HEADS, width, 2 * LANE), BF16),
                        pltpu.VMEM((lp, LANE), I32),
                        pltpu.VMEM((lp, 2 * LANE), BF16),
                        pltpu.VMEM((ATTN_KV_HEADS, PV_ROWS, lp), BF16),
                        pltpu.VMEM((ATTN_KV_HEADS, 1, width), F32),
                        pltpu.VMEM((ATTN_KV_HEADS, PV_ROWS, width), F32),
                        pltpu.VMEM((1, LANE), I32),
                        pltpu.VMEM((ATTN_KV_HEADS, 2, KEY_TILE, width), F32)],
        compiler_params=pltpu.CompilerParams(
            dimension_semantics=("arbitrary", "arbitrary"), vmem_limit_bytes=VMEM_LIMIT),
        name="dsa",
    )(q, qi, w, slot_onehot, kidx_all, k_all, v_all)


def _merge_kernel(x_ref, yn_ref, o_ref, gate_ref, wa_ref, wb_ref, wo_ref, out_ref):
    br_a = jnp.dot(yn_ref[...], wa_ref[...], preferred_element_type=F32)
    br_b = jnp.dot(o_ref[...], wb_ref[...], preferred_element_type=F32)
    gates = gate_ref[...].astype(F32)
    merged = _sigmoid(gates[:, :D_MODEL]) * br_a + _sigmoid(gates[:, D_MODEL:]) * br_b
    out_ref[...] = x_ref[...] + jnp.dot(merged.astype(BF16), wo_ref[...],
                                        preferred_element_type=F32)


def _merge(x2d, yn, o, gate, w_ssm_out, w_attn_out, w_o, tm):
    n = x2d.shape[0]
    assert n % tm == 0
    row = lambda w: pl.BlockSpec((tm, w), lambda i: (i, 0))
    full = lambda r, c: pl.BlockSpec((r, c), lambda i: (0, 0))
    return pl.pallas_call(
        _merge_kernel,
        out_shape=jax.ShapeDtypeStruct((n, D_MODEL), F32),
        grid=(n // tm,),
        in_specs=[row(D_MODEL), row(SSM_INNER), row(ATTN_WIDTH), row(2 * D_MODEL),
                  full(SSM_INNER, D_MODEL), full(ATTN_WIDTH, D_MODEL), full(D_MODEL, D_MODEL)],
        out_specs=row(D_MODEL),
        compiler_params=pltpu.CompilerParams(
            dimension_semantics=("arbitrary",), vmem_limit_bytes=VMEM_LIMIT),
        name="merge",
    )(x2d, yn, o, gate, w_ssm_out.astype(BF16), w_attn_out.astype(BF16), w_o.astype(BF16))


def _moe_kernel(x_ref, ng_ref, wr_ref, br_ref, w1_ref, w3_ref, w2_ref, fg_ref, out_ref,
                h_ref, comb_ref, acc_ref, *, n_steps):
    e_step = pl.program_id(1)
    tm = x_ref.shape[0]

    @pl.when(e_step == 0)
    def _():
        x = x_ref[...]
        ms = jnp.mean(x * x, axis=-1, keepdims=True)
        h = ((x * lax.rsqrt(ms + EPS)) * ng_ref[...]).astype(BF16)
        h_ref[...] = h
        logits = jnp.dot(h, wr_ref[...], preferred_element_type=F32) + br_ref[...]
        lane_i = lax.broadcasted_iota(I32, (tm, LANE), 1)
        lane = lane_i.astype(F32)
        first = lambda hit: jnp.min(jnp.where(hit, lane, float(LANE)), axis=1, keepdims=True)
        is_g = (lane_i >= N_EXPERTS) & (lane_i < N_EXPERTS + MOE_GROUPS)
        glog = jnp.where(is_g, logits, -jnp.inf)
        gmax = jnp.max(glog, axis=1, keepdims=True)
        gsel = first(glog == gmax) - float(N_EXPERTS)
        p_group = 1.0 / jnp.sum(jnp.exp(glog - gmax), axis=1, keepdims=True)
        grp = (lane_i >> EXPERT_GROUP_SHIFT).astype(F32)
        in_grp = (lane_i < N_EXPERTS) & (grp == gsel)
        el = jnp.where(in_grp, logits, -jnp.inf)
        v1 = jnp.max(el, axis=1, keepdims=True)
        i1 = first(el == v1)
        el2 = jnp.where(lane == i1, -jnp.inf, el)
        v2 = jnp.max(el2, axis=1, keepdims=True)
        i2 = first(el2 == v2)
        e21 = jnp.exp(v2 - v1)
        den = 1.0 + e21
        comb_ref[...] = (jnp.where(lane == i1, (1.0 / den) * p_group, 0.0)
                         + jnp.where(lane == i2, (e21 / den) * p_group, 0.0))
        acc_ref[...] = jnp.zeros(acc_ref.shape, F32)

    h = h_ref[...]
    comb = comb_ref[...]
    lane = lax.broadcasted_iota(I32, (tm, LANE), 1)
    for k in range(EXPERTS_PER_STEP):
        e = e_step * EXPERTS_PER_STEP + k
        a1 = jnp.dot(h, w1_ref[k], preferred_element_type=F32)
        a3 = jnp.dot(h, w3_ref[k], preferred_element_type=F32)
        he = (_silu(a1) * a3).astype(BF16)
        ye = jnp.dot(he, w2_ref[k], preferred_element_type=F32)
        wcol = jnp.sum(jnp.where(lane == e, comb, 0.0), axis=1, keepdims=True)
        acc_ref[...] += wcol * ye

    @pl.when(e_step == n_steps - 1)
    def _():
        y = x_ref[...] + acc_ref[...]
        ms = jnp.mean(y * y, axis=-1, keepdims=True)
        out_ref[...] = (y * lax.rsqrt(ms + EPS)) * fg_ref[...]


def _moe(x2d, norm_g, w_router_group, b_router_group, w_router_expert, b_router_expert,
         w1, w3, w2, norm_final_g, tm):
    n = x2d.shape[0]
    assert n % tm == 0
    n_steps = N_EXPERTS // EXPERTS_PER_STEP
    padw = LANE - N_EXPERTS - MOE_GROUPS
    wr = jnp.concatenate([w_router_expert, w_router_group,
                          jnp.zeros((D_MODEL, padw), F32)], axis=1).astype(BF16)
    br = jnp.concatenate([b_router_expert, b_router_group,
                          jnp.zeros((padw,), F32)]).astype(F32).reshape(1, LANE)
    row = pl.BlockSpec((tm, D_MODEL), lambda i, e: (i, 0))
    const = lambda r, c: pl.BlockSpec((r, c), lambda i, e: (0, 0))
    return pl.pallas_call(
        functools.partial(_moe_kernel, n_steps=n_steps),
        out_shape=jax.ShapeDtypeStruct((n, D_MODEL), F32),
        grid=(n // tm, n_steps),
        in_specs=[row, const(1, D_MODEL), const(D_MODEL, LANE), const(1, LANE),
                  pl.BlockSpec((EXPERTS_PER_STEP, D_MODEL, EXPERT_FF), lambda i, e: (e, 0, 0)),
                  pl.BlockSpec((EXPERTS_PER_STEP, D_MODEL, EXPERT_FF), lambda i, e: (e, 0, 0)),
                  pl.BlockSpec((EXPERTS_PER_STEP, EXPERT_FF, D_MODEL), lambda i, e: (e, 0, 0)),
                  const(1, D_MODEL)],
        out_specs=row,
        scratch_shapes=[pltpu.VMEM((tm, D_MODEL), BF16), pltpu.VMEM((tm, LANE), F32),
                        pltpu.VMEM((tm, D_MODEL), F32)],
        compiler_params=pltpu.CompilerParams(
            dimension_semantics=("arbitrary", "arbitrary"), vmem_limit_bytes=VMEM_LIMIT),
        name="moe",
    )(x2d, norm_g.reshape(1, D_MODEL), wr, br, w1.astype(BF16), w3.astype(BF16),
      w2.astype(BF16), norm_final_g.reshape(1, D_MODEL))


def _token_tile(n):
    for tm in (1024, 512, 256, 128, 64, 32, 16, 8):
        if n % tm == 0:
            return tm
    raise ValueError(f"token count {n} is not a multiple of 8")


def _pad_axis(a, axis, size):
    if a.shape[axis] == size:
        return a
    widths = [(0, 0)] * a.ndim
    widths[axis] = (0, size - a.shape[axis])
    return jnp.pad(a, widths)


def _round_up(n, m):
    return (n + m - 1) // m * m


def _stream_step(x, conv_state, ssm_state, k_past, v_past, kidx_past, w_packed, p):
    b, t, d = x.shape
    n = b * t
    tm = _token_tile(n)
    x2d = x.reshape(n, d)
    z, xbc, q, gate, k, v, qi, ki, dtw, k_b, v_b, ki_b = _in_proj(
        x2d, p["norm_mix_g"], w_packed, tm)

    yn, conv_new, ssm_new = _ssd(
        xbc.reshape(b, t, CONV_DIM), z.reshape(b, t, SSM_INNER), dtw.reshape(b, t, LANE),
        conv_state, ssm_state, p["conv_w"], p["conv_b"], p["dt_bias"], p["A_log"],
        p["D_skip"], p["ssm_norm_g"])

    kv_w = ATTN_KV_HEADS * HEAD_DIM
    k_all, v_all, kidx_all = (k_b.reshape(b, t, kv_w), v_b.reshape(b, t, kv_w),
                              ki_b.reshape(b, t, IDX_DIM))
    past_len = 0
    if k_past is not None:
        past_len = k_past.shape[1]
        k_all = jnp.concatenate([k_past.astype(BF16).reshape(b, past_len, kv_w), k_all], axis=1)
        v_all = jnp.concatenate([v_past.astype(BF16).reshape(b, past_len, kv_w), v_all], axis=1)
        kidx_all = jnp.concatenate([kidx_past.astype(BF16), kidx_all], axis=1)
    l_valid = past_len + t
    lp = _round_up(l_valid, 2 * KEY_TILE)
    if t % Q_BLOCK == 0:
        q_in = (q.reshape(b, t, ATTN_WIDTH), qi.reshape(b, t, IDX_HEADS * IDX_DIM),
                dtw.reshape(b, t, LANE))
    else:
        q_in = (q.reshape(b, t, ATTN_KV_HEADS, ATTN_REP, HEAD_DIM).transpose(0, 2, 3, 1, 4)
                .reshape(b, ATTN_KV_HEADS, ATTN_REP * t, HEAD_DIM),
                qi.reshape(b, t, IDX_HEADS, IDX_DIM).transpose(0, 2, 1, 3)
                .reshape(b, IDX_HEADS * t, IDX_DIM),
                dtw.reshape(b, t, LANE)[:, :, SSM_HEADS:SSM_HEADS + IDX_HEADS]
                .transpose(0, 2, 1).reshape(b, 1, IDX_HEADS * t))
    o = _dsa(*q_in, _pad_axis(kidx_all, 1, lp), _pad_axis(k_all, 1, lp), _pad_axis(v_all, 1, lp),
             past_len, l_valid, t)
    o = o.reshape(n, ATTN_WIDTH)

    x1 = _merge(x2d, yn.reshape(n, SSM_INNER), o, gate, p["w_ssm_out"], p["w_attn_out"],
                p["w_o"], min(tm, MERGE_TILE))
    y = _moe(x1, p["norm_moe_g"], p["w_router_group"], p["b_router_group"],
             p["w_router_expert"], p["b_router_expert"], p["w1"], p["w3"], p["w2"],
             p["norm_final_g"], tm)
    return (y.reshape(b, t, d), k.reshape(b, t, ATTN_KV_HEADS, HEAD_DIM),
            v.reshape(b, t, ATTN_KV_HEADS, HEAD_DIM), ki.reshape(b, t, IDX_DIM), conv_new, ssm_new)


def kernel(x_prompt, x_sample, cache_k, cache_v, cache_kidx, state_conv, state_ssm, norm_mix_g, w_in, conv_w, conv_b, dt_bias, A_log, D_skip, ssm_norm_g, w_ssm_out, w_attn_out, w_o, norm_moe_g, w_router_group, b_router_group, w_router_expert, b_router_expert, w1, w3, w2, norm_final_g):
    p = dict(norm_mix_g=norm_mix_g, conv_w=conv_w, conv_b=conv_b, dt_bias=dt_bias, A_log=A_log,
             D_skip=D_skip, ssm_norm_g=ssm_norm_g, w_ssm_out=w_ssm_out, w_attn_out=w_attn_out,
             w_o=w_o, norm_moe_g=norm_moe_g, w_router_group=w_router_group,
             b_router_group=b_router_group, w_router_expert=w_router_expert,
             b_router_expert=b_router_expert, w1=w1, w3=w3, w2=w2, norm_final_g=norm_final_g)
    w_packed = _pack_w_in(w_in)
    yp, kp, vp, kip, cp, sp = _stream_step(x_prompt, None, None, None, None, None, w_packed, p)
    ys, ksn, vsn, kisn, csn, ssn = _stream_step(x_sample, state_conv, state_ssm, cache_k,
                                                cache_v, cache_kidx, w_packed, p)
    return (yp, ys, kp, vp, kip, cp, sp, ksn, vsn, kisn, csn, ssn)
```

```python
import functools

import numpy as np
import jax
import jax.numpy as jnp
from jax import lax
from jax.experimental import pallas as pl
from jax.experimental.pallas import tpu as pltpu

F32 = jnp.float32
BF16 = jnp.bfloat16
I32 = jnp.int32

D_MODEL = 1024
CHUNK = 64
CHUNK_SHIFT = 6
EPS = 1e-6
SSM_INNER = 2048
SSM_HEADDIM = 64
SSM_HEADS = 32
SSM_GROUPS = 4
SSM_HEADS_PER_GROUP = SSM_HEADS // SSM_GROUPS
SSM_STATE = 128
CONV_WIDTH = 4
CONV_DIM = SSM_INNER + 2 * SSM_GROUPS * SSM_STATE
CONV_CARRY = 16
SSD_CHUNKS_PER_STEP = 2
ATTN_HEADS = 16
ATTN_KV_HEADS = 2
HEAD_DIM = 64
ATTN_REP = ATTN_HEADS // ATTN_KV_HEADS
ATTN_WIDTH = ATTN_HEADS * HEAD_DIM
IDX_HEADS = 8
IDX_DIM = 64
IDX_SCALE = (IDX_HEADS * IDX_DIM) ** -0.5
TOPK_MAX = 256
MOE_GROUPS = 4
EXPERTS_PER_GROUP = 8
EXPERT_GROUP_SHIFT = 3
N_EXPERTS = 32
EXPERT_FF = 256
N_BRANCHES = 2

LANE = 128
VMEM_LIMIT = 56 * 1024 * 1024
PROJ_TILE = 1024
PROJ_TILES = 9
Q_BLOCK = 128
SMALL_Q_BLOCK = 16
KEY_TILE = 256
EXPERTS_PER_STEP = 4
MOE_ROW_BLOCK = 128
MERGE_TILE = 512
INT_MIN = np.int32(-2 ** 31)
NEG_BIG = -1e30
Q_SCALE = HEAD_DIM ** -0.5 * float(np.log2(np.e))
PV_ROWS = 80

NT_DIMS = (((1,), (1,)), ((), ()))
TN_DIMS = (((0,), (0,)), ((), ()))


def _sigmoid(x):
    return 1.0 / (1.0 + jnp.exp(-x))


def _silu(x):
    h = 0.5 * x
    return h + h * jnp.tanh(h)


def _split3(x):
    hi = x.astype(BF16)
    r1 = x - hi.astype(F32)
    mid = r1.astype(BF16)
    r2 = r1 - mid.astype(F32)
    lo = r2.astype(BF16)
    return jnp.concatenate([hi, mid, lo], axis=1)


def _inproj_kernel(x_ref, g_ref, w_ref, z_ref, xbc_ref, q_ref, gate_ref, k_ref, v_ref,
                   qi_ref, ki_ref, dtw_ref, kb_ref, vb_ref, kib_ref, h_ref):
    j = pl.program_id(1)

    @pl.when(j == 0)
    def _():
        x = x_ref[...]
        ms = jnp.mean(x * x, axis=-1, keepdims=True)
        h_ref[...] = ((x * lax.rsqrt(ms + EPS)) * g_ref[...]).astype(BF16)

    def tile():
        return jnp.dot(h_ref[...], w_ref[...], preferred_element_type=F32)

    @pl.when(j < 2)
    def _():
        z_ref[...] = tile().astype(BF16)

    @pl.when((j >= 2) & (j < 5))
    def _():
        xbc_ref[...] = tile().astype(BF16)

    @pl.when(j == 5)
    def _():
        q_ref[...] = (tile() * Q_SCALE).astype(BF16)

    @pl.when((j >= 6) & (j < 8))
    def _():
        gate_ref[...] = tile().astype(BF16)

    @pl.when(j == 8)
    def _():
        t = tile()
        k_ref[...] = t[:, 0:128]
        v_ref[...] = t[:, 128:256]
        qi_ref[...] = t[:, 256:768].astype(BF16)
        ki_ref[...] = t[:, 768:832]
        dtw_ref[...] = t[:, 896:1024]
        kb_ref[...] = t[:, 0:128].astype(BF16)
        vb_ref[...] = t[:, 128:256].astype(BF16)
        kib_ref[...] = t[:, 768:832].astype(BF16)


def _pack_w_in(w_in):
    sizes = (SSM_INNER, CONV_DIM, SSM_HEADS, ATTN_WIDTH, ATTN_KV_HEADS * HEAD_DIM,
             ATTN_KV_HEADS * HEAD_DIM, IDX_HEADS * IDX_DIM, IDX_DIM, IDX_HEADS,
             N_BRANCHES * D_MODEL)
    offs = np.concatenate([[0], np.cumsum(sizes)])
    z, xbc, dt, q, k, v, qi, ki, wi, g = [w_in[:, offs[i]:offs[i + 1]] for i in range(10)]
    zeros = lambda n: jnp.zeros((D_MODEL, n), w_in.dtype)
    cols = [z, xbc, q, g, k, v, qi, ki, zeros(64), dt, wi, zeros(LANE - SSM_HEADS - IDX_HEADS)]
    return jnp.concatenate(cols, axis=1).astype(BF16)


def _in_proj(x2d, norm_g, w_packed, tm):
    n = x2d.shape[0]
    assert n % tm == 0
    bf = lambda c: jax.ShapeDtypeStruct((n, c), BF16)
    ff = lambda c: jax.ShapeDtypeStruct((n, c), F32)
    clip = lambda j, lo, hi: jnp.clip(j - lo, 0, hi - lo - 1)
    return pl.pallas_call(
        _inproj_kernel,
        out_shape=(bf(SSM_INNER), bf(CONV_DIM), bf(ATTN_WIDTH), bf(2 * D_MODEL),
                   ff(128), ff(128), bf(512), ff(IDX_DIM), ff(128), bf(128), bf(128), bf(IDX_DIM)),
        grid=(n // tm, PROJ_TILES),
        in_specs=[
            pl.BlockSpec((tm, D_MODEL), lambda i, j: (i, 0)),
            pl.BlockSpec((1, D_MODEL), lambda i, j: (0, 0)),
            pl.BlockSpec((D_MODEL, PROJ_TILE), lambda i, j: (0, j)),
        ],
        out_specs=(
            pl.BlockSpec((tm, PROJ_TILE), lambda i, j: (i, clip(j, 0, 2))),
            pl.BlockSpec((tm, PROJ_TILE), lambda i, j: (i, clip(j, 2, 5))),
            pl.BlockSpec((tm, PROJ_TILE), lambda i, j: (i, 0)),
            pl.BlockSpec((tm, PROJ_TILE), lambda i, j: (i, clip(j, 6, 8))),
            pl.BlockSpec((tm, 128), lambda i, j: (i, 0)),
            pl.BlockSpec((tm, 128), lambda i, j: (i, 0)),
            pl.BlockSpec((tm, 512), lambda i, j: (i, 0)),
            pl.BlockSpec((tm, IDX_DIM), lambda i, j: (i, 0)),
            pl.BlockSpec((tm, 128), lambda i, j: (i, 0)),
            pl.BlockSpec((tm, 128), lambda i, j: (i, 0)),
            pl.BlockSpec((tm, 128), lambda i, j: (i, 0)),
            pl.BlockSpec((tm, IDX_DIM), lambda i, j: (i, 0)),
        ),
        scratch_shapes=[pltpu.VMEM((tm, D_MODEL), BF16)],
        compiler_params=pltpu.CompilerParams(
            dimension_semantics=("arbitrary", "arbitrary"), vmem_limit_bytes=VMEM_LIMIT),
        name="in_proj",
    )(x2d, norm_g.reshape(1, D_MODEL), w_packed)


def _ssd_kernel(*refs, L, cps, nsteps, has_state):
    if has_state:
        (xbc_ref, z_ref, dtw_ref, cs_ref, s0_ref, shift_ref, convw_ref, convb_ref, dtb_ref, a_ref,
         dx_ref, ng_ref, e3p_ref, e3l_ref, tri_ref, bd_ref, yn_ref, cnew_ref, sout_ref,
         xe_ref, st) = refs
    else:
        (xbc_ref, z_ref, dtw_ref, shift_ref, convw_ref, convb_ref, dtb_ref, a_ref,
         dx_ref, ng_ref, e3p_ref, e3l_ref, tri_ref, bd_ref, yn_ref, cnew_ref, sout_ref,
         xe_ref, st) = refs
    c = pl.program_id(1)
    G, R, N, P = SSM_GROUPS, SSM_HEADS_PER_GROUP, SSM_STATE, SSM_HEADDIM
    GW = R * P
    RL = R * L
    LB = cps * L

    @pl.when(c == 0)
    def _():
        xe_ref[LB:, :] = jnp.zeros((xe_ref.shape[0] - LB, CONV_DIM), BF16)
        if has_state:
            xe_ref[LB:LB + CONV_CARRY, :] = cs_ref[0]
            st[...] = s0_ref[0].T
        else:
            st[...] = jnp.zeros(st.shape, F32)

    xe_ref[0:LB, :] = xbc_ref[0]
    shifted = jnp.dot(shift_ref[0], xe_ref[...], preferred_element_type=F32)
    w = convw_ref[...]
    conv = convb_ref[...] + shifted[2 * LB:3 * LB, :] * w[0:1, :]
    conv = conv + shifted[LB:2 * LB, :] * w[1:2, :]
    conv = conv + shifted[0:LB, :] * w[2:3, :]
    conv = conv + xbc_ref[0].astype(F32) * w[3:4, :]
    act = _silu(conv)

    @pl.when(c == nsteps - 1)
    def _():
        cnew_ref[0] = xbc_ref[0, LB - CONV_CARRY:LB, :].astype(F32)[
            CONV_CARRY - (CONV_WIDTH - 1):CONV_CARRY, :]

    if nsteps > 1:
        xe_ref[LB:LB + CONV_CARRY, :] = xe_ref[LB - CONV_CARRY:LB, :]

    u = dtw_ref[0] + dtb_ref[...]
    dt = jnp.maximum(u, 0.0) + jnp.log1p(jnp.exp(-jnp.abs(u)))
    a = dt * a_ref[...]
    ac3 = jnp.dot(tri_ref[...], _split3(a), preferred_element_type=F32)
    a_cum = ac3[:, 0:LANE] + ac3[:, LANE:2 * LANE] + ac3[:, 2 * LANE:]

    ex = jnp.dot(_split3(jnp.concatenate([a_cum, dt], axis=0)), e3p_ref[...],
                 preferred_element_type=F32)
    if L == P:
        acl_all = ex[:LB]
    else:
        acl_all = jnp.dot(_split3(a_cum), e3l_ref[...], preferred_element_type=F32)
    zs = _silu(z_ref[0].astype(F32))
    row = lax.broadcasted_iota(I32, (L, RL), 0)
    scol = lax.broadcasted_iota(I32, (L, RL), 1) & (L - 1)

    for sub in range(cps):
        rows = slice(sub * L, (sub + 1) * L)
        xs = act[rows, :SSM_INNER]
        bmat = act[rows, SSM_INNER:SSM_INNER + G * N].astype(BF16)
        cmat = act[rows, SSM_INNER + G * N:].astype(BF16)
        acx = ex[rows]
        dtx = ex[LB + sub * L:LB + (sub + 1) * L]
        acl = acl_all[rows]
        xdt = xs * dtx
        xdt_b = xdt.astype(BF16)
        eacx = jnp.exp(acx)
        alast = acx[L - 1:L, :]
        xw = (xdt * jnp.exp(alast - acx)).astype(BF16)
        ealast = jnp.exp(alast)

        ys = []
        for g in range(G):
            sl = slice(g * GW, (g + 1) * GW)
            b_g = bmat[:, g * N:(g + 1) * N]
            c_g = cmat[:, g * N:(g + 1) * N]
            acl_g = acl[:, g * RL:(g + 1) * RL]
            acs_g = jnp.sum(jnp.where(scol == row, acl_g, 0.0), axis=0, keepdims=True)
            decay = jnp.where(scol <= row, jnp.exp(acl_g - acs_g), 0.0)
            cb = lax.dot_general(c_g, jnp.concatenate([b_g] * R, axis=0), NT_DIMS,
                                 preferred_element_type=F32)
            m_g = (cb * decay).astype(BF16)
            xbd = jnp.concatenate([xdt_b[:, sl]] * R, axis=0) * bd_ref[...]
            y_in = jnp.dot(m_g, xbd, preferred_element_type=F32)
            st_g = st[:, sl]
            y_st = jnp.dot(c_g, st_g.astype(BF16), preferred_element_type=F32)
            ys.append(y_in + eacx[:, sl] * y_st)
            upd = lax.dot_general(b_g, xw[:, sl], TN_DIMS, preferred_element_type=F32)
            st[:, sl] = st_g * ealast[:, sl] + upd
        y = jnp.concatenate(ys, axis=1) + dx_ref[...] * xs

        yz = y * zs[rows]
        ms = jnp.mean(yz * yz, axis=-1, keepdims=True)
        yn_ref[0, rows, :] = ((yz * lax.rsqrt(ms + EPS)) * ng_ref[...]).astype(BF16)

    @pl.when(c == nsteps - 1)
    def _():
        sout_ref[0] = st[...].T


def _shift_matrices(L, conv_rows):
    out = np.zeros((2, 3 * L, conv_rows), np.float32)
    for k in range(CONV_WIDTH - 1):
        for t in range(L):
            src = t - (k + 1)
            if src >= 0:
                out[:, k * L + t, src] = 1.0
            else:
                out[1, k * L + t, L + CONV_CARRY + src] = 1.0
                j = CONV_WIDTH - 1 + src
                for piece in range(3):
                    out[0, k * L + t, L + 3 * piece + j] = 1.0
    return jnp.asarray(out, BF16)


def _expand_matrix(width_per_head):
    r = np.arange(3 * LANE)[:, None] % LANE
    c = np.arange(SSM_HEADS * width_per_head)[None, :] // width_per_head
    return jnp.asarray((r == c).astype(np.float32), BF16)


def _ssd(xbc, z, dtw, conv_state, ssm_state, conv_w, conv_b, dt_bias, a_log, d_skip, ssm_norm_g):
    b, t, _ = xbc.shape
    L = min(CHUNK, t)
    nc = t // L
    assert t % L == 0 and L & (L - 1) == 0
    has_state = conv_state is not None
    pad = lambda v: jnp.pad(v.astype(F32), (0, LANE - SSM_HEADS)).reshape(1, LANE)
    a_neg = pad(-jnp.exp(a_log.astype(F32)))
    cps = SSD_CHUNKS_PER_STEP if nc % SSD_CHUNKS_PER_STEP == 0 else 1
    nsteps = nc // cps
    lb = cps * L
    conv_rows = _round_up(lb + CONV_CARRY, LANE)
    tri = jnp.asarray(np.kron(np.eye(cps), np.tril(np.ones((L, L)))).astype(np.float32), BF16)
    rl = SSM_HEADS_PER_GROUP * L
    gw = SSM_HEADS_PER_GROUP * SSM_HEADDIM
    bd = jnp.asarray((np.arange(rl)[:, None] // L == np.arange(gw)[None, :] // SSM_HEADDIM)
                     .astype(np.float32), BF16)
    const = lambda shape: pl.BlockSpec(shape, lambda i, j: (0,) * len(shape))
    tok = lambda w: pl.BlockSpec((1, lb, w), lambda i, j: (i, j, 0))
    per_b = lambda s: pl.BlockSpec((1,) + s, lambda i, j: (i, 0, 0))
    ins = [xbc, z, dtw]
    specs = [tok(CONV_DIM), tok(SSM_INNER), tok(LANE)]
    if has_state:
        triples = _split3(conv_state.astype(F32))
        ins += [_pad_axis(triples, 1, CONV_CARRY), ssm_state.reshape(b, SSM_INNER, SSM_STATE)]
        specs += [per_b((CONV_CARRY, CONV_DIM)), per_b((SSM_INNER, SSM_STATE))]
    first = 0 if has_state else 1
    ins += [_shift_matrices(lb, conv_rows)]
    specs += [pl.BlockSpec((1, 3 * lb, conv_rows),
                           lambda i, j: (jnp.where(j == 0, first, 1), 0, 0))]
    ins += [conv_w, conv_b.reshape(1, CONV_DIM), pad(dt_bias), a_neg,
            jnp.repeat(d_skip.astype(F32), SSM_HEADDIM).reshape(1, SSM_INNER),
            ssm_norm_g.reshape(1, SSM_INNER), _expand_matrix(SSM_HEADDIM), _expand_matrix(L),
            tri, bd]
    specs += [const((CONV_WIDTH, CONV_DIM)), const((1, CONV_DIM)), const((1, LANE)),
              const((1, LANE)), const((1, SSM_INNER)), const((1, SSM_INNER)),
              const((3 * LANE, SSM_INNER)), const((3 * LANE, SSM_HEADS * L)),
              const((lb, lb)), const((rl, gw))]
    yn, cnew, sout = pl.pallas_call(
        functools.partial(_ssd_kernel, L=L, cps=cps, nsteps=nsteps, has_state=has_state),
        out_shape=(jax.ShapeDtypeStruct((b, t, SSM_INNER), BF16),
                   jax.ShapeDtypeStruct((b, CONV_WIDTH - 1, CONV_DIM), F32),
                   jax.ShapeDtypeStruct((b, SSM_INNER, SSM_STATE), F32)),
        grid=(b, nsteps),
        in_specs=specs,
        out_specs=(tok(SSM_INNER), per_b((CONV_WIDTH - 1, CONV_DIM)),
                   per_b((SSM_INNER, SSM_STATE))),
        scratch_shapes=[pltpu.VMEM((conv_rows, CONV_DIM), BF16),
                        pltpu.VMEM((SSM_STATE, SSM_INNER), F32)],
        compiler_params=pltpu.CompilerParams(
            dimension_semantics=("arbitrary", "arbitrary"), vmem_limit_bytes=VMEM_LIMIT),
        name="ssd",
    )(*ins)
    return yn, cnew, sout.reshape(b, SSM_HEADS, SSM_HEADDIM, SSM_STATE)


def _dsa_kernel(q_ref, qi_ref, w_ref, eq_ref, kidx_ref, k_ref, v_ref, o_ref,
                qs_ref, qx_ref, key_ref, kx_ref, vx_ref, m_ref, acc_ref, mlim_ref, s_ref,
                *, past_len, l_valid, topk, nq):
    qb = pl.program_id(1)
    TQ, TK = LANE, KEY_TILE
    W = ATTN_REP * nq
    wide = nq == LANE

    lane = lax.broadcasted_iota(I32, (1, TQ), 1)
    pos = past_len + qb * nq + (lane & (nq - 1))
    n_adm = jnp.minimum(((pos >> CHUNK_SHIFT) + 1) * CHUNK, l_valid)
    k_eff = jnp.minimum(n_adm, topk)
    last_pos = past_len + qb * nq + nq - 1
    n_max = jnp.minimum(((last_pos >> CHUNK_SHIFT) + 1) * CHUNK, l_valid)
    nt = (n_max + TK - 1) // TK
    ntp = 2 * ((nt + 1) // 2)
    lp = key_ref.shape[0]

    @pl.when(qb == 0)
    def _():
        kx_ref[:, LANE:] = k_ref[0]

        def v_tile(t, carry):
            s0 = pl.multiple_of(t * TK, TK)
            v_t = v_ref[0, pl.ds(s0, TK), :].astype(F32).T
            for g in range(ATTN_KV_HEADS):
                vx_ref[g, 0:HEAD_DIM, pl.ds(s0, TK)] = (
                    v_t[g * HEAD_DIM:(g + 1) * HEAD_DIM, :].astype(BF16))
            return carry

        lax.fori_loop(0, lp // TK, v_tile, 0)
        tail_row = lax.broadcasted_iota(I32, (PV_ROWS - HEAD_DIM, lp), 0)
        for g in range(ATTN_KV_HEADS):
            vx_ref[g, HEAD_DIM:, :] = jnp.where(tail_row == 0, 1.0, 0.0).astype(BF16)

    if wide:
        for h in range(IDX_HEADS):
            qs_ref[h * nq:(h + 1) * nq, :] = qi_ref[0, :, h * IDX_DIM:(h + 1) * IDX_DIM]
        w_t = w_ref[0].T[SSM_HEADS:SSM_HEADS + IDX_HEADS, :] * IDX_SCALE
    else:
        qs_ref[...] = qi_ref[0]
        w_row = w_ref[0] * IDX_SCALE

    def score_tile(t, carry):
        s0 = pl.multiple_of(t * TK, TK)
        d = lax.dot_general(kidx_ref[0, pl.ds(s0, TK), :], qs_ref[0:IDX_HEADS * nq, :],
                            NT_DIMS, preferred_element_type=F32)
        if wide:
            sc = w_t[0:1, :] * jnp.maximum(d[:, 0:nq], 0.0)
            for h in range(1, IDX_HEADS):
                sc = sc + w_t[h:h + 1, :] * jnp.maximum(d[:, h * nq:(h + 1) * nq], 0.0)
        else:
            sc = w_row * jnp.maximum(d, 0.0)
            for shift in (64, 32, 16):
                sc = sc + pltpu.roll(sc, shift, axis=1)
        sc = sc + 0.0
        bits = pltpu.bitcast(sc, I32)
        key = jnp.where(bits < 0, bits ^ np.int32(0x7FFFFFFF), bits)
        s_idx = s0 + lax.broadcasted_iota(I32, (TK, TQ), 0)
        key_ref[pl.ds(s0, TK), :] = jnp.where(s_idx < n_adm, key, INT_MIN)
        return carry

    lax.fori_loop(0, nt, score_tile, 0)

    def count(pred):
        def body(t, acc):
            s0 = pl.multiple_of(t * TK, TK)
            blk = key_ref[pl.ds(s0, TK), :]
            s_idx = s0 + lax.broadcasted_iota(I32, (TK, TQ), 0)
            hit = jnp.where(pred(blk, s_idx), 1, 0).astype(I32)
            return acc + jnp.sum(hit.reshape(TK // 8, 8, TQ), axis=0)
        acc = lax.fori_loop(0, nt, body, jnp.zeros((8, TQ), I32))
        return jnp.sum(acc, axis=0, keepdims=True)

    def bit_step(i, prefix):
        cand = prefix | jnp.left_shift(jnp.int32(1), 31 - i)
        cand_s = cand ^ INT_MIN
        cnt = count(lambda blk, s_idx: blk >= cand_s)
        return jnp.where(cnt >= k_eff, cand, prefix)

    thr = lax.fori_loop(0, 32, bit_step, jnp.zeros((1, TQ), I32)) ^ INT_MIN

    n_gt = count(lambda blk, s_idx: blk > thr)
    n_eq = count(lambda blk, s_idx: blk == thr)
    need = k_eff - n_gt
    mlim_ref[...] = jnp.full((1, TQ), 2 ** 30, I32)

    @pl.when(jnp.max(n_eq - need) > 0)
    def _():
        def idx_step(i, prefix):
            cand = prefix | jnp.left_shift(jnp.int32(1), 14 - i)
            cnt = count(lambda blk, s_idx: (blk == thr) & (s_idx < cand))
            return jnp.where(cnt < need, cand, prefix)
        mlim_ref[...] = lax.fori_loop(0, 15, idx_step, jnp.zeros((1, TQ), I32))

    mlim = mlim_ref[...]

    def bias_tile(t, carry):
        s0 = pl.multiple_of(t * TK, TK)
        blk = key_ref[pl.ds(s0, TK), :]
        s_idx = s0 + lax.broadcasted_iota(I32, (TK, TQ), 0)
        sel = (blk > thr) | ((blk == thr) & (s_idx <= mlim))
        kx_ref[pl.ds(s0, TK), 0:LANE] = jnp.where(sel, 0.0, NEG_BIG).astype(BF16)
        return carry

    lax.fori_loop(0, nt, bias_tile, 0)

    @pl.when(nt < ntp)
    def _():
        kx_ref[pl.ds(pl.multiple_of(nt * TK, TK), TK), 0:LANE] = jnp.full((TK, LANE), NEG_BIG,
                                                                           BF16)

    for g in range(ATTN_KV_HEADS):
        lanes = slice(LANE + g * HEAD_DIM, LANE + (g + 1) * HEAD_DIM)
        qx_ref[g, :, 0:LANE] = eq_ref[...]
        qx_ref[g, :, LANE:] = jnp.zeros((W, LANE), BF16)
        if wide:
            for r in range(ATTN_REP):
                hq = g * ATTN_REP + r
                qx_ref[g, r * nq:(r + 1) * nq, lanes] = (
                    q_ref[0, :, hq * HEAD_DIM:(hq + 1) * HEAD_DIM])
        else:
            qx_ref[g, :, lanes] = q_ref[0, g]
    m_ref[...] = jnp.full(m_ref.shape, -jnp.inf, F32)
    acc_ref[...] = jnp.zeros(acc_ref.shape, F32)

    def scores(t, slot):
        s0 = pl.multiple_of(jnp.minimum(t, ntp - 1) * TK, TK)
        k_tile = kx_ref[pl.ds(s0, TK), :]
        for g in range(ATTN_KV_HEADS):
            s_ref[g, slot] = lax.dot_general(k_tile, qx_ref[g], NT_DIMS,
                                             preferred_element_type=F32)

    def accumulate(t, slot):
        s0 = pl.multiple_of(t * TK, TK)
        for g in range(ATTN_KV_HEADS):
            s = s_ref[g, slot]
            m_prev = m_ref[g]
            m_new = jnp.maximum(m_prev, jnp.max(s, axis=0, keepdims=True))
            p = jnp.exp2(s - m_new)
            acc_ref[g] = jnp.exp2(m_prev - m_new) * acc_ref[g] + jnp.dot(
                vx_ref[g, :, pl.ds(s0, TK)], p.astype(BF16), preferred_element_type=F32)
            m_ref[g] = m_new

    scores(0, 0)

    def attn_pair(u, carry):
        scores(2 * u + 1, 1)
        accumulate(2 * u, 0)
        scores(2 * u + 2, 0)
        accumulate(2 * u + 1, 1)
        return carry

    lax.fori_loop(0, ntp // 2, attn_pair, 0)
    for g in range(ATTN_KV_HEADS):
        acc = acc_ref[g]
        o_t = acc[0:HEAD_DIM, :] * (1.0 / acc[HEAD_DIM:HEAD_DIM + 1, :])
        if wide:
            for r in range(0, ATTN_REP, 2):
                hq = g * ATTN_REP + r
                pair = jnp.concatenate([o_t[:, r * nq:(r + 1) * nq],
                                        o_t[:, (r + 1) * nq:(r + 2) * nq]], axis=0)
                o_ref[0, :, hq * HEAD_DIM:(hq + 2) * HEAD_DIM] = pair.T.astype(BF16)
        else:
            o_rq = jnp.concatenate([o_t, o_t], axis=0).T[:, 0:HEAD_DIM].astype(BF16)
            for r in range(ATTN_REP):
                hq = g * ATTN_REP + r
                o_ref[0, :, hq * HEAD_DIM:(hq + 1) * HEAD_DIM] = o_rq[r * nq:(r + 1) * nq, :]


def _dsa(q, qi, w, kidx_all, k_all, v_all, past_len, l_valid, t):
    b = kidx_all.shape[0]
    lp = kidx_all.shape[1]
    nq = Q_BLOCK if t % Q_BLOCK == 0 else SMALL_Q_BLOCK
    assert t % nq == 0 and lp % (2 * KEY_TILE) == 0 and lp < 2 ** 15
    topk = min(TOPK_MAX, l_valid // 4)
    width = ATTN_REP * nq
    if nq == Q_BLOCK:
        qblk = lambda c: pl.BlockSpec((1, nq, c), lambda i, j: (i, j, 0))
        q_specs = [qblk(ATTN_WIDTH), qblk(IDX_HEADS * IDX_DIM), qblk(LANE)]
    else:
        assert t == nq
        q_specs = [pl.BlockSpec((1, ATTN_KV_HEADS, width, HEAD_DIM), lambda i, j: (i, 0, 0, 0)),
                   pl.BlockSpec((1, IDX_HEADS * nq, IDX_DIM), lambda i, j: (i, 0, 0)),
                   pl.BlockSpec((1, 1, LANE), lambda i, j: (i, 0, 0))]
    slot_onehot = jnp.asarray(
        (np.arange(width)[:, None] % nq == np.arange(LANE)[None, :]).astype(np.float32), BF16)
    per_stream = lambda c: pl.BlockSpec((1, lp, c), lambda i, j: (i, 0, 0))
    return pl.pallas_call(
        functools.partial(_dsa_kernel, past_len=past_len, l_valid=l_valid, topk=topk, nq=nq),
        out_shape=jax.ShapeDtypeStruct((b, t, ATTN_WIDTH), BF16),
        grid=(b, t // nq),
        in_specs=q_specs + [pl.BlockSpec((width, LANE), lambda i, j: (0, 0)),
                            per_stream(IDX_DIM), per_stream(LANE), per_stream(LANE)],
        out_specs=pl.BlockSpec((1, nq, ATTN_WIDTH), lambda i, j: (i, j, 0)),
        scratch_shapes=[pltpu.VMEM((width, IDX_DIM), BF16),
                        pltpu.VMEM((ATTN_KV_HEADS, width, 2 * LANE), BF16),
                        pltpu.VMEM((lp, LANE), I32),
                        pltpu.VMEM((lp, 2 * LANE), BF16),
                        pltpu.VMEM((ATTN_KV_HEADS, PV_ROWS, lp), BF16),
                        pltpu.VMEM((ATTN_KV_HEADS, 1, width), F32),
                        pltpu.VMEM((ATTN_KV_HEADS, PV_ROWS, width), F32),
                        pltpu.VMEM((1, LANE), I32),
                        pltpu.VMEM((ATTN_KV_HEADS, 2, KEY_TILE, width), F32)],
        compiler_params=pltpu.CompilerParams(
            dimension_semantics=("arbitrary", "arbitrary"), vmem_limit_bytes=VMEM_LIMIT),
        name="dsa",
    )(q, qi, w, slot_onehot, kidx_all, k_all, v_all)


def _merge_kernel(x_ref, yn_ref, o_ref, gate_ref, wa_ref, wb_ref, wo_ref, out_ref):
    br_a = jnp.dot(yn_ref[...], wa_ref[...], preferred_element_type=F32)
    br_b = jnp.dot(o_ref[...], wb_ref[...], preferred_element_type=F32)
    gates = gate_ref[...].astype(F32)
    merged = _sigmoid(gates[:, :D_MODEL]) * br_a + _sigmoid(gates[:, D_MODEL:]) * br_b
    out_ref[...] = x_ref[...] + jnp.dot(merged.astype(BF16), wo_ref[...],
                                        preferred_element_type=F32)


def _merge(x2d, yn, o, gate, w_ssm_out, w_attn_out, w_o, tm):
    n = x2d.shape[0]
    assert n % tm == 0
    row = lambda w: pl.BlockSpec((tm, w), lambda i: (i, 0))
    full = lambda r, c: pl.BlockSpec((r, c), lambda i: (0, 0))
    return pl.pallas_call(
        _merge_kernel,
        out_shape=jax.ShapeDtypeStruct((n, D_MODEL), F32),
        grid=(n // tm,),
        in_specs=[row(D_MODEL), row(SSM_INNER), row(ATTN_WIDTH), row(2 * D_MODEL),
                  full(SSM_INNER, D_MODEL), full(ATTN_WIDTH, D_MODEL), full(D_MODEL, D_MODEL)],
        out_specs=row(D_MODEL),
        compiler_params=pltpu.CompilerParams(
            dimension_semantics=("arbitrary",), vmem_limit_bytes=VMEM_LIMIT),
        name="merge",
    )(x2d, yn, o, gate, w_ssm_out.astype(BF16), w_attn_out.astype(BF16), w_o.astype(BF16))


def _moe_kernel(x_ref, ng_ref, wr_ref, br_ref, w1_ref, w3_ref, w2_ref, fg_ref, tri_ref, out_ref,
                hs_ref, combs_ref, acc_ref, pt_ref, seg_ref, *, n_steps, blk_rows):
    e_step = pl.program_id(1)
    tm = x_ref.shape[0]
    sr = hs_ref.shape[0]
    BLK = blk_rows

    @pl.when(e_step == 0)
    def _():
        x = x_ref[...]
        ms = jnp.mean(x * x, axis=-1, keepdims=True)
        h = ((x * lax.rsqrt(ms + EPS)) * ng_ref[...]).astype(BF16)
        logits = jnp.dot(h, wr_ref[...], preferred_element_type=F32) + br_ref[...]
        lane_i = lax.broadcasted_iota(I32, (tm, LANE), 1)
        lane = lane_i.astype(F32)
        first = lambda hit: jnp.min(jnp.where(hit, lane, float(LANE)), axis=1, keepdims=True)
        is_g = (lane_i >= N_EXPERTS) & (lane_i < N_EXPERTS + MOE_GROUPS)
        glog = jnp.where(is_g, logits, -jnp.inf)
        gmax = jnp.max(glog, axis=1, keepdims=True)
        gsel = first(glog == gmax) - float(N_EXPERTS)
        p_group = 1.0 / jnp.sum(jnp.exp(glog - gmax), axis=1, keepdims=True)
        grp = (lane_i >> EXPERT_GROUP_SHIFT).astype(F32)
        in_grp = (lane_i < N_EXPERTS) & (grp == gsel)
        el = jnp.where(in_grp, logits, -jnp.inf)
        v1 = jnp.max(el, axis=1, keepdims=True)
        i1 = first(el == v1)
        el2 = jnp.where(lane == i1, -jnp.inf, el)
        v2 = jnp.max(el2, axis=1, keepdims=True)
        i2 = first(el2 == v2)
        e21 = jnp.exp(v2 - v1)
        den = 1.0 + e21
        comb = (jnp.where(lane == i1, (1.0 / den) * p_group, 0.0)
                + jnp.where(lane == i2, (e21 / den) * p_group, 0.0))

        own = lane == gsel
        own_f = jnp.where(own, 1.0, 0.0)
        before = jnp.dot(tri_ref[...], own_f.astype(BF16), preferred_element_type=F32)
        rank = jnp.sum(jnp.where(own, before, 0.0), axis=1, keepdims=True)
        counts = jnp.sum(own_f, axis=0, keepdims=True)
        blocks = jnp.floor((counts + (BLK - 1)) * (1.0 / BLK))
        upper = (lax.broadcasted_iota(I32, (LANE, LANE), 0)
                 < lax.broadcasted_iota(I32, (LANE, LANE), 1))
        starts = jnp.dot(jnp.broadcast_to(blocks, (8, LANE)).astype(BF16),
                         jnp.where(upper, 1.0, 0.0).astype(BF16),
                         preferred_element_type=F32)[0:1, :] * BLK
        dest = jnp.sum(jnp.where(own, starts, 0.0), axis=1, keepdims=True) + rank
        for g in range(MOE_GROUPS):
            seg_ref[g] = jnp.sum(starts[:, g:g + 1]).astype(I32)
            seg_ref[MOE_GROUPS + g] = jnp.sum(blocks[:, g:g + 1]).astype(I32)

        dest_i = dest.astype(I32)
        pt = jnp.where(dest_i == lax.broadcasted_iota(I32, (tm, sr), 1), 1.0, 0.0).astype(BF16)
        pt_ref[...] = pt
        dest_row = jnp.broadcast_to(dest, (tm, LANE)).T[0:1, :].astype(I32)
        p_mat = jnp.where(lax.broadcasted_iota(I32, (sr, tm), 0) == dest_row, 1.0, 0.0
                          ).astype(BF16)
        hs_ref[...] = jnp.dot(p_mat, h, preferred_element_type=F32).astype(BF16)
        c3 = jnp.dot(p_mat, _split3(comb), preferred_element_type=F32)
        combs_ref[...] = c3[:, 0:LANE] + c3[:, LANE:2 * LANE] + c3[:, 2 * LANE:]
        acc_ref[...] = jnp.zeros(acc_ref.shape, F32)

    group = e_step // (EXPERTS_PER_GROUP // EXPERTS_PER_STEP)
    seg_start = seg_ref[group]
    lane = lax.broadcasted_iota(I32, (BLK, LANE), 1)

    def row_block(blk, carry):
        rows = pl.ds(pl.multiple_of(seg_start + blk * BLK, BLK), BLK)
        h = hs_ref[rows, :]
        comb = combs_ref[rows, :]
        for k in range(EXPERTS_PER_STEP):
            e = e_step * EXPERTS_PER_STEP + k
            a1 = jnp.dot(h, w1_ref[k], preferred_element_type=F32)
            a3 = jnp.dot(h, w3_ref[k], preferred_element_type=F32)
            he = (_silu(a1) * a3).astype(BF16)
            ye = jnp.dot(he, w2_ref[k], preferred_element_type=F32)
            wcol = jnp.sum(jnp.where(lane == e, comb, 0.0), axis=1, keepdims=True)
            acc_ref[rows, :] += wcol * ye
        return carry

    lax.fori_loop(0, seg_ref[MOE_GROUPS + group], row_block, 0)

    @pl.when(e_step == n_steps - 1)
    def _():
        acc = acc_ref[...]
        hi = acc.astype(BF16)
        lo = (acc - hi.astype(F32)).astype(BF16)
        pt = pt_ref[...]
        y = x_ref[...] + (jnp.dot(pt, hi, preferred_element_type=F32)
                          + jnp.dot(pt, lo, preferred_element_type=F32))
        ms = jnp.mean(y * y, axis=-1, keepdims=True)
        out_ref[...] = (y * lax.rsqrt(ms + EPS)) * fg_ref[...]


def _moe(x2d, norm_g, w_router_group, b_router_group, w_router_expert, b_router_expert,
         w1, w3, w2, norm_final_g, tm):
    n = x2d.shape[0]
    blk_rows = min(MOE_ROW_BLOCK, tm)
    assert n % tm == 0 and tm % blk_rows == 0
    n_steps = N_EXPERTS // EXPERTS_PER_STEP
    sr = tm + MOE_GROUPS * blk_rows
    padw = LANE - N_EXPERTS - MOE_GROUPS
    wr = jnp.concatenate([w_router_expert, w_router_group,
                          jnp.zeros((D_MODEL, padw), F32)], axis=1).astype(BF16)
    br = jnp.concatenate([b_router_expert, b_router_group,
                          jnp.zeros((padw,), F32)]).astype(F32).reshape(1, LANE)
    tri = jnp.asarray(np.tril(np.ones((tm, tm), np.float32), -1), BF16)
    row = pl.BlockSpec((tm, D_MODEL), lambda i, e: (i, 0))
    const = lambda r, c: pl.BlockSpec((r, c), lambda i, e: (0, 0))
    return pl.pallas_call(
        functools.partial(_moe_kernel, n_steps=n_steps, blk_rows=blk_rows),
        out_shape=jax.ShapeDtypeStruct((n, D_MODEL), F32),
        grid=(n // tm, n_steps),
        in_specs=[row, const(1, D_MODEL), const(D_MODEL, LANE), const(1, LANE),
                  pl.BlockSpec((EXPERTS_PER_STEP, D_MODEL, EXPERT_FF), lambda i, e: (e, 0, 0)),
                  pl.BlockSpec((EXPERTS_PER_STEP, D_MODEL, EXPERT_FF), lambda i, e: (e, 0, 0)),
                  pl.BlockSpec((EXPERTS_PER_STEP, EXPERT_FF, D_MODEL), lambda i, e: (e, 0, 0)),
                  const(1, D_MODEL), const(tm, tm)],
        out_specs=row,
        scratch_shapes=[pltpu.VMEM((sr, D_MODEL), BF16), pltpu.VMEM((sr, LANE), F32),
                        pltpu.VMEM((sr, D_MODEL), F32), pltpu.VMEM((tm, sr), BF16),
                        pltpu.SMEM((2 * MOE_GROUPS,), I32)],
        compiler_params=pltpu.CompilerParams(
            dimension_semantics=("arbitrary", "arbitrary"), vmem_limit_bytes=VMEM_LIMIT),
        name="moe",
    )(x2d, norm_g.reshape(1, D_MODEL), wr, br, w1.astype(BF16), w3.astype(BF16),
      w2.astype(BF16), norm_final_g.reshape(1, D_MODEL), tri)


def _token_tile(n):
    for tm in (1024, 512, 256, 128, 64, 32, 16, 8):
        if n % tm == 0:
            return tm
    raise ValueError(f"token count {n} is not a multiple of 8")


def _pad_axis(a, axis, size):
    if a.shape[axis] == size:
        return a
    widths = [(0, 0)] * a.ndim
    widths[axis] = (0, size - a.shape[axis])
    return jnp.pad(a, widths)


def _round_up(n, m):
    return (n + m - 1) // m * m


def _stream_step(x, conv_state, ssm_state, k_past, v_past, kidx_past, w_packed, p):
    b, t, d = x.shape
    n = b * t
    tm = _token_tile(n)
    x2d = x.reshape(n, d)
    z, xbc, q, gate, k, v, qi, ki, dtw, k_b, v_b, ki_b = _in_proj(
        x2d, p["norm_mix_g"], w_packed, tm)

    yn, conv_new, ssm_new = _ssd(
        xbc.reshape(b, t, CONV_DIM), z.reshape(b, t, SSM_INNER), dtw.reshape(b, t, LANE),
        conv_state, ssm_state, p["conv_w"], p["conv_b"], p["dt_bias"], p["A_log"],
        p["D_skip"], p["ssm_norm_g"])

    kv_w = ATTN_KV_HEADS * HEAD_DIM
    k_all, v_all, kidx_all = (k_b.reshape(b, t, kv_w), v_b.reshape(b, t, kv_w),
                              ki_b.reshape(b, t, IDX_DIM))
    past_len = 0
    if k_past is not None:
        past_len = k_past.shape[1]
        k_all = jnp.concatenate([k_past.astype(BF16).reshape(b, past_len, kv_w), k_all], axis=1)
        v_all = jnp.concatenate([v_past.astype(BF16).reshape(b, past_len, kv_w), v_all], axis=1)
        kidx_all = jnp.concatenate([kidx_past.astype(BF16), kidx_all], axis=1)
    l_valid = past_len + t
    lp = _round_up(l_valid, 2 * KEY_TILE)
    if t % Q_BLOCK == 0:
        q_in = (q.reshape(b, t, ATTN_WIDTH), qi.reshape(b, t, IDX_HEADS * IDX_DIM),
                dtw.reshape(b, t, LANE))
    else:
        q_in = (q.reshape(b, t, ATTN_KV_HEADS, ATTN_REP, HEAD_DIM).transpose(0, 2, 3, 1, 4)
                .reshape(b, ATTN_KV_HEADS, ATTN_REP * t, HEAD_DIM),
                qi.reshape(b, t, IDX_HEADS, IDX_DIM).transpose(0, 2, 1, 3)
                .reshape(b, IDX_HEADS * t, IDX_DIM),
                dtw.reshape(b, t, LANE)[:, :, SSM_HEADS:SSM_HEADS + IDX_HEADS]
                .transpose(0, 2, 1).reshape(b, 1, IDX_HEADS * t))
    o = _dsa(*q_in, _pad_axis(kidx_all, 1, lp), _pad_axis(k_all, 1, lp), _pad_axis(v_all, 1, lp),
             past_len, l_valid, t)
    o = o.reshape(n, ATTN_WIDTH)

    x1 = _merge(x2d, yn.reshape(n, SSM_INNER), o, gate, p["w_ssm_out"], p["w_attn_out"],
                p["w_o"], min(tm, MERGE_TILE))
    y = _moe(x1, p["norm_moe_g"], p["w_router_group"], p["b_router_group"],
             p["w_router_expert"], p["b_router_expert"], p["w1"], p["w3"], p["w2"],
             p["norm_final_g"], tm)
    return (y.reshape(b, t, d), k.reshape(b, t, ATTN_KV_HEADS, HEAD_DIM),
            v.reshape(b, t, ATTN_KV_HEADS, HEAD_DIM), ki.reshape(b, t, IDX_DIM), conv_new, ssm_new)


def kernel(x_prompt, x_sample, cache_k, cache_v, cache_kidx, state_conv, state_ssm, norm_mix_g, w_in, conv_w, conv_b, dt_bias, A_log, D_skip, ssm_norm_g, w_ssm_out, w_attn_out, w_o, norm_moe_g, w_router_group, b_router_group, w_router_expert, b_router_expert, w1, w3, w2, norm_final_g):
    p = dict(norm_mix_g=norm_mix_g, conv_w=conv_w, conv_b=conv_b, dt_bias=dt_bias, A_log=A_log,
             D_skip=D_skip, ssm_norm_g=ssm_norm_g, w_ssm_out=w_ssm_out, w_attn_out=w_attn_out,
             w_o=w_o, norm_moe_g=norm_moe_g, w_router_group=w_router_group,
             b_router_group=b_router_group, w_router_expert=w_router_expert,
             b_router_expert=b_router_expert, w1=w1, w3=w3, w2=w2, norm_final_g=norm_final_g)
    w_packed = _pack_w_in(w_in)
    yp, kp, vp, kip, cp, sp = _stream_step(x_prompt, None, None, None, None, None, w_packed, p)
    ys, ksn, vsn, kisn, csn, ssn = _stream_step(x_sample, state_conv, state_ssm, cache_k,
                                                cache_v, cache_kidx, w_packed, p)
    return (yp, ys, kp, vp, kip, cp, sp, ksn, vsn, kisn, csn, ssn)
```

```python
import functools

import numpy as np
import jax
import jax.numpy as jnp
from jax import lax
from jax.experimental import pallas as pl
from jax.experimental.pallas import tpu as pltpu

F32 = jnp.float32
BF16 = jnp.bfloat16
I32 = jnp.int32

D_MODEL = 1024
CHUNK = 64
CHUNK_SHIFT = 6
EPS = 1e-6
SSM_INNER = 2048
SSM_HEADDIM = 64
SSM_HEADS = 32
SSM_GROUPS = 4
SSM_HEADS_PER_GROUP = SSM_HEADS // SSM_GROUPS
SSM_STATE = 128
CONV_WIDTH = 4
CONV_DIM = SSM_INNER + 2 * SSM_GROUPS * SSM_STATE
CONV_CARRY = 16
SSD_CHUNKS_PER_STEP = 2
ATTN_HEADS = 16
ATTN_KV_HEADS = 2
HEAD_DIM = 64
ATTN_REP = ATTN_HEADS // ATTN_KV_HEADS
ATTN_WIDTH = ATTN_HEADS * HEAD_DIM
IDX_HEADS = 8
IDX_DIM = 64
IDX_SCALE = (IDX_HEADS * IDX_DIM) ** -0.5
TOPK_MAX = 256
MOE_GROUPS = 4
EXPERTS_PER_GROUP = 8
EXPERT_GROUP_SHIFT = 3
N_EXPERTS = 32
EXPERT_FF = 256
N_BRANCHES = 2

LANE = 128
VMEM_LIMIT = 56 * 1024 * 1024
PROJ_TILE = 1024
PROJ_TILES = 9
Q_BLOCK = 128
SMALL_Q_BLOCK = 16
KEY_TILE = 256
EXPERTS_PER_STEP = 4
MOE_ROW_BLOCK = 128
MERGE_TILE = 512
INT_MIN = np.int32(-2 ** 31)
NEG_BIG = -1e30
Q_SCALE = HEAD_DIM ** -0.5 * float(np.log2(np.e))
PV_ROWS = 80

NT_DIMS = (((1,), (1,)), ((), ()))
TN_DIMS = (((0,), (0,)), ((), ()))


def _sigmoid(x):
    return 1.0 / (1.0 + jnp.exp(-x))


def _silu(x):
    h = 0.5 * x
    return h + h * jnp.tanh(h)


def _split3(x):
    hi = x.astype(BF16)
    r1 = x - hi.astype(F32)
    mid = r1.astype(BF16)
    r2 = r1 - mid.astype(F32)
    lo = r2.astype(BF16)
    return jnp.concatenate([hi, mid, lo], axis=1)


def _inproj_kernel(x_ref, g_ref, w_ref, z_ref, xbc_ref, q_ref, gate_ref, k_ref, v_ref,
                   qi_ref, ki_ref, dtw_ref, kb_ref, vb_ref, kib_ref, h_ref):
    j = pl.program_id(1)

    @pl.when(j == 0)
    def _():
        x = x_ref[...]
        ms = jnp.mean(x * x, axis=-1, keepdims=True)
        h_ref[...] = ((x * lax.rsqrt(ms + EPS)) * g_ref[...]).astype(BF16)

    def tile():
        return jnp.dot(h_ref[...], w_ref[...], preferred_element_type=F32)

    @pl.when(j < 2)
    def _():
        z_ref[...] = tile().astype(BF16)

    @pl.when((j >= 2) & (j < 5))
    def _():
        xbc_ref[...] = tile().astype(BF16)

    @pl.when(j == 5)
    def _():
        q_ref[...] = (tile() * Q_SCALE).astype(BF16)

    @pl.when((j >= 6) & (j < 8))
    def _():
        gate_ref[...] = tile().astype(BF16)

    @pl.when(j == 8)
    def _():
        t = tile()
        k_ref[...] = t[:, 0:128]
        v_ref[...] = t[:, 128:256]
        qi_ref[...] = t[:, 256:768].astype(BF16)
        ki_ref[...] = t[:, 768:832]
        dtw_ref[...] = t[:, 896:1024]
        kb_ref[...] = t[:, 0:128].astype(BF16)
        vb_ref[...] = t[:, 128:256].astype(BF16)
        kib_ref[...] = t[:, 768:832].astype(BF16)


def _pack_w_in(w_in):
    sizes = (SSM_INNER, CONV_DIM, SSM_HEADS, ATTN_WIDTH, ATTN_KV_HEADS * HEAD_DIM,
             ATTN_KV_HEADS * HEAD_DIM, IDX_HEADS * IDX_DIM, IDX_DIM, IDX_HEADS,
             N_BRANCHES * D_MODEL)
    offs = np.concatenate([[0], np.cumsum(sizes)])
    z, xbc, dt, q, k, v, qi, ki, wi, g = [w_in[:, offs[i]:offs[i + 1]] for i in range(10)]
    zeros = lambda n: jnp.zeros((D_MODEL, n), w_in.dtype)
    cols = [z, xbc, q, g, k, v, qi, ki, zeros(64), dt, wi, zeros(LANE - SSM_HEADS - IDX_HEADS)]
    return jnp.concatenate(cols, axis=1).astype(BF16)


def _in_proj(x2d, norm_g, w_packed, tm):
    n = x2d.shape[0]
    assert n % tm == 0
    bf = lambda c: jax.ShapeDtypeStruct((n, c), BF16)
    ff = lambda c: jax.ShapeDtypeStruct((n, c), F32)
    clip = lambda j, lo, hi: jnp.clip(j - lo, 0, hi - lo - 1)
    return pl.pallas_call(
        _inproj_kernel,
        out_shape=(bf(SSM_INNER), bf(CONV_DIM), bf(ATTN_WIDTH), bf(2 * D_MODEL),
                   ff(128), ff(128), bf(512), ff(IDX_DIM), ff(128), bf(128), bf(128), bf(IDX_DIM)),
        grid=(n // tm, PROJ_TILES),
        in_specs=[
            pl.BlockSpec((tm, D_MODEL), lambda i, j: (i, 0)),
            pl.BlockSpec((1, D_MODEL), lambda i, j: (0, 0)),
            pl.BlockSpec((D_MODEL, PROJ_TILE), lambda i, j: (0, j)),
        ],
        out_specs=(
            pl.BlockSpec((tm, PROJ_TILE), lambda i, j: (i, clip(j, 0, 2))),
            pl.BlockSpec((tm, PROJ_TILE), lambda i, j: (i, clip(j, 2, 5))),
            pl.BlockSpec((tm, PROJ_TILE), lambda i, j: (i, 0)),
            pl.BlockSpec((tm, PROJ_TILE), lambda i, j: (i, clip(j, 6, 8))),
            pl.BlockSpec((tm, 128), lambda i, j: (i, 0)),
            pl.BlockSpec((tm, 128), lambda i, j: (i, 0)),
            pl.BlockSpec((tm, 512), lambda i, j: (i, 0)),
            pl.BlockSpec((tm, IDX_DIM), lambda i, j: (i, 0)),
            pl.BlockSpec((tm, 128), lambda i, j: (i, 0)),
            pl.BlockSpec((tm, 128), lambda i, j: (i, 0)),
            pl.BlockSpec((tm, 128), lambda i, j: (i, 0)),
            pl.BlockSpec((tm, IDX_DIM), lambda i, j: (i, 0)),
        ),
        scratch_shapes=[pltpu.VMEM((tm, D_MODEL), BF16)],
        compiler_params=pltpu.CompilerParams(
            dimension_semantics=("arbitrary", "arbitrary"), vmem_limit_bytes=VMEM_LIMIT),
        name="in_proj",
    )(x2d, norm_g.reshape(1, D_MODEL), w_packed)


def _ssd_kernel(*refs, L, cps, nsteps, has_state):
    if has_state:
        (xbc_ref, z_ref, dtw_ref, cs_ref, s0_ref, shift_ref, convw_ref, convb_ref, dtb_ref, a_ref,
         dx_ref, ng_ref, e3p_ref, e3l_ref, tri_ref, bd_ref, yn_ref, cnew_ref, sout_ref,
         xe_ref, st) = refs
    else:
        (xbc_ref, z_ref, dtw_ref, shift_ref, convw_ref, convb_ref, dtb_ref, a_ref,
         dx_ref, ng_ref, e3p_ref, e3l_ref, tri_ref, bd_ref, yn_ref, cnew_ref, sout_ref,
         xe_ref, st) = refs
    c = pl.program_id(1)
    G, R, N, P = SSM_GROUPS, SSM_HEADS_PER_GROUP, SSM_STATE, SSM_HEADDIM
    GW = R * P
    RL = R * L
    LB = cps * L

    @pl.when(c == 0)
    def _():
        xe_ref[LB:, :] = jnp.zeros((xe_ref.shape[0] - LB, CONV_DIM), BF16)
        if has_state:
            xe_ref[LB:LB + CONV_CARRY, :] = cs_ref[0]
            st[...] = s0_ref[0].T
        else:
            st[...] = jnp.zeros(st.shape, F32)

    xe_ref[0:LB, :] = xbc_ref[0]
    shifted = jnp.dot(shift_ref[0], xe_ref[...], preferred_element_type=F32)
    w = convw_ref[...]
    conv = convb_ref[...] + shifted[2 * LB:3 * LB, :] * w[0:1, :]
    conv = conv + shifted[LB:2 * LB, :] * w[1:2, :]
    conv = conv + shifted[0:LB, :] * w[2:3, :]
    conv = conv + xbc_ref[0].astype(F32) * w[3:4, :]
    act = _silu(conv)

    @pl.when(c == nsteps - 1)
    def _():
        cnew_ref[0] = xbc_ref[0, LB - CONV_CARRY:LB, :].astype(F32)[
            CONV_CARRY - (CONV_WIDTH - 1):CONV_CARRY, :]

    if nsteps > 1:
        xe_ref[LB:LB + CONV_CARRY, :] = xe_ref[LB - CONV_CARRY:LB, :]

    u = dtw_ref[0] + dtb_ref[...]
    dt = jnp.maximum(u, 0.0) + jnp.log1p(jnp.exp(-jnp.abs(u)))
    a = dt * a_ref[...]
    ac3 = jnp.dot(tri_ref[...], _split3(a), preferred_element_type=F32)
    a_cum = ac3[:, 0:LANE] + ac3[:, LANE:2 * LANE] + ac3[:, 2 * LANE:]

    ex = jnp.dot(_split3(jnp.concatenate([a_cum, dt], axis=0)), e3p_ref[...],
                 preferred_element_type=F32)
    if L == P:
        acl_all = ex[:LB]
    else:
        acl_all = jnp.dot(_split3(a_cum), e3l_ref[...], preferred_element_type=F32)
    zs = _silu(z_ref[0].astype(F32))
    row = lax.broadcasted_iota(I32, (L, RL), 0)
    scol = lax.broadcasted_iota(I32, (L, RL), 1) & (L - 1)

    for sub in range(cps):
        rows = slice(sub * L, (sub + 1) * L)
        xs = act[rows, :SSM_INNER]
        bmat = act[rows, SSM_INNER:SSM_INNER + G * N].astype(BF16)
        cmat = act[rows, SSM_INNER + G * N:].astype(BF16)
        acx = ex[rows]
        dtx = ex[LB + sub * L:LB + (sub + 1) * L]
        acl = acl_all[rows]
        xdt = xs * dtx
        xdt_b = xdt.astype(BF16)
        eacx = jnp.exp(acx)
        alast = acx[L - 1:L, :]
        xw = (xdt * jnp.exp(alast - acx)).astype(BF16)
        ealast = jnp.exp(alast)

        ys = []
        for g in range(G):
            sl = slice(g * GW, (g + 1) * GW)
            b_g = bmat[:, g * N:(g + 1) * N]
            c_g = cmat[:, g * N:(g + 1) * N]
            acl_g = acl[:, g * RL:(g + 1) * RL]
            acs_g = jnp.sum(jnp.where(scol == row, acl_g, 0.0), axis=0, keepdims=True)
            decay = jnp.where(scol <= row, jnp.exp(acl_g - acs_g), 0.0)
            cb = lax.dot_general(c_g, jnp.concatenate([b_g] * R, axis=0), NT_DIMS,
                                 preferred_element_type=F32)
            m_g = (cb * decay).astype(BF16)
            xbd = jnp.concatenate([xdt_b[:, sl]] * R, axis=0) * bd_ref[...]
            y_in = jnp.dot(m_g, xbd, preferred_element_type=F32)
            st_g = st[:, sl]
            y_st = jnp.dot(c_g, st_g.astype(BF16), preferred_element_type=F32)
            ys.append(y_in + eacx[:, sl] * y_st)
            upd = lax.dot_general(b_g, xw[:, sl], TN_DIMS, preferred_element_type=F32)
            st[:, sl] = st_g * ealast[:, sl] + upd
        y = jnp.concatenate(ys, axis=1) + dx_ref[...] * xs

        yz = y * zs[rows]
        ms = jnp.mean(yz * yz, axis=-1, keepdims=True)
        yn_ref[0, rows, :] = ((yz * lax.rsqrt(ms + EPS)) * ng_ref[...]).astype(BF16)

    @pl.when(c == nsteps - 1)
    def _():
        sout_ref[0] = st[...].T


def _shift_matrices(L, conv_rows):
    out = np.zeros((2, 3 * L, conv_rows), np.float32)
    for k in range(CONV_WIDTH - 1):
        for t in range(L):
            src = t - (k + 1)
            if src >= 0:
                out[:, k * L + t, src] = 1.0
            else:
                out[1, k * L + t, L + CONV_CARRY + src] = 1.0
                j = CONV_WIDTH - 1 + src
                for piece in range(3):
                    out[0, k * L + t, L + 3 * piece + j] = 1.0
    return jnp.asarray(out, BF16)


def _expand_matrix(width_per_head):
    r = np.arange(3 * LANE)[:, None] % LANE
    c = np.arange(SSM_HEADS * width_per_head)[None, :] // width_per_head
    return jnp.asarray((r == c).astype(np.float32), BF16)


def _ssd(xbc, z, dtw, conv_state, ssm_state, conv_w, conv_b, dt_bias, a_log, d_skip, ssm_norm_g):
    b, t, _ = xbc.shape
    L = min(CHUNK, t)
    nc = t // L
    assert t % L == 0 and L & (L - 1) == 0
    has_state = conv_state is not None
    pad = lambda v: jnp.pad(v.astype(F32), (0, LANE - SSM_HEADS)).reshape(1, LANE)
    a_neg = pad(-jnp.exp(a_log.astype(F32)))
    cps = SSD_CHUNKS_PER_STEP if nc % SSD_CHUNKS_PER_STEP == 0 else 1
    nsteps = nc // cps
    lb = cps * L
    conv_rows = _round_up(lb + CONV_CARRY, LANE)
    tri = jnp.asarray(np.kron(np.eye(cps), np.tril(np.ones((L, L)))).astype(np.float32), BF16)
    rl = SSM_HEADS_PER_GROUP * L
    gw = SSM_HEADS_PER_GROUP * SSM_HEADDIM
    bd = jnp.asarray((np.arange(rl)[:, None] // L == np.arange(gw)[None, :] // SSM_HEADDIM)
                     .astype(np.float32), BF16)
    const = lambda shape: pl.BlockSpec(shape, lambda i, j: (0,) * len(shape))
    tok = lambda w: pl.BlockSpec((1, lb, w), lambda i, j: (i, j, 0))
    per_b = lambda s: pl.BlockSpec((1,) + s, lambda i, j: (i, 0, 0))
    ins = [xbc, z, dtw]
    specs = [tok(CONV_DIM), tok(SSM_INNER), tok(LANE)]
    if has_state:
        triples = _split3(conv_state.astype(F32))
        ins += [_pad_axis(triples, 1, CONV_CARRY), ssm_state.reshape(b, SSM_INNER, SSM_STATE)]
        specs += [per_b((CONV_CARRY, CONV_DIM)), per_b((SSM_INNER, SSM_STATE))]
    first = 0 if has_state else 1
    ins += [_shift_matrices(lb, conv_rows)]
    specs += [pl.BlockSpec((1, 3 * lb, conv_rows),
                           lambda i, j: (jnp.where(j == 0, first, 1), 0, 0))]
    ins += [conv_w, conv_b.reshape(1, CONV_DIM), pad(dt_bias), a_neg,
            jnp.repeat(d_skip.astype(F32), SSM_HEADDIM).reshape(1, SSM_INNER),
            ssm_norm_g.reshape(1, SSM_INNER), _expand_matrix(SSM_HEADDIM), _expand_matrix(L),
            tri, bd]
    specs += [const((CONV_WIDTH, CONV_DIM)), const((1, CONV_DIM)), const((1, LANE)),
              const((1, LANE)), const((1, SSM_INNER)), const((1, SSM_INNER)),
              const((3 * LANE, SSM_INNER)), const((3 * LANE, SSM_HEADS * L)),
              const((lb, lb)), const((rl, gw))]
    yn, cnew, sout = pl.pallas_call(
        functools.partial(_ssd_kernel, L=L, cps=cps, nsteps=nsteps, has_state=has_state),
        out_shape=(jax.ShapeDtypeStruct((b, t, SSM_INNER), BF16),
                   jax.ShapeDtypeStruct((b, CONV_WIDTH - 1, CONV_DIM), F32),
                   jax.ShapeDtypeStruct((b, SSM_INNER, SSM_STATE), F32)),
        grid=(b, nsteps),
        in_specs=specs,
        out_specs=(tok(SSM_INNER), per_b((CONV_WIDTH - 1, CONV_DIM)),
                   per_b((SSM_INNER, SSM_STATE))),
        scratch_shapes=[pltpu.VMEM((conv_rows, CONV_DIM), BF16),
                        pltpu.VMEM((SSM_STATE, SSM_INNER), F32)],
        compiler_params=pltpu.CompilerParams(
            dimension_semantics=("arbitrary", "arbitrary"), vmem_limit_bytes=VMEM_LIMIT),
        name="ssd",
    )(*ins)
    return yn, cnew, sout.reshape(b, SSM_HEADS, SSM_HEADDIM, SSM_STATE)


def _dsa_kernel(q_ref, qi_ref, w_ref, eq_ref, kidx_ref, k_ref, v_ref, o_ref,
                qs_ref, qx_ref, key_ref, kx_ref, vx_ref, m_ref, acc_ref, mlim_ref, s_ref,
                *, past_len, l_valid, topk, nq):
    qb = pl.program_id(1)
    TQ, TK = LANE, KEY_TILE
    W = ATTN_REP * nq
    wide = nq == LANE

    lane = lax.broadcasted_iota(I32, (1, TQ), 1)
    pos = past_len + qb * nq + (lane & (nq - 1))
    n_adm = jnp.minimum(((pos >> CHUNK_SHIFT) + 1) * CHUNK, l_valid)
    k_eff = jnp.minimum(n_adm, topk)
    last_pos = past_len + qb * nq + nq - 1
    n_max = jnp.minimum(((last_pos >> CHUNK_SHIFT) + 1) * CHUNK, l_valid)
    nt = (n_max + TK - 1) // TK
    ntp = 2 * ((nt + 1) // 2)
    lp = key_ref.shape[0]

    @pl.when(qb == 0)
    def _():
        kx_ref[:, LANE:] = k_ref[0]

        def v_tile(t, carry):
            s0 = pl.multiple_of(t * TK, TK)
            v_t = v_ref[0, pl.ds(s0, TK), :].astype(F32).T
            for g in range(ATTN_KV_HEADS):
                vx_ref[g, 0:HEAD_DIM, pl.ds(s0, TK)] = (
                    v_t[g * HEAD_DIM:(g + 1) * HEAD_DIM, :].astype(BF16))
            return carry

        lax.fori_loop(0, lp // TK, v_tile, 0)
        tail_row = lax.broadcasted_iota(I32, (PV_ROWS - HEAD_DIM, lp), 0)
        for g in range(ATTN_KV_HEADS):
            vx_ref[g, HEAD_DIM:, :] = jnp.where(tail_row == 0, 1.0, 0.0).astype(BF16)
            qx_ref[g, :, 0:LANE] = eq_ref[...]
            qx_ref[g, :, LANE:] = jnp.zeros((W, LANE), BF16)

    if wide:
        for h in range(IDX_HEADS):
            qs_ref[h * nq:(h + 1) * nq, :] = qi_ref[0, :, h * IDX_DIM:(h + 1) * IDX_DIM]
        w_t = w_ref[0].T[SSM_HEADS:SSM_HEADS + IDX_HEADS, :] * IDX_SCALE
    else:
        qs_ref[...] = qi_ref[0]
        w_row = w_ref[0] * IDX_SCALE

    def index_dots(t, slot):
        s0 = pl.multiple_of(jnp.minimum(t, ntp - 1) * TK, TK)
        s_ref[0, slot] = lax.dot_general(kidx_ref[0, pl.ds(s0, TK), :], qs_ref[...], NT_DIMS,
                                         preferred_element_type=F32)

    def score_tile(t, slot):
        s0 = pl.multiple_of(t * TK, TK)
        d = s_ref[0, slot]
        if wide:
            sc = w_t[0:1, :] * jnp.maximum(d[:, 0:nq], 0.0)
            for h in range(1, IDX_HEADS):
                sc = sc + w_t[h:h + 1, :] * jnp.maximum(d[:, h * nq:(h + 1) * nq], 0.0)
        else:
            sc = w_row * jnp.maximum(d, 0.0)
            for shift in (64, 32, 16):
                sc = sc + pltpu.roll(sc, shift, axis=1)
        sc = sc + 0.0
        bits = pltpu.bitcast(sc, I32)
        key = jnp.where(bits < 0, bits ^ np.int32(0x7FFFFFFF), bits)
        s_idx = s0 + lax.broadcasted_iota(I32, (TK, TQ), 0)
        key_ref[pl.ds(s0, TK), :] = jnp.where(s_idx < n_adm, key, INT_MIN)

    index_dots(0, 0)

    def score_pair(u, carry):
        index_dots(2 * u + 1, 1)
        score_tile(2 * u, 0)
        index_dots(2 * u + 2, 0)
        score_tile(2 * u + 1, 1)
        return carry

    lax.fori_loop(0, ntp // 2, score_pair, 0)

    def count(pred):
        def body(t, acc):
            s0 = pl.multiple_of(t * (2 * TK), 2 * TK)
            blk = key_ref[pl.ds(s0, 2 * TK), :]
            s_idx = s0 + lax.broadcasted_iota(I32, (2 * TK, TQ), 0)
            hit = jnp.where(pred(blk, s_idx), 1, 0).astype(I32)
            return acc + jnp.sum(hit.reshape(2 * TK // 8, 8, TQ), axis=0)
        acc = lax.fori_loop(0, ntp // 2, body, jnp.zeros((8, TQ), I32))
        return jnp.sum(acc, axis=0, keepdims=True)

    def bit_step(i, prefix):
        cand = prefix | jnp.left_shift(jnp.int32(1), 31 - i)
        cand_s = cand ^ INT_MIN
        cnt = count(lambda blk, s_idx: blk >= cand_s)
        return jnp.where(cnt >= k_eff, cand, prefix)

    thr = lax.fori_loop(0, 32, bit_step, jnp.zeros((1, TQ), I32)) ^ INT_MIN

    n_gt = count(lambda blk, s_idx: blk > thr)
    n_eq = count(lambda blk, s_idx: blk == thr)
    need = k_eff - n_gt
    mlim_ref[...] = jnp.full((1, TQ), 2 ** 30, I32)

    @pl.when(jnp.max(n_eq - need) > 0)
    def _():
        def idx_step(i, prefix):
            cand = prefix | jnp.left_shift(jnp.int32(1), 14 - i)
            cnt = count(lambda blk, s_idx: (blk == thr) & (s_idx < cand))
            return jnp.where(cnt < need, cand, prefix)
        mlim_ref[...] = lax.fori_loop(0, 15, idx_step, jnp.zeros((1, TQ), I32))

    mlim = mlim_ref[...]

    def bias_tile(t, carry):
        s0 = pl.multiple_of(t * TK, TK)
        blk = key_ref[pl.ds(s0, TK), :]
        s_idx = s0 + lax.broadcasted_iota(I32, (TK, TQ), 0)
        sel = (blk > thr) | ((blk == thr) & (s_idx <= mlim))
        kx_ref[pl.ds(s0, TK), 0:LANE] = jnp.where(sel, 0.0, NEG_BIG).astype(BF16)
        return carry

    lax.fori_loop(0, nt, bias_tile, 0)

    @pl.when(nt < ntp)
    def _():
        kx_ref[pl.ds(pl.multiple_of(nt * TK, TK), TK), 0:LANE] = jnp.full((TK, LANE), NEG_BIG,
                                                                           BF16)

    for g in range(ATTN_KV_HEADS):
        lanes = slice(LANE + g * HEAD_DIM, LANE + (g + 1) * HEAD_DIM)
        if wide:
            for r in range(ATTN_REP):
                hq = g * ATTN_REP + r
                qx_ref[g, r * nq:(r + 1) * nq, lanes] = (
                    q_ref[0, :, hq * HEAD_DIM:(hq + 1) * HEAD_DIM])
        else:
            qx_ref[g, :, lanes] = q_ref[0, g]
    m_ref[...] = jnp.full(m_ref.shape, -jnp.inf, F32)
    acc_ref[...] = jnp.zeros(acc_ref.shape, F32)

    def scores(t, slot):
        s0 = pl.multiple_of(jnp.minimum(t, ntp - 1) * TK, TK)
        k_tile = kx_ref[pl.ds(s0, TK), :]
        for g in range(ATTN_KV_HEADS):
            s_ref[g, slot] = lax.dot_general(k_tile, qx_ref[g], NT_DIMS,
                                             preferred_element_type=F32)

    def accumulate(t, slot):
        s0 = pl.multiple_of(t * TK, TK)
        for g in range(ATTN_KV_HEADS):
            s = s_ref[g, slot]
            m_prev = m_ref[g]
            m_new = jnp.maximum(m_prev, jnp.max(s, axis=0, keepdims=True))
            p = jnp.exp2(s - m_new)
            acc_ref[g] = jnp.exp2(m_prev - m_new) * acc_ref[g] + jnp.dot(
                vx_ref[g, :, pl.ds(s0, TK)], p.astype(BF16), preferred_element_type=F32)
            m_ref[g] = m_new

    scores(0, 0)

    def attn_pair(u, carry):
        scores(2 * u + 1, 1)
        accumulate(2 * u, 0)
        scores(2 * u + 2, 0)
        accumulate(2 * u + 1, 1)
        return carry

    lax.fori_loop(0, ntp // 2, attn_pair, 0)
    for g in range(ATTN_KV_HEADS):
        acc = acc_ref[g]
        o_t = acc[0:HEAD_DIM, :] * (1.0 / acc[HEAD_DIM:HEAD_DIM + 1, :])
        if wide:
            for r in range(0, ATTN_REP, 2):
                hq = g * ATTN_REP + r
                pair = jnp.concatenate([o_t[:, r * nq:(r + 1) * nq],
                                        o_t[:, (r + 1) * nq:(r + 2) * nq]], axis=0)
                o_ref[0, :, hq * HEAD_DIM:(hq + 2) * HEAD_DIM] = pair.T.astype(BF16)
        else:
            o_rq = jnp.concatenate([o_t, o_t], axis=0).T[:, 0:HEAD_DIM].astype(BF16)
            for r in range(ATTN_REP):
                hq = g * ATTN_REP + r
                o_ref[0, :, hq * HEAD_DIM:(hq + 1) * HEAD_DIM] = o_rq[r * nq:(r + 1) * nq, :]


def _dsa(q, qi, w, kidx_all, k_all, v_all, past_len, l_valid, t):
    b = kidx_all.shape[0]
    lp = kidx_all.shape[1]
    nq = Q_BLOCK if t % Q_BLOCK == 0 else SMALL_Q_BLOCK
    assert t % nq == 0 and lp % (2 * KEY_TILE) == 0 and lp < 2 ** 15
    topk = min(TOPK_MAX, l_valid // 4)
    width = ATTN_REP * nq
    if nq == Q_BLOCK:
        qblk = lambda c: pl.BlockSpec((1, nq, c), lambda i, j: (i, j, 0))
        q_specs = [qblk(ATTN_WIDTH), qblk(IDX_HEADS * IDX_DIM), qblk(LANE)]
    else:
        assert t == nq
        q_specs = [pl.BlockSpec((1, ATTN_KV_HEADS, width, HEAD_DIM), lambda i, j: (i, 0, 0, 0)),
                   pl.BlockSpec((1, IDX_HEADS * nq, IDX_DIM), lambda i, j: (i, 0, 0)),
                   pl.BlockSpec((1, 1, LANE), lambda i, j: (i, 0, 0))]
    slot_onehot = jnp.asarray(
        (np.arange(width)[:, None] % nq == np.arange(LANE)[None, :]).astype(np.float32), BF16)
    per_stream = lambda c: pl.BlockSpec((1, lp, c), lambda i, j: (i, 0, 0))
    return pl.pallas_call(
        functools.partial(_dsa_kernel, past_len=past_len, l_valid=l_valid, topk=topk, nq=nq),
        out_shape=jax.ShapeDtypeStruct((b, t, ATTN_WIDTH), BF16),
        grid=(b, t // nq),
        in_specs=q_specs + [pl.BlockSpec((width, LANE), lambda i, j: (0, 0)),
                            per_stream(IDX_DIM), per_stream(LANE), per_stream(LANE)],
        out_specs=pl.BlockSpec((1, nq, ATTN_WIDTH), lambda i, j: (i, j, 0)),
        scratch_shapes=[pltpu.VMEM((width, IDX_DIM), BF16),
                        pltpu.VMEM((ATTN_KV_HEADS, width, 2 * LANE), BF16),
                        pltpu.VMEM((lp, LANE), I32),
                        pltpu.VMEM((lp, 2 * LANE), BF16),
                        pltpu.VMEM((ATTN_KV_HEADS, PV_ROWS, lp), BF16),
                        pltpu.VMEM((ATTN_KV_HEADS, 1, width), F32),
                        pltpu.VMEM((ATTN_KV_HEADS, PV_ROWS, width), F32),
                        pltpu.VMEM((1, LANE), I32),
                        pltpu.VMEM((ATTN_KV_HEADS, 2, KEY_TILE, width), F32)],
        compiler_params=pltpu.CompilerParams(
            dimension_semantics=("arbitrary", "arbitrary"), vmem_limit_bytes=VMEM_LIMIT),
        name="dsa",
    )(q, qi, w, slot_onehot, kidx_all, k_all, v_all)


def _merge_kernel(x_ref, yn_ref, o_ref, gate_ref, wa_ref, wb_ref, wo_ref, out_ref):
    br_a = jnp.dot(yn_ref[...], wa_ref[...], preferred_element_type=F32)
    br_b = jnp.dot(o_ref[...], wb_ref[...], preferred_element_type=F32)
    gates = gate_ref[...].astype(F32)
    merged = _sigmoid(gates[:, :D_MODEL]) * br_a + _sigmoid(gates[:, D_MODEL:]) * br_b
    out_ref[...] = x_ref[...] + jnp.dot(merged.astype(BF16), wo_ref[...],
                                        preferred_element_type=F32)


def _merge(x2d, yn, o, gate, w_ssm_out, w_attn_out, w_o, tm):
    n = x2d.shape[0]
    assert n % tm == 0
    row = lambda w: pl.BlockSpec((tm, w), lambda i: (i, 0))
    full = lambda r, c: pl.BlockSpec((r, c), lambda i: (0, 0))
    return pl.pallas_call(
        _merge_kernel,
        out_shape=jax.ShapeDtypeStruct((n, D_MODEL), F32),
        grid=(n // tm,),
        in_specs=[row(D_MODEL), row(SSM_INNER), row(ATTN_WIDTH), row(2 * D_MODEL),
                  full(SSM_INNER, D_MODEL), full(ATTN_WIDTH, D_MODEL), full(D_MODEL, D_MODEL)],
        out_specs=row(D_MODEL),
        compiler_params=pltpu.CompilerParams(
            dimension_semantics=("arbitrary",), vmem_limit_bytes=VMEM_LIMIT),
        name="merge",
    )(x2d, yn, o, gate, w_ssm_out.astype(BF16), w_attn_out.astype(BF16), w_o.astype(BF16))


def _moe_kernel(x_ref, ng_ref, wr_ref, br_ref, w1_ref, w3_ref, w2_ref, fg_ref, tri_ref, out_ref,
                hs_ref, combs_ref, acc_ref, pt_ref, seg_ref, *, n_steps, blk_rows):
    e_step = pl.program_id(1)
    tm = x_ref.shape[0]
    sr = hs_ref.shape[0]
    BLK = blk_rows

    @pl.when(e_step == 0)
    def _():
        x = x_ref[...]
        ms = jnp.mean(x * x, axis=-1, keepdims=True)
        h = ((x * lax.rsqrt(ms + EPS)) * ng_ref[...]).astype(BF16)
        logits = jnp.dot(h, wr_ref[...], preferred_element_type=F32) + br_ref[...]
        lane_i = lax.broadcasted_iota(I32, (tm, LANE), 1)
        lane = lane_i.astype(F32)
        first = lambda hit: jnp.min(jnp.where(hit, lane, float(LANE)), axis=1, keepdims=True)
        is_g = (lane_i >= N_EXPERTS) & (lane_i < N_EXPERTS + MOE_GROUPS)
        glog = jnp.where(is_g, logits, -jnp.inf)
        gmax = jnp.max(glog, axis=1, keepdims=True)
        gsel = first(glog == gmax) - float(N_EXPERTS)
        p_group = 1.0 / jnp.sum(jnp.exp(glog - gmax), axis=1, keepdims=True)
        grp = (lane_i >> EXPERT_GROUP_SHIFT).astype(F32)
        in_grp = (lane_i < N_EXPERTS) & (grp == gsel)
        el = jnp.where(in_grp, logits, -jnp.inf)
        v1 = jnp.max(el, axis=1, keepdims=True)
        i1 = first(el == v1)
        el2 = jnp.where(lane == i1, -jnp.inf, el)
        v2 = jnp.max(el2, axis=1, keepdims=True)
        i2 = first(el2 == v2)
        e21 = jnp.exp(v2 - v1)
        den = 1.0 + e21
        comb = (jnp.where(lane == i1, (1.0 / den) * p_group, 0.0)
                + jnp.where(lane == i2, (e21 / den) * p_group, 0.0))

        own = lane == gsel
        own_f = jnp.where(own, 1.0, 0.0)
        before = jnp.dot(tri_ref[...], own_f.astype(BF16), preferred_element_type=F32)
        rank = jnp.sum(jnp.where(own, before, 0.0), axis=1, keepdims=True)
        counts = jnp.sum(own_f, axis=0, keepdims=True)
        blocks = jnp.floor((counts + (BLK - 1)) * (1.0 / BLK))
        upper = (lax.broadcasted_iota(I32, (LANE, LANE), 0)
                 < lax.broadcasted_iota(I32, (LANE, LANE), 1))
        starts = jnp.dot(jnp.broadcast_to(blocks, (8, LANE)).astype(BF16),
                         jnp.where(upper, 1.0, 0.0).astype(BF16),
                         preferred_element_type=F32)[0:1, :] * BLK
        dest = jnp.sum(jnp.where(own, starts, 0.0), axis=1, keepdims=True) + rank
        for g in range(MOE_GROUPS):
            seg_ref[g] = jnp.sum(starts[:, g:g + 1]).astype(I32)
            seg_ref[MOE_GROUPS + g] = jnp.sum(blocks[:, g:g + 1]).astype(I32)

        dest_i = dest.astype(I32)
        pt = jnp.where(dest_i == lax.broadcasted_iota(I32, (tm, sr), 1), 1.0, 0.0).astype(BF16)
        pt_ref[...] = pt
        dest_row = jnp.broadcast_to(dest, (tm, LANE)).T[0:1, :].astype(I32)
        p_mat = jnp.where(lax.broadcasted_iota(I32, (sr, tm), 0) == dest_row, 1.0, 0.0
                          ).astype(BF16)
        hs_ref[...] = jnp.dot(p_mat, h, preferred_element_type=F32).astype(BF16)
        c3 = jnp.dot(p_mat, _split3(comb), preferred_element_type=F32)
        combs_ref[...] = c3[:, 0:LANE] + c3[:, LANE:2 * LANE] + c3[:, 2 * LANE:]
        acc_ref[...] = jnp.zeros(acc_ref.shape, F32)

    group = e_step // (EXPERTS_PER_GROUP // EXPERTS_PER_STEP)
    seg_start = seg_ref[group]
    lane = lax.broadcasted_iota(I32, (BLK, LANE), 1)

    def row_block(blk, carry):
        rows = pl.ds(pl.multiple_of(seg_start + blk * BLK, BLK), BLK)
        h = hs_ref[rows, :]
        comb = combs_ref[rows, :]
        for k in range(EXPERTS_PER_STEP):
            e = e_step * EXPERTS_PER_STEP + k
            a1 = jnp.dot(h, w1_ref[k], preferred_element_type=F32)
            a3 = jnp.dot(h, w3_ref[k], preferred_element_type=F32)
            he = (_silu(a1) * a3).astype(BF16)
            ye = jnp.dot(he, w2_ref[k], preferred_element_type=F32)
            wcol = jnp.sum(jnp.where(lane == e, comb, 0.0), axis=1, keepdims=True)
            acc_ref[rows, :] += wcol * ye
        return carry

    lax.fori_loop(0, seg_ref[MOE_GROUPS + group], row_block, 0)

    @pl.when(e_step == n_steps - 1)
    def _():
        acc = acc_ref[...]
        hi = acc.astype(BF16)
        lo = (acc - hi.astype(F32)).astype(BF16)
        pt = pt_ref[...]
        y = x_ref[...] + (jnp.dot(pt, hi, preferred_element_type=F32)
                          + jnp.dot(pt, lo, preferred_element_type=F32))
        ms = jnp.mean(y * y, axis=-1, keepdims=True)
        out_ref[...] = (y * lax.rsqrt(ms + EPS)) * fg_ref[...]


def _moe(x2d, norm_g, w_router_group, b_router_group, w_router_expert, b_router_expert,
         w1, w3, w2, norm_final_g, tm):
    n = x2d.shape[0]
    blk_rows = min(MOE_ROW_BLOCK, tm)
    assert n % tm == 0 and tm % blk_rows == 0
    n_steps = N_EXPERTS // EXPERTS_PER_STEP
    sr = tm + (MOE_GROUPS - 1) * blk_rows
    padw = LANE - N_EXPERTS - MOE_GROUPS
    wr = jnp.concatenate([w_router_expert, w_router_group,
                          jnp.zeros((D_MODEL, padw), F32)], axis=1).astype(BF16)
    br = jnp.concatenate([b_router_expert, b_router_group,
                          jnp.zeros((padw,), F32)]).astype(F32).reshape(1, LANE)
    tri = jnp.asarray(np.tril(np.ones((tm, tm), np.float32), -1), BF16)
    row = pl.BlockSpec((tm, D_MODEL), lambda i, e: (i, 0))
    const = lambda r, c: pl.BlockSpec((r, c), lambda i, e: (0, 0))
    return pl.pallas_call(
        functools.partial(_moe_kernel, n_steps=n_steps, blk_rows=blk_rows),
        out_shape=jax.ShapeDtypeStruct((n, D_MODEL), F32),
        grid=(n // tm, n_steps),
        in_specs=[row, const(1, D_MODEL), const(D_MODEL, LANE), const(1, LANE),
                  pl.BlockSpec((EXPERTS_PER_STEP, D_MODEL, EXPERT_FF), lambda i, e: (e, 0, 0)),
                  pl.BlockSpec((EXPERTS_PER_STEP, D_MODEL, EXPERT_FF), lambda i, e: (e, 0, 0)),
                  pl.BlockSpec((EXPERTS_PER_STEP, EXPERT_FF, D_MODEL), lambda i, e: (e, 0, 0)),
                  const(1, D_MODEL), const(tm, tm)],
        out_specs=row,
        scratch_shapes=[pltpu.VMEM((sr, D_MODEL), BF16), pltpu.VMEM((sr, LANE), F32),
                        pltpu.VMEM((sr, D_MODEL), F32), pltpu.VMEM((tm, sr), BF16),
                        pltpu.SMEM((2 * MOE_GROUPS,), I32)],
        compiler_params=pltpu.CompilerParams(
            dimension_semantics=("arbitrary", "arbitrary"), vmem_limit_bytes=VMEM_LIMIT),
        name="moe",
    )(x2d, norm_g.reshape(1, D_MODEL), wr, br, w1.astype(BF16), w3.astype(BF16),
      w2.astype(BF16), norm_final_g.reshape(1, D_MODEL), tri)


def _token_tile(n):
    for tm in (1024, 512, 256, 128, 64, 32, 16, 8):
        if n % tm == 0:
            return tm
    raise ValueError(f"token count {n} is not a multiple of 8")


def _pad_axis(a, axis, size):
    if a.shape[axis] == size:
        return a
    widths = [(0, 0)] * a.ndim
    widths[axis] = (0, size - a.shape[axis])
    return jnp.pad(a, widths)


def _round_up(n, m):
    return (n + m - 1) // m * m


def _stream_step(x, conv_state, ssm_state, k_past, v_past, kidx_past, w_packed, p):
    b, t, d = x.shape
    n = b * t
    tm = _token_tile(n)
    x2d = x.reshape(n, d)
    z, xbc, q, gate, k, v, qi, ki, dtw, k_b, v_b, ki_b = _in_proj(
        x2d, p["norm_mix_g"], w_packed, tm)

    yn, conv_new, ssm_new = _ssd(
        xbc.reshape(b, t, CONV_DIM), z.reshape(b, t, SSM_INNER), dtw.reshape(b, t, LANE),
        conv_state, ssm_state, p["conv_w"], p["conv_b"], p["dt_bias"], p["A_log"],
        p["D_skip"], p["ssm_norm_g"])

    kv_w = ATTN_KV_HEADS * HEAD_DIM
    k_all, v_all, kidx_all = (k_b.reshape(b, t, kv_w), v_b.reshape(b, t, kv_w),
                              ki_b.reshape(b, t, IDX_DIM))
    past_len = 0
    if k_past is not None:
        past_len = k_past.shape[1]
        k_all = jnp.concatenate([k_past.astype(BF16).reshape(b, past_len, kv_w), k_all], axis=1)
        v_all = jnp.concatenate([v_past.astype(BF16).reshape(b, past_len, kv_w), v_all], axis=1)
        kidx_all = jnp.concatenate([kidx_past.astype(BF16), kidx_all], axis=1)
    l_valid = past_len + t
    lp = _round_up(l_valid, 2 * KEY_TILE)
    if t % Q_BLOCK == 0:
        q_in = (q.reshape(b, t, ATTN_WIDTH), qi.reshape(b, t, IDX_HEADS * IDX_DIM),
                dtw.reshape(b, t, LANE))
    else:
        q_in = (q.reshape(b, t, ATTN_KV_HEADS, ATTN_REP, HEAD_DIM).transpose(0, 2, 3, 1, 4)
                .reshape(b, ATTN_KV_HEADS, ATTN_REP * t, HEAD_DIM),
                qi.reshape(b, t, IDX_HEADS, IDX_DIM).transpose(0, 2, 1, 3)
                .reshape(b, IDX_HEADS * t, IDX_DIM),
                dtw.reshape(b, t, LANE)[:, :, SSM_HEADS:SSM_HEADS + IDX_HEADS]
                .transpose(0, 2, 1).reshape(b, 1, IDX_HEADS * t))
    o = _dsa(*q_in, _pad_axis(kidx_all, 1, lp), _pad_axis(k_all, 1, lp), _pad_axis(v_all, 1, lp),
             past_len, l_valid, t)
    o = o.reshape(n, ATTN_WIDTH)

    x1 = _merge(x2d, yn.reshape(n, SSM_INNER), o, gate, p["w_ssm_out"], p["w_attn_out"],
                p["w_o"], min(tm, MERGE_TILE))
    y = _moe(x1, p["norm_moe_g"], p["w_router_group"], p["b_router_group"],
             p["w_router_expert"], p["b_router_expert"], p["w1"], p["w3"], p["w2"],
             p["norm_final_g"], tm)
    return (y.reshape(b, t, d), k.reshape(b, t, ATTN_KV_HEADS, HEAD_DIM),
            v.reshape(b, t, ATTN_KV_HEADS, HEAD_DIM), ki.reshape(b, t, IDX_DIM), conv_new, ssm_new)


def kernel(x_prompt, x_sample, cache_k, cache_v, cache_kidx, state_conv, state_ssm, norm_mix_g, w_in, conv_w, conv_b, dt_bias, A_log, D_skip, ssm_norm_g, w_ssm_out, w_attn_out, w_o, norm_moe_g, w_router_group, b_router_group, w_router_expert, b_router_expert, w1, w3, w2, norm_final_g):
    p = dict(norm_mix_g=norm_mix_g, conv_w=conv_w, conv_b=conv_b, dt_bias=dt_bias, A_log=A_log,
             D_skip=D_skip, ssm_norm_g=ssm_norm_g, w_ssm_out=w_ssm_out, w_attn_out=w_attn_out,
             w_o=w_o, norm_moe_g=norm_moe_g, w_router_group=w_router_group,
             b_router_group=b_router_group, w_router_expert=w_router_expert,
             b_router_expert=b_router_expert, w1=w1, w3=w3, w2=w2, norm_final_g=norm_final_g)
    w_packed = _pack_w_in(w_in)
    yp, kp, vp, kip, cp, sp = _stream_step(x_prompt, None, None, None, None, None, w_packed, p)
    ys, ksn, vsn, kisn, csn, ssn = _stream_step(x_sample, state_conv, state_ssm, cache_k,
                                                cache_v, cache_kidx, w_packed, p)
    return (yp, ys, kp, vp, kip, cp, sp, ksn, vsn, kisn, csn, ssn)
```

```python
import functools

import numpy as np
import jax
import jax.numpy as jnp
from jax import lax
from jax.experimental import pallas as pl
from jax.experimental.pallas import tpu as pltpu

F32 = jnp.float32
BF16 = jnp.bfloat16
I32 = jnp.int32

D_MODEL = 1024
CHUNK = 64
CHUNK_SHIFT = 6
EPS = 1e-6
SSM_INNER = 2048
SSM_HEADDIM = 64
SSM_HEADS = 32
SSM_GROUPS = 4
SSM_HEADS_PER_GROUP = SSM_HEADS // SSM_GROUPS
SSM_STATE = 128
CONV_WIDTH = 4
CONV_DIM = SSM_INNER + 2 * SSM_GROUPS * SSM_STATE
CONV_CARRY = 16
SSD_CHUNKS_PER_STEP = 2
ATTN_HEADS = 16
ATTN_KV_HEADS = 2
HEAD_DIM = 64
ATTN_REP = ATTN_HEADS // ATTN_KV_HEADS
ATTN_WIDTH = ATTN_HEADS * HEAD_DIM
IDX_HEADS = 8
IDX_DIM = 64
IDX_SCALE = (IDX_HEADS * IDX_DIM) ** -0.5
TOPK_MAX = 256
MOE_GROUPS = 4
EXPERTS_PER_GROUP = 8
EXPERT_GROUP_SHIFT = 3
N_EXPERTS = 32
EXPERT_FF = 256
N_BRANCHES = 2

LANE = 128
VMEM_LIMIT = 56 * 1024 * 1024
PROJ_TILE = 1024
PROJ_TILES = 9
Q_BLOCK = 128
SMALL_Q_BLOCK = 16
KEY_TILE = 256
EXPERTS_PER_STEP = 4
MOE_ROW_BLOCK = 128
MERGE_TILE = 512
INT_MIN = np.int32(-2 ** 31)
NEG_BIG = -1e30
Q_SCALE = HEAD_DIM ** -0.5 * float(np.log2(np.e))
PV_ROWS = 80

NT_DIMS = (((1,), (1,)), ((), ()))
TN_DIMS = (((0,), (0,)), ((), ()))


def _sigmoid(x):
    return 1.0 / (1.0 + jnp.exp(-x))


def _silu(x):
    h = 0.5 * x
    return h + h * jnp.tanh(h)


def _split3(x):
    hi = x.astype(BF16)
    r1 = x - hi.astype(F32)
    mid = r1.astype(BF16)
    r2 = r1 - mid.astype(F32)
    lo = r2.astype(BF16)
    return jnp.concatenate([hi, mid, lo], axis=1)


def _inproj_kernel(x_ref, g_ref, w_ref, z_ref, xbc_ref, q_ref, gate_ref, k_ref, v_ref,
                   qi_ref, ki_ref, dtw_ref, kb_ref, vb_ref, kib_ref, h_ref):
    j = pl.program_id(1)

    @pl.when(j == 0)
    def _():
        x = x_ref[...]
        ms = jnp.mean(x * x, axis=-1, keepdims=True)
        h_ref[...] = ((x * lax.rsqrt(ms + EPS)) * g_ref[...]).astype(BF16)

    def tile():
        return jnp.dot(h_ref[...], w_ref[...], preferred_element_type=F32)

    @pl.when(j < 2)
    def _():
        z_ref[...] = tile().astype(BF16)

    @pl.when((j >= 2) & (j < 5))
    def _():
        xbc_ref[...] = tile().astype(BF16)

    @pl.when(j == 5)
    def _():
        q_ref[...] = (tile() * Q_SCALE).astype(BF16)

    @pl.when((j >= 6) & (j < 8))
    def _():
        gate_ref[...] = tile().astype(BF16)

    @pl.when(j == 8)
    def _():
        t = tile()
        k_ref[...] = t[:, 0:128]
        v_ref[...] = t[:, 128:256]
        qi_ref[...] = t[:, 256:768].astype(BF16)
        ki_ref[...] = t[:, 768:832]
        dtw_ref[...] = t[:, 896:1024]
        kb_ref[...] = t[:, 0:128].astype(BF16)
        vb_ref[...] = t[:, 128:256].astype(BF16)
        kib_ref[...] = t[:, 768:832].astype(BF16)


def _pack_w_in(w_in):
    sizes = (SSM_INNER, CONV_DIM, SSM_HEADS, ATTN_WIDTH, ATTN_KV_HEADS * HEAD_DIM,
             ATTN_KV_HEADS * HEAD_DIM, IDX_HEADS * IDX_DIM, IDX_DIM, IDX_HEADS,
             N_BRANCHES * D_MODEL)
    offs = np.concatenate([[0], np.cumsum(sizes)])
    z, xbc, dt, q, k, v, qi, ki, wi, g = [w_in[:, offs[i]:offs[i + 1]] for i in range(10)]
    zeros = lambda n: jnp.zeros((D_MODEL, n), w_in.dtype)
    cols = [z, xbc, q, g, k, v, qi, ki, zeros(64), dt, wi, zeros(LANE - SSM_HEADS - IDX_HEADS)]
    return jnp.concatenate(cols, axis=1).astype(BF16)


def _in_proj(x2d, norm_g, w_packed, tm):
    n = x2d.shape[0]
    assert n % tm == 0
    bf = lambda c: jax.ShapeDtypeStruct((n, c), BF16)
    ff = lambda c: jax.ShapeDtypeStruct((n, c), F32)
    clip = lambda j, lo, hi: jnp.clip(j - lo, 0, hi - lo - 1)
    return pl.pallas_call(
        _inproj_kernel,
        out_shape=(bf(SSM_INNER), bf(CONV_DIM), bf(ATTN_WIDTH), bf(2 * D_MODEL),
                   ff(128), ff(128), bf(512), ff(IDX_DIM), ff(128), bf(128), bf(128), bf(IDX_DIM)),
        grid=(n // tm, PROJ_TILES),
        in_specs=[
            pl.BlockSpec((tm, D_MODEL), lambda i, j: (i, 0)),
            pl.BlockSpec((1, D_MODEL), lambda i, j: (0, 0)),
            pl.BlockSpec((D_MODEL, PROJ_TILE), lambda i, j: (0, j)),
        ],
        out_specs=(
            pl.BlockSpec((tm, PROJ_TILE), lambda i, j: (i, clip(j, 0, 2))),
            pl.BlockSpec((tm, PROJ_TILE), lambda i, j: (i, clip(j, 2, 5))),
            pl.BlockSpec((tm, PROJ_TILE), lambda i, j: (i, 0)),
            pl.BlockSpec((tm, PROJ_TILE), lambda i, j: (i, clip(j, 6, 8))),
            pl.BlockSpec((tm, 128), lambda i, j: (i, 0)),
            pl.BlockSpec((tm, 128), lambda i, j: (i, 0)),
            pl.BlockSpec((tm, 512), lambda i, j: (i, 0)),
            pl.BlockSpec((tm, IDX_DIM), lambda i, j: (i, 0)),
            pl.BlockSpec((tm, 128), lambda i, j: (i, 0)),
            pl.BlockSpec((tm, 128), lambda i, j: (i, 0)),
            pl.BlockSpec((tm, 128), lambda i, j: (i, 0)),
            pl.BlockSpec((tm, IDX_DIM), lambda i, j: (i, 0)),
        ),
        scratch_shapes=[pltpu.VMEM((tm, D_MODEL), BF16)],
        compiler_params=pltpu.CompilerParams(
            dimension_semantics=("arbitrary", "arbitrary"), vmem_limit_bytes=VMEM_LIMIT),
        name="in_proj",
    )(x2d, norm_g.reshape(1, D_MODEL), w_packed)


def _ssd_kernel(*refs, L, cps, nsteps, has_state):
    if has_state:
        (xbc_ref, z_ref, dtw_ref, cs_ref, s0_ref, shift_ref, convw_ref, convb_ref, dtb_ref, a_ref,
         dx_ref, ng_ref, e3p_ref, e3l_ref, tri_ref, bd_ref, yn_ref, cnew_ref, sout_ref,
         xe_ref, st, act_ref, ex_ref, yz_ref, acl_ref) = refs
    else:
        (xbc_ref, z_ref, dtw_ref, shift_ref, convw_ref, convb_ref, dtb_ref, a_ref,
         dx_ref, ng_ref, e3p_ref, e3l_ref, tri_ref, bd_ref, yn_ref, cnew_ref, sout_ref,
         xe_ref, st, act_ref, ex_ref, yz_ref, acl_ref) = refs
    c = pl.program_id(1)
    G, R, N, P = SSM_GROUPS, SSM_HEADS_PER_GROUP, SSM_STATE, SSM_HEADDIM
    GW = R * P
    RL = R * L
    LB = cps * L

    @pl.when(c == 0)
    def _():
        xe_ref[LB:, :] = jnp.zeros((xe_ref.shape[0] - LB, CONV_DIM), BF16)
        if has_state:
            xe_ref[LB:LB + CONV_CARRY, :] = cs_ref[0]
            st[...] = s0_ref[0].T
        else:
            st[...] = jnp.zeros(st.shape, F32)

    xe_ref[0:LB, :] = xbc_ref[0]
    shifted = jnp.dot(shift_ref[0], xe_ref[...], preferred_element_type=F32)
    w = convw_ref[...]
    conv = convb_ref[...] + shifted[2 * LB:3 * LB, :] * w[0:1, :]
    conv = conv + shifted[LB:2 * LB, :] * w[1:2, :]
    conv = conv + shifted[0:LB, :] * w[2:3, :]
    conv = conv + xbc_ref[0].astype(F32) * w[3:4, :]
    act_ref[...] = _silu(conv)

    @pl.when(c == nsteps - 1)
    def _():
        cnew_ref[0] = xbc_ref[0, LB - CONV_CARRY:LB, :].astype(F32)[
            CONV_CARRY - (CONV_WIDTH - 1):CONV_CARRY, :]

    if nsteps > 1:
        xe_ref[LB:LB + CONV_CARRY, :] = xe_ref[LB - CONV_CARRY:LB, :]

    u = dtw_ref[0] + dtb_ref[...]
    dt = jnp.maximum(u, 0.0) + jnp.log1p(jnp.exp(-jnp.abs(u)))
    a = dt * a_ref[...]
    ac3 = jnp.dot(tri_ref[...], _split3(a), preferred_element_type=F32)
    a_cum = ac3[:, 0:LANE] + ac3[:, LANE:2 * LANE] + ac3[:, 2 * LANE:]

    ex_ref[...] = jnp.dot(_split3(jnp.concatenate([a_cum, dt], axis=0)), e3p_ref[...],
                          preferred_element_type=F32)
    if L != P:
        acl_ref[...] = jnp.dot(_split3(a_cum), e3l_ref[...], preferred_element_type=F32)
    row = lax.broadcasted_iota(I32, (L, RL), 0)
    scol = lax.broadcasted_iota(I32, (L, RL), 1) & (L - 1)

    for sub in range(cps):
        r0 = sub * L
        rows = slice(r0, r0 + L)
        ssq = jnp.zeros((L, 1), F32)
        for g in range(G):
            sl = slice(g * GW, (g + 1) * GW)
            xs = act_ref[rows, sl]
            b_g = act_ref[rows, SSM_INNER + g * N:SSM_INNER + (g + 1) * N].astype(BF16)
            c_g = act_ref[rows, SSM_INNER + (G + g) * N:SSM_INNER + (G + g + 1) * N].astype(BF16)
            acx = ex_ref[rows, sl]
            alast = ex_ref[r0 + L - 1:r0 + L, sl]
            xdt = xs * ex_ref[LB + r0:LB + r0 + L, sl]
            acl_g = acx if L == P else acl_ref[rows, g * RL:(g + 1) * RL]
            acs_g = jnp.sum(jnp.where(scol == row, acl_g, 0.0), axis=0, keepdims=True)
            decay = jnp.where(scol <= row, jnp.exp(acl_g - acs_g), 0.0)
            cb = lax.dot_general(c_g, jnp.concatenate([b_g] * R, axis=0), NT_DIMS,
                                 preferred_element_type=F32)
            m_g = (cb * decay).astype(BF16)
            xbd = jnp.concatenate([xdt.astype(BF16)] * R, axis=0) * bd_ref[...]
            y_in = jnp.dot(m_g, xbd, preferred_element_type=F32)
            st_g = st[:, sl]
            y_st = jnp.dot(c_g, st_g.astype(BF16), preferred_element_type=F32)
            xw = (xdt * jnp.exp(alast - acx)).astype(BF16)
            upd = lax.dot_general(b_g, xw, TN_DIMS, preferred_element_type=F32)
            st[:, sl] = st_g * jnp.exp(alast) + upd
            y = y_in + jnp.exp(acx) * y_st + dx_ref[:, sl] * xs
            yz = y * _silu(z_ref[0, rows, sl].astype(F32))
            ssq = ssq + jnp.sum(yz * yz, axis=-1, keepdims=True)
            yz_ref[rows, sl] = yz
        scale = lax.rsqrt(ssq * (1.0 / SSM_INNER) + EPS)
        yn_ref[0, rows, :] = ((yz_ref[rows, :] * scale) * ng_ref[...]).astype(BF16)

    @pl.when(c == nsteps - 1)
    def _():
        sout_ref[0] = st[...].T


def _shift_matrices(L, conv_rows):
    out = np.zeros((2, 3 * L, conv_rows), np.float32)
    for k in range(CONV_WIDTH - 1):
        for t in range(L):
            src = t - (k + 1)
            if src >= 0:
                out[:, k * L + t, src] = 1.0
            else:
                out[1, k * L + t, L + CONV_CARRY + src] = 1.0
                j = CONV_WIDTH - 1 + src
                for piece in range(3):
                    out[0, k * L + t, L + 3 * piece + j] = 1.0
    return jnp.asarray(out, BF16)


def _expand_matrix(width_per_head):
    r = np.arange(3 * LANE)[:, None] % LANE
    c = np.arange(SSM_HEADS * width_per_head)[None, :] // width_per_head
    return jnp.asarray((r == c).astype(np.float32), BF16)


def _ssd(xbc, z, dtw, conv_state, ssm_state, conv_w, conv_b, dt_bias, a_log, d_skip, ssm_norm_g):
    b, t, _ = xbc.shape
    L = min(CHUNK, t)
    nc = t // L
    assert t % L == 0 and L & (L - 1) == 0
    has_state = conv_state is not None
    pad = lambda v: jnp.pad(v.astype(F32), (0, LANE - SSM_HEADS)).reshape(1, LANE)
    a_neg = pad(-jnp.exp(a_log.astype(F32)))
    cps = SSD_CHUNKS_PER_STEP if nc % SSD_CHUNKS_PER_STEP == 0 else 1
    nsteps = nc // cps
    lb = cps * L
    conv_rows = _round_up(lb + CONV_CARRY, LANE)
    tri = jnp.asarray(np.kron(np.eye(cps), np.tril(np.ones((L, L)))).astype(np.float32), BF16)
    rl = SSM_HEADS_PER_GROUP * L
    gw = SSM_HEADS_PER_GROUP * SSM_HEADDIM
    bd = jnp.asarray((np.arange(rl)[:, None] // L == np.arange(gw)[None, :] // SSM_HEADDIM)
                     .astype(np.float32), BF16)
    const = lambda shape: pl.BlockSpec(shape, lambda i, j: (0,) * len(shape))
    tok = lambda w: pl.BlockSpec((1, lb, w), lambda i, j: (i, j, 0))
    per_b = lambda s: pl.BlockSpec((1,) + s, lambda i, j: (i, 0, 0))
    ins = [xbc, z, dtw]
    specs = [tok(CONV_DIM), tok(SSM_INNER), tok(LANE)]
    if has_state:
        triples = _split3(conv_state.astype(F32))
        ins += [_pad_axis(triples, 1, CONV_CARRY), ssm_state.reshape(b, SSM_INNER, SSM_STATE)]
        specs += [per_b((CONV_CARRY, CONV_DIM)), per_b((SSM_INNER, SSM_STATE))]
    first = 0 if has_state else 1
    ins += [_shift_matrices(lb, conv_rows)]
    specs += [pl.BlockSpec((1, 3 * lb, conv_rows),
                           lambda i, j: (jnp.where(j == 0, first, 1), 0, 0))]
    ins += [conv_w, conv_b.reshape(1, CONV_DIM), pad(dt_bias), a_neg,
            jnp.repeat(d_skip.astype(F32), SSM_HEADDIM).reshape(1, SSM_INNER),
            ssm_norm_g.reshape(1, SSM_INNER), _expand_matrix(SSM_HEADDIM), _expand_matrix(L),
            tri, bd]
    specs += [const((CONV_WIDTH, CONV_DIM)), const((1, CONV_DIM)), const((1, LANE)),
              const((1, LANE)), const((1, SSM_INNER)), const((1, SSM_INNER)),
              const((3 * LANE, SSM_INNER)), const((3 * LANE, SSM_HEADS * L)),
              const((lb, lb)), const((rl, gw))]
    yn, cnew, sout = pl.pallas_call(
        functools.partial(_ssd_kernel, L=L, cps=cps, nsteps=nsteps, has_state=has_state),
        out_shape=(jax.ShapeDtypeStruct((b, t, SSM_INNER), BF16),
                   jax.ShapeDtypeStruct((b, CONV_WIDTH - 1, CONV_DIM), F32),
                   jax.ShapeDtypeStruct((b, SSM_INNER, SSM_STATE), F32)),
        grid=(b, nsteps),
        in_specs=specs,
        out_specs=(tok(SSM_INNER), per_b((CONV_WIDTH - 1, CONV_DIM)),
                   per_b((SSM_INNER, SSM_STATE))),
        scratch_shapes=[pltpu.VMEM((conv_rows, CONV_DIM), BF16),
                        pltpu.VMEM((SSM_STATE, SSM_INNER), F32),
                        pltpu.VMEM((lb, CONV_DIM), F32),
                        pltpu.VMEM((2 * lb, SSM_INNER), F32),
                        pltpu.VMEM((lb, SSM_INNER), F32),
                        pltpu.VMEM((lb, SSM_HEADS * L), F32)],
        compiler_params=pltpu.CompilerParams(
            dimension_semantics=("arbitrary", "arbitrary"), vmem_limit_bytes=VMEM_LIMIT),
        name="ssd",
    )(*ins)
    return yn, cnew, sout.reshape(b, SSM_HEADS, SSM_HEADDIM, SSM_STATE)


def _dsa_kernel(q_ref, qi_ref, w_ref, eq_ref, kidx_ref, k_ref, v_ref, o_ref,
                qs_ref, qx_ref, key_ref, kx_ref, vx_ref, m_ref, acc_ref, mlim_ref, s_ref,
                *, past_len, l_valid, topk, nq):
    qb = pl.program_id(1)
    TQ, TK = LANE, KEY_TILE
    W = ATTN_REP * nq
    wide = nq == LANE

    lane = lax.broadcasted_iota(I32, (1, TQ), 1)
    pos = past_len + qb * nq + (lane & (nq - 1))
    n_adm = jnp.minimum(((pos >> CHUNK_SHIFT) + 1) * CHUNK, l_valid)
    k_eff = jnp.minimum(n_adm, topk)
    last_pos = past_len + qb * nq + nq - 1
    n_max = jnp.minimum(((last_pos >> CHUNK_SHIFT) + 1) * CHUNK, l_valid)
    nt = (n_max + TK - 1) // TK
    ntp = 2 * ((nt + 1) // 2)
    lp = key_ref.shape[0]

    @pl.when(qb == 0)
    def _():
        kx_ref[:, LANE:] = k_ref[0]

        def v_tile(t, carry):
            s0 = pl.multiple_of(t * TK, TK)
            v_t = v_ref[0, pl.ds(s0, TK), :].astype(F32).T
            for g in range(ATTN_KV_HEADS):
                vx_ref[g, 0:HEAD_DIM, pl.ds(s0, TK)] = (
                    v_t[g * HEAD_DIM:(g + 1) * HEAD_DIM, :].astype(BF16))
            return carry

        lax.fori_loop(0, lp // TK, v_tile, 0)
        tail_row = lax.broadcasted_iota(I32, (PV_ROWS - HEAD_DIM, lp), 0)
        for g in range(ATTN_KV_HEADS):
            vx_ref[g, HEAD_DIM:, :] = jnp.where(tail_row == 0, 1.0, 0.0).astype(BF16)
            qx_ref[g, :, 0:LANE] = eq_ref[...]
            qx_ref[g, :, LANE:] = jnp.zeros((W, LANE), BF16)

    if wide:
        for h in range(IDX_HEADS):
            qs_ref[h * nq:(h + 1) * nq, :] = qi_ref[0, :, h * IDX_DIM:(h + 1) * IDX_DIM]
        w_t = w_ref[0].T[SSM_HEADS:SSM_HEADS + IDX_HEADS, :] * IDX_SCALE
    else:
        qs_ref[...] = qi_ref[0]
        w_row = w_ref[0] * IDX_SCALE

    def index_dots(t, slot):
        s0 = pl.multiple_of(jnp.minimum(t, ntp - 1) * TK, TK)
        s_ref[0, slot] = lax.dot_general(kidx_ref[0, pl.ds(s0, TK), :], qs_ref[...], NT_DIMS,
                                         preferred_element_type=F32)

    def score_tile(t, slot):
        s0 = pl.multiple_of(t * TK, TK)
        d = s_ref[0, slot]
        if wide:
            sc = w_t[0:1, :] * jnp.maximum(d[:, 0:nq], 0.0)
            for h in range(1, IDX_HEADS):
                sc = sc + w_t[h:h + 1, :] * jnp.maximum(d[:, h * nq:(h + 1) * nq], 0.0)
        else:
            sc = w_row * jnp.maximum(d, 0.0)
            for shift in (64, 32, 16):
                sc = sc + pltpu.roll(sc, shift, axis=1)
        sc = sc + 0.0
        bits = pltpu.bitcast(sc, I32)
        key = jnp.where(bits < 0, bits ^ np.int32(0x7FFFFFFF), bits)
        s_idx = s0 + lax.broadcasted_iota(I32, (TK, TQ), 0)
        key_ref[pl.ds(s0, TK), :] = jnp.where(s_idx < n_adm, key, INT_MIN)

    index_dots(0, 0)

    def score_pair(u, carry):
        index_dots(2 * u + 1, 1)
        score_tile(2 * u, 0)
        index_dots(2 * u + 2, 0)
        score_tile(2 * u + 1, 1)
        return carry

    lax.fori_loop(0, ntp // 2, score_pair, 0)

    def count(pred):
        def body(t, acc):
            s0 = pl.multiple_of(t * (2 * TK), 2 * TK)
            blk = key_ref[pl.ds(s0, 2 * TK), :]
            s_idx = s0 + lax.broadcasted_iota(I32, (2 * TK, TQ), 0)
            hit = jnp.where(pred(blk, s_idx), 1, 0).astype(I32)
            return acc + jnp.sum(hit.reshape(2 * TK // 8, 8, TQ), axis=0)
        acc = lax.fori_loop(0, ntp // 2, body, jnp.zeros((8, TQ), I32))
        return jnp.sum(acc, axis=0, keepdims=True)

    def bit_step(i, prefix):
        cand = prefix | jnp.left_shift(jnp.int32(1), 31 - i)
        cand_s = cand ^ INT_MIN
        cnt = count(lambda blk, s_idx: blk >= cand_s)
        return jnp.where(cnt >= k_eff, cand, prefix)

    thr = lax.fori_loop(0, 32, bit_step, jnp.zeros((1, TQ), I32)) ^ INT_MIN

    n_gt = count(lambda blk, s_idx: blk > thr)
    n_eq = count(lambda blk, s_idx: blk == thr)
    need = k_eff - n_gt
    mlim_ref[...] = jnp.full((1, TQ), 2 ** 30, I32)

    @pl.when(jnp.max(n_eq - need) > 0)
    def _():
        def idx_step(i, prefix):
            cand = prefix | jnp.left_shift(jnp.int32(1), 14 - i)
            cnt = count(lambda blk, s_idx: (blk == thr) & (s_idx < cand))
            return jnp.where(cnt < need, cand, prefix)
        mlim_ref[...] = lax.fori_loop(0, 15, idx_step, jnp.zeros((1, TQ), I32))

    mlim = mlim_ref[...]

    def bias_tile(t, carry):
        s0 = pl.multiple_of(t * TK, TK)
        blk = key_ref[pl.ds(s0, TK), :]
        s_idx = s0 + lax.broadcasted_iota(I32, (TK, TQ), 0)
        sel = (blk > thr) | ((blk == thr) & (s_idx <= mlim))
        kx_ref[pl.ds(s0, TK), 0:LANE] = jnp.where(sel, 0.0, NEG_BIG).astype(BF16)
        return carry

    lax.fori_loop(0, nt, bias_tile, 0)

    @pl.when(nt < ntp)
    def _():
        kx_ref[pl.ds(pl.multiple_of(nt * TK, TK), TK), 0:LANE] = jnp.full((TK, LANE), NEG_BIG,
                                                                           BF16)

    for g in range(ATTN_KV_HEADS):
        lanes = slice(LANE + g * HEAD_DIM, LANE + (g + 1) * HEAD_DIM)
        if wide:
            for r in range(ATTN_REP):
                hq = g * ATTN_REP + r
                qx_ref[g, r * nq:(r + 1) * nq, lanes] = (
                    q_ref[0, :, hq * HEAD_DIM:(hq + 1) * HEAD_DIM])
        else:
            qx_ref[g, :, lanes] = q_ref[0, g]
    m_ref[...] = jnp.full(m_ref.shape, -jnp.inf, F32)
    acc_ref[...] = jnp.zeros(acc_ref.shape, F32)

    def scores(t, slot):
        s0 = pl.multiple_of(jnp.minimum(t, ntp - 1) * TK, TK)
        k_tile = kx_ref[pl.ds(s0, TK), :]
        for g in range(ATTN_KV_HEADS):
            s_ref[g, slot] = lax.dot_general(k_tile, qx_ref[g], NT_DIMS,
                                             preferred_element_type=F32)

    def accumulate(t, slot):
        s0 = pl.multiple_of(t * TK, TK)
        for g in range(ATTN_KV_HEADS):
            s = s_ref[g, slot]
            m_prev = m_ref[g]
            m_new = jnp.maximum(m_prev, jnp.max(s, axis=0, keepdims=True))
            p = jnp.exp2(s - m_new)
            acc_ref[g] = jnp.exp2(m_prev - m_new) * acc_ref[g] + jnp.dot(
                vx_ref[g, :, pl.ds(s0, TK)], p.astype(BF16), preferred_element_type=F32)
            m_ref[g] = m_new

    scores(0, 0)

    def attn_pair(u, carry):
        scores(2 * u + 1, 1)
        accumulate(2 * u, 0)
        scores(2 * u + 2, 0)
        accumulate(2 * u + 1, 1)
        return carry

    lax.fori_loop(0, ntp // 2, attn_pair, 0)
    for g in range(ATTN_KV_HEADS):
        acc = acc_ref[g]
        o_t = acc[0:HEAD_DIM, :] * (1.0 / acc[HEAD_DIM:HEAD_DIM + 1, :])
        if wide:
            for r in range(0, ATTN_REP, 2):
                hq = g * ATTN_REP + r
                pair = jnp.concatenate([o_t[:, r * nq:(r + 1) * nq],
                                        o_t[:, (r + 1) * nq:(r + 2) * nq]], axis=0)
                o_ref[0, :, hq * HEAD_DIM:(hq + 2) * HEAD_DIM] = pair.T.astype(BF16)
        else:
            o_rq = jnp.concatenate([o_t, o_t], axis=0).T[:, 0:HEAD_DIM].astype(BF16)
            for r in range(ATTN_REP):
                hq = g * ATTN_REP + r
                o_ref[0, :, hq * HEAD_DIM:(hq + 1) * HEAD_DIM] = o_rq[r * nq:(r + 1) * nq, :]


def _dsa(q, qi, w, kidx_all, k_all, v_all, past_len, l_valid, t):
    b = kidx_all.shape[0]
    lp = kidx_all.shape[1]
    nq = Q_BLOCK if t % Q_BLOCK == 0 else SMALL_Q_BLOCK
    assert t % nq == 0 and lp % (2 * KEY_TILE) == 0 and lp < 2 ** 15
    topk = min(TOPK_MAX, l_valid // 4)
    width = ATTN_REP * nq
    if nq == Q_BLOCK:
        qblk = lambda c: pl.BlockSpec((1, nq, c), lambda i, j: (i, j, 0))
        q_specs = [qblk(ATTN_WIDTH), qblk(IDX_HEADS * IDX_DIM), qblk(LANE)]
    else:
        assert t == nq
        q_specs = [pl.BlockSpec((1, ATTN_KV_HEADS, width, HEAD_DIM), lambda i, j: (i, 0, 0, 0)),
                   pl.BlockSpec((1, IDX_HEADS * nq, IDX_DIM), lambda i, j: (i, 0, 0)),
                   pl.BlockSpec((1, 1, LANE), lambda i, j: (i, 0, 0))]
    slot_onehot = jnp.asarray(
        (np.arange(width)[:, None] % nq == np.arange(LANE)[None, :]).astype(np.float32), BF16)
    per_stream = lambda c: pl.BlockSpec((1, lp, c), lambda i, j: (i, 0, 0))
    return pl.pallas_call(
        functools.partial(_dsa_kernel, past_len=past_len, l_valid=l_valid, topk=topk, nq=nq),
        out_shape=jax.ShapeDtypeStruct((b, t, ATTN_WIDTH), BF16),
        grid=(b, t // nq),
        in_specs=q_specs + [pl.BlockSpec((width, LANE), lambda i, j: (0, 0)),
                            per_stream(IDX_DIM), per_stream(LANE), per_stream(LANE)],
        out_specs=pl.BlockSpec((1, nq, ATTN_WIDTH), lambda i, j: (i, j, 0)),
        scratch_shapes=[pltpu.VMEM((width, IDX_DIM), BF16),
                        pltpu.VMEM((ATTN_KV_HEADS, width, 2 * LANE), BF16),
                        pltpu.VMEM((lp, LANE), I32),
                        pltpu.VMEM((lp, 2 * LANE), BF16),
                        pltpu.VMEM((ATTN_KV_HEADS, PV_ROWS, lp), BF16),
                        pltpu.VMEM((ATTN_KV_HEADS, 1, width), F32),
                        pltpu.VMEM((ATTN_KV_HEADS, PV_ROWS, width), F32),
                        pltpu.VMEM((1, LANE), I32),
                        pltpu.VMEM((ATTN_KV_HEADS, 2, KEY_TILE, width), F32)],
        compiler_params=pltpu.CompilerParams(
            dimension_semantics=("arbitrary", "arbitrary"), vmem_limit_bytes=VMEM_LIMIT),
        name="dsa",
    )(q, qi, w, slot_onehot, kidx_all, k_all, v_all)


def _merge_kernel(x_ref, yn_ref, o_ref, gate_ref, wa_ref, wb_ref, wo_ref, out_ref):
    br_a = jnp.dot(yn_ref[...], wa_ref[...], preferred_element_type=F32)
    br_b = jnp.dot(o_ref[...], wb_ref[...], preferred_element_type=F32)
    gates = gate_ref[...].astype(F32)
    merged = _sigmoid(gates[:, :D_MODEL]) * br_a + _sigmoid(gates[:, D_MODEL:]) * br_b
    out_ref[...] = x_ref[...] + jnp.dot(merged.astype(BF16), wo_ref[...],
                                        preferred_element_type=F32)


def _merge(x2d, yn, o, gate, w_ssm_out, w_attn_out, w_o, tm):
    n = x2d.shape[0]
    assert n % tm == 0
    row = lambda w: pl.BlockSpec((tm, w), lambda i: (i, 0))
    full = lambda r, c: pl.BlockSpec((r, c), lambda i: (0, 0))
    return pl.pallas_call(
        _merge_kernel,
        out_shape=jax.ShapeDtypeStruct((n, D_MODEL), F32),
        grid=(n // tm,),
        in_specs=[row(D_MODEL), row(SSM_INNER), row(ATTN_WIDTH), row(2 * D_MODEL),
                  full(SSM_INNER, D_MODEL), full(ATTN_WIDTH, D_MODEL), full(D_MODEL, D_MODEL)],
        out_specs=row(D_MODEL),
        compiler_params=pltpu.CompilerParams(
            dimension_semantics=("arbitrary",), vmem_limit_bytes=VMEM_LIMIT),
        name="merge",
    )(x2d, yn, o, gate, w_ssm_out.astype(BF16), w_attn_out.astype(BF16), w_o.astype(BF16))


def _moe_kernel(x_ref, ng_ref, wr_ref, br_ref, w1_ref, w3_ref, w2_ref, fg_ref, tri_ref, out_ref,
                hs_ref, combs_ref, acc_ref, pt_ref, seg_ref, *, n_steps, blk_rows):
    e_step = pl.program_id(1)
    tm = x_ref.shape[0]
    sr = hs_ref.shape[0]
    BLK = blk_rows

    @pl.when(e_step == 0)
    def _():
        x = x_ref[...]
        ms = jnp.mean(x * x, axis=-1, keepdims=True)
        h = ((x * lax.rsqrt(ms + EPS)) * ng_ref[...]).astype(BF16)
        logits = jnp.dot(h, wr_ref[...], preferred_element_type=F32) + br_ref[...]
        lane_i = lax.broadcasted_iota(I32, (tm, LANE), 1)
        lane = lane_i.astype(F32)
        first = lambda hit: jnp.min(jnp.where(hit, lane, float(LANE)), axis=1, keepdims=True)
        is_g = (lane_i >= N_EXPERTS) & (lane_i < N_EXPERTS + MOE_GROUPS)
        glog = jnp.where(is_g, logits, -jnp.inf)
        gmax = jnp.max(glog, axis=1, keepdims=True)
        gsel = first(glog == gmax) - float(N_EXPERTS)
        p_group = 1.0 / jnp.sum(jnp.exp(glog - gmax), axis=1, keepdims=True)
        grp = (lane_i >> EXPERT_GROUP_SHIFT).astype(F32)
        in_grp = (lane_i < N_EXPERTS) & (grp == gsel)
        el = jnp.where(in_grp, logits, -jnp.inf)
        v1 = jnp.max(el, axis=1, keepdims=True)
        i1 = first(el == v1)
        el2 = jnp.where(lane == i1, -jnp.inf, el)
        v2 = jnp.max(el2, axis=1, keepdims=True)
        i2 = first(el2 == v2)
        e21 = jnp.exp(v2 - v1)
        den = 1.0 + e21
        comb = (jnp.where(lane == i1, (1.0 / den) * p_group, 0.0)
                + jnp.where(lane == i2, (e21 / den) * p_group, 0.0))

        own = lane == gsel
        own_f = jnp.where(own, 1.0, 0.0)
        before = jnp.dot(tri_ref[...], own_f.astype(BF16), preferred_element_type=F32)
        rank = jnp.sum(jnp.where(own, before, 0.0), axis=1, keepdims=True)
        counts = jnp.sum(own_f, axis=0, keepdims=True)
        blocks = jnp.floor((counts + (BLK - 1)) * (1.0 / BLK))
        upper = (lax.broadcasted_iota(I32, (LANE, LANE), 0)
                 < lax.broadcasted_iota(I32, (LANE, LANE), 1))
        starts = jnp.dot(jnp.broadcast_to(blocks, (8, LANE)).astype(BF16),
                         jnp.where(upper, 1.0, 0.0).astype(BF16),
                         preferred_element_type=F32)[0:1, :] * BLK
        dest = jnp.sum(jnp.where(own, starts, 0.0), axis=1, keepdims=True) + rank
        for g in range(MOE_GROUPS):
            seg_ref[g] = jnp.sum(starts[:, g:g + 1]).astype(I32)
            seg_ref[MOE_GROUPS + g] = jnp.sum(blocks[:, g:g + 1]).astype(I32)

        dest_i = dest.astype(I32)
        pt = jnp.where(dest_i == lax.broadcasted_iota(I32, (tm, sr), 1), 1.0, 0.0).astype(BF16)
        pt_ref[...] = pt
        dest_row = jnp.broadcast_to(dest, (tm, LANE)).T[0:1, :].astype(I32)
        p_mat = jnp.where(lax.broadcasted_iota(I32, (sr, tm), 0) == dest_row, 1.0, 0.0
                          ).astype(BF16)
        hs_ref[...] = jnp.dot(p_mat, h, preferred_element_type=F32).astype(BF16)
        c3 = jnp.dot(p_mat, _split3(comb), preferred_element_type=F32)
        combs_ref[...] = c3[:, 0:LANE] + c3[:, LANE:2 * LANE] + c3[:, 2 * LANE:]
        acc_ref[...] = jnp.zeros(acc_ref.shape, F32)

    group = e_step // (EXPERTS_PER_GROUP // EXPERTS_PER_STEP)
    seg_start = seg_ref[group]
    lane = lax.broadcasted_iota(I32, (BLK, LANE), 1)

    def row_block(blk, carry):
        rows = pl.ds(pl.multiple_of(seg_start + blk * BLK, BLK), BLK)
        h = hs_ref[rows, :]
        comb = combs_ref[rows, :]
        for k in range(EXPERTS_PER_STEP):
            e = e_step * EXPERTS_PER_STEP + k
            a1 = jnp.dot(h, w1_ref[k], preferred_element_type=F32)
            a3 = jnp.dot(h, w3_ref[k], preferred_element_type=F32)
            he = (_silu(a1) * a3).astype(BF16)
            ye = jnp.dot(he, w2_ref[k], preferred_element_type=F32)
            wcol = jnp.sum(jnp.where(lane == e, comb, 0.0), axis=1, keepdims=True)
            acc_ref[rows, :] += wcol * ye
        return carry

    lax.fori_loop(0, seg_ref[MOE_GROUPS + group], row_block, 0)

    @pl.when(e_step == n_steps - 1)
    def _():
        y = x_ref[...] + jnp.dot(pt_ref[...], acc_ref[...].astype(BF16),
                                 preferred_element_type=F32)
        ms = jnp.mean(y * y, axis=-1, keepdims=True)
        out_ref[...] = (y * lax.rsqrt(ms + EPS)) * fg_ref[...]


def _moe(x2d, norm_g, w_router_group, b_router_group, w_router_expert, b_router_expert,
         w1, w3, w2, norm_final_g, tm):
    n = x2d.shape[0]
    blk_rows = min(MOE_ROW_BLOCK, tm)
    assert n % tm == 0 and tm % blk_rows == 0
    n_steps = N_EXPERTS // EXPERTS_PER_STEP
    sr = tm + (MOE_GROUPS - 1) * blk_rows
    padw = LANE - N_EXPERTS - MOE_GROUPS
    wr = jnp.concatenate([w_router_expert, w_router_group,
                          jnp.zeros((D_MODEL, padw), F32)], axis=1).astype(BF16)
    br = jnp.concatenate([b_router_expert, b_router_group,
                          jnp.zeros((padw,), F32)]).astype(F32).reshape(1, LANE)
    tri = jnp.asarray(np.tril(np.ones((tm, tm), np.float32), -1), BF16)
    row = pl.BlockSpec((tm, D_MODEL), lambda i, e: (i, 0))
    const = lambda r, c: pl.BlockSpec((r, c), lambda i, e: (0, 0))
    return pl.pallas_call(
        functools.partial(_moe_kernel, n_steps=n_steps, blk_rows=blk_rows),
        out_shape=jax.ShapeDtypeStruct((n, D_MODEL), F32),
        grid=(n // tm, n_steps),
        in_specs=[row, const(1, D_MODEL), const(D_MODEL, LANE), const(1, LANE),
                  pl.BlockSpec((EXPERTS_PER_STEP, D_MODEL, EXPERT_FF), lambda i, e: (e, 0, 0)),
                  pl.BlockSpec((EXPERTS_PER_STEP, D_MODEL, EXPERT_FF), lambda i, e: (e, 0, 0)),
                  pl.BlockSpec((EXPERTS_PER_STEP, EXPERT_FF, D_MODEL), lambda i, e: (e, 0, 0)),
                  const(1, D_MODEL), const(tm, tm)],
        out_specs=row,
        scratch_shapes=[pltpu.VMEM((sr, D_MODEL), BF16), pltpu.VMEM((sr, LANE), F32),
                        pltpu.VMEM((sr, D_MODEL), F32), pltpu.VMEM((tm, sr), BF16),
                        pltpu.SMEM((2 * MOE_GROUPS,), I32)],
        compiler_params=pltpu.CompilerParams(
            dimension_semantics=("arbitrary", "arbitrary"), vmem_limit_bytes=VMEM_LIMIT),
        name="moe",
    )(x2d, norm_g.reshape(1, D_MODEL), wr, br, w1.astype(BF16), w3.astype(BF16),
      w2.astype(BF16), norm_final_g.reshape(1, D_MODEL), tri)


def _token_tile(n):
    for tm in (1024, 512, 256, 128, 64, 32, 16, 8):
        if n % tm == 0:
            return tm
    raise ValueError(f"token count {n} is not a multiple of 8")


def _pad_axis(a, axis, size):
    if a.shape[axis] == size:
        return a
    widths = [(0, 0)] * a.ndim
    widths[axis] = (0, size - a.shape[axis])
    return jnp.pad(a, widths)


def _round_up(n, m):
    return (n + m - 1) // m * m


def _stream_step(x, conv_state, ssm_state, k_past, v_past, kidx_past, w_packed, p):
    b, t, d = x.shape
    n = b * t
    tm = _token_tile(n)
    x2d = x.reshape(n, d)
    z, xbc, q, gate, k, v, qi, ki, dtw, k_b, v_b, ki_b = _in_proj(
        x2d, p["norm_mix_g"], w_packed, tm)

    yn, conv_new, ssm_new = _ssd(
        xbc.reshape(b, t, CONV_DIM), z.reshape(b, t, SSM_INNER), dtw.reshape(b, t, LANE),
        conv_state, ssm_state, p["conv_w"], p["conv_b"], p["dt_bias"], p["A_log"],
        p["D_skip"], p["ssm_norm_g"])

    kv_w = ATTN_KV_HEADS * HEAD_DIM
    k_all, v_all, kidx_all = (k_b.reshape(b, t, kv_w), v_b.reshape(b, t, kv_w),
                              ki_b.reshape(b, t, IDX_DIM))
    past_len = 0
    if k_past is not None:
        past_len = k_past.shape[1]
        k_all = jnp.concatenate([k_past.astype(BF16).reshape(b, past_len, kv_w), k_all], axis=1)
        v_all = jnp.concatenate([v_past.astype(BF16).reshape(b, past_len, kv_w), v_all], axis=1)
        kidx_all = jnp.concatenate([kidx_past.astype(BF16), kidx_all], axis=1)
    l_valid = past_len + t
    lp = _round_up(l_valid, 2 * KEY_TILE)
    if t % Q_BLOCK == 0:
        q_in = (q.reshape(b, t, ATTN_WIDTH), qi.reshape(b, t, IDX_HEADS * IDX_DIM),
                dtw.reshape(b, t, LANE))
    else:
        q_in = (q.reshape(b, t, ATTN_KV_HEADS, ATTN_REP, HEAD_DIM).transpose(0, 2, 3, 1, 4)
                .reshape(b, ATTN_KV_HEADS, ATTN_REP * t, HEAD_DIM),
                qi.reshape(b, t, IDX_HEADS, IDX_DIM).transpose(0, 2, 1, 3)
                .reshape(b, IDX_HEADS * t, IDX_DIM),
                dtw.reshape(b, t, LANE)[:, :, SSM_HEADS:SSM_HEADS + IDX_HEADS]
                .transpose(0, 2, 1).reshape(b, 1, IDX_HEADS * t))
    o = _dsa(*q_in, _pad_axis(kidx_all, 1, lp), _pad_axis(k_all, 1, lp), _pad_axis(v_all, 1, lp),
             past_len, l_valid, t)
    o = o.reshape(n, ATTN_WIDTH)

    x1 = _merge(x2d, yn.reshape(n, SSM_INNER), o, gate, p["w_ssm_out"], p["w_attn_out"],
                p["w_o"], min(tm, MERGE_TILE))
    y = _moe(x1, p["norm_moe_g"], p["w_router_group"], p["b_router_group"],
             p["w_router_expert"], p["b_router_expert"], p["w1"], p["w3"], p["w2"],
             p["norm_final_g"], tm)
    return (y.reshape(b, t, d), k.reshape(b, t, ATTN_KV_HEADS, HEAD_DIM),
            v.reshape(b, t, ATTN_KV_HEADS, HEAD_DIM), ki.reshape(b, t, IDX_DIM), conv_new, ssm_new)


def kernel(x_prompt, x_sample, cache_k, cache_v, cache_kidx, state_conv, state_ssm, norm_mix_g, w_in, conv_w, conv_b, dt_bias, A_log, D_skip, ssm_norm_g, w_ssm_out, w_attn_out, w_o, norm_moe_g, w_router_group, b_router_group, w_router_expert, b_router_expert, w1, w3, w2, norm_final_g):
    p = dict(norm_mix_g=norm_mix_g, conv_w=conv_w, conv_b=conv_b, dt_bias=dt_bias, A_log=A_log,
             D_skip=D_skip, ssm_norm_g=ssm_norm_g, w_ssm_out=w_ssm_out, w_attn_out=w_attn_out,
             w_o=w_o, norm_moe_g=norm_moe_g, w_router_group=w_router_group,
             b_router_group=b_router_group, w_router_expert=w_router_expert,
             b_router_expert=b_router_expert, w1=w1, w3=w3, w2=w2, norm_final_g=norm_final_g)
    w_packed = _pack_w_in(w_in)
    yp, kp, vp, kip, cp, sp = _stream_step(x_prompt, None, None, None, None, None, w_packed, p)
    ys, ksn, vsn, kisn, csn, ssn = _stream_step(x_sample, state_conv, state_ssm, cache_k,
                                                cache_v, cache_kidx, w_packed, p)
    return (yp, ys, kp, vp, kip, cp, sp, ksn, vsn, kisn, csn, ssn)
```

```python
import functools

import numpy as np
import jax
import jax.numpy as jnp
from jax import lax
from jax.experimental import pallas as pl
from jax.experimental.pallas import tpu as pltpu

F32 = jnp.float32
BF16 = jnp.bfloat16
I32 = jnp.int32

D_MODEL = 1024
CHUNK = 64
CHUNK_SHIFT = 6
EPS = 1e-6
SSM_INNER = 2048
SSM_HEADDIM = 64
SSM_HEADS = 32
SSM_GROUPS = 4
SSM_HEADS_PER_GROUP = SSM_HEADS // SSM_GROUPS
SSM_STATE = 128
CONV_WIDTH = 4
CONV_DIM = SSM_INNER + 2 * SSM_GROUPS * SSM_STATE
CONV_CARRY = 16
SSD_CHUNKS_PER_STEP = 2
ATTN_HEADS = 16
ATTN_KV_HEADS = 2
HEAD_DIM = 64
ATTN_REP = ATTN_HEADS // ATTN_KV_HEADS
ATTN_WIDTH = ATTN_HEADS * HEAD_DIM
IDX_HEADS = 8
IDX_DIM = 64
IDX_SCALE = (IDX_HEADS * IDX_DIM) ** -0.5
TOPK_MAX = 256
MOE_GROUPS = 4
EXPERTS_PER_GROUP = 8
EXPERT_GROUP_SHIFT = 3
N_EXPERTS = 32
EXPERT_FF = 256
N_BRANCHES = 2

LANE = 128
VMEM_LIMIT = 56 * 1024 * 1024
PROJ_TILE = 1024
PROJ_TILES = 9
Q_BLOCK = 128
SMALL_Q_BLOCK = 16
KEY_TILE = 256
EXPERTS_PER_STEP = 4
MOE_ROW_BLOCK = 128
NORM_ROWS = 128
MERGE_TILE = 512
INT_MIN = np.int32(-2 ** 31)
NEG_BIG = -1e30
Q_SCALE = HEAD_DIM ** -0.5 * float(np.log2(np.e))
PV_ROWS = 80

NT_DIMS = (((1,), (1,)), ((), ()))
TN_DIMS = (((0,), (0,)), ((), ()))


def _sigmoid(x):
    return 1.0 / (1.0 + jnp.exp(-x))


def _silu(x):
    h = 0.5 * x
    return h + h * jnp.tanh(h)


def _rmsnorm_rows(src_ref, gain_ref, dst_ref, n_rows):
    blk = min(NORM_ROWS, n_rows)

    def body(i, carry):
        rows = pl.ds(pl.multiple_of(i * blk, blk), blk)
        x = src_ref[rows, :]
        ms = jnp.mean(x * x, axis=-1, keepdims=True)
        dst_ref[rows, :] = ((x * lax.rsqrt(ms + EPS)) * gain_ref[...]).astype(dst_ref.dtype)
        return carry

    lax.fori_loop(0, n_rows // blk, body, 0)


def _split3(x):
    hi = x.astype(BF16)
    r1 = x - hi.astype(F32)
    mid = r1.astype(BF16)
    r2 = r1 - mid.astype(F32)
    lo = r2.astype(BF16)
    return jnp.concatenate([hi, mid, lo], axis=1)


def _inproj_kernel(x_ref, g_ref, w_ref, z_ref, xbc_ref, q_ref, gate_ref, k_ref, v_ref,
                   qi_ref, ki_ref, dtw_ref, kb_ref, vb_ref, kib_ref, h_ref):
    j = pl.program_id(1)

    @pl.when(j == 0)
    def _():
        _rmsnorm_rows(x_ref, g_ref, h_ref, x_ref.shape[0])

    def tile():
        return jnp.dot(h_ref[...], w_ref[...], preferred_element_type=F32)

    @pl.when(j < 2)
    def _():
        z_ref[...] = tile().astype(BF16)

    @pl.when((j >= 2) & (j < 5))
    def _():
        xbc_ref[...] = tile().astype(BF16)

    @pl.when(j == 5)
    def _():
        q_ref[...] = (tile() * Q_SCALE).astype(BF16)

    @pl.when((j >= 6) & (j < 8))
    def _():
        gate_ref[...] = tile().astype(BF16)

    @pl.when(j == 8)
    def _():
        t = tile()
        k_ref[...] = t[:, 0:128]
        v_ref[...] = t[:, 128:256]
        qi_ref[...] = t[:, 256:768].astype(BF16)
        ki_ref[...] = t[:, 768:832]
        dtw_ref[...] = t[:, 896:1024]
        kb_ref[...] = t[:, 0:128].astype(BF16)
        vb_ref[...] = t[:, 128:256].astype(BF16)
        kib_ref[...] = t[:, 768:832].astype(BF16)


def _pack_w_in(w_in):
    sizes = (SSM_INNER, CONV_DIM, SSM_HEADS, ATTN_WIDTH, ATTN_KV_HEADS * HEAD_DIM,
             ATTN_KV_HEADS * HEAD_DIM, IDX_HEADS * IDX_DIM, IDX_DIM, IDX_HEADS,
             N_BRANCHES * D_MODEL)
    offs = np.concatenate([[0], np.cumsum(sizes)])
    z, xbc, dt, q, k, v, qi, ki, wi, g = [w_in[:, offs[i]:offs[i + 1]] for i in range(10)]
    zeros = lambda n: jnp.zeros((D_MODEL, n), w_in.dtype)
    cols = [z, xbc, q, g, k, v, qi, ki, zeros(64), dt, wi, zeros(LANE - SSM_HEADS - IDX_HEADS)]
    return jnp.concatenate(cols, axis=1).astype(BF16)


def _in_proj(x2d, norm_g, w_packed, tm):
    n = x2d.shape[0]
    assert n % tm == 0
    bf = lambda c: jax.ShapeDtypeStruct((n, c), BF16)
    ff = lambda c: jax.ShapeDtypeStruct((n, c), F32)
    clip = lambda j, lo, hi: jnp.clip(j - lo, 0, hi - lo - 1)
    return pl.pallas_call(
        _inproj_kernel,
        out_shape=(bf(SSM_INNER), bf(CONV_DIM), bf(ATTN_WIDTH), bf(2 * D_MODEL),
                   ff(128), ff(128), bf(512), ff(IDX_DIM), ff(128), bf(128), bf(128), bf(IDX_DIM)),
        grid=(n // tm, PROJ_TILES),
        in_specs=[
            pl.BlockSpec((tm, D_MODEL), lambda i, j: (i, 0)),
            pl.BlockSpec((1, D_MODEL), lambda i, j: (0, 0)),
            pl.BlockSpec((D_MODEL, PROJ_TILE), lambda i, j: (0, j)),
        ],
        out_specs=(
            pl.BlockSpec((tm, PROJ_TILE), lambda i, j: (i, clip(j, 0, 2))),
            pl.BlockSpec((tm, PROJ_TILE), lambda i, j: (i, clip(j, 2, 5))),
            pl.BlockSpec((tm, PROJ_TILE), lambda i, j: (i, 0)),
            pl.BlockSpec((tm, PROJ_TILE), lambda i, j: (i, clip(j, 6, 8))),
            pl.BlockSpec((tm, 128), lambda i, j: (i, 0)),
            pl.BlockSpec((tm, 128), lambda i, j: (i, 0)),
            pl.BlockSpec((tm, 512), lambda i, j: (i, 0)),
            pl.BlockSpec((tm, IDX_DIM), lambda i, j: (i, 0)),
            pl.BlockSpec((tm, 128), lambda i, j: (i, 0)),
            pl.BlockSpec((tm, 128), lambda i, j: (i, 0)),
            pl.BlockSpec((tm, 128), lambda i, j: (i, 0)),
            pl.BlockSpec((tm, IDX_DIM), lambda i, j: (i, 0)),
        ),
        scratch_shapes=[pltpu.VMEM((tm, D_MODEL), BF16)],
        compiler_params=pltpu.CompilerParams(
            dimension_semantics=("arbitrary", "arbitrary"), vmem_limit_bytes=VMEM_LIMIT),
        name="in_proj",
    )(x2d, norm_g.reshape(1, D_MODEL), w_packed)


def _ssd_kernel(*refs, L, cps, nsteps, has_state):
    if has_state:
        (xbc_ref, z_ref, dtw_ref, cs_ref, s0_ref, shift_ref, convw_ref, convb_ref, dtb_ref, a_ref,
         dx_ref, ng_ref, e3p_ref, e3l_ref, tri_ref, bd_ref, yn_ref, cnew_ref, sout_ref,
         xe_ref, st, act_ref, ex_ref, yz_ref, acl_ref) = refs
    else:
        (xbc_ref, z_ref, dtw_ref, shift_ref, convw_ref, convb_ref, dtb_ref, a_ref,
         dx_ref, ng_ref, e3p_ref, e3l_ref, tri_ref, bd_ref, yn_ref, cnew_ref, sout_ref,
         xe_ref, st, act_ref, ex_ref, yz_ref, acl_ref) = refs
    c = pl.program_id(1)
    G, R, N, P = SSM_GROUPS, SSM_HEADS_PER_GROUP, SSM_STATE, SSM_HEADDIM
    GW = R * P
    RL = R * L
    LB = cps * L

    @pl.when(c == 0)
    def _():
        xe_ref[LB:, :] = jnp.zeros((xe_ref.shape[0] - LB, CONV_DIM), BF16)
        if has_state:
            xe_ref[LB:LB + CONV_CARRY, :] = cs_ref[0]
            st[...] = s0_ref[0].T
        else:
            st[...] = jnp.zeros(st.shape, F32)

    xe_ref[0:LB, :] = xbc_ref[0]
    shifted = jnp.dot(shift_ref[0], xe_ref[...], preferred_element_type=F32)
    w = convw_ref[...]
    conv = convb_ref[...] + shifted[2 * LB:3 * LB, :] * w[0:1, :]
    conv = conv + shifted[LB:2 * LB, :] * w[1:2, :]
    conv = conv + shifted[0:LB, :] * w[2:3, :]
    conv = conv + xbc_ref[0].astype(F32) * w[3:4, :]
    act_ref[...] = _silu(conv)

    @pl.when(c == nsteps - 1)
    def _():
        cnew_ref[0] = xbc_ref[0, LB - CONV_CARRY:LB, :].astype(F32)[
            CONV_CARRY - (CONV_WIDTH - 1):CONV_CARRY, :]

    if nsteps > 1:
        xe_ref[LB:LB + CONV_CARRY, :] = xe_ref[LB - CONV_CARRY:LB, :]

    u = dtw_ref[0] + dtb_ref[...]
    dt = jnp.maximum(u, 0.0) + jnp.log1p(jnp.exp(-jnp.abs(u)))
    a = dt * a_ref[...]
    ac3 = jnp.dot(tri_ref[...], _split3(a), preferred_element_type=F32)
    a_cum = ac3[:, 0:LANE] + ac3[:, LANE:2 * LANE] + ac3[:, 2 * LANE:]

    ex_ref[...] = jnp.dot(_split3(jnp.concatenate([a_cum, dt], axis=0)), e3p_ref[...],
                          preferred_element_type=F32)
    if L != P:
        acl_ref[...] = jnp.dot(_split3(a_cum), e3l_ref[...], preferred_element_type=F32)
    row = lax.broadcasted_iota(I32, (L, RL), 0)
    scol = lax.broadcasted_iota(I32, (L, RL), 1) & (L - 1)

    for sub in range(cps):
        r0 = sub * L
        rows = slice(r0, r0 + L)
        ssq = jnp.zeros((L, 1), F32)
        for g in range(G):
            sl = slice(g * GW, (g + 1) * GW)
            xs = act_ref[rows, sl]
            b_g = act_ref[rows, SSM_INNER + g * N:SSM_INNER + (g + 1) * N].astype(BF16)
            c_g = act_ref[rows, SSM_INNER + (G + g) * N:SSM_INNER + (G + g + 1) * N].astype(BF16)
            acx = ex_ref[rows, sl]
            alast = ex_ref[r0 + L - 1:r0 + L, sl]
            xdt = xs * ex_ref[LB + r0:LB + r0 + L, sl]
            acl_g = acx if L == P else acl_ref[rows, g * RL:(g + 1) * RL]
            acs_g = jnp.sum(jnp.where(scol == row, acl_g, 0.0), axis=0, keepdims=True)
            decay = jnp.where(scol <= row, jnp.exp(acl_g - acs_g), 0.0)
            cb = lax.dot_general(c_g, jnp.concatenate([b_g] * R, axis=0), NT_DIMS,
                                 preferred_element_type=F32)
            m_g = (cb * decay).astype(BF16)
            xbd = jnp.concatenate([xdt.astype(BF16)] * R, axis=0) * bd_ref[...]
            y_in = jnp.dot(m_g, xbd, preferred_element_type=F32)
            st_g = st[:, sl]
            y_st = jnp.dot(c_g, st_g.astype(BF16), preferred_element_type=F32)
            xw = (xdt * jnp.exp(alast - acx)).astype(BF16)
            upd = lax.dot_general(b_g, xw, TN_DIMS, preferred_element_type=F32)
            st[:, sl] = st_g * jnp.exp(alast) + upd
            y = y_in + jnp.exp(acx) * y_st + dx_ref[:, sl] * xs
            yz = y * _silu(z_ref[0, rows, sl].astype(F32))
            ssq = ssq + jnp.sum(yz * yz, axis=-1, keepdims=True)
            yz_ref[rows, sl] = yz
        scale = lax.rsqrt(ssq * (1.0 / SSM_INNER) + EPS)
        yn_ref[0, rows, :] = ((yz_ref[rows, :] * scale) * ng_ref[...]).astype(BF16)

    @pl.when(c == nsteps - 1)
    def _():
        sout_ref[0] = st[...].T


def _shift_matrices(L, conv_rows):
    out = np.zeros((2, 3 * L, conv_rows), np.float32)
    for k in range(CONV_WIDTH - 1):
        for t in range(L):
            src = t - (k + 1)
            if src >= 0:
                out[:, k * L + t, src] = 1.0
            else:
                out[1, k * L + t, L + CONV_CARRY + src] = 1.0
                j = CONV_WIDTH - 1 + src
                for piece in range(3):
                    out[0, k * L + t, L + 3 * piece + j] = 1.0
    return jnp.asarray(out, BF16)


def _expand_matrix(width_per_head):
    r = np.arange(3 * LANE)[:, None] % LANE
    c = np.arange(SSM_HEADS * width_per_head)[None, :] // width_per_head
    return jnp.asarray((r == c).astype(np.float32), BF16)


def _ssd(xbc, z, dtw, conv_state, ssm_state, conv_w, conv_b, dt_bias, a_log, d_skip, ssm_norm_g):
    b, t, _ = xbc.shape
    L = min(CHUNK, t)
    nc = t // L
    assert t % L == 0 and L & (L - 1) == 0
    has_state = conv_state is not None
    pad = lambda v: jnp.pad(v.astype(F32), (0, LANE - SSM_HEADS)).reshape(1, LANE)
    a_neg = pad(-jnp.exp(a_log.astype(F32)))
    cps = SSD_CHUNKS_PER_STEP if nc % SSD_CHUNKS_PER_STEP == 0 else 1
    nsteps = nc // cps
    lb = cps * L
    conv_rows = _round_up(lb + CONV_CARRY, LANE)
    tri = jnp.asarray(np.kron(np.eye(cps), np.tril(np.ones((L, L)))).astype(np.float32), BF16)
    rl = SSM_HEADS_PER_GROUP * L
    gw = SSM_HEADS_PER_GROUP * SSM_HEADDIM
    bd = jnp.asarray((np.arange(rl)[:, None] // L == np.arange(gw)[None, :] // SSM_HEADDIM)
                     .astype(np.float32), BF16)
    const = lambda shape: pl.BlockSpec(shape, lambda i, j: (0,) * len(shape))
    tok = lambda w: pl.BlockSpec((1, lb, w), lambda i, j: (i, j, 0))
    per_b = lambda s: pl.BlockSpec((1,) + s, lambda i, j: (i, 0, 0))
    ins = [xbc, z, dtw]
    specs = [tok(CONV_DIM), tok(SSM_INNER), tok(LANE)]
    if has_state:
        triples = _split3(conv_state.astype(F32))
        ins += [_pad_axis(triples, 1, CONV_CARRY), ssm_state.reshape(b, SSM_INNER, SSM_STATE)]
        specs += [per_b((CONV_CARRY, CONV_DIM)), per_b((SSM_INNER, SSM_STATE))]
    first = 0 if has_state else 1
    ins += [_shift_matrices(lb, conv_rows)]
    specs += [pl.BlockSpec((1, 3 * lb, conv_rows),
                           lambda i, j: (jnp.where(j == 0, first, 1), 0, 0))]
    ins += [conv_w, conv_b.reshape(1, CONV_DIM), pad(dt_bias), a_neg,
            jnp.repeat(d_skip.astype(F32), SSM_HEADDIM).reshape(1, SSM_INNER),
            ssm_norm_g.reshape(1, SSM_INNER), _expand_matrix(SSM_HEADDIM), _expand_matrix(L),
            tri, bd]
    specs += [const((CONV_WIDTH, CONV_DIM)), const((1, CONV_DIM)), const((1, LANE)),
              const((1, LANE)), const((1, SSM_INNER)), const((1, SSM_INNER)),
              const((3 * LANE, SSM_INNER)), const((3 * LANE, SSM_HEADS * L)),
              const((lb, lb)), const((rl, gw))]
    yn, cnew, sout = pl.pallas_call(
        functools.partial(_ssd_kernel, L=L, cps=cps, nsteps=nsteps, has_state=has_state),
        out_shape=(jax.ShapeDtypeStruct((b, t, SSM_INNER), BF16),
                   jax.ShapeDtypeStruct((b, CONV_WIDTH - 1, CONV_DIM), F32),
                   jax.ShapeDtypeStruct((b, SSM_INNER, SSM_STATE), F32)),
        grid=(b, nsteps),
        in_specs=specs,
        out_specs=(tok(SSM_INNER), per_b((CONV_WIDTH - 1, CONV_DIM)),
                   per_b((SSM_INNER, SSM_STATE))),
        scratch_shapes=[pltpu.VMEM((conv_rows, CONV_DIM), BF16),
                        pltpu.VMEM((SSM_STATE, SSM_INNER), F32),
                        pltpu.VMEM((lb, CONV_DIM), F32),
                        pltpu.VMEM((2 * lb, SSM_INNER), F32),
                        pltpu.VMEM((lb, SSM_INNER), F32),
                        pltpu.VMEM((lb, SSM_HEADS * L), F32)],
        compiler_params=pltpu.CompilerParams(
            dimension_semantics=("arbitrary", "arbitrary"), vmem_limit_bytes=VMEM_LIMIT),
        name="ssd",
    )(*ins)
    return yn, cnew, sout.reshape(b, SSM_HEADS, SSM_HEADDIM, SSM_STATE)


def _dsa_kernel(q_ref, qi_ref, w_ref, eq_ref, kidx_ref, k_ref, v_ref, o_ref,
                qs_ref, qx_ref, key_ref, kx_ref, vx_ref, m_ref, acc_ref, mlim_ref, s_ref,
                *, past_len, l_valid, topk, nq):
    qb = pl.program_id(1)
    TQ, TK = LANE, KEY_TILE
    W = ATTN_REP * nq
    wide = nq == LANE

    lane = lax.broadcasted_iota(I32, (1, TQ), 1)
    pos = past_len + qb * nq + (lane & (nq - 1))
    n_adm = jnp.minimum(((pos >> CHUNK_SHIFT) + 1) * CHUNK, l_valid)
    k_eff = jnp.minimum(n_adm, topk)
    last_pos = past_len + qb * nq + nq - 1
    n_max = jnp.minimum(((last_pos >> CHUNK_SHIFT) + 1) * CHUNK, l_valid)
    nt = (n_max + TK - 1) // TK
    ntp = 2 * ((nt + 1) // 2)
    lp = key_ref.shape[0]

    @pl.when(qb == 0)
    def _():
        kx_ref[:, LANE:] = k_ref[0]

        def v_tile(t, carry):
            s0 = pl.multiple_of(t * TK, TK)
            v_t = v_ref[0, pl.ds(s0, TK), :].astype(F32).T
            for g in range(ATTN_KV_HEADS):
                vx_ref[g, 0:HEAD_DIM, pl.ds(s0, TK)] = (
                    v_t[g * HEAD_DIM:(g + 1) * HEAD_DIM, :].astype(BF16))
            return carry

        lax.fori_loop(0, lp // TK, v_tile, 0)
        tail_row = lax.broadcasted_iota(I32, (PV_ROWS - HEAD_DIM, lp), 0)
        for g in range(ATTN_KV_HEADS):
            vx_ref[g, HEAD_DIM:, :] = jnp.where(tail_row == 0, 1.0, 0.0).astype(BF16)
            qx_ref[g, :, 0:LANE] = eq_ref[...]
            qx_ref[g, :, LANE:] = jnp.zeros((W, LANE), BF16)

    if wide:
        for h in range(IDX_HEADS):
            qs_ref[h * nq:(h + 1) * nq, :] = qi_ref[0, :, h * IDX_DIM:(h + 1) * IDX_DIM]
        w_t = w_ref[0].T[SSM_HEADS:SSM_HEADS + IDX_HEADS, :] * IDX_SCALE
    else:
        qs_ref[...] = qi_ref[0]
        w_row = w_ref[0] * IDX_SCALE

    def index_dots(t, slot):
        s0 = pl.multiple_of(jnp.minimum(t, ntp - 1) * TK, TK)
        s_ref[0, slot] = lax.dot_general(kidx_ref[0, pl.ds(s0, TK), :], qs_ref[...], NT_DIMS,
                                         preferred_element_type=F32)

    def score_tile(t, slot):
        s0 = pl.multiple_of(t * TK, TK)
        d = s_ref[0, slot]
        if wide:
            sc = w_t[0:1, :] * jnp.maximum(d[:, 0:nq], 0.0)
            for h in range(1, IDX_HEADS):
                sc = sc + w_t[h:h + 1, :] * jnp.maximum(d[:, h * nq:(h + 1) * nq], 0.0)
        else:
            sc = w_row * jnp.maximum(d, 0.0)
            for shift in (64, 32, 16):
                sc = sc + pltpu.roll(sc, shift, axis=1)
        sc = sc + 0.0
        bits = pltpu.bitcast(sc, I32)
        key = jnp.where(bits < 0, bits ^ np.int32(0x7FFFFFFF), bits)
        s_idx = s0 + lax.broadcasted_iota(I32, (TK, TQ), 0)
        key_ref[pl.ds(s0, TK), :] = jnp.where(s_idx < n_adm, key, INT_MIN)

    index_dots(0, 0)

    def score_pair(u, carry):
        index_dots(2 * u + 1, 1)
        score_tile(2 * u, 0)
        index_dots(2 * u + 2, 0)
        score_tile(2 * u + 1, 1)
        return carry

    lax.fori_loop(0, ntp // 2, score_pair, 0)

    def count(pred):
        def body(t, acc):
            s0 = pl.multiple_of(t * (2 * TK), 2 * TK)
            blk = key_ref[pl.ds(s0, 2 * TK), :]
            s_idx = s0 + lax.broadcasted_iota(I32, (2 * TK, TQ), 0)
            hit = jnp.where(pred(blk, s_idx), 1, 0).astype(I32)
            return acc + jnp.sum(hit.reshape(2 * TK // 8, 8, TQ), axis=0)
        acc = lax.fori_loop(0, ntp // 2, body, jnp.zeros((8, TQ), I32))
        return jnp.sum(acc, axis=0, keepdims=True)

    def bit_step(i, prefix):
        cand = prefix | jnp.left_shift(jnp.int32(1), 31 - i)
        cand_s = cand ^ INT_MIN
        cnt = count(lambda blk, s_idx: blk >= cand_s)
        return jnp.where(cnt >= k_eff, cand, prefix)

    thr = lax.fori_loop(0, 32, bit_step, jnp.zeros((1, TQ), I32)) ^ INT_MIN

    n_gt = count(lambda blk, s_idx: blk > thr)
    n_eq = count(lambda blk, s_idx: blk == thr)
    need = k_eff - n_gt
    mlim_ref[...] = jnp.full((1, TQ), 2 ** 30, I32)

    @pl.when(jnp.max(n_eq - need) > 0)
    def _():
        def idx_step(i, prefix):
            cand = prefix | jnp.left_shift(jnp.int32(1), 14 - i)
            cnt = count(lambda blk, s_idx: (blk == thr) & (s_idx < cand))
            return jnp.where(cnt < need, cand, prefix)
        mlim_ref[...] = lax.fori_loop(0, 15, idx_step, jnp.zeros((1, TQ), I32))

    mlim = mlim_ref[...]

    def bias_tile(t, carry):
        s0 = pl.multiple_of(t * TK, TK)
        blk = key_ref[pl.ds(s0, TK), :]
        s_idx = s0 + lax.broadcasted_iota(I32, (TK, TQ), 0)
        sel = (blk > thr) | ((blk == thr) & (s_idx <= mlim))
        kx_ref[pl.ds(s0, TK), 0:LANE] = jnp.where(sel, 0.0, NEG_BIG).astype(BF16)
        return carry

    lax.fori_loop(0, nt, bias_tile, 0)

    @pl.when(nt < ntp)
    def _():
        kx_ref[pl.ds(pl.multiple_of(nt * TK, TK), TK), 0:LANE] = jnp.full((TK, LANE), NEG_BIG,
                                                                           BF16)

    for g in range(ATTN_KV_HEADS):
        lanes = slice(LANE + g * HEAD_DIM, LANE + (g + 1) * HEAD_DIM)
        if wide:
            for r in range(ATTN_REP):
                hq = g * ATTN_REP + r
                qx_ref[g, r * nq:(r + 1) * nq, lanes] = (
                    q_ref[0, :, hq * HEAD_DIM:(hq + 1) * HEAD_DIM])
        else:
            qx_ref[g, :, lanes] = q_ref[0, g]
    m_ref[...] = jnp.full(m_ref.shape, -jnp.inf, F32)
    acc_ref[...] = jnp.zeros(acc_ref.shape, F32)

    def scores(t, slot):
        s0 = pl.multiple_of(jnp.minimum(t, ntp - 1) * TK, TK)
        k_tile = kx_ref[pl.ds(s0, TK), :]
        for g in range(ATTN_KV_HEADS):
            s_ref[g, slot] = lax.dot_general(k_tile, qx_ref[g], NT_DIMS,
                                             preferred_element_type=F32)

    def accumulate(t, slot):
        s0 = pl.multiple_of(t * TK, TK)
        for g in range(ATTN_KV_HEADS):
            s = s_ref[g, slot]
            m_prev = m_ref[g]
            m_new = jnp.maximum(m_prev, jnp.max(s, axis=0, keepdims=True))
            p = jnp.exp2(s - m_new)
            acc_ref[g] = jnp.exp2(m_prev - m_new) * acc_ref[g] + jnp.dot(
                vx_ref[g, :, pl.ds(s0, TK)], p.astype(BF16), preferred_element_type=F32)
            m_ref[g] = m_new

    scores(0, 0)

    def attn_pair(u, carry):
        scores(2 * u + 1, 1)
        accumulate(2 * u, 0)
        scores(2 * u + 2, 0)
        accumulate(2 * u + 1, 1)
        return carry

    lax.fori_loop(0, ntp // 2, attn_pair, 0)
    for g in range(ATTN_KV_HEADS):
        acc = acc_ref[g]
        o_t = acc[0:HEAD_DIM, :] * (1.0 / acc[HEAD_DIM:HEAD_DIM + 1, :])
        if wide:
            for r in range(0, ATTN_REP, 2):
                hq = g * ATTN_REP + r
                pair = jnp.concatenate([o_t[:, r * nq:(r + 1) * nq],
                                        o_t[:, (r + 1) * nq:(r + 2) * nq]], axis=0)
                o_ref[0, :, hq * HEAD_DIM:(hq + 2) * HEAD_DIM] = pair.T.astype(BF16)
        else:
            o_rq = jnp.concatenate([o_t, o_t], axis=0).T[:, 0:HEAD_DIM].astype(BF16)
            for r in range(ATTN_REP):
                hq = g * ATTN_REP + r
                o_ref[0, :, hq * HEAD_DIM:(hq + 1) * HEAD_DIM] = o_rq[r * nq:(r + 1) * nq, :]


def _dsa(q, qi, w, kidx_all, k_all, v_all, past_len, l_valid, t):
    b = kidx_all.shape[0]
    lp = kidx_all.shape[1]
    nq = Q_BLOCK if t % Q_BLOCK == 0 else SMALL_Q_BLOCK
    assert t % nq == 0 and lp % (2 * KEY_TILE) == 0 and lp < 2 ** 15
    topk = min(TOPK_MAX, l_valid // 4)
    width = ATTN_REP * nq
    if nq == Q_BLOCK:
        qblk = lambda c: pl.BlockSpec((1, nq, c), lambda i, j: (i, j, 0))
        q_specs = [qblk(ATTN_WIDTH), qblk(IDX_HEADS * IDX_DIM), qblk(LANE)]
    else:
        assert t == nq
        q_specs = [pl.BlockSpec((1, ATTN_KV_HEADS, width, HEAD_DIM), lambda i, j: (i, 0, 0, 0)),
                   pl.BlockSpec((1, IDX_HEADS * nq, IDX_DIM), lambda i, j: (i, 0, 0)),
                   pl.BlockSpec((1, 1, LANE), lambda i, j: (i, 0, 0))]
    slot_onehot = jnp.asarray(
        (np.arange(width)[:, None] % nq == np.arange(LANE)[None, :]).astype(np.float32), BF16)
    per_stream = lambda c: pl.BlockSpec((1, lp, c), lambda i, j: (i, 0, 0))
    return pl.pallas_call(
        functools.partial(_dsa_kernel, past_len=past_len, l_valid=l_valid, topk=topk, nq=nq),
        out_shape=jax.ShapeDtypeStruct((b, t, ATTN_WIDTH), BF16),
        grid=(b, t // nq),
        in_specs=q_specs + [pl.BlockSpec((width, LANE), lambda i, j: (0, 0)),
                            per_stream(IDX_DIM), per_stream(LANE), per_stream(LANE)],
        out_specs=pl.BlockSpec((1, nq, ATTN_WIDTH), lambda i, j: (i, j, 0)),
        scratch_shapes=[pltpu.VMEM((width, IDX_DIM), BF16),
                        pltpu.VMEM((ATTN_KV_HEADS, width, 2 * LANE), BF16),
                        pltpu.VMEM((lp, LANE), I32),
                        pltpu.VMEM((lp, 2 * LANE), BF16),
                        pltpu.VMEM((ATTN_KV_HEADS, PV_ROWS, lp), BF16),
                        pltpu.VMEM((ATTN_KV_HEADS, 1, width), F32),
                        pltpu.VMEM((ATTN_KV_HEADS, PV_ROWS, width), F32),
                        pltpu.VMEM((1, LANE), I32),
                        pltpu.VMEM((ATTN_KV_HEADS, 2, KEY_TILE, width), F32)],
        compiler_params=pltpu.CompilerParams(
            dimension_semantics=("arbitrary", "arbitrary"), vmem_limit_bytes=VMEM_LIMIT),
        name="dsa",
    )(q, qi, w, slot_onehot, kidx_all, k_all, v_all)


def _merge_kernel(x_ref, yn_ref, o_ref, gate_ref, wa_ref, wb_ref, wo_ref, out_ref):
    br_a = jnp.dot(yn_ref[...], wa_ref[...], preferred_element_type=F32)
    br_b = jnp.dot(o_ref[...], wb_ref[...], preferred_element_type=F32)
    gates = gate_ref[...].astype(F32)
    merged = _sigmoid(gates[:, :D_MODEL]) * br_a + _sigmoid(gates[:, D_MODEL:]) * br_b
    out_ref[...] = x_ref[...] + jnp.dot(merged.astype(BF16), wo_ref[...],
                                        preferred_element_type=F32)


def _merge(x2d, yn, o, gate, w_ssm_out, w_attn_out, w_o, tm):
    n = x2d.shape[0]
    assert n % tm == 0
    row = lambda w: pl.BlockSpec((tm, w), lambda i: (i, 0))
    full = lambda r, c: pl.BlockSpec((r, c), lambda i: (0, 0))
    return pl.pallas_call(
        _merge_kernel,
        out_shape=jax.ShapeDtypeStruct((n, D_MODEL), F32),
        grid=(n // tm,),
        in_specs=[row(D_MODEL), row(SSM_INNER), row(ATTN_WIDTH), row(2 * D_MODEL),
                  full(SSM_INNER, D_MODEL), full(ATTN_WIDTH, D_MODEL), full(D_MODEL, D_MODEL)],
        out_specs=row(D_MODEL),
        compiler_params=pltpu.CompilerParams(
            dimension_semantics=("arbitrary",), vmem_limit_bytes=VMEM_LIMIT),
        name="merge",
    )(x2d, yn, o, gate, w_ssm_out.astype(BF16), w_attn_out.astype(BF16), w_o.astype(BF16))


def _moe_kernel(x_ref, ng_ref, wr_ref, br_ref, w1_ref, w3_ref, w2_ref, fg_ref, tri_ref, out_ref,
                hs_ref, combs_ref, acc_ref, pt_ref, seg_ref, *, n_steps, blk_rows):
    e_step = pl.program_id(1)
    tm = x_ref.shape[0]
    sr = hs_ref.shape[0]
    BLK = blk_rows

    @pl.when(e_step == 0)
    def _():
        _rmsnorm_rows(x_ref, ng_ref, hs_ref, tm)
        h = hs_ref[0:tm, :]
        logits = jnp.dot(h, wr_ref[...], preferred_element_type=F32) + br_ref[...]
        lane_i = lax.broadcasted_iota(I32, (tm, LANE), 1)
        lane = lane_i.astype(F32)
        first = lambda hit: jnp.min(jnp.where(hit, lane, float(LANE)), axis=1, keepdims=True)
        is_g = (lane_i >= N_EXPERTS) & (lane_i < N_EXPERTS + MOE_GROUPS)
        glog = jnp.where(is_g, logits, -jnp.inf)
        gmax = jnp.max(glog, axis=1, keepdims=True)
        gsel = first(glog == gmax) - float(N_EXPERTS)
        p_group = 1.0 / jnp.sum(jnp.exp(glog - gmax), axis=1, keepdims=True)
        grp = (lane_i >> EXPERT_GROUP_SHIFT).astype(F32)
        in_grp = (lane_i < N_EXPERTS) & (grp == gsel)
        el = jnp.where(in_grp, logits, -jnp.inf)
        v1 = jnp.max(el, axis=1, keepdims=True)
        i1 = first(el == v1)
        el2 = jnp.where(lane == i1, -jnp.inf, el)
        v2 = jnp.max(el2, axis=1, keepdims=True)
        i2 = first(el2 == v2)
        e21 = jnp.exp(v2 - v1)
        den = 1.0 + e21
        comb = (jnp.where(lane == i1, (1.0 / den) * p_group, 0.0)
                + jnp.where(lane == i2, (e21 / den) * p_group, 0.0))

        own = lane == gsel
        own_f = jnp.where(own, 1.0, 0.0)
        before = jnp.dot(tri_ref[...], own_f.astype(BF16), preferred_element_type=F32)
        rank = jnp.sum(jnp.where(own, before, 0.0), axis=1, keepdims=True)
        counts = jnp.sum(own_f, axis=0, keepdims=True)
        blocks = jnp.floor((counts + (BLK - 1)) * (1.0 / BLK))
        upper = (lax.broadcasted_iota(I32, (LANE, LANE), 0)
                 < lax.broadcasted_iota(I32, (LANE, LANE), 1))
        starts = jnp.dot(jnp.broadcast_to(blocks, (8, LANE)).astype(BF16),
                         jnp.where(upper, 1.0, 0.0).astype(BF16),
                         preferred_element_type=F32)[0:1, :] * BLK
        dest = jnp.sum(jnp.where(own, starts, 0.0), axis=1, keepdims=True) + rank
        for g in range(MOE_GROUPS):
            seg_ref[g] = jnp.sum(starts[:, g:g + 1]).astype(I32)
            seg_ref[MOE_GROUPS + g] = jnp.sum(blocks[:, g:g + 1]).astype(I32)

        dest_i = dest.astype(I32)
        pt = jnp.where(dest_i == lax.broadcasted_iota(I32, (tm, sr), 1), 1.0, 0.0).astype(BF16)
        pt_ref[...] = pt
        dest_row = jnp.broadcast_to(dest, (tm, LANE)).T[0:1, :].astype(I32)
        p_mat = jnp.where(lax.broadcasted_iota(I32, (sr, tm), 0) == dest_row, 1.0, 0.0
                          ).astype(BF16)
        hs_ref[...] = jnp.dot(p_mat, h, preferred_element_type=F32).astype(BF16)
        c3 = jnp.dot(p_mat, _split3(comb), preferred_element_type=F32)
        combs_ref[...] = c3[:, 0:LANE] + c3[:, LANE:2 * LANE] + c3[:, 2 * LANE:]
        acc_ref[...] = jnp.zeros(acc_ref.shape, F32)

    group = e_step // (EXPERTS_PER_GROUP // EXPERTS_PER_STEP)
    seg_start = seg_ref[group]
    lane = lax.broadcasted_iota(I32, (BLK, LANE), 1)

    def row_block(blk, carry):
        rows = pl.ds(pl.multiple_of(seg_start + blk * BLK, BLK), BLK)
        h = hs_ref[rows, :]
        comb = combs_ref[rows, :]
        for k in range(EXPERTS_PER_STEP):
            e = e_step * EXPERTS_PER_STEP + k
            a1 = jnp.dot(h, w1_ref[k], preferred_element_type=F32)
            a3 = jnp.dot(h, w3_ref[k], preferred_element_type=F32)
            he = (_silu(a1) * a3).astype(BF16)
            ye = jnp.dot(he, w2_ref[k], preferred_element_type=F32)
            wcol = jnp.sum(jnp.where(lane == e, comb, 0.0), axis=1, keepdims=True)
            acc_ref[rows, :] += wcol * ye
        return carry

    lax.fori_loop(0, seg_ref[MOE_GROUPS + group], row_block, 0)

    @pl.when(e_step == n_steps - 1)
    def _():
        y = x_ref[...] + jnp.dot(pt_ref[...], acc_ref[...].astype(BF16),
                                 preferred_element_type=F32)
        acc_ref[0:tm, :] = y
        _rmsnorm_rows(acc_ref, fg_ref, out_ref, tm)


def _moe(x2d, norm_g, w_router_group, b_router_group, w_router_expert, b_router_expert,
         w1, w3, w2, norm_final_g, tm):
    n = x2d.shape[0]
    blk_rows = min(MOE_ROW_BLOCK, tm)
    assert n % tm == 0 and tm % blk_rows == 0
    n_steps = N_EXPERTS // EXPERTS_PER_STEP
    sr = tm + (MOE_GROUPS - 1) * blk_rows
    padw = LANE - N_EXPERTS - MOE_GROUPS
    wr = jnp.concatenate([w_router_expert, w_router_group,
                          jnp.zeros((D_MODEL, padw), F32)], axis=1).astype(BF16)
    br = jnp.concatenate([b_router_expert, b_router_group,
                          jnp.zeros((padw,), F32)]).astype(F32).reshape(1, LANE)
    tri = jnp.asarray(np.tril(np.ones((tm, tm), np.float32), -1), BF16)
    row = pl.BlockSpec((tm, D_MODEL), lambda i, e: (i, 0))
    const = lambda r, c: pl.BlockSpec((r, c), lambda i, e: (0, 0))
    return pl.pallas_call(
        functools.partial(_moe_kernel, n_steps=n_steps, blk_rows=blk_rows),
        out_shape=jax.ShapeDtypeStruct((n, D_MODEL), F32),
        grid=(n // tm, n_steps),
        in_specs=[row, const(1, D_MODEL), const(D_MODEL, LANE), const(1, LANE),
                  pl.BlockSpec((EXPERTS_PER_STEP, D_MODEL, EXPERT_FF), lambda i, e: (e, 0, 0)),
                  pl.BlockSpec((EXPERTS_PER_STEP, D_MODEL, EXPERT_FF), lambda i, e: (e, 0, 0)),
                  pl.BlockSpec((EXPERTS_PER_STEP, EXPERT_FF, D_MODEL), lambda i, e: (e, 0, 0)),
                  const(1, D_MODEL), const(tm, tm)],
        out_specs=row,
        scratch_shapes=[pltpu.VMEM((sr, D_MODEL), BF16), pltpu.VMEM((sr, LANE), F32),
                        pltpu.VMEM((sr, D_MODEL), F32), pltpu.VMEM((tm, sr), BF16),
                        pltpu.SMEM((2 * MOE_GROUPS,), I32)],
        compiler_params=pltpu.CompilerParams(
            dimension_semantics=("arbitrary", "arbitrary"), vmem_limit_bytes=VMEM_LIMIT),
        name="moe",
    )(x2d, norm_g.reshape(1, D_MODEL), wr, br, w1.astype(BF16), w3.astype(BF16),
      w2.astype(BF16), norm_final_g.reshape(1, D_MODEL), tri)


def _token_tile(n):
    for tm in (1024, 512, 256, 128, 64, 32, 16, 8):
        if n % tm == 0:
            return tm
    raise ValueError(f"token count {n} is not a multiple of 8")


def _pad_axis(a, axis, size):
    if a.shape[axis] == size:
        return a
    widths = [(0, 0)] * a.ndim
    widths[axis] = (0, size - a.shape[axis])
    return jnp.pad(a, widths)


def _round_up(n, m):
    return (n + m - 1) // m * m


def _stream_step(x, conv_state, ssm_state, k_past, v_past, kidx_past, w_packed, p):
    b, t, d = x.shape
    n = b * t
    tm = _token_tile(n)
    x2d = x.reshape(n, d)
    z, xbc, q, gate, k, v, qi, ki, dtw, k_b, v_b, ki_b = _in_proj(
        x2d, p["norm_mix_g"], w_packed, tm)

    yn, conv_new, ssm_new = _ssd(
        xbc.reshape(b, t, CONV_DIM), z.reshape(b, t, SSM_INNER), dtw.reshape(b, t, LANE),
        conv_state, ssm_state, p["conv_w"], p["conv_b"], p["dt_bias"], p["A_log"],
        p["D_skip"], p["ssm_norm_g"])

    kv_w = ATTN_KV_HEADS * HEAD_DIM
    k_all, v_all, kidx_all = (k_b.reshape(b, t, kv_w), v_b.reshape(b, t, kv_w),
                              ki_b.reshape(b, t, IDX_DIM))
    past_len = 0
    if k_past is not None:
        past_len = k_past.shape[1]
        k_all = jnp.concatenate([k_past.astype(BF16).reshape(b, past_len, kv_w), k_all], axis=1)
        v_all = jnp.concatenate([v_past.astype(BF16).reshape(b, past_len, kv_w), v_all], axis=1)
        kidx_all = jnp.concatenate([kidx_past.astype(BF16), kidx_all], axis=1)
    l_valid = past_len + t
    lp = _round_up(l_valid, 2 * KEY_TILE)
    if t % Q_BLOCK == 0:
        q_in = (q.reshape(b, t, ATTN_WIDTH), qi.reshape(b, t, IDX_HEADS * IDX_DIM),
                dtw.reshape(b, t, LANE))
    else:
        q_in = (q.reshape(b, t, ATTN_KV_HEADS, ATTN_REP, HEAD_DIM).transpose(0, 2, 3, 1, 4)
                .reshape(b, ATTN_KV_HEADS, ATTN_REP * t, HEAD_DIM),
                qi.reshape(b, t, IDX_HEADS, IDX_DIM).transpose(0, 2, 1, 3)
                .reshape(b, IDX_HEADS * t, IDX_DIM),
                dtw.reshape(b, t, LANE)[:, :, SSM_HEADS:SSM_HEADS + IDX_HEADS]
                .transpose(0, 2, 1).reshape(b, 1, IDX_HEADS * t))
    o = _dsa(*q_in, _pad_axis(kidx_all, 1, lp), _pad_axis(k_all, 1, lp), _pad_axis(v_all, 1, lp),
             past_len, l_valid, t)
    o = o.reshape(n, ATTN_WIDTH)

    x1 = _merge(x2d, yn.reshape(n, SSM_INNER), o, gate, p["w_ssm_out"], p["w_attn_out"],
                p["w_o"], min(tm, MERGE_TILE))
    y = _moe(x1, p["norm_moe_g"], p["w_router_group"], p["b_router_group"],
             p["w_router_expert"], p["b_router_expert"], p["w1"], p["w3"], p["w2"],
             p["norm_final_g"], tm)
    return (y.reshape(b, t, d), k.reshape(b, t, ATTN_KV_HEADS, HEAD_DIM),
            v.reshape(b, t, ATTN_KV_HEADS, HEAD_DIM), ki.reshape(b, t, IDX_DIM), conv_new, ssm_new)


def kernel(x_prompt, x_sample, cache_k, cache_v, cache_kidx, state_conv, state_ssm, norm_mix_g, w_in, conv_w, conv_b, dt_bias, A_log, D_skip, ssm_norm_g, w_ssm_out, w_attn_out, w_o, norm_moe_g, w_router_group, b_router_group, w_router_expert, b_router_expert, w1, w3, w2, norm_final_g):
    p = dict(norm_mix_g=norm_mix_g, conv_w=conv_w, conv_b=conv_b, dt_bias=dt_bias, A_log=A_log,
             D_skip=D_skip, ssm_norm_g=ssm_norm_g, w_ssm_out=w_ssm_out, w_attn_out=w_attn_out,
             w_o=w_o, norm_moe_g=norm_moe_g, w_router_group=w_router_group,
             b_router_group=b_router_group, w_router_expert=w_router_expert,
             b_router_expert=b_router_expert, w1=w1, w3=w3, w2=w2, norm_final_g=norm_final_g)
    w_packed = _pack_w_in(w_in)
    yp, kp, vp, kip, cp, sp = _stream_step(x_prompt, None, None, None, None, None, w_packed, p)
    ys, ksn, vsn, kisn, csn, ssn = _stream_step(x_sample, state_conv, state_ssm, cache_k,
                                                cache_v, cache_kidx, w_packed, p)
    return (yp, ys, kp, vp, kip, cp, sp, ksn, vsn, kisn, csn, ssn)
```

```python
import functools

import numpy as np
import jax
import jax.numpy as jnp
from jax import lax
from jax.experimental import pallas as pl
from jax.experimental.pallas import tpu as pltpu

F32 = jnp.float32
BF16 = jnp.bfloat16
I32 = jnp.int32

D_MODEL = 1024
CHUNK = 64
CHUNK_SHIFT = 6
EPS = 1e-6
SSM_INNER = 2048
SSM_HEADDIM = 64
SSM_HEADS = 32
SSM_GROUPS = 4
SSM_HEADS_PER_GROUP = SSM_HEADS // SSM_GROUPS
SSM_STATE = 128
CONV_WIDTH = 4
CONV_DIM = SSM_INNER + 2 * SSM_GROUPS * SSM_STATE
CONV_CARRY = 16
SSD_CHUNKS_PER_STEP = 2
ATTN_HEADS = 16
ATTN_KV_HEADS = 2
HEAD_DIM = 64
ATTN_REP = ATTN_HEADS // ATTN_KV_HEADS
ATTN_WIDTH = ATTN_HEADS * HEAD_DIM
IDX_HEADS = 8
IDX_DIM = 64
IDX_SCALE = (IDX_HEADS * IDX_DIM) ** -0.5
TOPK_MAX = 256
MOE_GROUPS = 4
EXPERTS_PER_GROUP = 8
EXPERT_GROUP_SHIFT = 3
N_EXPERTS = 32
EXPERT_FF = 256
N_BRANCHES = 2

LANE = 128
VMEM_LIMIT = 56 * 1024 * 1024
PROJ_TILE = 1024
PROJ_TILES = 9
Q_BLOCK = 128
SMALL_Q_BLOCK = 16
KEY_TILE = 256
EXPERTS_PER_STEP = 4
MOE_ROW_BLOCK = 128
MERGE_TILE = 512
INT_MIN = np.int32(-2 ** 31)
NEG_BIG = -1e30
Q_SCALE = HEAD_DIM ** -0.5 * float(np.log2(np.e))
PV_ROWS = 80

NT_DIMS = (((1,), (1,)), ((), ()))
TN_DIMS = (((0,), (0,)), ((), ()))


def _sigmoid(x):
    return 1.0 / (1.0 + jnp.exp(-x))


def _silu(x):
    h = 0.5 * x
    return h + h * jnp.tanh(h)


def _split3(x):
    hi = x.astype(BF16)
    r1 = x - hi.astype(F32)
    mid = r1.astype(BF16)
    r2 = r1 - mid.astype(F32)
    lo = r2.astype(BF16)
    return jnp.concatenate([hi, mid, lo], axis=1)


def _inproj_kernel(x_ref, g_ref, w_ref, z_ref, xbc_ref, q_ref, gate_ref, k_ref, v_ref,
                   qi_ref, ki_ref, dtw_ref, kb_ref, vb_ref, kib_ref, h_ref):
    j = pl.program_id(1)

    @pl.when(j == 0)
    def _():
        x = x_ref[...]
        ms = jnp.mean(x * x, axis=-1, keepdims=True)
        h_ref[...] = ((x * lax.rsqrt(ms + EPS)) * g_ref[...]).astype(BF16)

    def tile():
        return jnp.dot(h_ref[...], w_ref[...], preferred_element_type=F32)

    @pl.when(j < 2)
    def _():
        z_ref[...] = tile().astype(BF16)

    @pl.when((j >= 2) & (j < 5))
    def _():
        xbc_ref[...] = tile().astype(BF16)

    @pl.when(j == 5)
    def _():
        q_ref[...] = (tile() * Q_SCALE).astype(BF16)

    @pl.when((j >= 6) & (j < 8))
    def _():
        gate_ref[...] = tile().astype(BF16)

    @pl.when(j == 8)
    def _():
        t = tile()
        k_ref[...] = t[:, 0:128]
        v_ref[...] = t[:, 128:256]
        qi_ref[...] = t[:, 256:768].astype(BF16)
        ki_ref[...] = t[:, 768:832]
        dtw_ref[...] = t[:, 896:1024]
        kb_ref[...] = t[:, 0:128].astype(BF16)
        vb_ref[...] = t[:, 128:256].astype(BF16)
        kib_ref[...] = t[:, 768:832].astype(BF16)


def _pack_w_in(w_in):
    sizes = (SSM_INNER, CONV_DIM, SSM_HEADS, ATTN_WIDTH, ATTN_KV_HEADS * HEAD_DIM,
             ATTN_KV_HEADS * HEAD_DIM, IDX_HEADS * IDX_DIM, IDX_DIM, IDX_HEADS,
             N_BRANCHES * D_MODEL)
    offs = np.concatenate([[0], np.cumsum(sizes)])
    z, xbc, dt, q, k, v, qi, ki, wi, g = [w_in[:, offs[i]:offs[i + 1]] for i in range(10)]
    zeros = lambda n: jnp.zeros((D_MODEL, n), w_in.dtype)
    cols = [z, xbc, q, g, k, v, qi, ki, zeros(64), dt, wi, zeros(LANE - SSM_HEADS - IDX_HEADS)]
    return jnp.concatenate(cols, axis=1).astype(BF16)


def _in_proj(x2d, norm_g, w_packed, tm):
    n = x2d.shape[0]
    assert n % tm == 0
    bf = lambda c: jax.ShapeDtypeStruct((n, c), BF16)
    ff = lambda c: jax.ShapeDtypeStruct((n, c), F32)
    clip = lambda j, lo, hi: jnp.clip(j - lo, 0, hi - lo - 1)
    return pl.pallas_call(
        _inproj_kernel,
        out_shape=(bf(SSM_INNER), bf(CONV_DIM), bf(ATTN_WIDTH), bf(2 * D_MODEL),
                   ff(128), ff(128), bf(512), ff(IDX_DIM), ff(128), bf(128), bf(128), bf(IDX_DIM)),
        grid=(n // tm, PROJ_TILES),
        in_specs=[
            pl.BlockSpec((tm, D_MODEL), lambda i, j: (i, 0)),
            pl.BlockSpec((1, D_MODEL), lambda i, j: (0, 0)),
            pl.BlockSpec((D_MODEL, PROJ_TILE), lambda i, j: (0, j)),
        ],
        out_specs=(
            pl.BlockSpec((tm, PROJ_TILE), lambda i, j: (i, clip(j, 0, 2))),
            pl.BlockSpec((tm, PROJ_TILE), lambda i, j: (i, clip(j, 2, 5))),
            pl.BlockSpec((tm, PROJ_TILE), lambda i, j: (i, 0)),
            pl.BlockSpec((tm, PROJ_TILE), lambda i, j: (i, clip(j, 6, 8))),
            pl.BlockSpec((tm, 128), lambda i, j: (i, 0)),
            pl.BlockSpec((tm, 128), lambda i, j: (i, 0)),
            pl.BlockSpec((tm, 512), lambda i, j: (i, 0)),
            pl.BlockSpec((tm, IDX_DIM), lambda i, j: (i, 0)),
            pl.BlockSpec((tm, 128), lambda i, j: (i, 0)),
            pl.BlockSpec((tm, 128), lambda i, j: (i, 0)),
            pl.BlockSpec((tm, 128), lambda i, j: (i, 0)),
            pl.BlockSpec((tm, IDX_DIM), lambda i, j: (i, 0)),
        ),
        scratch_shapes=[pltpu.VMEM((tm, D_MODEL), BF16)],
        compiler_params=pltpu.CompilerParams(
            dimension_semantics=("arbitrary", "arbitrary"), vmem_limit_bytes=VMEM_LIMIT),
        name="in_proj",
    )(x2d, norm_g.reshape(1, D_MODEL), w_packed)


def _ssd_kernel(*refs, L, cps, nsteps, has_state):
    if has_state:
        (xbc_ref, z_ref, dtw_ref, cs_ref, s0_ref, shift_ref, convw_ref, convb_ref, dtb_ref, a_ref,
         dx_ref, ng_ref, e3p_ref, e3l_ref, tri_ref, bd_ref, yn_ref, cnew_ref, sout_ref,
         xe_ref, st, act_ref, ex_ref, yz_ref, acl_ref) = refs
    else:
        (xbc_ref, z_ref, dtw_ref, shift_ref, convw_ref, convb_ref, dtb_ref, a_ref,
         dx_ref, ng_ref, e3p_ref, e3l_ref, tri_ref, bd_ref, yn_ref, cnew_ref, sout_ref,
         xe_ref, st, act_ref, ex_ref, yz_ref, acl_ref) = refs
    c = pl.program_id(1)
    G, R, N, P = SSM_GROUPS, SSM_HEADS_PER_GROUP, SSM_STATE, SSM_HEADDIM
    GW = R * P
    RL = R * L
    LB = cps * L

    @pl.when(c == 0)
    def _():
        xe_ref[LB:, :] = jnp.zeros((xe_ref.shape[0] - LB, CONV_DIM), BF16)
        if has_state:
            xe_ref[LB:LB + CONV_CARRY, :] = cs_ref[0]
            st[...] = s0_ref[0].T
        else:
            st[...] = jnp.zeros(st.shape, F32)

    xe_ref[0:LB, :] = xbc_ref[0]
    shifted = jnp.dot(shift_ref[0], xe_ref[...], preferred_element_type=F32)
    w = convw_ref[...]
    conv = convb_ref[...] + shifted[2 * LB:3 * LB, :] * w[0:1, :]
    conv = conv + shifted[LB:2 * LB, :] * w[1:2, :]
    conv = conv + shifted[0:LB, :] * w[2:3, :]
    conv = conv + xbc_ref[0].astype(F32) * w[3:4, :]
    act_ref[...] = _silu(conv)

    @pl.when(c == nsteps - 1)
    def _():
        cnew_ref[0] = xbc_ref[0, LB - CONV_CARRY:LB, :].astype(F32)[
            CONV_CARRY - (CONV_WIDTH - 1):CONV_CARRY, :]

    if nsteps > 1:
        xe_ref[LB:LB + CONV_CARRY, :] = xe_ref[LB - CONV_CARRY:LB, :]

    u = dtw_ref[0] + dtb_ref[...]
    dt = jnp.maximum(u, 0.0) + jnp.log1p(jnp.exp(-jnp.abs(u)))
    a = dt * a_ref[...]
    ac3 = jnp.dot(tri_ref[...], _split3(a), preferred_element_type=F32)
    a_cum = ac3[:, 0:LANE] + ac3[:, LANE:2 * LANE] + ac3[:, 2 * LANE:]

    ex_ref[...] = jnp.dot(_split3(jnp.concatenate([a_cum, dt], axis=0)), e3p_ref[...],
                          preferred_element_type=F32)
    if L != P:
        acl_ref[...] = jnp.dot(_split3(a_cum), e3l_ref[...], preferred_element_type=F32)
    row = lax.broadcasted_iota(I32, (L, RL), 0)
    scol = lax.broadcasted_iota(I32, (L, RL), 1) & (L - 1)

    for sub in range(cps):
        r0 = sub * L
        rows = slice(r0, r0 + L)
        ssq = jnp.zeros((L, 1), F32)
        for g in range(G):
            sl = slice(g * GW, (g + 1) * GW)
            xs = act_ref[rows, sl]
            b_g = act_ref[rows, SSM_INNER + g * N:SSM_INNER + (g + 1) * N].astype(BF16)
            c_g = act_ref[rows, SSM_INNER + (G + g) * N:SSM_INNER + (G + g + 1) * N].astype(BF16)
            acx = ex_ref[rows, sl]
            alast = ex_ref[r0 + L - 1:r0 + L, sl]
            xdt = xs * ex_ref[LB + r0:LB + r0 + L, sl]
            acl_g = acx if L == P else acl_ref[rows, g * RL:(g + 1) * RL]
            acs_g = jnp.sum(jnp.where(scol == row, acl_g, 0.0), axis=0, keepdims=True)
            decay = jnp.where(scol <= row, jnp.exp(acl_g - acs_g), 0.0)
            cb = lax.dot_general(c_g, jnp.concatenate([b_g] * R, axis=0), NT_DIMS,
                                 preferred_element_type=F32)
            m_g = (cb * decay).astype(BF16)
            xbd = jnp.concatenate([xdt.astype(BF16)] * R, axis=0) * bd_ref[...]
            y_in = jnp.dot(m_g, xbd, preferred_element_type=F32)
            st_g = st[:, sl]
            y_st = jnp.dot(c_g, st_g.astype(BF16), preferred_element_type=F32)
            xw = (xdt * jnp.exp(alast - acx)).astype(BF16)
            upd = lax.dot_general(b_g, xw, TN_DIMS, preferred_element_type=F32)
            st[:, sl] = st_g * jnp.exp(alast) + upd
            y = y_in + jnp.exp(acx) * y_st + dx_ref[:, sl] * xs
            yz = y * _silu(z_ref[0, rows, sl].astype(F32))
            ssq = ssq + jnp.sum(yz * yz, axis=-1, keepdims=True)
            yz_ref[rows, sl] = yz
        scale = lax.rsqrt(ssq * (1.0 / SSM_INNER) + EPS)
        yn_ref[0, rows, :] = ((yz_ref[rows, :] * scale) * ng_ref[...]).astype(BF16)

    @pl.when(c == nsteps - 1)
    def _():
        sout_ref[0] = st[...].T


def _shift_matrices(L, conv_rows):
    out = np.zeros((2, 3 * L, conv_rows), np.float32)
    for k in range(CONV_WIDTH - 1):
        for t in range(L):
            src = t - (k + 1)
            if src >= 0:
                out[:, k * L + t, src] = 1.0
            else:
                out[1, k * L + t, L + CONV_CARRY + src] = 1.0
                j = CONV_WIDTH - 1 + src
                for piece in range(3):
                    out[0, k * L + t, L + 3 * piece + j] = 1.0
    return jnp.asarray(out, BF16)


def _expand_matrix(width_per_head):
    r = np.arange(3 * LANE)[:, None] % LANE
    c = np.arange(SSM_HEADS * width_per_head)[None, :] // width_per_head
    return jnp.asarray((r == c).astype(np.float32), BF16)


def _ssd(xbc, z, dtw, conv_state, ssm_state, conv_w, conv_b, dt_bias, a_log, d_skip, ssm_norm_g):
    b, t, _ = xbc.shape
    L = min(CHUNK, t)
    nc = t // L
    assert t % L == 0 and L & (L - 1) == 0
    has_state = conv_state is not None
    pad = lambda v: jnp.pad(v.astype(F32), (0, LANE - SSM_HEADS)).reshape(1, LANE)
    a_neg = pad(-jnp.exp(a_log.astype(F32)))
    cps = SSD_CHUNKS_PER_STEP if nc % SSD_CHUNKS_PER_STEP == 0 else 1
    nsteps = nc // cps
    lb = cps * L
    conv_rows = _round_up(lb + CONV_CARRY, LANE)
    tri = jnp.asarray(np.kron(np.eye(cps), np.tril(np.ones((L, L)))).astype(np.float32), BF16)
    rl = SSM_HEADS_PER_GROUP * L
    gw = SSM_HEADS_PER_GROUP * SSM_HEADDIM
    bd = jnp.asarray((np.arange(rl)[:, None] // L == np.arange(gw)[None, :] // SSM_HEADDIM)
                     .astype(np.float32), BF16)
    const = lambda shape: pl.BlockSpec(shape, lambda i, j: (0,) * len(shape))
    tok = lambda w: pl.BlockSpec((1, lb, w), lambda i, j: (i, j, 0))
    per_b = lambda s: pl.BlockSpec((1,) + s, lambda i, j: (i, 0, 0))
    ins = [xbc, z, dtw]
    specs = [tok(CONV_DIM), tok(SSM_INNER), tok(LANE)]
    if has_state:
        triples = _split3(conv_state.astype(F32))
        ins += [_pad_axis(triples, 1, CONV_CARRY), ssm_state.reshape(b, SSM_INNER, SSM_STATE)]
        specs += [per_b((CONV_CARRY, CONV_DIM)), per_b((SSM_INNER, SSM_STATE))]
    first = 0 if has_state else 1
    ins += [_shift_matrices(lb, conv_rows)]
    specs += [pl.BlockSpec((1, 3 * lb, conv_rows),
                           lambda i, j: (jnp.where(j == 0, first, 1), 0, 0))]
    ins += [conv_w, conv_b.reshape(1, CONV_DIM), pad(dt_bias), a_neg,
            jnp.repeat(d_skip.astype(F32), SSM_HEADDIM).reshape(1, SSM_INNER),
            ssm_norm_g.reshape(1, SSM_INNER), _expand_matrix(SSM_HEADDIM), _expand_matrix(L),
            tri, bd]
    specs += [const((CONV_WIDTH, CONV_DIM)), const((1, CONV_DIM)), const((1, LANE)),
              const((1, LANE)), const((1, SSM_INNER)), const((1, SSM_INNER)),
              const((3 * LANE, SSM_INNER)), const((3 * LANE, SSM_HEADS * L)),
              const((lb, lb)), const((rl, gw))]
    yn, cnew, sout = pl.pallas_call(
        functools.partial(_ssd_kernel, L=L, cps=cps, nsteps=nsteps, has_state=has_state),
        out_shape=(jax.ShapeDtypeStruct((b, t, SSM_INNER), BF16),
                   jax.ShapeDtypeStruct((b, CONV_WIDTH - 1, CONV_DIM), F32),
                   jax.ShapeDtypeStruct((b, SSM_INNER, SSM_STATE), F32)),
        grid=(b, nsteps),
        in_specs=specs,
        out_specs=(tok(SSM_INNER), per_b((CONV_WIDTH - 1, CONV_DIM)),
                   per_b((SSM_INNER, SSM_STATE))),
        scratch_shapes=[pltpu.VMEM((conv_rows, CONV_DIM), BF16),
                        pltpu.VMEM((SSM_STATE, SSM_INNER), F32),
                        pltpu.VMEM((lb, CONV_DIM), F32),
                        pltpu.VMEM((2 * lb, SSM_INNER), F32),
                        pltpu.VMEM((lb, SSM_INNER), F32),
                        pltpu.VMEM((lb, SSM_HEADS * L), F32)],
        compiler_params=pltpu.CompilerParams(
            dimension_semantics=("arbitrary", "arbitrary"), vmem_limit_bytes=VMEM_LIMIT),
        name="ssd",
    )(*ins)
    return yn, cnew, sout.reshape(b, SSM_HEADS, SSM_HEADDIM, SSM_STATE)


def _dsa_kernel(q_ref, qi_ref, w_ref, eq_ref, kidx_ref, k_ref, v_ref, o_ref,
                qs_ref, qx_ref, key_ref, kx_ref, vx_ref, m_ref, acc_ref, mlim_ref, s_ref,
                *, past_len, l_valid, topk, nq):
    qb = pl.program_id(1)
    TQ, TK = LANE, KEY_TILE
    W = ATTN_REP * nq
    wide = nq == LANE

    lane = lax.broadcasted_iota(I32, (1, TQ), 1)
    pos = past_len + qb * nq + (lane & (nq - 1))
    n_adm = jnp.minimum(((pos >> CHUNK_SHIFT) + 1) * CHUNK, l_valid)
    k_eff = jnp.minimum(n_adm, topk)
    last_pos = past_len + qb * nq + nq - 1
    n_max = jnp.minimum(((last_pos >> CHUNK_SHIFT) + 1) * CHUNK, l_valid)
    nt = (n_max + TK - 1) // TK
    ntp = 2 * ((nt + 1) // 2)
    lp = key_ref.shape[0]

    @pl.when(qb == 0)
    def _():
        kx_ref[:, LANE:] = k_ref[0]

        def v_tile(t, carry):
            s0 = pl.multiple_of(t * TK, TK)
            v_t = v_ref[0, pl.ds(s0, TK), :].astype(F32).T
            for g in range(ATTN_KV_HEADS):
                vx_ref[g, 0:HEAD_DIM, pl.ds(s0, TK)] = (
                    v_t[g * HEAD_DIM:(g + 1) * HEAD_DIM, :].astype(BF16))
            return carry

        lax.fori_loop(0, lp // TK, v_tile, 0)
        tail_row = lax.broadcasted_iota(I32, (PV_ROWS - HEAD_DIM, lp), 0)
        for g in range(ATTN_KV_HEADS):
            vx_ref[g, HEAD_DIM:, :] = jnp.where(tail_row == 0, 1.0, 0.0).astype(BF16)
            qx_ref[g, :, 0:LANE] = eq_ref[...]
            qx_ref[g, :, LANE:] = jnp.zeros((W, LANE), BF16)

    if wide:
        for h in range(IDX_HEADS):
            qs_ref[h * nq:(h + 1) * nq, :] = qi_ref[0, :, h * IDX_DIM:(h + 1) * IDX_DIM]
        w_t = w_ref[0].T[SSM_HEADS:SSM_HEADS + IDX_HEADS, :] * IDX_SCALE
    else:
        qs_ref[...] = qi_ref[0]
        w_row = w_ref[0] * IDX_SCALE

    def index_dots(t, slot):
        s0 = pl.multiple_of(jnp.minimum(t, ntp - 1) * TK, TK)
        s_ref[0, slot] = lax.dot_general(kidx_ref[0, pl.ds(s0, TK), :], qs_ref[...], NT_DIMS,
                                         preferred_element_type=F32)

    def score_tile(t, slot):
        s0 = pl.multiple_of(t * TK, TK)
        d = s_ref[0, slot]
        if wide:
            sc = w_t[0:1, :] * jnp.maximum(d[:, 0:nq], 0.0)
            for h in range(1, IDX_HEADS):
                sc = sc + w_t[h:h + 1, :] * jnp.maximum(d[:, h * nq:(h + 1) * nq], 0.0)
        else:
            sc = w_row * jnp.maximum(d, 0.0)
            for shift in (64, 32, 16):
                sc = sc + pltpu.roll(sc, shift, axis=1)
        sc = sc + 0.0
        bits = pltpu.bitcast(sc, I32)
        key = jnp.where(bits < 0, bits ^ np.int32(0x7FFFFFFF), bits)
        s_idx = s0 + lax.broadcasted_iota(I32, (TK, TQ), 0)
        key_ref[pl.ds(s0, TK), :] = jnp.where(s_idx < n_adm, key, INT_MIN)

    index_dots(0, 0)

    def score_pair(u, carry):
        index_dots(2 * u + 1, 1)
        score_tile(2 * u, 0)
        index_dots(2 * u + 2, 0)
        score_tile(2 * u + 1, 1)
        return carry

    lax.fori_loop(0, ntp // 2, score_pair, 0)

    def count(pred):
        def body(t, acc):
            s0 = pl.multiple_of(t * (2 * TK), 2 * TK)
            blk = key_ref[pl.ds(s0, 2 * TK), :]
            s_idx = s0 + lax.broadcasted_iota(I32, (2 * TK, TQ), 0)
            hit = jnp.where(pred(blk, s_idx), 1, 0).astype(I32)
            return acc + jnp.sum(hit.reshape(2 * TK // 8, 8, TQ), axis=0)
        acc = lax.fori_loop(0, ntp // 2, body, jnp.zeros((8, TQ), I32))
        return jnp.sum(acc, axis=0, keepdims=True)

    def bit_step(i, prefix):
        cand = prefix | jnp.left_shift(jnp.int32(1), 31 - i)
        cand_s = cand ^ INT_MIN
        cnt = count(lambda blk, s_idx: blk >= cand_s)
        return jnp.where(cnt >= k_eff, cand, prefix)

    thr = lax.fori_loop(0, 32, bit_step, jnp.zeros((1, TQ), I32)) ^ INT_MIN

    n_gt = count(lambda blk, s_idx: blk > thr)
    n_eq = count(lambda blk, s_idx: blk == thr)
    need = k_eff - n_gt
    mlim_ref[...] = jnp.full((1, TQ), 2 ** 30, I32)

    @pl.when(jnp.max(n_eq - need) > 0)
    def _():
        def idx_step(i, prefix):
            cand = prefix | jnp.left_shift(jnp.int32(1), 14 - i)
            cnt = count(lambda blk, s_idx: (blk == thr) & (s_idx < cand))
            return jnp.where(cnt < need, cand, prefix)
        mlim_ref[...] = lax.fori_loop(0, 15, idx_step, jnp.zeros((1, TQ), I32))

    mlim = mlim_ref[...]

    def bias_tile(t, carry):
        s0 = pl.multiple_of(t * TK, TK)
        blk = key_ref[pl.ds(s0, TK), :]
        s_idx = s0 + lax.broadcasted_iota(I32, (TK, TQ), 0)
        sel = (blk > thr) | ((blk == thr) & (s_idx <= mlim))
        kx_ref[pl.ds(s0, TK), 0:LANE] = jnp.where(sel, 0.0, NEG_BIG).astype(BF16)
        return carry

    lax.fori_loop(0, nt, bias_tile, 0)

    @pl.when(nt < ntp)
    def _():
        kx_ref[pl.ds(pl.multiple_of(nt * TK, TK), TK), 0:LANE] = jnp.full((TK, LANE), NEG_BIG,
                                                                           BF16)

    for g in range(ATTN_KV_HEADS):
        lanes = slice(LANE + g * HEAD_DIM, LANE + (g + 1) * HEAD_DIM)
        if wide:
            for r in range(ATTN_REP):
                hq = g * ATTN_REP + r
                qx_ref[g, r * nq:(r + 1) * nq, lanes] = (
                    q_ref[0, :, hq * HEAD_DIM:(hq + 1) * HEAD_DIM])
        else:
            qx_ref[g, :, lanes] = q_ref[0, g]
    m_ref[...] = jnp.full(m_ref.shape, -jnp.inf, F32)
    acc_ref[...] = jnp.zeros(acc_ref.shape, F32)

    def scores(t, slot):
        s0 = pl.multiple_of(jnp.minimum(t, ntp - 1) * TK, TK)
        k_tile = kx_ref[pl.ds(s0, TK), :]
        for g in range(ATTN_KV_HEADS):
            s_ref[g, slot] = lax.dot_general(k_tile, qx_ref[g], NT_DIMS,
                                             preferred_element_type=F32)

    def accumulate(t, slot):
        s0 = pl.multiple_of(t * TK, TK)
        for g in range(ATTN_KV_HEADS):
            s = s_ref[g, slot]
            m_prev = m_ref[g]
            m_new = jnp.maximum(m_prev, jnp.max(s, axis=0, keepdims=True))
            p = jnp.exp2(s - m_new)
            acc_ref[g] = jnp.exp2(m_prev - m_new) * acc_ref[g] + jnp.dot(
                vx_ref[g, :, pl.ds(s0, TK)], p.astype(BF16), preferred_element_type=F32)
            m_ref[g] = m_new

    scores(0, 0)

    def attn_pair(u, carry):
        scores(2 * u + 1, 1)
        accumulate(2 * u, 0)
        scores(2 * u + 2, 0)
        accumulate(2 * u + 1, 1)
        return carry

    lax.fori_loop(0, ntp // 2, attn_pair, 0)
    for g in range(ATTN_KV_HEADS):
        acc = acc_ref[g]
        o_t = acc[0:HEAD_DIM, :] * (1.0 / acc[HEAD_DIM:HEAD_DIM + 1, :])
        if wide:
            for r in range(0, ATTN_REP, 2):
                hq = g * ATTN_REP + r
                pair = jnp.concatenate([o_t[:, r * nq:(r + 1) * nq],
                                        o_t[:, (r + 1) * nq:(r + 2) * nq]], axis=0)
                o_ref[0, :, hq * HEAD_DIM:(hq + 2) * HEAD_DIM] = pair.T.astype(BF16)
        else:
            o_rq = jnp.concatenate([o_t, o_t], axis=0).T[:, 0:HEAD_DIM].astype(BF16)
            for r in range(ATTN_REP):
                hq = g * ATTN_REP + r
                o_ref[0, :, hq * HEAD_DIM:(hq + 1) * HEAD_DIM] = o_rq[r * nq:(r + 1) * nq, :]


def _dsa(q, qi, w, kidx_all, k_all, v_all, past_len, l_valid, t):
    b = kidx_all.shape[0]
    lp = kidx_all.shape[1]
    nq = Q_BLOCK if t % Q_BLOCK == 0 else SMALL_Q_BLOCK
    assert t % nq == 0 and lp % (2 * KEY_TILE) == 0 and lp < 2 ** 15
    topk = min(TOPK_MAX, l_valid // 4)
    width = ATTN_REP * nq
    if nq == Q_BLOCK:
        qblk = lambda c: pl.BlockSpec((1, nq, c), lambda i, j: (i, j, 0))
        q_specs = [qblk(ATTN_WIDTH), qblk(IDX_HEADS * IDX_DIM), qblk(LANE)]
    else:
        assert t == nq
        q_specs = [pl.BlockSpec((1, ATTN_KV_HEADS, width, HEAD_DIM), lambda i, j: (i, 0, 0, 0)),
                   pl.BlockSpec((1, IDX_HEADS * nq, IDX_DIM), lambda i, j: (i, 0, 0)),
                   pl.BlockSpec((1, 1, LANE), lambda i, j: (i, 0, 0))]
    slot_onehot = jnp.asarray(
        (np.arange(width)[:, None] % nq == np.arange(LANE)[None, :]).astype(np.float32), BF16)
    per_stream = lambda c: pl.BlockSpec((1, lp, c), lambda i, j: (i, 0, 0))
    return pl.pallas_call(
        functools.partial(_dsa_kernel, past_len=past_len, l_valid=l_valid, topk=topk, nq=nq),
        out_shape=jax.ShapeDtypeStruct((b, t, ATTN_WIDTH), BF16),
        grid=(b, t // nq),
        in_specs=q_specs + [pl.BlockSpec((width, LANE), lambda i, j: (0, 0)),
                            per_stream(IDX_DIM), per_stream(LANE), per_stream(LANE)],
        out_specs=pl.BlockSpec((1, nq, ATTN_WIDTH), lambda i, j: (i, j, 0)),
        scratch_shapes=[pltpu.VMEM((width, IDX_DIM), BF16),
                        pltpu.VMEM((ATTN_KV_HEADS, width, 2 * LANE), BF16),
                        pltpu.VMEM((lp, LANE), I32),
                        pltpu.VMEM((lp, 2 * LANE), BF16),
                        pltpu.VMEM((ATTN_KV_HEADS, PV_ROWS, lp), BF16),
                        pltpu.VMEM((ATTN_KV_HEADS, 1, width), F32),
                        pltpu.VMEM((ATTN_KV_HEADS, PV_ROWS, width), F32),
                        pltpu.VMEM((1, LANE), I32),
                        pltpu.VMEM((ATTN_KV_HEADS, 2, KEY_TILE, width), F32)],
        compiler_params=pltpu.CompilerParams(
            dimension_semantics=("arbitrary", "arbitrary"), vmem_limit_bytes=VMEM_LIMIT),
        name="dsa",
    )(q, qi, w, slot_onehot, kidx_all, k_all, v_all)


def _merge_kernel(x_ref, yn_ref, o_ref, gate_ref, wa_ref, wb_ref, wo_ref, out_ref):
    br_a = jnp.dot(yn_ref[...], wa_ref[...], preferred_element_type=F32)
    br_b = jnp.dot(o_ref[...], wb_ref[...], preferred_element_type=F32)
    gates = gate_ref[...].astype(F32)
    merged = _sigmoid(gates[:, :D_MODEL]) * br_a + _sigmoid(gates[:, D_MODEL:]) * br_b
    out_ref[...] = x_ref[...] + jnp.dot(merged.astype(BF16), wo_ref[...],
                                        preferred_element_type=F32)


def _merge(x2d, yn, o, gate, w_ssm_out, w_attn_out, w_o, tm):
    n = x2d.shape[0]
    assert n % tm == 0
    row = lambda w: pl.BlockSpec((tm, w), lambda i: (i, 0))
    full = lambda r, c: pl.BlockSpec((r, c), lambda i: (0, 0))
    return pl.pallas_call(
        _merge_kernel,
        out_shape=jax.ShapeDtypeStruct((n, D_MODEL), F32),
        grid=(n // tm,),
        in_specs=[row(D_MODEL), row(SSM_INNER), row(ATTN_WIDTH), row(2 * D_MODEL),
                  full(SSM_INNER, D_MODEL), full(ATTN_WIDTH, D_MODEL), full(D_MODEL, D_MODEL)],
        out_specs=row(D_MODEL),
        compiler_params=pltpu.CompilerParams(
            dimension_semantics=("arbitrary",), vmem_limit_bytes=VMEM_LIMIT),
        name="merge",
    )(x2d, yn, o, gate, w_ssm_out.astype(BF16), w_attn_out.astype(BF16), w_o.astype(BF16))


def _moe_kernel(x_ref, ng_ref, wr_ref, br_ref, w1_ref, w3_ref, w2_ref, fg_ref, tri_ref, out_ref,
                hs_ref, combs_ref, acc_ref, pt_ref, seg_ref, *, n_steps, blk_rows):
    e_step = pl.program_id(1)
    tm = x_ref.shape[0]
    sr = hs_ref.shape[0]
    BLK = blk_rows

    @pl.when(e_step == 0)
    def _():
        x = x_ref[...]
        ms = jnp.mean(x * x, axis=-1, keepdims=True)
        h = ((x * lax.rsqrt(ms + EPS)) * ng_ref[...]).astype(BF16)
        logits = jnp.dot(h, wr_ref[...], preferred_element_type=F32) + br_ref[...]
        lane_i = lax.broadcasted_iota(I32, (tm, LANE), 1)
        lane = lane_i.astype(F32)
        first = lambda hit: jnp.min(jnp.where(hit, lane, float(LANE)), axis=1, keepdims=True)
        is_g = (lane_i >= N_EXPERTS) & (lane_i < N_EXPERTS + MOE_GROUPS)
        glog = jnp.where(is_g, logits, -jnp.inf)
        gmax = jnp.max(glog, axis=1, keepdims=True)
        gsel = first(glog == gmax) - float(N_EXPERTS)
        p_group = 1.0 / jnp.sum(jnp.exp(glog - gmax), axis=1, keepdims=True)
        grp = (lane_i >> EXPERT_GROUP_SHIFT).astype(F32)
        in_grp = (lane_i < N_EXPERTS) & (grp == gsel)
        el = jnp.where(in_grp, logits, -jnp.inf)
        v1 = jnp.max(el, axis=1, keepdims=True)
        i1 = first(el == v1)
        el2 = jnp.where(lane == i1, -jnp.inf, el)
        v2 = jnp.max(el2, axis=1, keepdims=True)
        i2 = first(el2 == v2)
        e21 = jnp.exp(v2 - v1)
        den = 1.0 + e21
        comb = (jnp.where(lane == i1, (1.0 / den) * p_group, 0.0)
                + jnp.where(lane == i2, (e21 / den) * p_group, 0.0))

        own = lane == gsel
        own_f = jnp.where(own, 1.0, 0.0)
        before = jnp.dot(tri_ref[...], own_f.astype(BF16), preferred_element_type=F32)
        rank = jnp.sum(jnp.where(own, before, 0.0), axis=1, keepdims=True)
        counts = jnp.sum(own_f, axis=0, keepdims=True)
        blocks = jnp.floor((counts + (BLK - 1)) * (1.0 / BLK))
        upper = (lax.broadcasted_iota(I32, (LANE, LANE), 0)
                 < lax.broadcasted_iota(I32, (LANE, LANE), 1))
        starts = jnp.dot(jnp.broadcast_to(blocks, (8, LANE)).astype(BF16),
                         jnp.where(upper, 1.0, 0.0).astype(BF16),
                         preferred_element_type=F32)[0:1, :] * BLK
        dest = jnp.sum(jnp.where(own, starts, 0.0), axis=1, keepdims=True) + rank
        for g in range(MOE_GROUPS):
            seg_ref[g] = jnp.sum(starts[:, g:g + 1]).astype(I32)
            seg_ref[MOE_GROUPS + g] = jnp.sum(blocks[:, g:g + 1]).astype(I32)

        dest_i = dest.astype(I32)
        pt = jnp.where(dest_i == lax.broadcasted_iota(I32, (tm, sr), 1), 1.0, 0.0).astype(BF16)
        pt_ref[...] = pt
        dest_row = jnp.broadcast_to(dest, (tm, LANE)).T[0:1, :].astype(I32)
        p_mat = jnp.where(lax.broadcasted_iota(I32, (sr, tm), 0) == dest_row, 1.0, 0.0
                          ).astype(BF16)
        hs_ref[...] = jnp.dot(p_mat, h, preferred_element_type=F32).astype(BF16)
        c3 = jnp.dot(p_mat, _split3(comb), preferred_element_type=F32)
        combs_ref[...] = c3[:, 0:LANE] + c3[:, LANE:2 * LANE] + c3[:, 2 * LANE:]
        acc_ref[...] = jnp.zeros(acc_ref.shape, F32)

    group = e_step // (EXPERTS_PER_GROUP // EXPERTS_PER_STEP)
    seg_start = seg_ref[group]

    def expert_rows(first_blk, n_rows):
        rows = pl.ds(pl.multiple_of(seg_start + first_blk * BLK, BLK), n_rows)
        h = hs_ref[rows, :]
        comb = combs_ref[rows, :]
        lane = lax.broadcasted_iota(I32, (n_rows, LANE), 1)
        for k in range(EXPERTS_PER_STEP):
            e = e_step * EXPERTS_PER_STEP + k
            a1 = jnp.dot(h, w1_ref[k], preferred_element_type=F32)
            a3 = jnp.dot(h, w3_ref[k], preferred_element_type=F32)
            he = (_silu(a1) * a3).astype(BF16)
            ye = jnp.dot(he, w2_ref[k], preferred_element_type=F32)
            wcol = jnp.sum(jnp.where(lane == e, comb, 0.0), axis=1, keepdims=True)
            acc_ref[rows, :] += wcol * ye

    n_blocks = seg_ref[MOE_GROUPS + group]

    def block_pair(u, carry):
        expert_rows(2 * u, 2 * BLK)
        return carry

    lax.fori_loop(0, n_blocks // 2, block_pair, 0)

    @pl.when(n_blocks % 2 == 1)
    def _():
        expert_rows(n_blocks - 1, BLK)

    @pl.when(e_step == n_steps - 1)
    def _():
        y = x_ref[...] + jnp.dot(pt_ref[...], acc_ref[...].astype(BF16),
                                 preferred_element_type=F32)
        ms = jnp.mean(y * y, axis=-1, keepdims=True)
        out_ref[...] = (y * lax.rsqrt(ms + EPS)) * fg_ref[...]


def _moe(x2d, norm_g, w_router_group, b_router_group, w_router_expert, b_router_expert,
         w1, w3, w2, norm_final_g, tm):
    n = x2d.shape[0]
    blk_rows = min(MOE_ROW_BLOCK, tm)
    assert n % tm == 0 and tm % blk_rows == 0
    n_steps = N_EXPERTS // EXPERTS_PER_STEP
    sr = tm + (MOE_GROUPS - 1) * blk_rows
    padw = LANE - N_EXPERTS - MOE_GROUPS
    wr = jnp.concatenate([w_router_expert, w_router_group,
                          jnp.zeros((D_MODEL, padw), F32)], axis=1).astype(BF16)
    br = jnp.concatenate([b_router_expert, b_router_group,
                          jnp.zeros((padw,), F32)]).astype(F32).reshape(1, LANE)
    tri = jnp.asarray(np.tril(np.ones((tm, tm), np.float32), -1), BF16)
    row = pl.BlockSpec((tm, D_MODEL), lambda i, e: (i, 0))
    const = lambda r, c: pl.BlockSpec((r, c), lambda i, e: (0, 0))
    return pl.pallas_call(
        functools.partial(_moe_kernel, n_steps=n_steps, blk_rows=blk_rows),
        out_shape=jax.ShapeDtypeStruct((n, D_MODEL), F32),
        grid=(n // tm, n_steps),
        in_specs=[row, const(1, D_MODEL), const(D_MODEL, LANE), const(1, LANE),
                  pl.BlockSpec((EXPERTS_PER_STEP, D_MODEL, EXPERT_FF), lambda i, e: (e, 0, 0)),
                  pl.BlockSpec((EXPERTS_PER_STEP, D_MODEL, EXPERT_FF), lambda i, e: (e, 0, 0)),
                  pl.BlockSpec((EXPERTS_PER_STEP, EXPERT_FF, D_MODEL), lambda i, e: (e, 0, 0)),
                  const(1, D_MODEL), const(tm, tm)],
        out_specs=row,
        scratch_shapes=[pltpu.VMEM((sr, D_MODEL), BF16), pltpu.VMEM((sr, LANE), F32),
                        pltpu.VMEM((sr, D_MODEL), F32), pltpu.VMEM((tm, sr), BF16),
                        pltpu.SMEM((2 * MOE_GROUPS,), I32)],
        compiler_params=pltpu.CompilerParams(
            dimension_semantics=("arbitrary", "arbitrary"), vmem_limit_bytes=VMEM_LIMIT),
        name="moe",
    )(x2d, norm_g.reshape(1, D_MODEL), wr, br, w1.astype(BF16), w3.astype(BF16),
      w2.astype(BF16), norm_final_g.reshape(1, D_MODEL), tri)


def _token_tile(n):
    for tm in (1024, 512, 256, 128, 64, 32, 16, 8):
        if n % tm == 0:
            return tm
    raise ValueError(f"token count {n} is not a multiple of 8")


def _pad_axis(a, axis, size):
    if a.shape[axis] == size:
        return a
    widths = [(0, 0)] * a.ndim
    widths[axis] = (0, size - a.shape[axis])
    return jnp.pad(a, widths)


def _round_up(n, m):
    return (n + m - 1) // m * m


def _stream_step(x, conv_state, ssm_state, k_past, v_past, kidx_past, w_packed, p):
    b, t, d = x.shape
    n = b * t
    tm = _token_tile(n)
    x2d = x.reshape(n, d)
    z, xbc, q, gate, k, v, qi, ki, dtw, k_b, v_b, ki_b = _in_proj(
        x2d, p["norm_mix_g"], w_packed, tm)

    yn, conv_new, ssm_new = _ssd(
        xbc.reshape(b, t, CONV_DIM), z.reshape(b, t, SSM_INNER), dtw.reshape(b, t, LANE),
        conv_state, ssm_state, p["conv_w"], p["conv_b"], p["dt_bias"], p["A_log"],
        p["D_skip"], p["ssm_norm_g"])

    kv_w = ATTN_KV_HEADS * HEAD_DIM
    k_all, v_all, kidx_all = (k_b.reshape(b, t, kv_w), v_b.reshape(b, t, kv_w),
                              ki_b.reshape(b, t, IDX_DIM))
    past_len = 0
    if k_past is not None:
        past_len = k_past.shape[1]
        k_all = jnp.concatenate([k_past.astype(BF16).reshape(b, past_len, kv_w), k_all], axis=1)
        v_all = jnp.concatenate([v_past.astype(BF16).reshape(b, past_len, kv_w), v_all], axis=1)
        kidx_all = jnp.concatenate([kidx_past.astype(BF16), kidx_all], axis=1)
    l_valid = past_len + t
    lp = _round_up(l_valid, 2 * KEY_TILE)
    if t % Q_BLOCK == 0:
        q_in = (q.reshape(b, t, ATTN_WIDTH), qi.reshape(b, t, IDX_HEADS * IDX_DIM),
                dtw.reshape(b, t, LANE))
    else:
        q_in = (q.reshape(b, t, ATTN_KV_HEADS, ATTN_REP, HEAD_DIM).transpose(0, 2, 3, 1, 4)
                .reshape(b, ATTN_KV_HEADS, ATTN_REP * t, HEAD_DIM),
                qi.reshape(b, t, IDX_HEADS, IDX_DIM).transpose(0, 2, 1, 3)
                .reshape(b, IDX_HEADS * t, IDX_DIM),
                dtw.reshape(b, t, LANE)[:, :, SSM_HEADS:SSM_HEADS + IDX_HEADS]
                .transpose(0, 2, 1).reshape(b, 1, IDX_HEADS * t))
    o = _dsa(*q_in, _pad_axis(kidx_all, 1, lp), _pad_axis(k_all, 1, lp), _pad_axis(v_all, 1, lp),
             past_len, l_valid, t)
    o = o.reshape(n, ATTN_WIDTH)

    x1 = _merge(x2d, yn.reshape(n, SSM_INNER), o, gate, p["w_ssm_out"], p["w_attn_out"],
                p["w_o"], min(tm, MERGE_TILE))
    y = _moe(x1, p["norm_moe_g"], p["w_router_group"], p["b_router_group"],
             p["w_router_expert"], p["b_router_expert"], p["w1"], p["w3"], p["w2"],
             p["norm_final_g"], tm)
    return (y.reshape(b, t, d), k.reshape(b, t, ATTN_KV_HEADS, HEAD_DIM),
            v.reshape(b, t, ATTN_KV_HEADS, HEAD_DIM), ki.reshape(b, t, IDX_DIM), conv_new, ssm_new)


def kernel(x_prompt, x_sample, cache_k, cache_v, cache_kidx, state_conv, state_ssm, norm_mix_g, w_in, conv_w, conv_b, dt_bias, A_log, D_skip, ssm_norm_g, w_ssm_out, w_attn_out, w_o, norm_moe_g, w_router_group, b_router_group, w_router_expert, b_router_expert, w1, w3, w2, norm_final_g):
    p = dict(norm_mix_g=norm_mix_g, conv_w=conv_w, conv_b=conv_b, dt_bias=dt_bias, A_log=A_log,
             D_skip=D_skip, ssm_norm_g=ssm_norm_g, w_ssm_out=w_ssm_out, w_attn_out=w_attn_out,
             w_o=w_o, norm_moe_g=norm_moe_g, w_router_group=w_router_group,
             b_router_group=b_router_group, w_router_expert=w_router_expert,
             b_router_expert=b_router_expert, w1=w1, w3=w3, w2=w2, norm_final_g=norm_final_g)
    w_packed = _pack_w_in(w_in)
    yp, kp, vp, kip, cp, sp = _stream_step(x_prompt, None, None, None, None, None, w_packed, p)
    ys, ksn, vsn, kisn, csn, ssn = _stream_step(x_sample, state_conv, state_ssm, cache_k,
                                                cache_v, cache_kidx, w_packed, p)
    return (yp, ys, kp, vp, kip, cp, sp, ksn, vsn, kisn, csn, ssn)
```

```python
import functools

import numpy as np
import jax
import jax.numpy as jnp
from jax import lax
from jax.experimental import pallas as pl
from jax.experimental.pallas import tpu as pltpu

F32 = jnp.float32
BF16 = jnp.bfloat16
I32 = jnp.int32

D_MODEL = 1024
CHUNK = 64
CHUNK_SHIFT = 6
EPS = 1e-6
SSM_INNER = 2048
SSM_HEADDIM = 64
SSM_HEADS = 32
SSM_GROUPS = 4
SSM_HEADS_PER_GROUP = SSM_HEADS // SSM_GROUPS
SSM_STATE = 128
CONV_WIDTH = 4
CONV_DIM = SSM_INNER + 2 * SSM_GROUPS * SSM_STATE
CONV_CARRY = 16
SSD_CHUNKS_PER_STEP = 2
ATTN_HEADS = 16
ATTN_KV_HEADS = 2
HEAD_DIM = 64
ATTN_REP = ATTN_HEADS // ATTN_KV_HEADS
ATTN_WIDTH = ATTN_HEADS * HEAD_DIM
IDX_HEADS = 8
IDX_DIM = 64
IDX_SCALE = (IDX_HEADS * IDX_DIM) ** -0.5
TOPK_MAX = 256
MOE_GROUPS = 4
EXPERTS_PER_GROUP = 8
EXPERT_GROUP_SHIFT = 3
N_EXPERTS = 32
EXPERT_FF = 256
N_BRANCHES = 2

LANE = 128
VMEM_LIMIT = 56 * 1024 * 1024
PROJ_TILE = 1024
PROJ_TILES = 9
Q_BLOCK = 128
SMALL_Q_BLOCK = 16
KEY_TILE = 256
EXPERTS_PER_STEP = 4
MOE_ROW_BLOCK = 128
MERGE_TILE = 512
INT_MIN = np.int32(-2 ** 31)
NEG_BIG = -1e30
Q_SCALE = HEAD_DIM ** -0.5 * float(np.log2(np.e))
PV_ROWS = 80

NT_DIMS = (((1,), (1,)), ((), ()))
TN_DIMS = (((0,), (0,)), ((), ()))


def _sigmoid(x):
    return 1.0 / (1.0 + jnp.exp(-x))


def _silu(x):
    h = 0.5 * x
    return h + h * jnp.tanh(h)


def _split3(x):
    hi = x.astype(BF16)
    r1 = x - hi.astype(F32)
    mid = r1.astype(BF16)
    r2 = r1 - mid.astype(F32)
    lo = r2.astype(BF16)
    return jnp.concatenate([hi, mid, lo], axis=1)


def _inproj_kernel(x_ref, g_ref, w_ref, z_ref, xbc_ref, q_ref, gate_ref, k_ref, v_ref,
                   qi_ref, ki_ref, dtw_ref, kb_ref, vb_ref, kib_ref, h_ref):
    j = pl.program_id(1)

    @pl.when(j == 0)
    def _():
        x = x_ref[...]
        ms = jnp.mean(x * x, axis=-1, keepdims=True)
        h_ref[...] = ((x * lax.rsqrt(ms + EPS)) * g_ref[...]).astype(BF16)

    def tile():
        return jnp.dot(h_ref[...], w_ref[...], preferred_element_type=F32)

    @pl.when(j < 2)
    def _():
        z_ref[...] = tile().astype(BF16)

    @pl.when((j >= 2) & (j < 5))
    def _():
        xbc_ref[...] = tile().astype(BF16)

    @pl.when(j == 5)
    def _():
        q_ref[...] = (tile() * Q_SCALE).astype(BF16)

    @pl.when((j >= 6) & (j < 8))
    def _():
        gate_ref[...] = tile().astype(BF16)

    @pl.when(j == 8)
    def _():
        t = tile()
        k_ref[...] = t[:, 0:128]
        v_ref[...] = t[:, 128:256]
        qi_ref[...] = t[:, 256:768].astype(BF16)
        ki_ref[...] = t[:, 768:832]
        dtw_ref[...] = t[:, 896:1024]
        kb_ref[...] = t[:, 0:128].astype(BF16)
        vb_ref[...] = t[:, 128:256].astype(BF16)
        kib_ref[...] = t[:, 768:832].astype(BF16)


def _pack_w_in(w_in):
    sizes = (SSM_INNER, CONV_DIM, SSM_HEADS, ATTN_WIDTH, ATTN_KV_HEADS * HEAD_DIM,
             ATTN_KV_HEADS * HEAD_DIM, IDX_HEADS * IDX_DIM, IDX_DIM, IDX_HEADS,
             N_BRANCHES * D_MODEL)
    offs = np.concatenate([[0], np.cumsum(sizes)])
    z, xbc, dt, q, k, v, qi, ki, wi, g = [w_in[:, offs[i]:offs[i + 1]] for i in range(10)]
    zeros = lambda n: jnp.zeros((D_MODEL, n), w_in.dtype)
    cols = [z, xbc, q, g, k, v, qi, ki, zeros(64), dt, wi, zeros(LANE - SSM_HEADS - IDX_HEADS)]
    return jnp.concatenate(cols, axis=1).astype(BF16)


def _in_proj(x2d, norm_g, w_packed, tm):
    n = x2d.shape[0]
    assert n % tm == 0
    bf = lambda c: jax.ShapeDtypeStruct((n, c), BF16)
    ff = lambda c: jax.ShapeDtypeStruct((n, c), F32)
    clip = lambda j, lo, hi: jnp.clip(j - lo, 0, hi - lo - 1)
    return pl.pallas_call(
        _inproj_kernel,
        out_shape=(bf(SSM_INNER), bf(CONV_DIM), bf(ATTN_WIDTH), bf(2 * D_MODEL),
                   ff(128), ff(128), bf(512), ff(IDX_DIM), ff(128), bf(128), bf(128), bf(IDX_DIM)),
        grid=(n // tm, PROJ_TILES),
        in_specs=[
            pl.BlockSpec((tm, D_MODEL), lambda i, j: (i, 0)),
            pl.BlockSpec((1, D_MODEL), lambda i, j: (0, 0)),
            pl.BlockSpec((D_MODEL, PROJ_TILE), lambda i, j: (0, j)),
        ],
        out_specs=(
            pl.BlockSpec((tm, PROJ_TILE), lambda i, j: (i, clip(j, 0, 2))),
            pl.BlockSpec((tm, PROJ_TILE), lambda i, j: (i, clip(j, 2, 5))),
            pl.BlockSpec((tm, PROJ_TILE), lambda i, j: (i, 0)),
            pl.BlockSpec((tm, PROJ_TILE), lambda i, j: (i, clip(j, 6, 8))),
            pl.BlockSpec((tm, 128), lambda i, j: (i, 0)),
            pl.BlockSpec((tm, 128), lambda i, j: (i, 0)),
            pl.BlockSpec((tm, 512), lambda i, j: (i, 0)),
            pl.BlockSpec((tm, IDX_DIM), lambda i, j: (i, 0)),
            pl.BlockSpec((tm, 128), lambda i, j: (i, 0)),
            pl.BlockSpec((tm, 128), lambda i, j: (i, 0)),
            pl.BlockSpec((tm, 128), lambda i, j: (i, 0)),
            pl.BlockSpec((tm, IDX_DIM), lambda i, j: (i, 0)),
        ),
        scratch_shapes=[pltpu.VMEM((tm, D_MODEL), BF16)],
        compiler_params=pltpu.CompilerParams(
            dimension_semantics=("arbitrary", "arbitrary"), vmem_limit_bytes=VMEM_LIMIT),
        name="in_proj",
    )(x2d, norm_g.reshape(1, D_MODEL), w_packed)


def _ssd_kernel(*refs, L, cps, nsteps, has_state):
    if has_state:
        (xbc_ref, z_ref, dtw_ref, cs_ref, s0_ref, shift_ref, convw_ref, convb_ref, dtb_ref, a_ref,
         dx_ref, ng_ref, e3p_ref, e3l_ref, tri_ref, bd_ref, yn_ref, cnew_ref, sout_ref,
         xe_ref, st, act_ref, ex_ref, yz_ref, acl_ref) = refs
    else:
        (xbc_ref, z_ref, dtw_ref, shift_ref, convw_ref, convb_ref, dtb_ref, a_ref,
         dx_ref, ng_ref, e3p_ref, e3l_ref, tri_ref, bd_ref, yn_ref, cnew_ref, sout_ref,
         xe_ref, st, act_ref, ex_ref, yz_ref, acl_ref) = refs
    c = pl.program_id(1)
    G, R, N, P = SSM_GROUPS, SSM_HEADS_PER_GROUP, SSM_STATE, SSM_HEADDIM
    GW = R * P
    RL = R * L
    LB = cps * L

    @pl.when(c == 0)
    def _():
        xe_ref[LB:, :] = jnp.zeros((xe_ref.shape[0] - LB, CONV_DIM), BF16)
        if has_state:
            xe_ref[LB:LB + CONV_CARRY, :] = cs_ref[0]
            st[...] = s0_ref[0].T
        else:
            st[...] = jnp.zeros(st.shape, F32)

    xe_ref[0:LB, :] = xbc_ref[0]
    shifted = jnp.dot(shift_ref[0], xe_ref[...], preferred_element_type=F32)
    w = convw_ref[...]
    conv = convb_ref[...] + shifted[2 * LB:3 * LB, :] * w[0:1, :]
    conv = conv + shifted[LB:2 * LB, :] * w[1:2, :]
    conv = conv + shifted[0:LB, :] * w[2:3, :]
    conv = conv + xbc_ref[0].astype(F32) * w[3:4, :]
    act_ref[...] = _silu(conv)

    @pl.when(c == nsteps - 1)
    def _():
        cnew_ref[0] = xbc_ref[0, LB - CONV_CARRY:LB, :].astype(F32)[
            CONV_CARRY - (CONV_WIDTH - 1):CONV_CARRY, :]

    if nsteps > 1:
        xe_ref[LB:LB + CONV_CARRY, :] = xe_ref[LB - CONV_CARRY:LB, :]

    u = dtw_ref[0] + dtb_ref[...]
    dt = jnp.maximum(u, 0.0) + jnp.log1p(jnp.exp(-jnp.abs(u)))
    a = dt * a_ref[...]
    ac3 = jnp.dot(tri_ref[...], _split3(a), preferred_element_type=F32)
    a_cum = ac3[:, 0:LANE] + ac3[:, LANE:2 * LANE] + ac3[:, 2 * LANE:]

    ex_ref[...] = jnp.dot(_split3(jnp.concatenate([a_cum, dt], axis=0)), e3p_ref[...],
                          preferred_element_type=F32)
    if L != P:
        acl_ref[...] = jnp.dot(_split3(a_cum), e3l_ref[...], preferred_element_type=F32)
    row = lax.broadcasted_iota(I32, (L, RL), 0)
    scol = lax.broadcasted_iota(I32, (L, RL), 1) & (L - 1)

    for sub in range(cps):
        r0 = sub * L
        rows = slice(r0, r0 + L)
        ssq = jnp.zeros((L, 1), F32)
        for g in range(G):
            sl = slice(g * GW, (g + 1) * GW)
            xs = act_ref[rows, sl]
            b_g = act_ref[rows, SSM_INNER + g * N:SSM_INNER + (g + 1) * N].astype(BF16)
            c_g = act_ref[rows, SSM_INNER + (G + g) * N:SSM_INNER + (G + g + 1) * N].astype(BF16)
            acx = ex_ref[rows, sl]
            alast = ex_ref[r0 + L - 1:r0 + L, sl]
            xdt = xs * ex_ref[LB + r0:LB + r0 + L, sl]
            acl_g = acx if L == P else acl_ref[rows, g * RL:(g + 1) * RL]
            acs_g = jnp.sum(jnp.where(scol == row, acl_g, 0.0), axis=0, keepdims=True)
            decay = jnp.where(scol <= row, jnp.exp(acl_g - acs_g), 0.0)
            cb = lax.dot_general(c_g, jnp.concatenate([b_g] * R, axis=0), NT_DIMS,
                                 preferred_element_type=F32)
            m_g = (cb * decay).astype(BF16)
            xbd = jnp.concatenate([xdt.astype(BF16)] * R, axis=0) * bd_ref[...]
            y_in = jnp.dot(m_g, xbd, preferred_element_type=F32)
            st_g = st[:, sl]
            y_st = jnp.dot(c_g, st_g.astype(BF16), preferred_element_type=F32)
            xw = (xdt * jnp.exp(alast - acx)).astype(BF16)
            upd = lax.dot_general(b_g, xw, TN_DIMS, preferred_element_type=F32)
            st[:, sl] = st_g * jnp.exp(alast) + upd
            y = y_in + jnp.exp(acx) * y_st + dx_ref[:, sl] * xs
            yz = y * _silu(z_ref[0, rows, sl].astype(F32))
            ssq = ssq + jnp.sum(yz * yz, axis=-1, keepdims=True)
            yz_ref[rows, sl] = yz
        scale = lax.rsqrt(ssq * (1.0 / SSM_INNER) + EPS)
        yn_ref[0, rows, :] = ((yz_ref[rows, :] * scale) * ng_ref[...]).astype(BF16)

    @pl.when(c == nsteps - 1)
    def _():
        sout_ref[0] = st[...].T


def _shift_matrices(L, conv_rows):
    out = np.zeros((2, 3 * L, conv_rows), np.float32)
    for k in range(CONV_WIDTH - 1):
        for t in range(L):
            src = t - (k + 1)
            if src >= 0:
                out[:, k * L + t, src] = 1.0
            else:
                out[1, k * L + t, L + CONV_CARRY + src] = 1.0
                j = CONV_WIDTH - 1 + src
                for piece in range(3):
                    out[0, k * L + t, L + 3 * piece + j] = 1.0
    return jnp.asarray(out, BF16)


def _expand_matrix(width_per_head):
    r = np.arange(3 * LANE)[:, None] % LANE
    c = np.arange(SSM_HEADS * width_per_head)[None, :] // width_per_head
    return jnp.asarray((r == c).astype(np.float32), BF16)


def _ssd(xbc, z, dtw, conv_state, ssm_state, conv_w, conv_b, dt_bias, a_log, d_skip, ssm_norm_g):
    b, t, _ = xbc.shape
    L = min(CHUNK, t)
    nc = t // L
    assert t % L == 0 and L & (L - 1) == 0
    has_state = conv_state is not None
    pad = lambda v: jnp.pad(v.astype(F32), (0, LANE - SSM_HEADS)).reshape(1, LANE)
    a_neg = pad(-jnp.exp(a_log.astype(F32)))
    cps = SSD_CHUNKS_PER_STEP if nc % SSD_CHUNKS_PER_STEP == 0 else 1
    nsteps = nc // cps
    lb = cps * L
    conv_rows = _round_up(lb + CONV_CARRY, LANE)
    tri = jnp.asarray(np.kron(np.eye(cps), np.tril(np.ones((L, L)))).astype(np.float32), BF16)
    rl = SSM_HEADS_PER_GROUP * L
    gw = SSM_HEADS_PER_GROUP * SSM_HEADDIM
    bd = jnp.asarray((np.arange(rl)[:, None] // L == np.arange(gw)[None, :] // SSM_HEADDIM)
                     .astype(np.float32), BF16)
    const = lambda shape: pl.BlockSpec(shape, lambda i, j: (0,) * len(shape))
    tok = lambda w: pl.BlockSpec((1, lb, w), lambda i, j: (i, j, 0))
    per_b = lambda s: pl.BlockSpec((1,) + s, lambda i, j: (i, 0, 0))
    ins = [xbc, z, dtw]
    specs = [tok(CONV_DIM), tok(SSM_INNER), tok(LANE)]
    if has_state:
        triples = _split3(conv_state.astype(F32))
        ins += [_pad_axis(triples, 1, CONV_CARRY), ssm_state.reshape(b, SSM_INNER, SSM_STATE)]
        specs += [per_b((CONV_CARRY, CONV_DIM)), per_b((SSM_INNER, SSM_STATE))]
    first = 0 if has_state else 1
    ins += [_shift_matrices(lb, conv_rows)]
    specs += [pl.BlockSpec((1, 3 * lb, conv_rows),
                           lambda i, j: (jnp.where(j == 0, first, 1), 0, 0))]
    ins += [conv_w, conv_b.reshape(1, CONV_DIM), pad(dt_bias), a_neg,
            jnp.repeat(d_skip.astype(F32), SSM_HEADDIM).reshape(1, SSM_INNER),
            ssm_norm_g.reshape(1, SSM_INNER), _expand_matrix(SSM_HEADDIM), _expand_matrix(L),
            tri, bd]
    specs += [const((CONV_WIDTH, CONV_DIM)), const((1, CONV_DIM)), const((1, LANE)),
              const((1, LANE)), const((1, SSM_INNER)), const((1, SSM_INNER)),
              const((3 * LANE, SSM_INNER)), const((3 * LANE, SSM_HEADS * L)),
              const((lb, lb)), const((rl, gw))]
    yn, cnew, sout = pl.pallas_call(
        functools.partial(_ssd_kernel, L=L, cps=cps, nsteps=nsteps, has_state=has_state),
        out_shape=(jax.ShapeDtypeStruct((b, t, SSM_INNER), BF16),
                   jax.ShapeDtypeStruct((b, CONV_WIDTH - 1, CONV_DIM), F32),
                   jax.ShapeDtypeStruct((b, SSM_INNER, SSM_STATE), F32)),
        grid=(b, nsteps),
        in_specs=specs,
        out_specs=(tok(SSM_INNER), per_b((CONV_WIDTH - 1, CONV_DIM)),
                   per_b((SSM_INNER, SSM_STATE))),
        scratch_shapes=[pltpu.VMEM((conv_rows, CONV_DIM), BF16),
                        pltpu.VMEM((SSM_STATE, SSM_INNER), F32),
                        pltpu.VMEM((lb, CONV_DIM), F32),
                        pltpu.VMEM((2 * lb, SSM_INNER), F32),
                        pltpu.VMEM((lb, SSM_INNER), F32),
                        pltpu.VMEM((lb, SSM_HEADS * L), F32)],
        compiler_params=pltpu.CompilerParams(
            dimension_semantics=("arbitrary", "arbitrary"), vmem_limit_bytes=VMEM_LIMIT),
        name="ssd",
    )(*ins)
    return yn, cnew, sout.reshape(b, SSM_HEADS, SSM_HEADDIM, SSM_STATE)


def _dsa_kernel(q_ref, qi_ref, w_ref, eq_ref, kidx_ref, k_ref, v_ref, o_ref,
                qs_ref, qx_ref, key_ref, kx_ref, vx_ref, m_ref, acc_ref, mlim_ref, s_ref,
                *, past_len, l_valid, topk, nq):
    qb = pl.program_id(1)
    TQ, TK = LANE, KEY_TILE
    W = ATTN_REP * nq
    wide = nq == LANE

    lane = lax.broadcasted_iota(I32, (1, TQ), 1)
    pos = past_len + qb * nq + (lane & (nq - 1))
    n_adm = jnp.minimum(((pos >> CHUNK_SHIFT) + 1) * CHUNK, l_valid)
    k_eff = jnp.minimum(n_adm, topk)
    last_pos = past_len + qb * nq + nq - 1
    n_max = jnp.minimum(((last_pos >> CHUNK_SHIFT) + 1) * CHUNK, l_valid)
    nt = (n_max + TK - 1) // TK
    ntp = 2 * ((nt + 1) // 2)
    lp = key_ref.shape[0]

    @pl.when(qb == 0)
    def _():
        kx_ref[:, LANE:] = k_ref[0]

        def v_tile(t, carry):
            s0 = pl.multiple_of(t * TK, TK)
            v_t = v_ref[0, pl.ds(s0, TK), :].astype(F32).T
            for g in range(ATTN_KV_HEADS):
                vx_ref[g, 0:HEAD_DIM, pl.ds(s0, TK)] = (
                    v_t[g * HEAD_DIM:(g + 1) * HEAD_DIM, :].astype(BF16))
            return carry

        lax.fori_loop(0, lp // TK, v_tile, 0)
        tail_row = lax.broadcasted_iota(I32, (PV_ROWS - HEAD_DIM, lp), 0)
        for g in range(ATTN_KV_HEADS):
            vx_ref[g, HEAD_DIM:, :] = jnp.where(tail_row == 0, 1.0, 0.0).astype(BF16)
            qx_ref[g, :, 0:LANE] = eq_ref[...]
            qx_ref[g, :, LANE:] = jnp.zeros((W, LANE), BF16)

    if wide:
        for h in range(IDX_HEADS):
            qs_ref[h * nq:(h + 1) * nq, :] = qi_ref[0, :, h * IDX_DIM:(h + 1) * IDX_DIM]
        w_t = w_ref[0].T[SSM_HEADS:SSM_HEADS + IDX_HEADS, :] * IDX_SCALE
    else:
        qs_ref[...] = qi_ref[0]
        w_row = w_ref[0] * IDX_SCALE

    def index_dots(t, slot):
        s0 = pl.multiple_of(jnp.minimum(t, ntp - 1) * TK, TK)
        s_ref[0, slot] = lax.dot_general(kidx_ref[0, pl.ds(s0, TK), :], qs_ref[...], NT_DIMS,
                                         preferred_element_type=F32)

    def score_tile(t, slot):
        s0 = pl.multiple_of(t * TK, TK)
        d = s_ref[0, slot]
        if wide:
            sc = w_t[0:1, :] * jnp.maximum(d[:, 0:nq], 0.0)
            for h in range(1, IDX_HEADS):
                sc = sc + w_t[h:h + 1, :] * jnp.maximum(d[:, h * nq:(h + 1) * nq], 0.0)
        else:
            sc = w_row * jnp.maximum(d, 0.0)
            for shift in (64, 32, 16):
                sc = sc + pltpu.roll(sc, shift, axis=1)
        sc = sc + 0.0
        bits = pltpu.bitcast(sc, I32)
        key = jnp.where(bits < 0, bits ^ np.int32(0x7FFFFFFF), bits)
        s_idx = s0 + lax.broadcasted_iota(I32, (TK, TQ), 0)
        key_ref[pl.ds(s0, TK), :] = jnp.where(s_idx < n_adm, key, INT_MIN)

    index_dots(0, 0)

    def score_pair(u, carry):
        index_dots(2 * u + 1, 1)
        score_tile(2 * u, 0)
        index_dots(2 * u + 2, 0)
        score_tile(2 * u + 1, 1)
        return carry

    lax.fori_loop(0, ntp // 2, score_pair, 0)

    def count(pred):
        def body(t, acc):
            s0 = pl.multiple_of(t * (2 * TK), 2 * TK)
            blk = key_ref[pl.ds(s0, 2 * TK), :]
            s_idx = s0 + lax.broadcasted_iota(I32, (2 * TK, TQ), 0)
            hit = jnp.where(pred(blk, s_idx), 1, 0).astype(I32)
            return acc + jnp.sum(hit.reshape(2 * TK // 8, 8, TQ), axis=0)
        acc = lax.fori_loop(0, ntp // 2, body, jnp.zeros((8, TQ), I32))
        return jnp.sum(acc, axis=0, keepdims=True)

    def bit_step(i, prefix):
        cand = prefix | jnp.left_shift(jnp.int32(1), 31 - i)
        cand_s = cand ^ INT_MIN
        cnt = count(lambda blk, s_idx: blk >= cand_s)
        return jnp.where(cnt >= k_eff, cand, prefix)

    thr = lax.fori_loop(0, 32, bit_step, jnp.zeros((1, TQ), I32)) ^ INT_MIN

    n_gt = count(lambda blk, s_idx: blk > thr)
    n_eq = count(lambda blk, s_idx: blk == thr)
    need = k_eff - n_gt
    mlim_ref[...] = jnp.full((1, TQ), 2 ** 30, I32)

    @pl.when(jnp.max(n_eq - need) > 0)
    def _():
        def idx_step(i, prefix):
            cand = prefix | jnp.left_shift(jnp.int32(1), 14 - i)
            cnt = count(lambda blk, s_idx: (blk == thr) & (s_idx < cand))
            return jnp.where(cnt < need, cand, prefix)
        mlim_ref[...] = lax.fori_loop(0, 15, idx_step, jnp.zeros((1, TQ), I32))

    mlim = mlim_ref[...]

    def bias_tile(t, carry):
        s0 = pl.multiple_of(t * TK, TK)
        blk = key_ref[pl.ds(s0, TK), :]
        s_idx = s0 + lax.broadcasted_iota(I32, (TK, TQ), 0)
        sel = (blk > thr) | ((blk == thr) & (s_idx <= mlim))
        kx_ref[pl.ds(s0, TK), 0:LANE] = jnp.where(sel, 0.0, NEG_BIG).astype(BF16)
        return carry

    lax.fori_loop(0, nt, bias_tile, 0)

    @pl.when(nt < ntp)
    def _():
        kx_ref[pl.ds(pl.multiple_of(nt * TK, TK), TK), 0:LANE] = jnp.full((TK, LANE), NEG_BIG,
                                                                           BF16)

    for g in range(ATTN_KV_HEADS):
        lanes = slice(LANE + g * HEAD_DIM, LANE + (g + 1) * HEAD_DIM)
        if wide:
            for r in range(ATTN_REP):
                hq = g * ATTN_REP + r
                qx_ref[g, r * nq:(r + 1) * nq, lanes] = (
                    q_ref[0, :, hq * HEAD_DIM:(hq + 1) * HEAD_DIM])
        else:
            qx_ref[g, :, lanes] = q_ref[0, g]
    m_ref[...] = jnp.full(m_ref.shape, -jnp.inf, F32)
    acc_ref[...] = jnp.zeros(acc_ref.shape, F32)

    def scores(t, slot):
        s0 = pl.multiple_of(jnp.minimum(t, ntp - 1) * TK, TK)
        k_tile = kx_ref[pl.ds(s0, TK), :]
        for g in range(ATTN_KV_HEADS):
            s_ref[g, slot] = lax.dot_general(k_tile, qx_ref[g], NT_DIMS,
                                             preferred_element_type=F32)

    def accumulate(t, slot):
        s0 = pl.multiple_of(t * TK, TK)
        for g in range(ATTN_KV_HEADS):
            s = s_ref[g, slot]
            m_prev = m_ref[g]
            m_new = jnp.maximum(m_prev, jnp.max(s, axis=0, keepdims=True))
            p = jnp.exp2(s - m_new)
            acc_ref[g] = jnp.exp2(m_prev - m_new) * acc_ref[g] + jnp.dot(
                vx_ref[g, :, pl.ds(s0, TK)], p.astype(BF16), preferred_element_type=F32)
            m_ref[g] = m_new

    scores(0, 0)

    def attn_pair(u, carry):
        scores(2 * u + 1, 1)
        accumulate(2 * u, 0)
        scores(2 * u + 2, 0)
        accumulate(2 * u + 1, 1)
        return carry

    lax.fori_loop(0, ntp // 2, attn_pair, 0)
    for g in range(ATTN_KV_HEADS):
        acc = acc_ref[g]
        o_t = acc[0:HEAD_DIM, :] * (1.0 / acc[HEAD_DIM:HEAD_DIM + 1, :])
        if wide:
            for r in range(0, ATTN_REP, 2):
                hq = g * ATTN_REP + r
                pair = jnp.concatenate([o_t[:, r * nq:(r + 1) * nq],
                                        o_t[:, (r + 1) * nq:(r + 2) * nq]], axis=0)
                o_ref[0, :, hq * HEAD_DIM:(hq + 2) * HEAD_DIM] = pair.T.astype(BF16)
        else:
            o_rq = jnp.concatenate([o_t, o_t], axis=0).T[:, 0:HEAD_DIM].astype(BF16)
            for r in range(ATTN_REP):
                hq = g * ATTN_REP + r
                o_ref[0, :, hq * HEAD_DIM:(hq + 1) * HEAD_DIM] = o_rq[r * nq:(r + 1) * nq, :]


def _dsa(q, qi, w, kidx_all, k_all, v_all, past_len, l_valid, t):
    b = kidx_all.shape[0]
    lp = kidx_all.shape[1]
    nq = Q_BLOCK if t % Q_BLOCK == 0 else SMALL_Q_BLOCK
    assert t % nq == 0 and lp % (2 * KEY_TILE) == 0 and lp < 2 ** 15
    topk = min(TOPK_MAX, l_valid // 4)
    width = ATTN_REP * nq
    if nq == Q_BLOCK:
        qblk = lambda c: pl.BlockSpec((1, nq, c), lambda i, j: (i, j, 0))
        q_specs = [qblk(ATTN_WIDTH), qblk(IDX_HEADS * IDX_DIM), qblk(LANE)]
    else:
        assert t == nq
        q_specs = [pl.BlockSpec((1, ATTN_KV_HEADS, width, HEAD_DIM), lambda i, j: (i, 0, 0, 0)),
                   pl.BlockSpec((1, IDX_HEADS * nq, IDX_DIM), lambda i, j: (i, 0, 0)),
                   pl.BlockSpec((1, 1, LANE), lambda i, j: (i, 0, 0))]
    slot_onehot = jnp.asarray(
        (np.arange(width)[:, None] % nq == np.arange(LANE)[None, :]).astype(np.float32), BF16)
    per_stream = lambda c: pl.BlockSpec((1, lp, c), lambda i, j: (i, 0, 0))
    return pl.pallas_call(
        functools.partial(_dsa_kernel, past_len=past_len, l_valid=l_valid, topk=topk, nq=nq),
        out_shape=jax.ShapeDtypeStruct((b, t, ATTN_WIDTH), BF16),
        grid=(b, t // nq),
        in_specs=q_specs + [pl.BlockSpec((width, LANE), lambda i, j: (0, 0)),
                            per_stream(IDX_DIM), per_stream(LANE), per_stream(LANE)],
        out_specs=pl.BlockSpec((1, nq, ATTN_WIDTH), lambda i, j: (i, j, 0)),
        scratch_shapes=[pltpu.VMEM((width, IDX_DIM), BF16),
                        pltpu.VMEM((ATTN_KV_HEADS, width, 2 * LANE), BF16),
                        pltpu.VMEM((lp, LANE), I32),
                        pltpu.VMEM((lp, 2 * LANE), BF16),
                        pltpu.VMEM((ATTN_KV_HEADS, PV_ROWS, lp), BF16),
                        pltpu.VMEM((ATTN_KV_HEADS, 1, width), F32),
                        pltpu.VMEM((ATTN_KV_HEADS, PV_ROWS, width), F32),
                        pltpu.VMEM((1, LANE), I32),
                        pltpu.VMEM((ATTN_KV_HEADS, 2, KEY_TILE, width), F32)],
        compiler_params=pltpu.CompilerParams(
            dimension_semantics=("arbitrary", "arbitrary"), vmem_limit_bytes=VMEM_LIMIT),
        name="dsa",
    )(q, qi, w, slot_onehot, kidx_all, k_all, v_all)


def _merge_kernel(x_ref, yn_ref, o_ref, gate_ref, wa_ref, wb_ref, wo_ref, out_ref):
    br_a = jnp.dot(yn_ref[...], wa_ref[...], preferred_element_type=F32)
    br_b = jnp.dot(o_ref[...], wb_ref[...], preferred_element_type=F32)
    gates = gate_ref[...].astype(F32)
    merged = _sigmoid(gates[:, :D_MODEL]) * br_a + _sigmoid(gates[:, D_MODEL:]) * br_b
    out_ref[...] = x_ref[...] + jnp.dot(merged.astype(BF16), wo_ref[...],
                                        preferred_element_type=F32)


def _merge(x2d, yn, o, gate, w_ssm_out, w_attn_out, w_o, tm):
    n = x2d.shape[0]
    assert n % tm == 0
    row = lambda w: pl.BlockSpec((tm, w), lambda i: (i, 0))
    full = lambda r, c: pl.BlockSpec((r, c), lambda i: (0, 0))
    return pl.pallas_call(
        _merge_kernel,
        out_shape=jax.ShapeDtypeStruct((n, D_MODEL), F32),
        grid=(n // tm,),
        in_specs=[row(D_MODEL), row(SSM_INNER), row(ATTN_WIDTH), row(2 * D_MODEL),
                  full(SSM_INNER, D_MODEL), full(ATTN_WIDTH, D_MODEL), full(D_MODEL, D_MODEL)],
        out_specs=row(D_MODEL),
        compiler_params=pltpu.CompilerParams(
            dimension_semantics=("arbitrary",), vmem_limit_bytes=VMEM_LIMIT),
        name="merge",
    )(x2d, yn, o, gate, w_ssm_out.astype(BF16), w_attn_out.astype(BF16), w_o.astype(BF16))


def _moe_kernel(x_ref, ng_ref, wr_ref, br_ref, w1_ref, w3_ref, w2_ref, fg_ref, tri_ref, out_ref,
                hs_ref, combs_ref, acc_ref, pt_ref, seg_ref, *, n_steps, blk_rows):
    e_step = pl.program_id(1)
    tm = x_ref.shape[0]
    sr = hs_ref.shape[0]
    BLK = blk_rows

    @pl.when(e_step == 0)
    def _():
        x = x_ref[...]
        ms = jnp.mean(x * x, axis=-1, keepdims=True)
        h = ((x * lax.rsqrt(ms + EPS)) * ng_ref[...]).astype(BF16)
        logits = jnp.dot(h, wr_ref[...], preferred_element_type=F32) + br_ref[...]
        lane_i = lax.broadcasted_iota(I32, (tm, LANE), 1)
        lane = lane_i.astype(F32)
        first = lambda hit: jnp.min(jnp.where(hit, lane, float(LANE)), axis=1, keepdims=True)
        is_g = (lane_i >= N_EXPERTS) & (lane_i < N_EXPERTS + MOE_GROUPS)
        glog = jnp.where(is_g, logits, -jnp.inf)
        gmax = jnp.max(glog, axis=1, keepdims=True)
        gsel = first(glog == gmax) - float(N_EXPERTS)
        p_group = 1.0 / jnp.sum(jnp.exp(glog - gmax), axis=1, keepdims=True)
        grp = (lane_i >> EXPERT_GROUP_SHIFT).astype(F32)
        in_grp = (lane_i < N_EXPERTS) & (grp == gsel)
        el = jnp.where(in_grp, logits, -jnp.inf)
        v1 = jnp.max(el, axis=1, keepdims=True)
        i1 = first(el == v1)
        el2 = jnp.where(lane == i1, -jnp.inf, el)
        v2 = jnp.max(el2, axis=1, keepdims=True)
        i2 = first(el2 == v2)
        e21 = jnp.exp(v2 - v1)
        den = 1.0 + e21
        comb = (jnp.where(lane == i1, (1.0 / den) * p_group, 0.0)
                + jnp.where(lane == i2, (e21 / den) * p_group, 0.0))

        own = lane == gsel
        own_f = jnp.where(own, 1.0, 0.0)
        before = jnp.dot(tri_ref[...], own_f.astype(BF16), preferred_element_type=F32)
        rank = jnp.sum(jnp.where(own, before, 0.0), axis=1, keepdims=True)
        counts = jnp.sum(own_f, axis=0, keepdims=True)
        blocks = jnp.floor((counts + (BLK - 1)) * (1.0 / BLK))
        upper = (lax.broadcasted_iota(I32, (LANE, LANE), 0)
                 < lax.broadcasted_iota(I32, (LANE, LANE), 1))
        starts = jnp.dot(jnp.broadcast_to(blocks, (8, LANE)).astype(BF16),
                         jnp.where(upper, 1.0, 0.0).astype(BF16),
                         preferred_element_type=F32)[0:1, :] * BLK
        dest = jnp.sum(jnp.where(own, starts, 0.0), axis=1, keepdims=True) + rank
        for g in range(MOE_GROUPS):
            seg_ref[g] = jnp.sum(starts[:, g:g + 1]).astype(I32)
            seg_ref[MOE_GROUPS + g] = jnp.sum(blocks[:, g:g + 1]).astype(I32)

        dest_i = dest.astype(I32)
        pt = jnp.where(dest_i == lax.broadcasted_iota(I32, (tm, sr), 1), 1.0, 0.0).astype(BF16)
        pt_ref[...] = pt
        dest_row = jnp.broadcast_to(dest, (tm, LANE)).T[0:1, :].astype(I32)
        p_mat = jnp.where(lax.broadcasted_iota(I32, (sr, tm), 0) == dest_row, 1.0, 0.0
                          ).astype(BF16)
        hs_ref[...] = jnp.dot(p_mat, h, preferred_element_type=F32).astype(BF16)
        c3 = jnp.dot(p_mat, _split3(comb), preferred_element_type=F32)
        combs_ref[...] = c3[:, 0:LANE] + c3[:, LANE:2 * LANE] + c3[:, 2 * LANE:]
        acc_ref[...] = jnp.zeros(acc_ref.shape, F32)

    group = e_step // (EXPERTS_PER_GROUP // EXPERTS_PER_STEP)
    seg_start = seg_ref[group]

    def expert_rows(first_blk, n_rows):
        rows = pl.ds(pl.multiple_of(seg_start + first_blk * BLK, BLK), n_rows)
        h = hs_ref[rows, :]
        comb = combs_ref[rows, :]
        lane = lax.broadcasted_iota(I32, (n_rows, LANE), 1)
        for k in range(EXPERTS_PER_STEP):
            e = e_step * EXPERTS_PER_STEP + k
            a1 = jnp.dot(h, w1_ref[k], preferred_element_type=F32)
            a3 = jnp.dot(h, w3_ref[k], preferred_element_type=F32)
            he = (_silu(a1) * a3).astype(BF16)
            ye = jnp.dot(he, w2_ref[k], preferred_element_type=F32)
            wcol = jnp.sum(jnp.where(lane == e, comb, 0.0), axis=1, keepdims=True)
            acc_ref[rows, :] += wcol * ye

    n_blocks = seg_ref[MOE_GROUPS + group]
    odd_tail = (n_blocks % 2 == 1) & (n_blocks >= 3)
    n_pairs = jnp.where(odd_tail, (n_blocks - 3) // 2, n_blocks // 2)

    def block_pair(u, carry):
        expert_rows(2 * u, 2 * BLK)
        return carry

    lax.fori_loop(0, n_pairs, block_pair, 0)

    @pl.when(odd_tail)
    def _():
        expert_rows(n_blocks - 3, 3 * BLK)

    @pl.when(n_blocks == 1)
    def _():
        expert_rows(0, BLK)

    @pl.when(e_step == n_steps - 1)
    def _():
        y = x_ref[...] + jnp.dot(pt_ref[...], acc_ref[...].astype(BF16),
                                 preferred_element_type=F32)
        ms = jnp.mean(y * y, axis=-1, keepdims=True)
        out_ref[...] = (y * lax.rsqrt(ms + EPS)) * fg_ref[...]


def _moe(x2d, norm_g, w_router_group, b_router_group, w_router_expert, b_router_expert,
         w1, w3, w2, norm_final_g, tm):
    n = x2d.shape[0]
    blk_rows = min(MOE_ROW_BLOCK, tm)
    assert n % tm == 0 and tm % blk_rows == 0
    n_steps = N_EXPERTS // EXPERTS_PER_STEP
    sr = tm + (MOE_GROUPS - 1) * blk_rows
    padw = LANE - N_EXPERTS - MOE_GROUPS
    wr = jnp.concatenate([w_router_expert, w_router_group,
                          jnp.zeros((D_MODEL, padw), F32)], axis=1).astype(BF16)
    br = jnp.concatenate([b_router_expert, b_router_group,
                          jnp.zeros((padw,), F32)]).astype(F32).reshape(1, LANE)
    tri = jnp.asarray(np.tril(np.ones((tm, tm), np.float32), -1), BF16)
    row = pl.BlockSpec((tm, D_MODEL), lambda i, e: (i, 0))
    const = lambda r, c: pl.BlockSpec((r, c), lambda i, e: (0, 0))
    return pl.pallas_call(
        functools.partial(_moe_kernel, n_steps=n_steps, blk_rows=blk_rows),
        out_shape=jax.ShapeDtypeStruct((n, D_MODEL), F32),
        grid=(n // tm, n_steps),
        in_specs=[row, const(1, D_MODEL), const(D_MODEL, LANE), const(1, LANE),
                  pl.BlockSpec((EXPERTS_PER_STEP, D_MODEL, EXPERT_FF), lambda i, e: (e, 0, 0)),
                  pl.BlockSpec((EXPERTS_PER_STEP, D_MODEL, EXPERT_FF), lambda i, e: (e, 0, 0)),
                  pl.BlockSpec((EXPERTS_PER_STEP, EXPERT_FF, D_MODEL), lambda i, e: (e, 0, 0)),
                  const(1, D_MODEL), const(tm, tm)],
        out_specs=row,
        scratch_shapes=[pltpu.VMEM((sr, D_MODEL), BF16), pltpu.VMEM((sr, LANE), F32),
                        pltpu.VMEM((sr, D_MODEL), F32), pltpu.VMEM((tm, sr), BF16),
                        pltpu.SMEM((2 * MOE_GROUPS,), I32)],
        compiler_params=pltpu.CompilerParams(
            dimension_semantics=("arbitrary", "arbitrary"), vmem_limit_bytes=VMEM_LIMIT),
        name="moe",
    )(x2d, norm_g.reshape(1, D_MODEL), wr, br, w1.astype(BF16), w3.astype(BF16),
      w2.astype(BF16), norm_final_g.reshape(1, D_MODEL), tri)


def _token_tile(n):
    for tm in (1024, 512, 256, 128, 64, 32, 16, 8):
        if n % tm == 0:
            return tm
    raise ValueError(f"token count {n} is not a multiple of 8")


def _pad_axis(a, axis, size):
    if a.shape[axis] == size:
        return a
    widths = [(0, 0)] * a.ndim
    widths[axis] = (0, size - a.shape[axis])
    return jnp.pad(a, widths)


def _round_up(n, m):
    return (n + m - 1) // m * m


def _stream_step(x, conv_state, ssm_state, k_past, v_past, kidx_past, w_packed, p):
    b, t, d = x.shape
    n = b * t
    tm = _token_tile(n)
    x2d = x.reshape(n, d)
    z, xbc, q, gate, k, v, qi, ki, dtw, k_b, v_b, ki_b = _in_proj(
        x2d, p["norm_mix_g"], w_packed, tm)

    yn, conv_new, ssm_new = _ssd(
        xbc.reshape(b, t, CONV_DIM), z.reshape(b, t, SSM_INNER), dtw.reshape(b, t, LANE),
        conv_state, ssm_state, p["conv_w"], p["conv_b"], p["dt_bias"], p["A_log"],
        p["D_skip"], p["ssm_norm_g"])

    kv_w = ATTN_KV_HEADS * HEAD_DIM
    k_all, v_all, kidx_all = (k_b.reshape(b, t, kv_w), v_b.reshape(b, t, kv_w),
                              ki_b.reshape(b, t, IDX_DIM))
    past_len = 0
    if k_past is not None:
        past_len = k_past.shape[1]
        k_all = jnp.concatenate([k_past.astype(BF16).reshape(b, past_len, kv_w), k_all], axis=1)
        v_all = jnp.concatenate([v_past.astype(BF16).reshape(b, past_len, kv_w), v_all], axis=1)
        kidx_all = jnp.concatenate([kidx_past.astype(BF16), kidx_all], axis=1)
    l_valid = past_len + t
    lp = _round_up(l_valid, 2 * KEY_TILE)
    if t % Q_BLOCK == 0:
        q_in = (q.reshape(b, t, ATTN_WIDTH), qi.reshape(b, t, IDX_HEADS * IDX_DIM),
                dtw.reshape(b, t, LANE))
    else:
        q_in = (q.reshape(b, t, ATTN_KV_HEADS, ATTN_REP, HEAD_DIM).transpose(0, 2, 3, 1, 4)
                .reshape(b, ATTN_KV_HEADS, ATTN_REP * t, HEAD_DIM),
                qi.reshape(b, t, IDX_HEADS, IDX_DIM).transpose(0, 2, 1, 3)
                .reshape(b, IDX_HEADS * t, IDX_DIM),
                dtw.reshape(b, t, LANE)[:, :, SSM_HEADS:SSM_HEADS + IDX_HEADS]
                .transpose(0, 2, 1).reshape(b, 1, IDX_HEADS * t))
    o = _dsa(*q_in, _pad_axis(kidx_all, 1, lp), _pad_axis(k_all, 1, lp), _pad_axis(v_all, 1, lp),
             past_len, l_valid, t)
    o = o.reshape(n, ATTN_WIDTH)

    x1 = _merge(x2d, yn.reshape(n, SSM_INNER), o, gate, p["w_ssm_out"], p["w_attn_out"],
                p["w_o"], min(tm, MERGE_TILE))
    y = _moe(x1, p["norm_moe_g"], p["w_router_group"], p["b_router_group"],
             p["w_router_expert"], p["b_router_expert"], p["w1"], p["w3"], p["w2"],
             p["norm_final_g"], tm)
    return (y.reshape(b, t, d), k.reshape(b, t, ATTN_KV_HEADS, HEAD_DIM),
            v.reshape(b, t, ATTN_KV_HEADS, HEAD_DIM), ki.reshape(b, t, IDX_DIM), conv_new, ssm_new)


def kernel(x_prompt, x_sample, cache_k, cache_v, cache_kidx, state_conv, state_ssm, norm_mix_g, w_in, conv_w, conv_b, dt_bias, A_log, D_skip, ssm_norm_g, w_ssm_out, w_attn_out, w_o, norm_moe_g, w_router_group, b_router_group, w_router_expert, b_router_expert, w1, w3, w2, norm_final_g):
    p = dict(norm_mix_g=norm_mix_g, conv_w=conv_w, conv_b=conv_b, dt_bias=dt_bias, A_log=A_log,
             D_skip=D_skip, ssm_norm_g=ssm_norm_g, w_ssm_out=w_ssm_out, w_attn_out=w_attn_out,
             w_o=w_o, norm_moe_g=norm_moe_g, w_router_group=w_router_group,
             b_router_group=b_router_group, w_router_expert=w_router_expert,
             b_router_expert=b_router_expert, w1=w1, w3=w3, w2=w2, norm_final_g=norm_final_g)
    w_packed = _pack_w_in(w_in)
    yp, kp, vp, kip, cp, sp = _stream_step(x_prompt, None, None, None, None, None, w_packed, p)
    ys, ksn, vsn, kisn, csn, ssn = _stream_step(x_sample, state_conv, state_ssm, cache_k,
                                                cache_v, cache_kidx, w_packed, p)
    return (yp, ys, kp, vp, kip, cp, sp, ksn, vsn, kisn, csn, ssn)
```

```python
import functools

import numpy as np
import jax
import jax.numpy as jnp
from jax import lax
from jax.experimental import pallas as pl
from jax.experimental.pallas import tpu as pltpu

F32 = jnp.float32
BF16 = jnp.bfloat16
I32 = jnp.int32

D_MODEL = 1024
CHUNK = 64
CHUNK_SHIFT = 6
EPS = 1e-6
SSM_INNER = 2048
SSM_HEADDIM = 64
SSM_HEADS = 32
SSM_GROUPS = 4
SSM_HEADS_PER_GROUP = SSM_HEADS // SSM_GROUPS
SSM_STATE = 128
CONV_WIDTH = 4
CONV_DIM = SSM_INNER + 2 * SSM_GROUPS * SSM_STATE
CONV_CARRY = 16
SSD_CHUNKS_PER_STEP = 2
ATTN_HEADS = 16
ATTN_KV_HEADS = 2
HEAD_DIM = 64
ATTN_REP = ATTN_HEADS // ATTN_KV_HEADS
ATTN_WIDTH = ATTN_HEADS * HEAD_DIM
IDX_HEADS = 8
IDX_DIM = 64
IDX_SCALE = (IDX_HEADS * IDX_DIM) ** -0.5
TOPK_MAX = 256
MOE_GROUPS = 4
EXPERTS_PER_GROUP = 8
EXPERT_GROUP_SHIFT = 3
N_EXPERTS = 32
EXPERT_FF = 256
N_BRANCHES = 2

LANE = 128
VMEM_LIMIT = 56 * 1024 * 1024
PROJ_TILE = 1024
PROJ_TILES = 9
Q_BLOCK = 128
SMALL_Q_BLOCK = 16
KEY_TILE = 256
KEY_INDEX_BITS = 15
EXPERTS_PER_STEP = 4
MOE_ROW_BLOCK = 128
MERGE_TILE = 512
INT_MIN = np.int32(-2 ** 31)
NEG_BIG = -1e30
Q_SCALE = HEAD_DIM ** -0.5 * float(np.log2(np.e))
PV_ROWS = 80

NT_DIMS = (((1,), (1,)), ((), ()))
TN_DIMS = (((0,), (0,)), ((), ()))


def _sigmoid(x):
    return 1.0 / (1.0 + jnp.exp(-x))


def _silu(x):
    h = 0.5 * x
    return h + h * jnp.tanh(h)


def _split3(x):
    hi = x.astype(BF16)
    r1 = x - hi.astype(F32)
    mid = r1.astype(BF16)
    r2 = r1 - mid.astype(F32)
    lo = r2.astype(BF16)
    return jnp.concatenate([hi, mid, lo], axis=1)


def _inproj_kernel(x_ref, g_ref, w_ref, z_ref, xbc_ref, q_ref, gate_ref, k_ref, v_ref,
                   qi_ref, ki_ref, dtw_ref, kb_ref, vb_ref, kib_ref, h_ref):
    j = pl.program_id(1)

    @pl.when(j == 0)
    def _():
        x = x_ref[...]
        ms = jnp.mean(x * x, axis=-1, keepdims=True)
        h_ref[...] = ((x * lax.rsqrt(ms + EPS)) * g_ref[...]).astype(BF16)

    def tile():
        return jnp.dot(h_ref[...], w_ref[...], preferred_element_type=F32)

    @pl.when(j < 2)
    def _():
        z_ref[...] = tile().astype(BF16)

    @pl.when((j >= 2) & (j < 5))
    def _():
        xbc_ref[...] = tile().astype(BF16)

    @pl.when(j == 5)
    def _():
        q_ref[...] = (tile() * Q_SCALE).astype(BF16)

    @pl.when((j >= 6) & (j < 8))
    def _():
        gate_ref[...] = tile().astype(BF16)

    @pl.when(j == 8)
    def _():
        t = tile()
        k_ref[...] = t[:, 0:128]
        v_ref[...] = t[:, 128:256]
        qi_ref[...] = t[:, 256:768].astype(BF16)
        ki_ref[...] = t[:, 768:832]
        dtw_ref[...] = t[:, 896:1024]
        kb_ref[...] = t[:, 0:128].astype(BF16)
        vb_ref[...] = t[:, 128:256].astype(BF16)
        kib_ref[...] = t[:, 768:832].astype(BF16)


def _pack_w_in(w_in):
    sizes = (SSM_INNER, CONV_DIM, SSM_HEADS, ATTN_WIDTH, ATTN_KV_HEADS * HEAD_DIM,
             ATTN_KV_HEADS * HEAD_DIM, IDX_HEADS * IDX_DIM, IDX_DIM, IDX_HEADS,
             N_BRANCHES * D_MODEL)
    offs = np.concatenate([[0], np.cumsum(sizes)])
    z, xbc, dt, q, k, v, qi, ki, wi, g = [w_in[:, offs[i]:offs[i + 1]] for i in range(10)]
    zeros = lambda n: jnp.zeros((D_MODEL, n), w_in.dtype)
    cols = [z, xbc, q, g, k, v, qi, ki, zeros(64), dt, wi, zeros(LANE - SSM_HEADS - IDX_HEADS)]
    return jnp.concatenate(cols, axis=1).astype(BF16)


def _in_proj(x2d, norm_g, w_packed, tm):
    n = x2d.shape[0]
    assert n % tm == 0
    bf = lambda c: jax.ShapeDtypeStruct((n, c), BF16)
    ff = lambda c: jax.ShapeDtypeStruct((n, c), F32)
    clip = lambda j, lo, hi: jnp.clip(j - lo, 0, hi - lo - 1)
    return pl.pallas_call(
        _inproj_kernel,
        out_shape=(bf(SSM_INNER), bf(CONV_DIM), bf(ATTN_WIDTH), bf(2 * D_MODEL),
                   ff(128), ff(128), bf(512), ff(IDX_DIM), ff(128), bf(128), bf(128), bf(IDX_DIM)),
        grid=(n // tm, PROJ_TILES),
        in_specs=[
            pl.BlockSpec((tm, D_MODEL), lambda i, j: (i, 0)),
            pl.BlockSpec((1, D_MODEL), lambda i, j: (0, 0)),
            pl.BlockSpec((D_MODEL, PROJ_TILE), lambda i, j: (0, j)),
        ],
        out_specs=(
            pl.BlockSpec((tm, PROJ_TILE), lambda i, j: (i, clip(j, 0, 2))),
            pl.BlockSpec((tm, PROJ_TILE), lambda i, j: (i, clip(j, 2, 5))),
            pl.BlockSpec((tm, PROJ_TILE), lambda i, j: (i, 0)),
            pl.BlockSpec((tm, PROJ_TILE), lambda i, j: (i, clip(j, 6, 8))),
            pl.BlockSpec((tm, 128), lambda i, j: (i, 0)),
            pl.BlockSpec((tm, 128), lambda i, j: (i, 0)),
            pl.BlockSpec((tm, 512), lambda i, j: (i, 0)),
            pl.BlockSpec((tm, IDX_DIM), lambda i, j: (i, 0)),
            pl.BlockSpec((tm, 128), lambda i, j: (i, 0)),
            pl.BlockSpec((tm, 128), lambda i, j: (i, 0)),
            pl.BlockSpec((tm, 128), lambda i, j: (i, 0)),
            pl.BlockSpec((tm, IDX_DIM), lambda i, j: (i, 0)),
        ),
        scratch_shapes=[pltpu.VMEM((tm, D_MODEL), BF16)],
        compiler_params=pltpu.CompilerParams(
            dimension_semantics=("arbitrary", "arbitrary"), vmem_limit_bytes=VMEM_LIMIT),
        name="in_proj",
    )(x2d, norm_g.reshape(1, D_MODEL), w_packed)


def _ssd_kernel(*refs, L, cps, nsteps, has_state):
    if has_state:
        (xbc_ref, z_ref, dtw_ref, cs_ref, s0_ref, shift_ref, convw_ref, convb_ref, dtb_ref, a_ref,
         dx_ref, ng_ref, e3p_ref, e3l_ref, tri_ref, bd_ref, yn_ref, cnew_ref, sout_ref,
         xe_ref, st, act_ref, ex_ref, yz_ref, acl_ref) = refs
    else:
        (xbc_ref, z_ref, dtw_ref, shift_ref, convw_ref, convb_ref, dtb_ref, a_ref,
         dx_ref, ng_ref, e3p_ref, e3l_ref, tri_ref, bd_ref, yn_ref, cnew_ref, sout_ref,
         xe_ref, st, act_ref, ex_ref, yz_ref, acl_ref) = refs
    c = pl.program_id(1)
    G, R, N, P = SSM_GROUPS, SSM_HEADS_PER_GROUP, SSM_STATE, SSM_HEADDIM
    GW = R * P
    RL = R * L
    LB = cps * L

    @pl.when(c == 0)
    def _():
        xe_ref[LB:, :] = jnp.zeros((xe_ref.shape[0] - LB, CONV_DIM), BF16)
        if has_state:
            xe_ref[LB:LB + CONV_CARRY, :] = cs_ref[0]
            st[...] = s0_ref[0].T
        else:
            st[...] = jnp.zeros(st.shape, F32)

    xe_ref[0:LB, :] = xbc_ref[0]
    shifted = jnp.dot(shift_ref[0], xe_ref[...], preferred_element_type=F32)
    w = convw_ref[...]
    conv = convb_ref[...] + shifted[2 * LB:3 * LB, :] * w[0:1, :]
    conv = conv + shifted[LB:2 * LB, :] * w[1:2, :]
    conv = conv + shifted[0:LB, :] * w[2:3, :]
    conv = conv + xbc_ref[0].astype(F32) * w[3:4, :]
    act_ref[...] = _silu(conv)

    @pl.when(c == nsteps - 1)
    def _():
        cnew_ref[0] = xbc_ref[0, LB - CONV_CARRY:LB, :].astype(F32)[
            CONV_CARRY - (CONV_WIDTH - 1):CONV_CARRY, :]

    if nsteps > 1:
        xe_ref[LB:LB + CONV_CARRY, :] = xe_ref[LB - CONV_CARRY:LB, :]

    u = dtw_ref[0] + dtb_ref[...]
    dt = jnp.maximum(u, 0.0) + jnp.log1p(jnp.exp(-jnp.abs(u)))
    a = dt * a_ref[...]
    ac3 = jnp.dot(tri_ref[...], _split3(a), preferred_element_type=F32)
    a_cum = ac3[:, 0:LANE] + ac3[:, LANE:2 * LANE] + ac3[:, 2 * LANE:]

    ex_ref[...] = jnp.dot(_split3(jnp.concatenate([a_cum, dt], axis=0)), e3p_ref[...],
                          preferred_element_type=F32)
    if L != P:
        acl_ref[...] = jnp.dot(_split3(a_cum), e3l_ref[...], preferred_element_type=F32)
    row = lax.broadcasted_iota(I32, (L, RL), 0)
    scol = lax.broadcasted_iota(I32, (L, RL), 1) & (L - 1)

    for sub in range(cps):
        r0 = sub * L
        rows = slice(r0, r0 + L)
        ssq = jnp.zeros((L, 1), F32)
        for g in range(G):
            sl = slice(g * GW, (g + 1) * GW)
            xs = act_ref[rows, sl]
            b_g = act_ref[rows, SSM_INNER + g * N:SSM_INNER + (g + 1) * N].astype(BF16)
            c_g = act_ref[rows, SSM_INNER + (G + g) * N:SSM_INNER + (G + g + 1) * N].astype(BF16)
            acx = ex_ref[rows, sl]
            alast = ex_ref[r0 + L - 1:r0 + L, sl]
            xdt = xs * ex_ref[LB + r0:LB + r0 + L, sl]
            acl_g = acx if L == P else acl_ref[rows, g * RL:(g + 1) * RL]
            acs_g = jnp.sum(jnp.where(scol == row, acl_g, 0.0), axis=0, keepdims=True)
            decay = jnp.where(scol <= row, jnp.exp(acl_g - acs_g), 0.0)
            cb = lax.dot_general(c_g, jnp.concatenate([b_g] * R, axis=0), NT_DIMS,
                                 preferred_element_type=F32)
            m_g = (cb * decay).astype(BF16)
            xbd = jnp.concatenate([xdt.astype(BF16)] * R, axis=0) * bd_ref[...]
            y_in = jnp.dot(m_g, xbd, preferred_element_type=F32)
            st_g = st[:, sl]
            y_st = jnp.dot(c_g, st_g.astype(BF16), preferred_element_type=F32)
            xw = (xdt * jnp.exp(alast - acx)).astype(BF16)
            upd = lax.dot_general(b_g, xw, TN_DIMS, preferred_element_type=F32)
            st[:, sl] = st_g * jnp.exp(alast) + upd
            y = y_in + jnp.exp(acx) * y_st + dx_ref[:, sl] * xs
            yz = y * _silu(z_ref[0, rows, sl].astype(F32))
            ssq = ssq + jnp.sum(yz * yz, axis=-1, keepdims=True)
            yz_ref[rows, sl] = yz
        scale = lax.rsqrt(ssq * (1.0 / SSM_INNER) + EPS)
        yn_ref[0, rows, :] = ((yz_ref[rows, :] * scale) * ng_ref[...]).astype(BF16)

    @pl.when(c == nsteps - 1)
    def _():
        sout_ref[0] = st[...].T


def _shift_matrices(L, conv_rows):
    out = np.zeros((2, 3 * L, conv_rows), np.float32)
    for k in range(CONV_WIDTH - 1):
        for t in range(L):
            src = t - (k + 1)
            if src >= 0:
                out[:, k * L + t, src] = 1.0
            else:
                out[1, k * L + t, L + CONV_CARRY + src] = 1.0
                j = CONV_WIDTH - 1 + src
                for piece in range(3):
                    out[0, k * L + t, L + 3 * piece + j] = 1.0
    return jnp.asarray(out, BF16)


def _expand_matrix(width_per_head):
    r = np.arange(3 * LANE)[:, None] % LANE
    c = np.arange(SSM_HEADS * width_per_head)[None, :] // width_per_head
    return jnp.asarray((r == c).astype(np.float32), BF16)


def _ssd(xbc, z, dtw, conv_state, ssm_state, conv_w, conv_b, dt_bias, a_log, d_skip, ssm_norm_g):
    b, t, _ = xbc.shape
    L = min(CHUNK, t)
    nc = t // L
    assert t % L == 0 and L & (L - 1) == 0
    has_state = conv_state is not None
    pad = lambda v: jnp.pad(v.astype(F32), (0, LANE - SSM_HEADS)).reshape(1, LANE)
    a_neg = pad(-jnp.exp(a_log.astype(F32)))
    cps = SSD_CHUNKS_PER_STEP if nc % SSD_CHUNKS_PER_STEP == 0 else 1
    nsteps = nc // cps
    lb = cps * L
    conv_rows = _round_up(lb + CONV_CARRY, LANE)
    tri = jnp.asarray(np.kron(np.eye(cps), np.tril(np.ones((L, L)))).astype(np.float32), BF16)
    rl = SSM_HEADS_PER_GROUP * L
    gw = SSM_HEADS_PER_GROUP * SSM_HEADDIM
    bd = jnp.asarray((np.arange(rl)[:, None] // L == np.arange(gw)[None, :] // SSM_HEADDIM)
                     .astype(np.float32), BF16)
    const = lambda shape: pl.BlockSpec(shape, lambda i, j: (0,) * len(shape))
    tok = lambda w: pl.BlockSpec((1, lb, w), lambda i, j: (i, j, 0))
    per_b = lambda s: pl.BlockSpec((1,) + s, lambda i, j: (i, 0, 0))
    ins = [xbc, z, dtw]
    specs = [tok(CONV_DIM), tok(SSM_INNER), tok(LANE)]
    if has_state:
        triples = _split3(conv_state.astype(F32))
        ins += [_pad_axis(triples, 1, CONV_CARRY), ssm_state.reshape(b, SSM_INNER, SSM_STATE)]
        specs += [per_b((CONV_CARRY, CONV_DIM)), per_b((SSM_INNER, SSM_STATE))]
    first = 0 if has_state else 1
    ins += [_shift_matrices(lb, conv_rows)]
    specs += [pl.BlockSpec((1, 3 * lb, conv_rows),
                           lambda i, j: (jnp.where(j == 0, first, 1), 0, 0))]
    ins += [conv_w, conv_b.reshape(1, CONV_DIM), pad(dt_bias), a_neg,
            jnp.repeat(d_skip.astype(F32), SSM_HEADDIM).reshape(1, SSM_INNER),
            ssm_norm_g.reshape(1, SSM_INNER), _expand_matrix(SSM_HEADDIM), _expand_matrix(L),
            tri, bd]
    specs += [const((CONV_WIDTH, CONV_DIM)), const((1, CONV_DIM)), const((1, LANE)),
              const((1, LANE)), const((1, SSM_INNER)), const((1, SSM_INNER)),
              const((3 * LANE, SSM_INNER)), const((3 * LANE, SSM_HEADS * L)),
              const((lb, lb)), const((rl, gw))]
    yn, cnew, sout = pl.pallas_call(
        functools.partial(_ssd_kernel, L=L, cps=cps, nsteps=nsteps, has_state=has_state),
        out_shape=(jax.ShapeDtypeStruct((b, t, SSM_INNER), BF16),
                   jax.ShapeDtypeStruct((b, CONV_WIDTH - 1, CONV_DIM), F32),
                   jax.ShapeDtypeStruct((b, SSM_INNER, SSM_STATE), F32)),
        grid=(b, nsteps),
        in_specs=specs,
        out_specs=(tok(SSM_INNER), per_b((CONV_WIDTH - 1, CONV_DIM)),
                   per_b((SSM_INNER, SSM_STATE))),
        scratch_shapes=[pltpu.VMEM((conv_rows, CONV_DIM), BF16),
                        pltpu.VMEM((SSM_STATE, SSM_INNER), F32),
                        pltpu.VMEM((lb, CONV_DIM), F32),
                        pltpu.VMEM((2 * lb, SSM_INNER), F32),
                        pltpu.VMEM((lb, SSM_INNER), F32),
                        pltpu.VMEM((lb, SSM_HEADS * L), F32)],
        compiler_params=pltpu.CompilerParams(
            dimension_semantics=("arbitrary", "arbitrary"), vmem_limit_bytes=VMEM_LIMIT),
        name="ssd",
    )(*ins)
    return yn, cnew, sout.reshape(b, SSM_HEADS, SSM_HEADDIM, SSM_STATE)


def _dsa_kernel(q_ref, qi_ref, w_ref, eq_ref, kidx_ref, k_ref, v_ref, o_ref,
                qs_ref, qx_ref, key_ref, kx_ref, vx_ref, m_ref, acc_ref, mlim_ref, s_ref,
                *, past_len, l_valid, topk, nq):
    qb = pl.program_id(1)
    TQ, TK = LANE, KEY_TILE
    W = ATTN_REP * nq
    wide = nq == LANE

    lane = lax.broadcasted_iota(I32, (1, TQ), 1)
    pos = past_len + qb * nq + (lane & (nq - 1))
    n_adm = jnp.minimum(((pos >> CHUNK_SHIFT) + 1) * CHUNK, l_valid)
    k_eff = jnp.minimum(n_adm, topk)
    last_pos = past_len + qb * nq + nq - 1
    n_max = jnp.minimum(((last_pos >> CHUNK_SHIFT) + 1) * CHUNK, l_valid)
    nt = (n_max + TK - 1) // TK
    ntp = 2 * ((nt + 1) // 2)
    lp = key_ref.shape[0]

    @pl.when(qb == 0)
    def _():
        kx_ref[:, LANE:] = k_ref[0]

        def v_tile(t, carry):
            s0 = pl.multiple_of(t * TK, TK)
            v_t = v_ref[0, pl.ds(s0, TK), :].astype(F32).T
            for g in range(ATTN_KV_HEADS):
                vx_ref[g, 0:HEAD_DIM, pl.ds(s0, TK)] = (
                    v_t[g * HEAD_DIM:(g + 1) * HEAD_DIM, :].astype(BF16))
            return carry

        lax.fori_loop(0, lp // TK, v_tile, 0)
        tail_row = lax.broadcasted_iota(I32, (PV_ROWS - HEAD_DIM, lp), 0)
        for g in range(ATTN_KV_HEADS):
            vx_ref[g, HEAD_DIM:, :] = jnp.where(tail_row == 0, 1.0, 0.0).astype(BF16)
            qx_ref[g, :, 0:LANE] = eq_ref[...]
            qx_ref[g, :, LANE:] = jnp.zeros((W, LANE), BF16)

    if wide:
        for h in range(IDX_HEADS):
            qs_ref[h * nq:(h + 1) * nq, :] = qi_ref[0, :, h * IDX_DIM:(h + 1) * IDX_DIM]
        w_t = w_ref[0].T[SSM_HEADS:SSM_HEADS + IDX_HEADS, :] * IDX_SCALE
    else:
        qs_ref[...] = qi_ref[0]
        w_row = w_ref[0] * IDX_SCALE

    def index_dots(t, slot):
        s0 = pl.multiple_of(jnp.minimum(t, nt - 1) * TK, TK)
        s_ref[0, slot] = lax.dot_general(kidx_ref[0, pl.ds(s0, TK), :], qs_ref[...], NT_DIMS,
                                         preferred_element_type=F32)

    def score_tile(t, slot):
        s0 = pl.multiple_of(t * TK, TK)
        d = s_ref[0, slot]
        if wide:
            sc = w_t[0:1, :] * jnp.maximum(d[:, 0:nq], 0.0)
            for h in range(1, IDX_HEADS):
                sc = sc + w_t[h:h + 1, :] * jnp.maximum(d[:, h * nq:(h + 1) * nq], 0.0)
        else:
            sc = w_row * jnp.maximum(d, 0.0)
            for shift in [nq << i for i in reversed(range(IDX_HEADS.bit_length() - 1))]:
                sc = sc + pltpu.roll(sc, shift, axis=1)
        sc = sc + 0.0
        bits = pltpu.bitcast(sc, I32)
        key = jnp.where(bits < 0, bits ^ np.int32(0x7FFFFFFF), bits)
        s_idx = s0 + lax.broadcasted_iota(I32, (TK, TQ), 0)
        key_ref[pl.ds(s0, TK), :] = jnp.where(s_idx < n_adm, key, INT_MIN)

    index_dots(0, 0)

    def score_pair(u, carry):
        index_dots(2 * u + 1, 1)
        score_tile(2 * u, 0)
        index_dots(2 * u + 2, 0)
        score_tile(2 * u + 1, 1)
        return carry

    lax.fori_loop(0, nt // 2, score_pair, 0)

    @pl.when(nt < ntp)
    def _():
        score_tile(nt - 1, 0)
        key_ref[pl.ds(pl.multiple_of(nt * TK, TK), TK), :] = jnp.full((TK, TQ), INT_MIN, I32)

    def count(pred):
        def body(t, acc):
            s0 = pl.multiple_of(t * (2 * TK), 2 * TK)
            blk = key_ref[pl.ds(s0, 2 * TK), :]
            s_idx = s0 + lax.broadcasted_iota(I32, (2 * TK, TQ), 0)
            hit = jnp.where(pred(blk, s_idx), 1, 0).astype(I32)
            return acc + jnp.sum(hit.reshape(2 * TK // 8, 8, TQ), axis=0)
        acc = lax.fori_loop(0, ntp // 2, body, jnp.zeros((8, TQ), I32))
        return jnp.sum(acc, axis=0, keepdims=True)

    def bit_step(i, prefix):
        cand = prefix | jnp.left_shift(jnp.int32(1), 31 - i)
        cand_s = cand ^ INT_MIN
        cnt = count(lambda blk, s_idx: blk >= cand_s)
        return jnp.where(cnt >= k_eff, cand, prefix)

    thr = lax.fori_loop(0, 32, bit_step, jnp.zeros((1, TQ), I32)) ^ INT_MIN

    n_gt = count(lambda blk, s_idx: blk > thr)
    n_eq = count(lambda blk, s_idx: blk == thr)
    need = k_eff - n_gt
    mlim_ref[...] = jnp.full((1, TQ), 1 << KEY_INDEX_BITS, I32)

    @pl.when(jnp.max(n_eq - need) > 0)
    def _():
        def idx_step(i, prefix):
            cand = prefix | jnp.left_shift(jnp.int32(1), KEY_INDEX_BITS - 1 - i)
            cnt = count(lambda blk, s_idx: (blk == thr) & (s_idx < cand))
            return jnp.where(cnt < need, cand, prefix)
        mlim_ref[...] = lax.fori_loop(0, KEY_INDEX_BITS, idx_step, jnp.zeros((1, TQ), I32))

    mlim = mlim_ref[...]

    def bias_tile(t, carry):
        s0 = pl.multiple_of(t * TK, TK)
        blk = key_ref[pl.ds(s0, TK), :]
        s_idx = s0 + lax.broadcasted_iota(I32, (TK, TQ), 0)
        sel = (blk > thr) | ((blk == thr) & (s_idx <= mlim))
        kx_ref[pl.ds(s0, TK), 0:LANE] = jnp.where(sel, 0.0, NEG_BIG).astype(BF16)
        return carry

    lax.fori_loop(0, nt, bias_tile, 0)

    for g in range(ATTN_KV_HEADS):
        lanes = slice(LANE + g * HEAD_DIM, LANE + (g + 1) * HEAD_DIM)
        if wide:
            for r in range(ATTN_REP):
                hq = g * ATTN_REP + r
                qx_ref[g, r * nq:(r + 1) * nq, lanes] = (
                    q_ref[0, :, hq * HEAD_DIM:(hq + 1) * HEAD_DIM])
        else:
            qx_ref[g, :, lanes] = q_ref[0, g]
    m_ref[...] = jnp.full(m_ref.shape, -jnp.inf, F32)
    acc_ref[...] = jnp.zeros(acc_ref.shape, F32)

    def scores(t, slot):
        s0 = pl.multiple_of(jnp.minimum(t, nt - 1) * TK, TK)
        k_tile = kx_ref[pl.ds(s0, TK), :]
        for g in range(ATTN_KV_HEADS):
            s_ref[g, slot] = lax.dot_general(k_tile, qx_ref[g], NT_DIMS,
                                             preferred_element_type=F32)

    def accumulate(t, slot):
        s0 = pl.multiple_of(t * TK, TK)
        for g in range(ATTN_KV_HEADS):
            s = s_ref[g, slot]
            m_prev = m_ref[g]
            m_new = jnp.maximum(m_prev, jnp.max(s, axis=0, keepdims=True))
            p = jnp.exp2(s - m_new)
            acc_ref[g] = jnp.exp2(m_prev - m_new) * acc_ref[g] + jnp.dot(
                vx_ref[g, :, pl.ds(s0, TK)], p.astype(BF16), preferred_element_type=F32)
            m_ref[g] = m_new

    scores(0, 0)

    def attn_pair(u, carry):
        scores(2 * u + 1, 1)
        accumulate(2 * u, 0)
        scores(2 * u + 2, 0)
        accumulate(2 * u + 1, 1)
        return carry

    lax.fori_loop(0, nt // 2, attn_pair, 0)

    @pl.when(nt < ntp)
    def _():
        accumulate(nt - 1, 0)

    for g in range(ATTN_KV_HEADS):
        acc = acc_ref[g]
        o_t = acc[0:HEAD_DIM, :] * (1.0 / acc[HEAD_DIM:HEAD_DIM + 1, :])
        if wide:
            for r in range(0, ATTN_REP, 2):
                hq = g * ATTN_REP + r
                pair = jnp.concatenate([o_t[:, r * nq:(r + 1) * nq],
                                        o_t[:, (r + 1) * nq:(r + 2) * nq]], axis=0)
                o_ref[0, :, hq * HEAD_DIM:(hq + 2) * HEAD_DIM] = pair.T.astype(BF16)
        else:
            o_rq = jnp.concatenate([o_t, o_t], axis=0).T[:, 0:HEAD_DIM].astype(BF16)
            for r in range(ATTN_REP):
                hq = g * ATTN_REP + r
                o_ref[0, :, hq * HEAD_DIM:(hq + 1) * HEAD_DIM] = o_rq[r * nq:(r + 1) * nq, :]


def _dsa(q, qi, w, kidx_all, k_all, v_all, past_len, l_valid, t):
    b = kidx_all.shape[0]
    lp = kidx_all.shape[1]
    nq = Q_BLOCK if t % Q_BLOCK == 0 else SMALL_Q_BLOCK
    assert t % nq == 0 and lp % (2 * KEY_TILE) == 0 and lp < 1 << KEY_INDEX_BITS
    topk = min(TOPK_MAX, l_valid // 4)
    width = ATTN_REP * nq
    if nq == Q_BLOCK:
        qblk = lambda c: pl.BlockSpec((1, nq, c), lambda i, j: (i, j, 0))
        q_specs = [qblk(ATTN_WIDTH), qblk(IDX_HEADS * IDX_DIM), qblk(LANE)]
    else:
        assert t == nq
        q_specs = [pl.BlockSpec((1, ATTN_KV_HEADS, width, HEAD_DIM), lambda i, j: (i, 0, 0, 0)),
                   pl.BlockSpec((1, IDX_HEADS * nq, IDX_DIM), lambda i, j: (i, 0, 0)),
                   pl.BlockSpec((1, 1, LANE), lambda i, j: (i, 0, 0))]
    slot_onehot = jnp.asarray(
        (np.arange(width)[:, None] % nq == np.arange(LANE)[None, :]).astype(np.float32), BF16)
    per_stream = lambda c: pl.BlockSpec((1, lp, c), lambda i, j: (i, 0, 0))
    return pl.pallas_call(
        functools.partial(_dsa_kernel, past_len=past_len, l_valid=l_valid, topk=topk, nq=nq),
        out_shape=jax.ShapeDtypeStruct((b, t, ATTN_WIDTH), BF16),
        grid=(b, t // nq),
        in_specs=q_specs + [pl.BlockSpec((width, LANE), lambda i, j: (0, 0)),
                            per_stream(IDX_DIM), per_stream(LANE), per_stream(LANE)],
        out_specs=pl.BlockSpec((1, nq, ATTN_WIDTH), lambda i, j: (i, j, 0)),
        scratch_shapes=[pltpu.VMEM((width, IDX_DIM), BF16),
                        pltpu.VMEM((ATTN_KV_HEADS, width, 2 * LANE), BF16),
                        pltpu.VMEM((lp, LANE), I32),
                        pltpu.VMEM((lp, 2 * LANE), BF16),
                        pltpu.VMEM((ATTN_KV_HEADS, PV_ROWS, lp), BF16),
                        pltpu.VMEM((ATTN_KV_HEADS, 1, width), F32),
                        pltpu.VMEM((ATTN_KV_HEADS, PV_ROWS, width), F32),
                        pltpu.VMEM((1, LANE), I32),
                        pltpu.VMEM((ATTN_KV_HEADS, 2, KEY_TILE, width), F32)],
        compiler_params=pltpu.CompilerParams(
            dimension_semantics=("arbitrary", "arbitrary"), vmem_limit_bytes=VMEM_LIMIT),
        name="dsa",
    )(q, qi, w, slot_onehot, kidx_all, k_all, v_all)


def _merge_kernel(x_ref, yn_ref, o_ref, gate_ref, wa_ref, wb_ref, wo_ref, out_ref):
    br_a = jnp.dot(yn_ref[...], wa_ref[...], preferred_element_type=F32)
    br_b = jnp.dot(o_ref[...], wb_ref[...], preferred_element_type=F32)
    gates = gate_ref[...].astype(F32)
    merged = _sigmoid(gates[:, :D_MODEL]) * br_a + _sigmoid(gates[:, D_MODEL:]) * br_b
    out_ref[...] = x_ref[...] + jnp.dot(merged.astype(BF16), wo_ref[...],
                                        preferred_element_type=F32)


def _merge(x2d, yn, o, gate, w_ssm_out, w_attn_out, w_o, tm):
    n = x2d.shape[0]
    assert n % tm == 0
    row = lambda w: pl.BlockSpec((tm, w), lambda i: (i, 0))
    full = lambda r, c: pl.BlockSpec((r, c), lambda i: (0, 0))
    return pl.pallas_call(
        _merge_kernel,
        out_shape=jax.ShapeDtypeStruct((n, D_MODEL), F32),
        grid=(n // tm,),
        in_specs=[row(D_MODEL), row(SSM_INNER), row(ATTN_WIDTH), row(2 * D_MODEL),
                  full(SSM_INNER, D_MODEL), full(ATTN_WIDTH, D_MODEL), full(D_MODEL, D_MODEL)],
        out_specs=row(D_MODEL),
        compiler_params=pltpu.CompilerParams(
            dimension_semantics=("arbitrary",), vmem_limit_bytes=VMEM_LIMIT),
        name="merge",
    )(x2d, yn, o, gate, w_ssm_out.astype(BF16), w_attn_out.astype(BF16), w_o.astype(BF16))


def _moe_kernel(x_ref, ng_ref, wr_ref, br_ref, w1_ref, w3_ref, w2_ref, fg_ref, tri_ref, out_ref,
                hs_ref, combs_ref, acc_ref, pt_ref, seg_ref, *, n_steps, blk_rows):
    e_step = pl.program_id(1)
    tm = x_ref.shape[0]
    sr = hs_ref.shape[0]
    BLK = blk_rows

    @pl.when(e_step == 0)
    def _():
        x = x_ref[...]
        ms = jnp.mean(x * x, axis=-1, keepdims=True)
        h = ((x * lax.rsqrt(ms + EPS)) * ng_ref[...]).astype(BF16)
        logits = jnp.dot(h, wr_ref[...], preferred_element_type=F32) + br_ref[...]
        lane_i = lax.broadcasted_iota(I32, (tm, LANE), 1)
        lane = lane_i.astype(F32)
        first = lambda hit: jnp.min(jnp.where(hit, lane, float(LANE)), axis=1, keepdims=True)
        is_g = (lane_i >= N_EXPERTS) & (lane_i < N_EXPERTS + MOE_GROUPS)
        glog = jnp.where(is_g, logits, -jnp.inf)
        gmax = jnp.max(glog, axis=1, keepdims=True)
        gsel = first(glog == gmax) - float(N_EXPERTS)
        p_group = 1.0 / jnp.sum(jnp.exp(glog - gmax), axis=1, keepdims=True)
        grp = (lane_i >> EXPERT_GROUP_SHIFT).astype(F32)
        in_grp = (lane_i < N_EXPERTS) & (grp == gsel)
        el = jnp.where(in_grp, logits, -jnp.inf)
        v1 = jnp.max(el, axis=1, keepdims=True)
        i1 = first(el == v1)
        el2 = jnp.where(lane == i1, -jnp.inf, el)
        v2 = jnp.max(el2, axis=1, keepdims=True)
        i2 = first(el2 == v2)
        e21 = jnp.exp(v2 - v1)
        den = 1.0 + e21
        comb = (jnp.where(lane == i1, (1.0 / den) * p_group, 0.0)
                + jnp.where(lane == i2, (e21 / den) * p_group, 0.0))

        own = lane == gsel
        own_f = jnp.where(own, 1.0, 0.0)
        before = jnp.dot(tri_ref[...], own_f.astype(BF16), preferred_element_type=F32)
        rank = jnp.sum(jnp.where(own, before, 0.0), axis=1, keepdims=True)
        counts = jnp.sum(own_f, axis=0, keepdims=True)
        blocks = jnp.floor((counts + (BLK - 1)) * (1.0 / BLK))
        upper = (lax.broadcasted_iota(I32, (LANE, LANE), 0)
                 < lax.broadcasted_iota(I32, (LANE, LANE), 1))
        starts = jnp.dot(jnp.broadcast_to(blocks, (8, LANE)).astype(BF16),
                         jnp.where(upper, 1.0, 0.0).astype(BF16),
                         preferred_element_type=F32)[0:1, :] * BLK
        dest = jnp.sum(jnp.where(own, starts, 0.0), axis=1, keepdims=True) + rank
        for g in range(MOE_GROUPS):
            seg_ref[g] = jnp.sum(starts[:, g:g + 1]).astype(I32)
            seg_ref[MOE_GROUPS + g] = jnp.sum(blocks[:, g:g + 1]).astype(I32)

        dest_i = dest.astype(I32)
        pt = jnp.where(dest_i == lax.broadcasted_iota(I32, (tm, sr), 1), 1.0, 0.0).astype(BF16)
        pt_ref[...] = pt
        dest_row = jnp.broadcast_to(dest, (tm, LANE)).T[0:1, :].astype(I32)
        p_mat = jnp.where(lax.broadcasted_iota(I32, (sr, tm), 0) == dest_row, 1.0, 0.0
                          ).astype(BF16)
        hs_ref[...] = jnp.dot(p_mat, h, preferred_element_type=F32).astype(BF16)
        c3 = jnp.dot(p_mat, _split3(comb), preferred_element_type=F32)
        combs_ref[...] = c3[:, 0:LANE] + c3[:, LANE:2 * LANE] + c3[:, 2 * LANE:]
        acc_ref[...] = jnp.zeros(acc_ref.shape, F32)

    group = e_step // (EXPERTS_PER_GROUP // EXPERTS_PER_STEP)
    seg_start = seg_ref[group]

    def expert_rows(first_blk, n_rows):
        rows = pl.ds(pl.multiple_of(seg_start + first_blk * BLK, BLK), n_rows)
        h = hs_ref[rows, :]
        comb = combs_ref[rows, :]
        lane = lax.broadcasted_iota(I32, (n_rows, LANE), 1)
        for k in range(EXPERTS_PER_STEP):
            e = e_step * EXPERTS_PER_STEP + k
            a1 = jnp.dot(h, w1_ref[k], preferred_element_type=F32)
            a3 = jnp.dot(h, w3_ref[k], preferred_element_type=F32)
            he = (_silu(a1) * a3).astype(BF16)
            ye = jnp.dot(he, w2_ref[k], preferred_element_type=F32)
            wcol = jnp.sum(jnp.where(lane == e, comb, 0.0), axis=1, keepdims=True)
            acc_ref[rows, :] += wcol * ye

    n_blocks = seg_ref[MOE_GROUPS + group]
    odd_tail = (n_blocks % 2 == 1) & (n_blocks >= 3)
    n_pairs = jnp.where(odd_tail, (n_blocks - 3) // 2, n_blocks // 2)

    def block_pair(u, carry):
        expert_rows(2 * u, 2 * BLK)
        return carry

    lax.fori_loop(0, n_pairs, block_pair, 0)

    @pl.when(odd_tail)
    def _():
        expert_rows(n_blocks - 3, 3 * BLK)

    @pl.when(n_blocks == 1)
    def _():
        expert_rows(0, BLK)

    @pl.when(e_step == n_steps - 1)
    def _():
        y = x_ref[...] + jnp.dot(pt_ref[...], acc_ref[...].astype(BF16),
                                 preferred_element_type=F32)
        ms = jnp.mean(y * y, axis=-1, keepdims=True)
        out_ref[...] = (y * lax.rsqrt(ms + EPS)) * fg_ref[...]


def _moe(x2d, norm_g, w_router_group, b_router_group, w_router_expert, b_router_expert,
         w1, w3, w2, norm_final_g, tm):
    n = x2d.shape[0]
    blk_rows = min(MOE_ROW_BLOCK, tm)
    assert n % tm == 0 and tm % blk_rows == 0
    n_steps = N_EXPERTS // EXPERTS_PER_STEP
    sr = tm + (MOE_GROUPS - 1) * blk_rows
    padw = LANE - N_EXPERTS - MOE_GROUPS
    wr = jnp.concatenate([w_router_expert, w_router_group,
                          jnp.zeros((D_MODEL, padw), F32)], axis=1).astype(BF16)
    br = jnp.concatenate([b_router_expert, b_router_group,
                          jnp.zeros((padw,), F32)]).astype(F32).reshape(1, LANE)
    tri = jnp.asarray(np.tril(np.ones((tm, tm), np.float32), -1), BF16)
    row = pl.BlockSpec((tm, D_MODEL), lambda i, e: (i, 0))
    const = lambda r, c: pl.BlockSpec((r, c), lambda i, e: (0, 0))
    return pl.pallas_call(
        functools.partial(_moe_kernel, n_steps=n_steps, blk_rows=blk_rows),
        out_shape=jax.ShapeDtypeStruct((n, D_MODEL), F32),
        grid=(n // tm, n_steps),
        in_specs=[row, const(1, D_MODEL), const(D_MODEL, LANE), const(1, LANE),
                  pl.BlockSpec((EXPERTS_PER_STEP, D_MODEL, EXPERT_FF), lambda i, e: (e, 0, 0)),
                  pl.BlockSpec((EXPERTS_PER_STEP, D_MODEL, EXPERT_FF), lambda i, e: (e, 0, 0)),
                  pl.BlockSpec((EXPERTS_PER_STEP, EXPERT_FF, D_MODEL), lambda i, e: (e, 0, 0)),
                  const(1, D_MODEL), const(tm, tm)],
        out_specs=row,
        scratch_shapes=[pltpu.VMEM((sr, D_MODEL), BF16), pltpu.VMEM((sr, LANE), F32),
                        pltpu.VMEM((sr, D_MODEL), F32), pltpu.VMEM((tm, sr), BF16),
                        pltpu.SMEM((2 * MOE_GROUPS,), I32)],
        compiler_params=pltpu.CompilerParams(
            dimension_semantics=("arbitrary", "arbitrary"), vmem_limit_bytes=VMEM_LIMIT),
        name="moe",
    )(x2d, norm_g.reshape(1, D_MODEL), wr, br, w1.astype(BF16), w3.astype(BF16),
      w2.astype(BF16), norm_final_g.reshape(1, D_MODEL), tri)


def _token_tile(n):
    for tm in (1024, 512, 256, 128, 64, 32, 16, 8):
        if n % tm == 0:
            return tm
    raise ValueError(f"token count {n} is not a multiple of 8")


def _pad_axis(a, axis, size):
    if a.shape[axis] == size:
        return a
    widths = [(0, 0)] * a.ndim
    widths[axis] = (0, size - a.shape[axis])
    return jnp.pad(a, widths)


def _round_up(n, m):
    return (n + m - 1) // m * m


def _stream_step(x, conv_state, ssm_state, k_past, v_past, kidx_past, w_packed, p):
    b, t, d = x.shape
    n = b * t
    tm = _token_tile(n)
    x2d = x.reshape(n, d)
    z, xbc, q, gate, k, v, qi, ki, dtw, k_b, v_b, ki_b = _in_proj(
        x2d, p["norm_mix_g"], w_packed, tm)

    yn, conv_new, ssm_new = _ssd(
        xbc.reshape(b, t, CONV_DIM), z.reshape(b, t, SSM_INNER), dtw.reshape(b, t, LANE),
        conv_state, ssm_state, p["conv_w"], p["conv_b"], p["dt_bias"], p["A_log"],
        p["D_skip"], p["ssm_norm_g"])

    kv_w = ATTN_KV_HEADS * HEAD_DIM
    k_all, v_all, kidx_all = (k_b.reshape(b, t, kv_w), v_b.reshape(b, t, kv_w),
                              ki_b.reshape(b, t, IDX_DIM))
    past_len = 0
    if k_past is not None:
        past_len = k_past.shape[1]
        k_all = jnp.concatenate([k_past.astype(BF16).reshape(b, past_len, kv_w), k_all], axis=1)
        v_all = jnp.concatenate([v_past.astype(BF16).reshape(b, past_len, kv_w), v_all], axis=1)
        kidx_all = jnp.concatenate([kidx_past.astype(BF16), kidx_all], axis=1)
    l_valid = past_len + t
    lp = _round_up(l_valid, 2 * KEY_TILE)
    if t % Q_BLOCK == 0:
        q_in = (q.reshape(b, t, ATTN_WIDTH), qi.reshape(b, t, IDX_HEADS * IDX_DIM),
                dtw.reshape(b, t, LANE))
    else:
        q_in = (q.reshape(b, t, ATTN_KV_HEADS, ATTN_REP, HEAD_DIM).transpose(0, 2, 3, 1, 4)
                .reshape(b, ATTN_KV_HEADS, ATTN_REP * t, HEAD_DIM),
                qi.reshape(b, t, IDX_HEADS, IDX_DIM).transpose(0, 2, 1, 3)
                .reshape(b, IDX_HEADS * t, IDX_DIM),
                dtw.reshape(b, t, LANE)[:, :, SSM_HEADS:SSM_HEADS + IDX_HEADS]
                .transpose(0, 2, 1).reshape(b, 1, IDX_HEADS * t))
    o = _dsa(*q_in, _pad_axis(kidx_all, 1, lp), _pad_axis(k_all, 1, lp), _pad_axis(v_all, 1, lp),
             past_len, l_valid, t)
    o = o.reshape(n, ATTN_WIDTH)

    x1 = _merge(x2d, yn.reshape(n, SSM_INNER), o, gate, p["w_ssm_out"], p["w_attn_out"],
                p["w_o"], min(tm, MERGE_TILE))
    y = _moe(x1, p["norm_moe_g"], p["w_router_group"], p["b_router_group"],
             p["w_router_expert"], p["b_router_expert"], p["w1"], p["w3"], p["w2"],
             p["norm_final_g"], tm)
    return (y.reshape(b, t, d), k.reshape(b, t, ATTN_KV_HEADS, HEAD_DIM),
            v.reshape(b, t, ATTN_KV_HEADS, HEAD_DIM), ki.reshape(b, t, IDX_DIM), conv_new, ssm_new)


def kernel(x_prompt, x_sample, cache_k, cache_v, cache_kidx, state_conv, state_ssm, norm_mix_g, w_in, conv_w, conv_b, dt_bias, A_log, D_skip, ssm_norm_g, w_ssm_out, w_attn_out, w_o, norm_moe_g, w_router_group, b_router_group, w_router_expert, b_router_expert, w1, w3, w2, norm_final_g):
    p = dict(norm_mix_g=norm_mix_g, conv_w=conv_w, conv_b=conv_b, dt_bias=dt_bias, A_log=A_log,
             D_skip=D_skip, ssm_norm_g=ssm_norm_g, w_ssm_out=w_ssm_out, w_attn_out=w_attn_out,
             w_o=w_o, norm_moe_g=norm_moe_g, w_router_group=w_router_group,
             b_router_group=b_router_group, w_router_expert=w_router_expert,
             b_router_expert=b_router_expert, w1=w1, w3=w3, w2=w2, norm_final_g=norm_final_g)
    w_packed = _pack_w_in(w_in)
    yp, kp, vp, kip, cp, sp = _stream_step(x_prompt, None, None, None, None, None, w_packed, p)
    ys, ksn, vsn, kisn, csn, ssn = _stream_step(x_sample, state_conv, state_ssm, cache_k,
                                                cache_v, cache_kidx, w_packed, p)
    return (yp, ys, kp, vp, kip, cp, sp, ksn, vsn, kisn, csn, ssn)
```

```python
import functools

import numpy as np
import jax
import jax.numpy as jnp
from jax import lax
from jax.experimental import pallas as pl
from jax.experimental.pallas import tpu as pltpu

F32 = jnp.float32
BF16 = jnp.bfloat16
I32 = jnp.int32

D_MODEL = 1024
CHUNK = 64
CHUNK_SHIFT = 6
EPS = 1e-6
SSM_INNER = 2048
SSM_HEADDIM = 64
SSM_HEADS = 32
SSM_GROUPS = 4
SSM_HEADS_PER_GROUP = SSM_HEADS // SSM_GROUPS
SSM_STATE = 128
CONV_WIDTH = 4
CONV_DIM = SSM_INNER + 2 * SSM_GROUPS * SSM_STATE
CONV_CARRY = 16
SSD_CHUNKS_PER_STEP = 2
ATTN_HEADS = 16
ATTN_KV_HEADS = 2
HEAD_DIM = 64
ATTN_REP = ATTN_HEADS // ATTN_KV_HEADS
ATTN_WIDTH = ATTN_HEADS * HEAD_DIM
IDX_HEADS = 8
IDX_DIM = 64
IDX_SCALE = (IDX_HEADS * IDX_DIM) ** -0.5
TOPK_MAX = 256
MOE_GROUPS = 4
EXPERTS_PER_GROUP = 8
EXPERT_GROUP_SHIFT = 3
N_EXPERTS = 32
EXPERT_FF = 256
N_BRANCHES = 2

LANE = 128
VMEM_LIMIT = 56 * 1024 * 1024
PROJ_TILE = 1024
PROJ_TILES = 9
Q_BLOCK = 128
SMALL_Q_BLOCK = 16
KEY_TILE = 256
KEY_INDEX_BITS = 15
COUNT_CHAINS = 4
EXPERTS_PER_STEP = 4
MOE_ROW_BLOCK = 128
MERGE_TILE = 512
INT_MIN = np.int32(-2 ** 31)
NEG_BIG = -1e30
Q_SCALE = HEAD_DIM ** -0.5 * float(np.log2(np.e))
PV_ROWS = 80

NT_DIMS = (((1,), (1,)), ((), ()))
TN_DIMS = (((0,), (0,)), ((), ()))


def _sigmoid(x):
    return 1.0 / (1.0 + jnp.exp(-x))


def _silu(x):
    h = 0.5 * x
    return h + h * jnp.tanh(h)


def _split3(x):
    hi = x.astype(BF16)
    r1 = x - hi.astype(F32)
    mid = r1.astype(BF16)
    r2 = r1 - mid.astype(F32)
    lo = r2.astype(BF16)
    return jnp.concatenate([hi, mid, lo], axis=1)


def _inproj_kernel(x_ref, g_ref, w_ref, z_ref, xbc_ref, q_ref, gate_ref, k_ref, v_ref,
                   qi_ref, ki_ref, dtw_ref, kb_ref, vb_ref, kib_ref, h_ref):
    j = pl.program_id(1)

    @pl.when(j == 0)
    def _():
        x = x_ref[...]
        ms = jnp.mean(x * x, axis=-1, keepdims=True)
        h_ref[...] = ((x * lax.rsqrt(ms + EPS)) * g_ref[...]).astype(BF16)

    def tile():
        return jnp.dot(h_ref[...], w_ref[...], preferred_element_type=F32)

    @pl.when(j < 2)
    def _():
        z_ref[...] = tile().astype(BF16)

    @pl.when((j >= 2) & (j < 5))
    def _():
        xbc_ref[...] = tile().astype(BF16)

    @pl.when(j == 5)
    def _():
        q_ref[...] = (tile() * Q_SCALE).astype(BF16)

    @pl.when((j >= 6) & (j < 8))
    def _():
        gate_ref[...] = tile().astype(BF16)

    @pl.when(j == 8)
    def _():
        t = tile()
        k_ref[...] = t[:, 0:128]
        v_ref[...] = t[:, 128:256]
        qi_ref[...] = t[:, 256:768].astype(BF16)
        ki_ref[...] = t[:, 768:832]
        dtw_ref[...] = t[:, 896:1024]
        kb_ref[...] = t[:, 0:128].astype(BF16)
        vb_ref[...] = t[:, 128:256].astype(BF16)
        kib_ref[...] = t[:, 768:832].astype(BF16)


def _pack_w_in(w_in):
    sizes = (SSM_INNER, CONV_DIM, SSM_HEADS, ATTN_WIDTH, ATTN_KV_HEADS * HEAD_DIM,
             ATTN_KV_HEADS * HEAD_DIM, IDX_HEADS * IDX_DIM, IDX_DIM, IDX_HEADS,
             N_BRANCHES * D_MODEL)
    offs = np.concatenate([[0], np.cumsum(sizes)])
    z, xbc, dt, q, k, v, qi, ki, wi, g = [w_in[:, offs[i]:offs[i + 1]] for i in range(10)]
    zeros = lambda n: jnp.zeros((D_MODEL, n), w_in.dtype)
    cols = [z, xbc, q, g, k, v, qi, ki, zeros(64), dt, wi, zeros(LANE - SSM_HEADS - IDX_HEADS)]
    return jnp.concatenate(cols, axis=1).astype(BF16)


def _in_proj(x2d, norm_g, w_packed, tm):
    n = x2d.shape[0]
    assert n % tm == 0
    bf = lambda c: jax.ShapeDtypeStruct((n, c), BF16)
    ff = lambda c: jax.ShapeDtypeStruct((n, c), F32)
    clip = lambda j, lo, hi: jnp.clip(j - lo, 0, hi - lo - 1)
    return pl.pallas_call(
        _inproj_kernel,
        out_shape=(bf(SSM_INNER), bf(CONV_DIM), bf(ATTN_WIDTH), bf(2 * D_MODEL),
                   ff(128), ff(128), bf(512), ff(IDX_DIM), ff(128), bf(128), bf(128), bf(IDX_DIM)),
        grid=(n // tm, PROJ_TILES),
        in_specs=[
            pl.BlockSpec((tm, D_MODEL), lambda i, j: (i, 0)),
            pl.BlockSpec((1, D_MODEL), lambda i, j: (0, 0)),
            pl.BlockSpec((D_MODEL, PROJ_TILE), lambda i, j: (0, j)),
        ],
        out_specs=(
            pl.BlockSpec((tm, PROJ_TILE), lambda i, j: (i, clip(j, 0, 2))),
            pl.BlockSpec((tm, PROJ_TILE), lambda i, j: (i, clip(j, 2, 5))),
            pl.BlockSpec((tm, PROJ_TILE), lambda i, j: (i, 0)),
            pl.BlockSpec((tm, PROJ_TILE), lambda i, j: (i, clip(j, 6, 8))),
            pl.BlockSpec((tm, 128), lambda i, j: (i, 0)),
            pl.BlockSpec((tm, 128), lambda i, j: (i, 0)),
            pl.BlockSpec((tm, 512), lambda i, j: (i, 0)),
            pl.BlockSpec((tm, IDX_DIM), lambda i, j: (i, 0)),
            pl.BlockSpec((tm, 128), lambda i, j: (i, 0)),
            pl.BlockSpec((tm, 128), lambda i, j: (i, 0)),
            pl.BlockSpec((tm, 128), lambda i, j: (i, 0)),
            pl.BlockSpec((tm, IDX_DIM), lambda i, j: (i, 0)),
        ),
        scratch_shapes=[pltpu.VMEM((tm, D_MODEL), BF16)],
        compiler_params=pltpu.CompilerParams(
            dimension_semantics=("arbitrary", "arbitrary"), vmem_limit_bytes=VMEM_LIMIT),
        name="in_proj",
    )(x2d, norm_g.reshape(1, D_MODEL), w_packed)


def _ssd_kernel(*refs, L, cps, nsteps, has_state):
    if has_state:
        (xbc_ref, z_ref, dtw_ref, cs_ref, s0_ref, shift_ref, convw_ref, convb_ref, dtb_ref, a_ref,
         dx_ref, ng_ref, e3p_ref, e3l_ref, tri_ref, bd_ref, yn_ref, cnew_ref, sout_ref,
         xe_ref, st, act_ref, ex_ref, yz_ref, acl_ref) = refs
    else:
        (xbc_ref, z_ref, dtw_ref, shift_ref, convw_ref, convb_ref, dtb_ref, a_ref,
         dx_ref, ng_ref, e3p_ref, e3l_ref, tri_ref, bd_ref, yn_ref, cnew_ref, sout_ref,
         xe_ref, st, act_ref, ex_ref, yz_ref, acl_ref) = refs
    c = pl.program_id(1)
    G, R, N, P = SSM_GROUPS, SSM_HEADS_PER_GROUP, SSM_STATE, SSM_HEADDIM
    GW = R * P
    RL = R * L
    LB = cps * L

    @pl.when(c == 0)
    def _():
        xe_ref[LB:, :] = jnp.zeros((xe_ref.shape[0] - LB, CONV_DIM), BF16)
        if has_state:
            xe_ref[LB:LB + CONV_CARRY, :] = cs_ref[0]
            st[...] = s0_ref[0].T
        else:
            st[...] = jnp.zeros(st.shape, F32)

    xe_ref[0:LB, :] = xbc_ref[0]
    shifted = jnp.dot(shift_ref[0], xe_ref[...], preferred_element_type=F32)
    w = convw_ref[...]
    conv = convb_ref[...] + shifted[2 * LB:3 * LB, :] * w[0:1, :]
    conv = conv + shifted[LB:2 * LB, :] * w[1:2, :]
    conv = conv + shifted[0:LB, :] * w[2:3, :]
    conv = conv + xbc_ref[0].astype(F32) * w[3:4, :]
    act_ref[...] = _silu(conv)

    @pl.when(c == nsteps - 1)
    def _():
        cnew_ref[0] = xbc_ref[0, LB - CONV_CARRY:LB, :].astype(F32)[
            CONV_CARRY - (CONV_WIDTH - 1):CONV_CARRY, :]

    if nsteps > 1:
        xe_ref[LB:LB + CONV_CARRY, :] = xe_ref[LB - CONV_CARRY:LB, :]

    u = dtw_ref[0] + dtb_ref[...]
    dt = jnp.maximum(u, 0.0) + jnp.log1p(jnp.exp(-jnp.abs(u)))
    a = dt * a_ref[...]
    ac3 = jnp.dot(tri_ref[...], _split3(a), preferred_element_type=F32)
    a_cum = ac3[:, 0:LANE] + ac3[:, LANE:2 * LANE] + ac3[:, 2 * LANE:]

    ex_ref[...] = jnp.dot(_split3(jnp.concatenate([a_cum, dt], axis=0)), e3p_ref[...],
                          preferred_element_type=F32)
    if L != P:
        acl_ref[...] = jnp.dot(_split3(a_cum), e3l_ref[...], preferred_element_type=F32)
    row = lax.broadcasted_iota(I32, (L, RL), 0)
    scol = lax.broadcasted_iota(I32, (L, RL), 1) & (L - 1)

    for sub in range(cps):
        r0 = sub * L
        rows = slice(r0, r0 + L)
        ssq = jnp.zeros((L, 1), F32)
        for g in range(G):
            sl = slice(g * GW, (g + 1) * GW)
            xs = act_ref[rows, sl]
            b_g = act_ref[rows, SSM_INNER + g * N:SSM_INNER + (g + 1) * N].astype(BF16)
            c_g = act_ref[rows, SSM_INNER + (G + g) * N:SSM_INNER + (G + g + 1) * N].astype(BF16)
            acx = ex_ref[rows, sl]
            alast = ex_ref[r0 + L - 1:r0 + L, sl]
            xdt = xs * ex_ref[LB + r0:LB + r0 + L, sl]
            acl_g = acx if L == P else acl_ref[rows, g * RL:(g + 1) * RL]
            acs_g = jnp.sum(jnp.where(scol == row, acl_g, 0.0), axis=0, keepdims=True)
            decay = jnp.where(scol <= row, jnp.exp(acl_g - acs_g), 0.0)
            cb = lax.dot_general(c_g, jnp.concatenate([b_g] * R, axis=0), NT_DIMS,
                                 preferred_element_type=F32)
            m_g = (cb * decay).astype(BF16)
            xbd = jnp.concatenate([xdt.astype(BF16)] * R, axis=0) * bd_ref[...]
            y_in = jnp.dot(m_g, xbd, preferred_element_type=F32)
            st_g = st[:, sl]
            y_st = jnp.dot(c_g, st_g.astype(BF16), preferred_element_type=F32)
            xw = (xdt * jnp.exp(alast - acx)).astype(BF16)
            upd = lax.dot_general(b_g, xw, TN_DIMS, preferred_element_type=F32)
            st[:, sl] = st_g * jnp.exp(alast) + upd
            y = y_in + jnp.exp(acx) * y_st + dx_ref[:, sl] * xs
            yz = y * _silu(z_ref[0, rows, sl].astype(F32))
            ssq = ssq + jnp.sum(yz * yz, axis=-1, keepdims=True)
            yz_ref[rows, sl] = yz
        scale = lax.rsqrt(ssq * (1.0 / SSM_INNER) + EPS)
        yn_ref[0, rows, :] = ((yz_ref[rows, :] * scale) * ng_ref[...]).astype(BF16)

    @pl.when(c == nsteps - 1)
    def _():
        sout_ref[0] = st[...].T


def _shift_matrices(L, conv_rows):
    out = np.zeros((2, 3 * L, conv_rows), np.float32)
    for k in range(CONV_WIDTH - 1):
        for t in range(L):
            src = t - (k + 1)
            if src >= 0:
                out[:, k * L + t, src] = 1.0
            else:
                out[1, k * L + t, L + CONV_CARRY + src] = 1.0
                j = CONV_WIDTH - 1 + src
                for piece in range(3):
                    out[0, k * L + t, L + 3 * piece + j] = 1.0
    return jnp.asarray(out, BF16)


def _expand_matrix(width_per_head):
    r = np.arange(3 * LANE)[:, None] % LANE
    c = np.arange(SSM_HEADS * width_per_head)[None, :] // width_per_head
    return jnp.asarray((r == c).astype(np.float32), BF16)


def _ssd(xbc, z, dtw, conv_state, ssm_state, conv_w, conv_b, dt_bias, a_log, d_skip, ssm_norm_g):
    b, t, _ = xbc.shape
    L = min(CHUNK, t)
    nc = t // L
    assert t % L == 0 and L & (L - 1) == 0
    has_state = conv_state is not None
    pad = lambda v: jnp.pad(v.astype(F32), (0, LANE - SSM_HEADS)).reshape(1, LANE)
    a_neg = pad(-jnp.exp(a_log.astype(F32)))
    cps = SSD_CHUNKS_PER_STEP if nc % SSD_CHUNKS_PER_STEP == 0 else 1
    nsteps = nc // cps
    lb = cps * L
    conv_rows = _round_up(lb + CONV_CARRY, LANE)
    tri = jnp.asarray(np.kron(np.eye(cps), np.tril(np.ones((L, L)))).astype(np.float32), BF16)
    rl = SSM_HEADS_PER_GROUP * L
    gw = SSM_HEADS_PER_GROUP * SSM_HEADDIM
    bd = jnp.asarray((np.arange(rl)[:, None] // L == np.arange(gw)[None, :] // SSM_HEADDIM)
                     .astype(np.float32), BF16)
    const = lambda shape: pl.BlockSpec(shape, lambda i, j: (0,) * len(shape))
    tok = lambda w: pl.BlockSpec((1, lb, w), lambda i, j: (i, j, 0))
    per_b = lambda s: pl.BlockSpec((1,) + s, lambda i, j: (i, 0, 0))
    ins = [xbc, z, dtw]
    specs = [tok(CONV_DIM), tok(SSM_INNER), tok(LANE)]
    if has_state:
        triples = _split3(conv_state.astype(F32))
        ins += [_pad_axis(triples, 1, CONV_CARRY), ssm_state.reshape(b, SSM_INNER, SSM_STATE)]
        specs += [per_b((CONV_CARRY, CONV_DIM)), per_b((SSM_INNER, SSM_STATE))]
    first = 0 if has_state else 1
    ins += [_shift_matrices(lb, conv_rows)]
    specs += [pl.BlockSpec((1, 3 * lb, conv_rows),
                           lambda i, j: (jnp.where(j == 0, first, 1), 0, 0))]
    ins += [conv_w, conv_b.reshape(1, CONV_DIM), pad(dt_bias), a_neg,
            jnp.repeat(d_skip.astype(F32), SSM_HEADDIM).reshape(1, SSM_INNER),
            ssm_norm_g.reshape(1, SSM_INNER), _expand_matrix(SSM_HEADDIM), _expand_matrix(L),
            tri, bd]
    specs += [const((CONV_WIDTH, CONV_DIM)), const((1, CONV_DIM)), const((1, LANE)),
              const((1, LANE)), const((1, SSM_INNER)), const((1, SSM_INNER)),
              const((3 * LANE, SSM_INNER)), const((3 * LANE, SSM_HEADS * L)),
              const((lb, lb)), const((rl, gw))]
    yn, cnew, sout = pl.pallas_call(
        functools.partial(_ssd_kernel, L=L, cps=cps, nsteps=nsteps, has_state=has_state),
        out_shape=(jax.ShapeDtypeStruct((b, t, SSM_INNER), BF16),
                   jax.ShapeDtypeStruct((b, CONV_WIDTH - 1, CONV_DIM), F32),
                   jax.ShapeDtypeStruct((b, SSM_INNER, SSM_STATE), F32)),
        grid=(b, nsteps),
        in_specs=specs,
        out_specs=(tok(SSM_INNER), per_b((CONV_WIDTH - 1, CONV_DIM)),
                   per_b((SSM_INNER, SSM_STATE))),
        scratch_shapes=[pltpu.VMEM((conv_rows, CONV_DIM), BF16),
                        pltpu.VMEM((SSM_STATE, SSM_INNER), F32),
                        pltpu.VMEM((lb, CONV_DIM), F32),
                        pltpu.VMEM((2 * lb, SSM_INNER), F32),
                        pltpu.VMEM((lb, SSM_INNER), F32),
                        pltpu.VMEM((lb, SSM_HEADS * L), F32)],
        compiler_params=pltpu.CompilerParams(
            dimension_semantics=("arbitrary", "arbitrary"), vmem_limit_bytes=VMEM_LIMIT),
        name="ssd",
    )(*ins)
    return yn, cnew, sout.reshape(b, SSM_HEADS, SSM_HEADDIM, SSM_STATE)


def _dsa_kernel(q_ref, qi_ref, w_ref, eq_ref, kidx_ref, k_ref, v_ref, o_ref,
                qs_ref, qx_ref, key_ref, kx_ref, vx_ref, m_ref, acc_ref, mlim_ref, s_ref,
                *, past_len, l_valid, topk, nq):
    qb = pl.program_id(1)
    TQ, TK = LANE, KEY_TILE
    W = ATTN_REP * nq
    wide = nq == LANE

    lane = lax.broadcasted_iota(I32, (1, TQ), 1)
    pos = past_len + qb * nq + (lane & (nq - 1))
    n_adm = jnp.minimum(((pos >> CHUNK_SHIFT) + 1) * CHUNK, l_valid)
    k_eff = jnp.minimum(n_adm, topk)
    last_pos = past_len + qb * nq + nq - 1
    n_max = jnp.minimum(((last_pos >> CHUNK_SHIFT) + 1) * CHUNK, l_valid)
    nt = (n_max + TK - 1) // TK
    ntp = 2 * ((nt + 1) // 2)
    lp = key_ref.shape[0]

    @pl.when(qb == 0)
    def _():
        kx_ref[:, LANE:] = k_ref[0]

        def v_tile(t, carry):
            s0 = pl.multiple_of(t * TK, TK)
            v_t = v_ref[0, pl.ds(s0, TK), :].astype(F32).T
            for g in range(ATTN_KV_HEADS):
                vx_ref[g, 0:HEAD_DIM, pl.ds(s0, TK)] = (
                    v_t[g * HEAD_DIM:(g + 1) * HEAD_DIM, :].astype(BF16))
            return carry

        lax.fori_loop(0, lp // TK, v_tile, 0)
        tail_row = lax.broadcasted_iota(I32, (PV_ROWS - HEAD_DIM, lp), 0)
        for g in range(ATTN_KV_HEADS):
            vx_ref[g, HEAD_DIM:, :] = jnp.where(tail_row == 0, 1.0, 0.0).astype(BF16)
            qx_ref[g, :, 0:LANE] = eq_ref[...]
            qx_ref[g, :, LANE:] = jnp.zeros((W, LANE), BF16)

    if wide:
        for h in range(IDX_HEADS):
            qs_ref[h * nq:(h + 1) * nq, :] = qi_ref[0, :, h * IDX_DIM:(h + 1) * IDX_DIM]
        w_t = w_ref[0].T[SSM_HEADS:SSM_HEADS + IDX_HEADS, :] * IDX_SCALE
    else:
        qs_ref[...] = qi_ref[0]
        w_row = w_ref[0] * IDX_SCALE

    def index_dots(t, slot):
        s0 = pl.multiple_of(jnp.minimum(t, nt - 1) * TK, TK)
        s_ref[0, slot] = lax.dot_general(kidx_ref[0, pl.ds(s0, TK), :], qs_ref[...], NT_DIMS,
                                         preferred_element_type=F32)

    def score_tile(t, slot):
        s0 = pl.multiple_of(t * TK, TK)
        d = s_ref[0, slot]
        if wide:
            sc = w_t[0:1, :] * jnp.maximum(d[:, 0:nq], 0.0)
            for h in range(1, IDX_HEADS):
                sc = sc + w_t[h:h + 1, :] * jnp.maximum(d[:, h * nq:(h + 1) * nq], 0.0)
        else:
            sc = w_row * jnp.maximum(d, 0.0)
            for shift in [nq << i for i in reversed(range(IDX_HEADS.bit_length() - 1))]:
                sc = sc + pltpu.roll(sc, shift, axis=1)
        sc = sc + 0.0
        bits = pltpu.bitcast(sc, I32)
        key = jnp.where(bits < 0, bits ^ np.int32(0x7FFFFFFF), bits)
        s_idx = s0 + lax.broadcasted_iota(I32, (TK, TQ), 0)
        key_ref[pl.ds(s0, TK), :] = jnp.where(s_idx < n_adm, key, INT_MIN)

    index_dots(0, 0)

    def score_pair(u, carry):
        index_dots(2 * u + 1, 1)
        score_tile(2 * u, 0)
        index_dots(2 * u + 2, 0)
        score_tile(2 * u + 1, 1)
        return carry

    lax.fori_loop(0, nt // 2, score_pair, 0)

    @pl.when(nt < ntp)
    def _():
        score_tile(nt - 1, 0)
        key_ref[pl.ds(pl.multiple_of(nt * TK, TK), TK), :] = jnp.full((TK, TQ), INT_MIN, I32)

    def count(pred):
        n_vregs = 2 * TK // 8

        def body(t, accs):
            s0 = pl.multiple_of(t * (2 * TK), 2 * TK)
            blk = key_ref[pl.ds(s0, 2 * TK), :].reshape(n_vregs, 8, TQ)
            s_idx = s0 + lax.broadcasted_iota(I32, (2 * TK, TQ), 0).reshape(n_vregs, 8, TQ)
            accs = list(accs)
            for r in range(n_vregs):
                a = accs[r % COUNT_CHAINS]
                accs[r % COUNT_CHAINS] = jnp.where(pred(blk[r], s_idx[r]), a + 1, a)
            return tuple(accs)

        zero = jnp.zeros((8, TQ), I32)
        accs = lax.fori_loop(0, ntp // 2, body, (zero,) * COUNT_CHAINS)
        acc = accs[0]
        for a in accs[1:]:
            acc = acc + a
        return jnp.sum(acc, axis=0, keepdims=True)

    def bit_step(i, prefix):
        cand = prefix | jnp.left_shift(jnp.int32(1), 31 - i)
        cand_s = cand ^ INT_MIN
        cnt = count(lambda blk, s_idx: blk >= cand_s)
        return jnp.where(cnt >= k_eff, cand, prefix)

    thr = lax.fori_loop(0, 32, bit_step, jnp.zeros((1, TQ), I32)) ^ INT_MIN

    n_gt = count(lambda blk, s_idx: blk > thr)
    n_eq = count(lambda blk, s_idx: blk == thr)
    need = k_eff - n_gt
    mlim_ref[...] = jnp.full((1, TQ), 1 << KEY_INDEX_BITS, I32)

    @pl.when(jnp.max(n_eq - need) > 0)
    def _():
        def idx_step(i, prefix):
            cand = prefix | jnp.left_shift(jnp.int32(1), KEY_INDEX_BITS - 1 - i)
            cnt = count(lambda blk, s_idx: (blk == thr) & (s_idx < cand))
            return jnp.where(cnt < need, cand, prefix)
        mlim_ref[...] = lax.fori_loop(0, KEY_INDEX_BITS, idx_step, jnp.zeros((1, TQ), I32))

    mlim = mlim_ref[...]

    def bias_tile(t, carry):
        s0 = pl.multiple_of(t * TK, TK)
        blk = key_ref[pl.ds(s0, TK), :]
        s_idx = s0 + lax.broadcasted_iota(I32, (TK, TQ), 0)
        sel = (blk > thr) | ((blk == thr) & (s_idx <= mlim))
        kx_ref[pl.ds(s0, TK), 0:LANE] = jnp.where(sel, 0.0, NEG_BIG).astype(BF16)
        return carry

    lax.fori_loop(0, nt, bias_tile, 0)

    for g in range(ATTN_KV_HEADS):
        lanes = slice(LANE + g * HEAD_DIM, LANE + (g + 1) * HEAD_DIM)
        if wide:
            for r in range(ATTN_REP):
                hq = g * ATTN_REP + r
                qx_ref[g, r * nq:(r + 1) * nq, lanes] = (
                    q_ref[0, :, hq * HEAD_DIM:(hq + 1) * HEAD_DIM])
        else:
            qx_ref[g, :, lanes] = q_ref[0, g]
    m_ref[...] = jnp.full(m_ref.shape, -jnp.inf, F32)
    acc_ref[...] = jnp.zeros(acc_ref.shape, F32)

    def scores(t, slot):
        s0 = pl.multiple_of(jnp.minimum(t, nt - 1) * TK, TK)
        k_tile = kx_ref[pl.ds(s0, TK), :]
        for g in range(ATTN_KV_HEADS):
            s_ref[g, slot] = lax.dot_general(k_tile, qx_ref[g], NT_DIMS,
                                             preferred_element_type=F32)

    def accumulate(t, slot):
        s0 = pl.multiple_of(t * TK, TK)
        for g in range(ATTN_KV_HEADS):
            s = s_ref[g, slot]
            m_prev = m_ref[g]
            m_new = jnp.maximum(m_prev, jnp.max(s, axis=0, keepdims=True))
            p = jnp.exp2(s - m_new)
            acc_ref[g] = jnp.exp2(m_prev - m_new) * acc_ref[g] + jnp.dot(
                vx_ref[g, :, pl.ds(s0, TK)], p.astype(BF16), preferred_element_type=F32)
            m_ref[g] = m_new

    scores(0, 0)

    def attn_pair(u, carry):
        scores(2 * u + 1, 1)
        accumulate(2 * u, 0)
        scores(2 * u + 2, 0)
        accumulate(2 * u + 1, 1)
        return carry

    lax.fori_loop(0, nt // 2, attn_pair, 0)

    @pl.when(nt < ntp)
    def _():
        accumulate(nt - 1, 0)

    for g in range(ATTN_KV_HEADS):
        acc = acc_ref[g]
        o_t = acc[0:HEAD_DIM, :] * (1.0 / acc[HEAD_DIM:HEAD_DIM + 1, :])
        if wide:
            for r in range(0, ATTN_REP, 2):
                hq = g * ATTN_REP + r
                pair = jnp.concatenate([o_t[:, r * nq:(r + 1) * nq],
                                        o_t[:, (r + 1) * nq:(r + 2) * nq]], axis=0)
                o_ref[0, :, hq * HEAD_DIM:(hq + 2) * HEAD_DIM] = pair.T.astype(BF16)
        else:
            o_rq = jnp.concatenate([o_t, o_t], axis=0).T[:, 0:HEAD_DIM].astype(BF16)
            for r in range(ATTN_REP):
                hq = g * ATTN_REP + r
                o_ref[0, :, hq * HEAD_DIM:(hq + 1) * HEAD_DIM] = o_rq[r * nq:(r + 1) * nq, :]


def _dsa(q, qi, w, kidx_all, k_all, v_all, past_len, l_valid, t):
    b = kidx_all.shape[0]
    lp = kidx_all.shape[1]
    nq = Q_BLOCK if t % Q_BLOCK == 0 else SMALL_Q_BLOCK
    assert t % nq == 0 and lp % (2 * KEY_TILE) == 0 and lp < 1 << KEY_INDEX_BITS
    topk = min(TOPK_MAX, l_valid // 4)
    width = ATTN_REP * nq
    if nq == Q_BLOCK:
        qblk = lambda c: pl.BlockSpec((1, nq, c), lambda i, j: (i, j, 0))
        q_specs = [qblk(ATTN_WIDTH), qblk(IDX_HEADS * IDX_DIM), qblk(LANE)]
    else:
        assert t == nq
        q_specs = [pl.BlockSpec((1, ATTN_KV_HEADS, width, HEAD_DIM), lambda i, j: (i, 0, 0, 0)),
                   pl.BlockSpec((1, IDX_HEADS * nq, IDX_DIM), lambda i, j: (i, 0, 0)),
                   pl.BlockSpec((1, 1, LANE), lambda i, j: (i, 0, 0))]
    slot_onehot = jnp.asarray(
        (np.arange(width)[:, None] % nq == np.arange(LANE)[None, :]).astype(np.float32), BF16)
    per_stream = lambda c: pl.BlockSpec((1, lp, c), lambda i, j: (i, 0, 0))
    return pl.pallas_call(
        functools.partial(_dsa_kernel, past_len=past_len, l_valid=l_valid, topk=topk, nq=nq),
        out_shape=jax.ShapeDtypeStruct((b, t, ATTN_WIDTH), BF16),
        grid=(b, t // nq),
        in_specs=q_specs + [pl.BlockSpec((width, LANE), lambda i, j: (0, 0)),
                            per_stream(IDX_DIM), per_stream(LANE), per_stream(LANE)],
        out_specs=pl.BlockSpec((1, nq, ATTN_WIDTH), lambda i, j: (i, j, 0)),
        scratch_shapes=[pltpu.VMEM((width, IDX_DIM), BF16),
                        pltpu.VMEM((ATTN_KV_HEADS, width, 2 * LANE), BF16),
                        pltpu.VMEM((lp, LANE), I32),
                        pltpu.VMEM((lp, 2 * LANE), BF16),
                        pltpu.VMEM((ATTN_KV_HEADS, PV_ROWS, lp), BF16),
                        pltpu.VMEM((ATTN_KV_HEADS, 1, width), F32),
                        pltpu.VMEM((ATTN_KV_HEADS, PV_ROWS, width), F32),
                        pltpu.VMEM((1, LANE), I32),
                        pltpu.VMEM((ATTN_KV_HEADS, 2, KEY_TILE, width), F32)],
        compiler_params=pltpu.CompilerParams(
            dimension_semantics=("arbitrary", "arbitrary"), vmem_limit_bytes=VMEM_LIMIT),
        name="dsa",
    )(q, qi, w, slot_onehot, kidx_all, k_all, v_all)


def _merge_kernel(x_ref, yn_ref, o_ref, gate_ref, wa_ref, wb_ref, wo_ref, out_ref):
    br_a = jnp.dot(yn_ref[...], wa_ref[...], preferred_element_type=F32)
    br_b = jnp.dot(o_ref[...], wb_ref[...], preferred_element_type=F32)
    gates = gate_ref[...].astype(F32)
    merged = _sigmoid(gates[:, :D_MODEL]) * br_a + _sigmoid(gates[:, D_MODEL:]) * br_b
    out_ref[...] = x_ref[...] + jnp.dot(merged.astype(BF16), wo_ref[...],
                                        preferred_element_type=F32)


def _merge(x2d, yn, o, gate, w_ssm_out, w_attn_out, w_o, tm):
    n = x2d.shape[0]
    assert n % tm == 0
    row = lambda w: pl.BlockSpec((tm, w), lambda i: (i, 0))
    full = lambda r, c: pl.BlockSpec((r, c), lambda i: (0, 0))
    return pl.pallas_call(
        _merge_kernel,
        out_shape=jax.ShapeDtypeStruct((n, D_MODEL), F32),
        grid=(n // tm,),
        in_specs=[row(D_MODEL), row(SSM_INNER), row(ATTN_WIDTH), row(2 * D_MODEL),
                  full(SSM_INNER, D_MODEL), full(ATTN_WIDTH, D_MODEL), full(D_MODEL, D_MODEL)],
        out_specs=row(D_MODEL),
        compiler_params=pltpu.CompilerParams(
            dimension_semantics=("arbitrary",), vmem_limit_bytes=VMEM_LIMIT),
        name="merge",
    )(x2d, yn, o, gate, w_ssm_out.astype(BF16), w_attn_out.astype(BF16), w_o.astype(BF16))


def _moe_kernel(x_ref, ng_ref, wr_ref, br_ref, w1_ref, w3_ref, w2_ref, fg_ref, tri_ref, out_ref,
                hs_ref, combs_ref, acc_ref, pt_ref, seg_ref, *, n_steps, blk_rows):
    e_step = pl.program_id(1)
    tm = x_ref.shape[0]
    sr = hs_ref.shape[0]
    BLK = blk_rows

    @pl.when(e_step == 0)
    def _():
        x = x_ref[...]
        ms = jnp.mean(x * x, axis=-1, keepdims=True)
        h = ((x * lax.rsqrt(ms + EPS)) * ng_ref[...]).astype(BF16)
        logits = jnp.dot(h, wr_ref[...], preferred_element_type=F32) + br_ref[...]
        lane_i = lax.broadcasted_iota(I32, (tm, LANE), 1)
        lane = lane_i.astype(F32)
        first = lambda hit: jnp.min(jnp.where(hit, lane, float(LANE)), axis=1, keepdims=True)
        is_g = (lane_i >= N_EXPERTS) & (lane_i < N_EXPERTS + MOE_GROUPS)
        glog = jnp.where(is_g, logits, -jnp.inf)
        gmax = jnp.max(glog, axis=1, keepdims=True)
        gsel = first(glog == gmax) - float(N_EXPERTS)
        p_group = 1.0 / jnp.sum(jnp.exp(glog - gmax), axis=1, keepdims=True)
        grp = (lane_i >> EXPERT_GROUP_SHIFT).astype(F32)
        in_grp = (lane_i < N_EXPERTS) & (grp == gsel)
        el = jnp.where(in_grp, logits, -jnp.inf)
        v1 = jnp.max(el, axis=1, keepdims=True)
        i1 = first(el == v1)
        el2 = jnp.where(lane == i1, -jnp.inf, el)
        v2 = jnp.max(el2, axis=1, keepdims=True)
        i2 = first(el2 == v2)
        e21 = jnp.exp(v2 - v1)
        den = 1.0 + e21
        comb = (jnp.where(lane == i1, (1.0 / den) * p_group, 0.0)
                + jnp.where(lane == i2, (e21 / den) * p_group, 0.0))

        own = lane == gsel
        own_f = jnp.where(own, 1.0, 0.0)
        before = jnp.dot(tri_ref[...], own_f.astype(BF16), preferred_element_type=F32)
        rank = jnp.sum(jnp.where(own, before, 0.0), axis=1, keepdims=True)
        counts = jnp.sum(own_f, axis=0, keepdims=True)
        blocks = jnp.floor((counts + (BLK - 1)) * (1.0 / BLK))
        upper = (lax.broadcasted_iota(I32, (LANE, LANE), 0)
                 < lax.broadcasted_iota(I32, (LANE, LANE), 1))
        starts = jnp.dot(jnp.broadcast_to(blocks, (8, LANE)).astype(BF16),
                         jnp.where(upper, 1.0, 0.0).astype(BF16),
                         preferred_element_type=F32)[0:1, :] * BLK
        dest = jnp.sum(jnp.where(own, starts, 0.0), axis=1, keepdims=True) + rank
        for g in range(MOE_GROUPS):
            seg_ref[g] = jnp.sum(starts[:, g:g + 1]).astype(I32)
            seg_ref[MOE_GROUPS + g] = jnp.sum(blocks[:, g:g + 1]).astype(I32)

        dest_i = dest.astype(I32)
        pt = jnp.where(dest_i == lax.broadcasted_iota(I32, (tm, sr), 1), 1.0, 0.0).astype(BF16)
        pt_ref[...] = pt
        dest_row = jnp.broadcast_to(dest, (tm, LANE)).T[0:1, :].astype(I32)
        p_mat = jnp.where(lax.broadcasted_iota(I32, (sr, tm), 0) == dest_row, 1.0, 0.0
                          ).astype(BF16)
        hs_ref[...] = jnp.dot(p_mat, h, preferred_element_type=F32).astype(BF16)
        c3 = jnp.dot(p_mat, _split3(comb), preferred_element_type=F32)
        combs_ref[...] = c3[:, 0:LANE] + c3[:, LANE:2 * LANE] + c3[:, 2 * LANE:]
        acc_ref[...] = jnp.zeros(acc_ref.shape, F32)

    group = e_step // (EXPERTS_PER_GROUP // EXPERTS_PER_STEP)
    seg_start = seg_ref[group]

    def expert_rows(first_blk, n_rows):
        rows = pl.ds(pl.multiple_of(seg_start + first_blk * BLK, BLK), n_rows)
        h = hs_ref[rows, :]
        comb = combs_ref[rows, :]
        lane = lax.broadcasted_iota(I32, (n_rows, LANE), 1)
        for k in range(EXPERTS_PER_STEP):
            e = e_step * EXPERTS_PER_STEP + k
            a1 = jnp.dot(h, w1_ref[k], preferred_element_type=F32)
            a3 = jnp.dot(h, w3_ref[k], preferred_element_type=F32)
            he = (_silu(a1) * a3).astype(BF16)
            ye = jnp.dot(he, w2_ref[k], preferred_element_type=F32)
            wcol = jnp.sum(jnp.where(lane == e, comb, 0.0), axis=1, keepdims=True)
            acc_ref[rows, :] += wcol * ye

    n_blocks = seg_ref[MOE_GROUPS + group]
    odd_tail = (n_blocks % 2 == 1) & (n_blocks >= 3)
    n_pairs = jnp.where(odd_tail, (n_blocks - 3) // 2, n_blocks // 2)

    def block_pair(u, carry):
        expert_rows(2 * u, 2 * BLK)
        return carry

    lax.fori_loop(0, n_pairs, block_pair, 0)

    @pl.when(odd_tail)
    def _():
        expert_rows(n_blocks - 3, 3 * BLK)

    @pl.when(n_blocks == 1)
    def _():
        expert_rows(0, BLK)

    @pl.when(e_step == n_steps - 1)
    def _():
        y = x_ref[...] + jnp.dot(pt_ref[...], acc_ref[...].astype(BF16),
                                 preferred_element_type=F32)
        ms = jnp.mean(y * y, axis=-1, keepdims=True)
        out_ref[...] = (y * lax.rsqrt(ms + EPS)) * fg_ref[...]


def _moe(x2d, norm_g, w_router_group, b_router_group, w_router_expert, b_router_expert,
         w1, w3, w2, norm_final_g, tm):
    n = x2d.shape[0]
    blk_rows = min(MOE_ROW_BLOCK, tm)
    assert n % tm == 0 and tm % blk_rows == 0
    n_steps = N_EXPERTS // EXPERTS_PER_STEP
    sr = tm + (MOE_GROUPS - 1) * blk_rows
    padw = LANE - N_EXPERTS - MOE_GROUPS
    wr = jnp.concatenate([w_router_expert, w_router_group,
                          jnp.zeros((D_MODEL, padw), F32)], axis=1).astype(BF16)
    br = jnp.concatenate([b_router_expert, b_router_group,
                          jnp.zeros((padw,), F32)]).astype(F32).reshape(1, LANE)
    tri = jnp.asarray(np.tril(np.ones((tm, tm), np.float32), -1), BF16)
    row = pl.BlockSpec((tm, D_MODEL), lambda i, e: (i, 0))
    const = lambda r, c: pl.BlockSpec((r, c), lambda i, e: (0, 0))
    return pl.pallas_call(
        functools.partial(_moe_kernel, n_steps=n_steps, blk_rows=blk_rows),
        out_shape=jax.ShapeDtypeStruct((n, D_MODEL), F32),
        grid=(n // tm, n_steps),
        in_specs=[row, const(1, D_MODEL), const(D_MODEL, LANE), const(1, LANE),
                  pl.BlockSpec((EXPERTS_PER_STEP, D_MODEL, EXPERT_FF), lambda i, e: (e, 0, 0)),
                  pl.BlockSpec((EXPERTS_PER_STEP, D_MODEL, EXPERT_FF), lambda i, e: (e, 0, 0)),
                  pl.BlockSpec((EXPERTS_PER_STEP, EXPERT_FF, D_MODEL), lambda i, e: (e, 0, 0)),
                  const(1, D_MODEL), const(tm, tm)],
        out_specs=row,
        scratch_shapes=[pltpu.VMEM((sr, D_MODEL), BF16), pltpu.VMEM((sr, LANE), F32),
                        pltpu.VMEM((sr, D_MODEL), F32), pltpu.VMEM((tm, sr), BF16),
                        pltpu.SMEM((2 * MOE_GROUPS,), I32)],
        compiler_params=pltpu.CompilerParams(
            dimension_semantics=("arbitrary", "arbitrary"), vmem_limit_bytes=VMEM_LIMIT),
        name="moe",
    )(x2d, norm_g.reshape(1, D_MODEL), wr, br, w1.astype(BF16), w3.astype(BF16),
      w2.astype(BF16), norm_final_g.reshape(1, D_MODEL), tri)


def _token_tile(n):
    for tm in (1024, 512, 256, 128, 64, 32, 16, 8):
        if n % tm == 0:
            return tm
    raise ValueError(f"token count {n} is not a multiple of 8")


def _pad_axis(a, axis, size):
    if a.shape[axis] == size:
        return a
    widths = [(0, 0)] * a.ndim
    widths[axis] = (0, size - a.shape[axis])
    return jnp.pad(a, widths)


def _round_up(n, m):
    return (n + m - 1) // m * m


def _stream_step(x, conv_state, ssm_state, k_past, v_past, kidx_past, w_packed, p):
    b, t, d = x.shape
    n = b * t
    tm = _token_tile(n)
    x2d = x.reshape(n, d)
    z, xbc, q, gate, k, v, qi, ki, dtw, k_b, v_b, ki_b = _in_proj(
        x2d, p["norm_mix_g"], w_packed, tm)

    yn, conv_new, ssm_new = _ssd(
        xbc.reshape(b, t, CONV_DIM), z.reshape(b, t, SSM_INNER), dtw.reshape(b, t, LANE),
        conv_state, ssm_state, p["conv_w"], p["conv_b"], p["dt_bias"], p["A_log"],
        p["D_skip"], p["ssm_norm_g"])

    kv_w = ATTN_KV_HEADS * HEAD_DIM
    k_all, v_all, kidx_all = (k_b.reshape(b, t, kv_w), v_b.reshape(b, t, kv_w),
                              ki_b.reshape(b, t, IDX_DIM))
    past_len = 0
    if k_past is not None:
        past_len = k_past.shape[1]
        k_all = jnp.concatenate([k_past.astype(BF16).reshape(b, past_len, kv_w), k_all], axis=1)
        v_all = jnp.concatenate([v_past.astype(BF16).reshape(b, past_len, kv_w), v_all], axis=1)
        kidx_all = jnp.concatenate([kidx_past.astype(BF16), kidx_all], axis=1)
    l_valid = past_len + t
    lp = _round_up(l_valid, 2 * KEY_TILE)
    if t % Q_BLOCK == 0:
        q_in = (q.reshape(b, t, ATTN_WIDTH), qi.reshape(b, t, IDX_HEADS * IDX_DIM),
                dtw.reshape(b, t, LANE))
    else:
        q_in = (q.reshape(b, t, ATTN_KV_HEADS, ATTN_REP, HEAD_DIM).transpose(0, 2, 3, 1, 4)
                .reshape(b, ATTN_KV_HEADS, ATTN_REP * t, HEAD_DIM),
                qi.reshape(b, t, IDX_HEADS, IDX_DIM).transpose(0, 2, 1, 3)
                .reshape(b, IDX_HEADS * t, IDX_DIM),
                dtw.reshape(b, t, LANE)[:, :, SSM_HEADS:SSM_HEADS + IDX_HEADS]
                .transpose(0, 2, 1).reshape(b, 1, IDX_HEADS * t))
    o = _dsa(*q_in, _pad_axis(kidx_all, 1, lp), _pad_axis(k_all, 1, lp), _pad_axis(v_all, 1, lp),
             past_len, l_valid, t)
    o = o.reshape(n, ATTN_WIDTH)

    x1 = _merge(x2d, yn.reshape(n, SSM_INNER), o, gate, p["w_ssm_out"], p["w_attn_out"],
                p["w_o"], min(tm, MERGE_TILE))
    y = _moe(x1, p["norm_moe_g"], p["w_router_group"], p["b_router_group"],
             p["w_router_expert"], p["b_router_expert"], p["w1"], p["w3"], p["w2"],
             p["norm_final_g"], tm)
    return (y.reshape(b, t, d), k.reshape(b, t, ATTN_KV_HEADS, HEAD_DIM),
            v.reshape(b, t, ATTN_KV_HEADS, HEAD_DIM), ki.reshape(b, t, IDX_DIM), conv_new, ssm_new)


def kernel(x_prompt, x_sample, cache_k, cache_v, cache_kidx, state_conv, state_ssm, norm_mix_g, w_in, conv_w, conv_b, dt_bias, A_log, D_skip, ssm_norm_g, w_ssm_out, w_attn_out, w_o, norm_moe_g, w_router_group, b_router_group, w_router_expert, b_router_expert, w1, w3, w2, norm_final_g):
    p = dict(norm_mix_g=norm_mix_g, conv_w=conv_w, conv_b=conv_b, dt_bias=dt_bias, A_log=A_log,
             D_skip=D_skip, ssm_norm_g=ssm_norm_g, w_ssm_out=w_ssm_out, w_attn_out=w_attn_out,
             w_o=w_o, norm_moe_g=norm_moe_g, w_router_group=w_router_group,
             b_router_group=b_router_group, w_router_expert=w_router_expert,
             b_router_expert=b_router_expert, w1=w1, w3=w3, w2=w2, norm_final_g=norm_final_g)
    w_packed = _pack_w_in(w_in)
    yp, kp, vp, kip, cp, sp = _stream_step(x_prompt, None, None, None, None, None, w_packed, p)
    ys, ksn, vsn, kisn, csn, ssn = _stream_step(x_sample, state_conv, state_ssm, cache_k,
                                                cache_v, cache_kidx, w_packed, p)
    return (yp, ys, kp, vp, kip, cp, sp, ksn, vsn, kisn, csn, ssn)
```

```python
import functools

import numpy as np
import jax
import jax.numpy as jnp
from jax import lax
from jax.experimental import pallas as pl
from jax.experimental.pallas import tpu as pltpu

F32 = jnp.float32
BF16 = jnp.bfloat16
I32 = jnp.int32

D_MODEL = 1024
CHUNK = 64
CHUNK_SHIFT = 6
EPS = 1e-6
SSM_INNER = 2048
SSM_HEADDIM = 64
SSM_HEADS = 32
SSM_GROUPS = 4
SSM_HEADS_PER_GROUP = SSM_HEADS // SSM_GROUPS
SSM_STATE = 128
CONV_WIDTH = 4
CONV_DIM = SSM_INNER + 2 * SSM_GROUPS * SSM_STATE
CONV_CARRY = 16
SSD_CHUNKS_PER_STEP = 2
ATTN_HEADS = 16
ATTN_KV_HEADS = 2
HEAD_DIM = 64
ATTN_REP = ATTN_HEADS // ATTN_KV_HEADS
ATTN_WIDTH = ATTN_HEADS * HEAD_DIM
IDX_HEADS = 8
IDX_DIM = 64
IDX_SCALE = (IDX_HEADS * IDX_DIM) ** -0.5
TOPK_MAX = 256
MOE_GROUPS = 4
EXPERTS_PER_GROUP = 8
EXPERT_GROUP_SHIFT = 3
N_EXPERTS = 32
EXPERT_FF = 256
N_BRANCHES = 2

LANE = 128
VMEM_LIMIT = 56 * 1024 * 1024
PROJ_TILE = 1024
PROJ_ROWS = 512
PROJ_TILES = 9
Q_BLOCK = 128
SMALL_Q_BLOCK = 16
KEY_TILE = 256
KEY_INDEX_BITS = 15
COUNT_CHAINS = 4
EXPERTS_PER_STEP = 4
MOE_ROW_BLOCK = 128
MERGE_TILE = 512
INT_MIN = np.int32(-2 ** 31)
NEG_BIG = -1e30
Q_SCALE = HEAD_DIM ** -0.5 * float(np.log2(np.e))
PV_ROWS = 80

NT_DIMS = (((1,), (1,)), ((), ()))
TN_DIMS = (((0,), (0,)), ((), ()))


def _sigmoid(x):
    return 1.0 / (1.0 + jnp.exp(-x))


def _silu(x):
    h = 0.5 * x
    return h + h * jnp.tanh(h)


def _split3(x):
    hi = x.astype(BF16)
    r1 = x - hi.astype(F32)
    mid = r1.astype(BF16)
    r2 = r1 - mid.astype(F32)
    lo = r2.astype(BF16)
    return jnp.concatenate([hi, mid, lo], axis=1)


def _inproj_kernel(x_ref, g_ref, w_ref, z_ref, xbc_ref, q_ref, gate_ref, k_ref, v_ref,
                   qi_ref, ki_ref, dtw_ref, kb_ref, vb_ref, kib_ref):
    x = x_ref[...]
    ms = jnp.mean(x * x, axis=-1, keepdims=True)
    h = ((x * lax.rsqrt(ms + EPS)) * g_ref[...]).astype(BF16)

    def cols(first, width):
        lo = first * PROJ_TILE
        return jnp.dot(h, w_ref[:, lo:lo + width], preferred_element_type=F32)

    for j in range(2):
        z_ref[:, j * PROJ_TILE:(j + 1) * PROJ_TILE] = cols(j, PROJ_TILE).astype(BF16)
    for j in range(3):
        xbc_ref[:, j * PROJ_TILE:(j + 1) * PROJ_TILE] = cols(2 + j, PROJ_TILE).astype(BF16)
    q_ref[...] = (cols(5, PROJ_TILE) * Q_SCALE).astype(BF16)
    for j in range(2):
        gate_ref[:, j * PROJ_TILE:(j + 1) * PROJ_TILE] = cols(6 + j, PROJ_TILE).astype(BF16)
    t = cols(8, PROJ_TILE)
    k_ref[...] = t[:, 0:128]
    v_ref[...] = t[:, 128:256]
    qi_ref[...] = t[:, 256:768].astype(BF16)
    ki_ref[...] = t[:, 768:832]
    dtw_ref[...] = t[:, 896:1024]
    kb_ref[...] = t[:, 0:128].astype(BF16)
    vb_ref[...] = t[:, 128:256].astype(BF16)
    kib_ref[...] = t[:, 768:832].astype(BF16)


def _pack_w_in(w_in):
    sizes = (SSM_INNER, CONV_DIM, SSM_HEADS, ATTN_WIDTH, ATTN_KV_HEADS * HEAD_DIM,
             ATTN_KV_HEADS * HEAD_DIM, IDX_HEADS * IDX_DIM, IDX_DIM, IDX_HEADS,
             N_BRANCHES * D_MODEL)
    offs = np.concatenate([[0], np.cumsum(sizes)])
    z, xbc, dt, q, k, v, qi, ki, wi, g = [w_in[:, offs[i]:offs[i + 1]] for i in range(10)]
    zeros = lambda n: jnp.zeros((D_MODEL, n), w_in.dtype)
    cols = [z, xbc, q, g, k, v, qi, ki, zeros(64), dt, wi, zeros(LANE - SSM_HEADS - IDX_HEADS)]
    return jnp.concatenate(cols, axis=1).astype(BF16)


def _in_proj(x2d, norm_g, w_packed, tm):
    n = x2d.shape[0]
    assert n % tm == 0
    bf = lambda c: jax.ShapeDtypeStruct((n, c), BF16)
    ff = lambda c: jax.ShapeDtypeStruct((n, c), F32)
    rows = lambda c: pl.BlockSpec((tm, c), lambda i: (i, 0))
    widths = (SSM_INNER, CONV_DIM, ATTN_WIDTH, 2 * D_MODEL, 128, 128, 512, IDX_DIM, 128,
              128, 128, IDX_DIM)
    return pl.pallas_call(
        _inproj_kernel,
        out_shape=(bf(SSM_INNER), bf(CONV_DIM), bf(ATTN_WIDTH), bf(2 * D_MODEL),
                   ff(128), ff(128), bf(512), ff(IDX_DIM), ff(128), bf(128), bf(128), bf(IDX_DIM)),
        grid=(n // tm,),
        in_specs=[
            rows(D_MODEL),
            pl.BlockSpec((1, D_MODEL), lambda i: (0, 0)),
            pl.BlockSpec((D_MODEL, PROJ_TILES * PROJ_TILE), lambda i: (0, 0),
                         pipeline_mode=pl.Buffered(1)),
        ],
        out_specs=tuple(rows(c) for c in widths),
        compiler_params=pltpu.CompilerParams(
            dimension_semantics=("arbitrary",), vmem_limit_bytes=VMEM_LIMIT),
        name="in_proj",
    )(x2d, norm_g.reshape(1, D_MODEL), w_packed)


def _ssd_kernel(*refs, L, cps, nsteps, has_state):
    if has_state:
        (xbc_ref, z_ref, dtw_ref, cs_ref, s0_ref, shift_ref, convw_ref, convb_ref, dtb_ref, a_ref,
         dx_ref, ng_ref, e3p_ref, e3l_ref, tri_ref, bd_ref, yn_ref, cnew_ref, sout_ref,
         xe_ref, st, act_ref, ex_ref, yz_ref, acl_ref) = refs
    else:
        (xbc_ref, z_ref, dtw_ref, shift_ref, convw_ref, convb_ref, dtb_ref, a_ref,
         dx_ref, ng_ref, e3p_ref, e3l_ref, tri_ref, bd_ref, yn_ref, cnew_ref, sout_ref,
         xe_ref, st, act_ref, ex_ref, yz_ref, acl_ref) = refs
    c = pl.program_id(1)
    G, R, N, P = SSM_GROUPS, SSM_HEADS_PER_GROUP, SSM_STATE, SSM_HEADDIM
    GW = R * P
    RL = R * L
    LB = cps * L

    @pl.when(c == 0)
    def _():
        xe_ref[LB:, :] = jnp.zeros((xe_ref.shape[0] - LB, CONV_DIM), BF16)
        if has_state:
            xe_ref[LB:LB + CONV_CARRY, :] = cs_ref[0]
            st[...] = s0_ref[0].T
        else:
            st[...] = jnp.zeros(st.shape, F32)

    xe_ref[0:LB, :] = xbc_ref[0]
    shifted = jnp.dot(shift_ref[0], xe_ref[...], preferred_element_type=F32)
    w = convw_ref[...]
    conv = convb_ref[...] + shifted[2 * LB:3 * LB, :] * w[0:1, :]
    conv = conv + shifted[LB:2 * LB, :] * w[1:2, :]
    conv = conv + shifted[0:LB, :] * w[2:3, :]
    conv = conv + xbc_ref[0].astype(F32) * w[3:4, :]
    act_ref[...] = _silu(conv)

    @pl.when(c == nsteps - 1)
    def _():
        cnew_ref[0] = xbc_ref[0, LB - CONV_CARRY:LB, :].astype(F32)[
            CONV_CARRY - (CONV_WIDTH - 1):CONV_CARRY, :]

    if nsteps > 1:
        xe_ref[LB:LB + CONV_CARRY, :] = xe_ref[LB - CONV_CARRY:LB, :]

    u = dtw_ref[0] + dtb_ref[...]
    dt = jnp.maximum(u, 0.0) + jnp.log1p(jnp.exp(-jnp.abs(u)))
    a = dt * a_ref[...]
    ac3 = jnp.dot(tri_ref[...], _split3(a), preferred_element_type=F32)
    a_cum = ac3[:, 0:LANE] + ac3[:, LANE:2 * LANE] + ac3[:, 2 * LANE:]

    ex_ref[...] = jnp.dot(_split3(jnp.concatenate([a_cum, dt], axis=0)), e3p_ref[...],
                          preferred_element_type=F32)
    if L != P:
        acl_ref[...] = jnp.dot(_split3(a_cum), e3l_ref[...], preferred_element_type=F32)
    row = lax.broadcasted_iota(I32, (L, RL), 0)
    scol = lax.broadcasted_iota(I32, (L, RL), 1) & (L - 1)

    for sub in range(cps):
        r0 = sub * L
        rows = slice(r0, r0 + L)
        ssq = jnp.zeros((L, 1), F32)
        for g in range(G):
            sl = slice(g * GW, (g + 1) * GW)
            xs = act_ref[rows, sl]
            b_g = act_ref[rows, SSM_INNER + g * N:SSM_INNER + (g + 1) * N].astype(BF16)
            c_g = act_ref[rows, SSM_INNER + (G + g) * N:SSM_INNER + (G + g + 1) * N].astype(BF16)
            acx = ex_ref[rows, sl]
            alast = ex_ref[r0 + L - 1:r0 + L, sl]
            xdt = xs * ex_ref[LB + r0:LB + r0 + L, sl]
            acl_g = acx if L == P else acl_ref[rows, g * RL:(g + 1) * RL]
            acs_g = jnp.sum(jnp.where(scol == row, acl_g, 0.0), axis=0, keepdims=True)
            decay = jnp.where(scol <= row, jnp.exp(acl_g - acs_g), 0.0)
            cb = lax.dot_general(c_g, jnp.concatenate([b_g] * R, axis=0), NT_DIMS,
                                 preferred_element_type=F32)
            m_g = (cb * decay).astype(BF16)
            xbd = jnp.concatenate([xdt.astype(BF16)] * R, axis=0) * bd_ref[...]
            y_in = jnp.dot(m_g, xbd, preferred_element_type=F32)
            st_g = st[:, sl]
            y_st = jnp.dot(c_g, st_g.astype(BF16), preferred_element_type=F32)
            xw = (xdt * jnp.exp(alast - acx)).astype(BF16)
            upd = lax.dot_general(b_g, xw, TN_DIMS, preferred_element_type=F32)
            st[:, sl] = st_g * jnp.exp(alast) + upd
            y = y_in + jnp.exp(acx) * y_st + dx_ref[:, sl] * xs
            yz = y * _silu(z_ref[0, rows, sl].astype(F32))
            ssq = ssq + jnp.sum(yz * yz, axis=-1, keepdims=True)
            yz_ref[rows, sl] = yz
        scale = lax.rsqrt(ssq * (1.0 / SSM_INNER) + EPS)
        yn_ref[0, rows, :] = ((yz_ref[rows, :] * scale) * ng_ref[...]).astype(BF16)

    @pl.when(c == nsteps - 1)
    def _():
        sout_ref[0] = st[...].T


def _shift_matrices(L, conv_rows):
    out = np.zeros((2, 3 * L, conv_rows), np.float32)
    for k in range(CONV_WIDTH - 1):
        for t in range(L):
            src = t - (k + 1)
            if src >= 0:
                out[:, k * L + t, src] = 1.0
            else:
                out[1, k * L + t, L + CONV_CARRY + src] = 1.0
                j = CONV_WIDTH - 1 + src
                for piece in range(3):
                    out[0, k * L + t, L + 3 * piece + j] = 1.0
    return jnp.asarray(out, BF16)


def _expand_matrix(width_per_head):
    r = np.arange(3 * LANE)[:, None] % LANE
    c = np.arange(SSM_HEADS * width_per_head)[None, :] // width_per_head
    return jnp.asarray((r == c).astype(np.float32), BF16)


def _ssd(xbc, z, dtw, conv_state, ssm_state, conv_w, conv_b, dt_bias, a_log, d_skip, ssm_norm_g):
    b, t, _ = xbc.shape
    L = min(CHUNK, t)
    nc = t // L
    assert t % L == 0 and L & (L - 1) == 0
    has_state = conv_state is not None
    pad = lambda v: jnp.pad(v.astype(F32), (0, LANE - SSM_HEADS)).reshape(1, LANE)
    a_neg = pad(-jnp.exp(a_log.astype(F32)))
    cps = SSD_CHUNKS_PER_STEP if nc % SSD_CHUNKS_PER_STEP == 0 else 1
    nsteps = nc // cps
    lb = cps * L
    conv_rows = _round_up(lb + CONV_CARRY, LANE)
    tri = jnp.asarray(np.kron(np.eye(cps), np.tril(np.ones((L, L)))).astype(np.float32), BF16)
    rl = SSM_HEADS_PER_GROUP * L
    gw = SSM_HEADS_PER_GROUP * SSM_HEADDIM
    bd = jnp.asarray((np.arange(rl)[:, None] // L == np.arange(gw)[None, :] // SSM_HEADDIM)
                     .astype(np.float32), BF16)
    const = lambda shape: pl.BlockSpec(shape, lambda i, j: (0,) * len(shape))
    tok = lambda w: pl.BlockSpec((1, lb, w), lambda i, j: (i, j, 0))
    per_b = lambda s: pl.BlockSpec((1,) + s, lambda i, j: (i, 0, 0))
    ins = [xbc, z, dtw]
    specs = [tok(CONV_DIM), tok(SSM_INNER), tok(LANE)]
    if has_state:
        triples = _split3(conv_state.astype(F32))
        ins += [_pad_axis(triples, 1, CONV_CARRY), ssm_state.reshape(b, SSM_INNER, SSM_STATE)]
        specs += [per_b((CONV_CARRY, CONV_DIM)), per_b((SSM_INNER, SSM_STATE))]
    first = 0 if has_state else 1
    ins += [_shift_matrices(lb, conv_rows)]
    specs += [pl.BlockSpec((1, 3 * lb, conv_rows),
                           lambda i, j: (jnp.where(j == 0, first, 1), 0, 0))]
    ins += [conv_w, conv_b.reshape(1, CONV_DIM), pad(dt_bias), a_neg,
            jnp.repeat(d_skip.astype(F32), SSM_HEADDIM).reshape(1, SSM_INNER),
            ssm_norm_g.reshape(1, SSM_INNER), _expand_matrix(SSM_HEADDIM), _expand_matrix(L),
            tri, bd]
    specs += [const((CONV_WIDTH, CONV_DIM)), const((1, CONV_DIM)), const((1, LANE)),
              const((1, LANE)), const((1, SSM_INNER)), const((1, SSM_INNER)),
              const((3 * LANE, SSM_INNER)), const((3 * LANE, SSM_HEADS * L)),
              const((lb, lb)), const((rl, gw))]
    yn, cnew, sout = pl.pallas_call(
        functools.partial(_ssd_kernel, L=L, cps=cps, nsteps=nsteps, has_state=has_state),
        out_shape=(jax.ShapeDtypeStruct((b, t, SSM_INNER), BF16),
                   jax.ShapeDtypeStruct((b, CONV_WIDTH - 1, CONV_DIM), F32),
                   jax.ShapeDtypeStruct((b, SSM_INNER, SSM_STATE), F32)),
        grid=(b, nsteps),
        in_specs=specs,
        out_specs=(tok(SSM_INNER), per_b((CONV_WIDTH - 1, CONV_DIM)),
                   per_b((SSM_INNER, SSM_STATE))),
        scratch_shapes=[pltpu.VMEM((conv_rows, CONV_DIM), BF16),
                        pltpu.VMEM((SSM_STATE, SSM_INNER), F32),
                        pltpu.VMEM((lb, CONV_DIM), F32),
                        pltpu.VMEM((2 * lb, SSM_INNER), F32),
                        pltpu.VMEM((lb, SSM_INNER), F32),
                        pltpu.VMEM((lb, SSM_HEADS * L), F32)],
        compiler_params=pltpu.CompilerParams(
            dimension_semantics=("arbitrary", "arbitrary"), vmem_limit_bytes=VMEM_LIMIT),
        name="ssd",
    )(*ins)
    return yn, cnew, sout.reshape(b, SSM_HEADS, SSM_HEADDIM, SSM_STATE)


def _dsa_kernel(q_ref, qi_ref, w_ref, eq_ref, kidx_ref, k_ref, v_ref, o_ref,
                qs_ref, qx_ref, key_ref, kx_ref, vx_ref, m_ref, acc_ref, mlim_ref, s_ref,
                *, past_len, l_valid, topk, nq):
    qb = pl.program_id(1)
    TQ, TK = LANE, KEY_TILE
    W = ATTN_REP * nq
    wide = nq == LANE

    lane = lax.broadcasted_iota(I32, (1, TQ), 1)
    pos = past_len + qb * nq + (lane & (nq - 1))
    n_adm = jnp.minimum(((pos >> CHUNK_SHIFT) + 1) * CHUNK, l_valid)
    k_eff = jnp.minimum(n_adm, topk)
    last_pos = past_len + qb * nq + nq - 1
    n_max = jnp.minimum(((last_pos >> CHUNK_SHIFT) + 1) * CHUNK, l_valid)
    nt = (n_max + TK - 1) // TK
    ntp = 2 * ((nt + 1) // 2)
    lp = key_ref.shape[0]

    @pl.when(qb == 0)
    def _():
        kx_ref[:, LANE:] = k_ref[0]

        def v_tile(t, carry):
            s0 = pl.multiple_of(t * TK, TK)
            v_t = v_ref[0, pl.ds(s0, TK), :].astype(F32).T
            for g in range(ATTN_KV_HEADS):
                vx_ref[g, 0:HEAD_DIM, pl.ds(s0, TK)] = (
                    v_t[g * HEAD_DIM:(g + 1) * HEAD_DIM, :].astype(BF16))
            return carry

        lax.fori_loop(0, lp // TK, v_tile, 0)
        tail_row = lax.broadcasted_iota(I32, (PV_ROWS - HEAD_DIM, lp), 0)
        for g in range(ATTN_KV_HEADS):
            vx_ref[g, HEAD_DIM:, :] = jnp.where(tail_row == 0, 1.0, 0.0).astype(BF16)
            qx_ref[g, :, 0:LANE] = eq_ref[...]
            qx_ref[g, :, LANE:] = jnp.zeros((W, LANE), BF16)

    if wide:
        for h in range(IDX_HEADS):
            qs_ref[h * nq:(h + 1) * nq, :] = qi_ref[0, :, h * IDX_DIM:(h + 1) * IDX_DIM]
        w_t = w_ref[0].T[SSM_HEADS:SSM_HEADS + IDX_HEADS, :] * IDX_SCALE
    else:
        qs_ref[...] = qi_ref[0]
        w_row = w_ref[0] * IDX_SCALE

    def index_dots(t, slot):
        s0 = pl.multiple_of(jnp.minimum(t, nt - 1) * TK, TK)
        s_ref[0, slot] = lax.dot_general(kidx_ref[0, pl.ds(s0, TK), :], qs_ref[...], NT_DIMS,
                                         preferred_element_type=F32)

    def score_tile(t, slot):
        s0 = pl.multiple_of(t * TK, TK)
        d = s_ref[0, slot]
        if wide:
            sc = w_t[0:1, :] * jnp.maximum(d[:, 0:nq], 0.0)
            for h in range(1, IDX_HEADS):
                sc = sc + w_t[h:h + 1, :] * jnp.maximum(d[:, h * nq:(h + 1) * nq], 0.0)
        else:
            sc = w_row * jnp.maximum(d, 0.0)
            for shift in [nq << i for i in reversed(range(IDX_HEADS.bit_length() - 1))]:
                sc = sc + pltpu.roll(sc, shift, axis=1)
        sc = sc + 0.0
        bits = pltpu.bitcast(sc, I32)
        key = jnp.where(bits < 0, bits ^ np.int32(0x7FFFFFFF), bits)
        s_idx = s0 + lax.broadcasted_iota(I32, (TK, TQ), 0)
        key_ref[pl.ds(s0, TK), :] = jnp.where(s_idx < n_adm, key, INT_MIN)

    index_dots(0, 0)

    def score_pair(u, carry):
        index_dots(2 * u + 1, 1)
        score_tile(2 * u, 0)
        index_dots(2 * u + 2, 0)
        score_tile(2 * u + 1, 1)
        return carry

    lax.fori_loop(0, nt // 2, score_pair, 0)

    @pl.when(nt < ntp)
    def _():
        score_tile(nt - 1, 0)
        key_ref[pl.ds(pl.multiple_of(nt * TK, TK), TK), :] = jnp.full((TK, TQ), INT_MIN, I32)

    def count(pred):
        n_vregs = 2 * TK // 8

        def body(t, accs):
            s0 = pl.multiple_of(t * (2 * TK), 2 * TK)
            blk = key_ref[pl.ds(s0, 2 * TK), :].reshape(n_vregs, 8, TQ)
            s_idx = s0 + lax.broadcasted_iota(I32, (2 * TK, TQ), 0).reshape(n_vregs, 8, TQ)
            accs = list(accs)
            for r in range(n_vregs):
                a = accs[r % COUNT_CHAINS]
                accs[r % COUNT_CHAINS] = jnp.where(pred(blk[r], s_idx[r]), a + 1, a)
            return tuple(accs)

        zero = jnp.zeros((8, TQ), I32)
        accs = lax.fori_loop(0, ntp // 2, body, (zero,) * COUNT_CHAINS)
        acc = accs[0]
        for a in accs[1:]:
            acc = acc + a
        return jnp.sum(acc, axis=0, keepdims=True)

    def bit_step(i, prefix):
        cand = prefix | jnp.left_shift(jnp.int32(1), 31 - i)
        cand_s = cand ^ INT_MIN
        cnt = count(lambda blk, s_idx: blk >= cand_s)
        return jnp.where(cnt >= k_eff, cand, prefix)

    thr = lax.fori_loop(0, 32, bit_step, jnp.zeros((1, TQ), I32)) ^ INT_MIN

    n_gt = count(lambda blk, s_idx: blk > thr)
    n_eq = count(lambda blk, s_idx: blk == thr)
    need = k_eff - n_gt
    mlim_ref[...] = jnp.full((1, TQ), 1 << KEY_INDEX_BITS, I32)

    @pl.when(jnp.max(n_eq - need) > 0)
    def _():
        def idx_step(i, prefix):
            cand = prefix | jnp.left_shift(jnp.int32(1), KEY_INDEX_BITS - 1 - i)
            cnt = count(lambda blk, s_idx: (blk == thr) & (s_idx < cand))
            return jnp.where(cnt < need, cand, prefix)
        mlim_ref[...] = lax.fori_loop(0, KEY_INDEX_BITS, idx_step, jnp.zeros((1, TQ), I32))

    mlim = mlim_ref[...]

    def bias_tile(t, carry):
        s0 = pl.multiple_of(t * TK, TK)
        blk = key_ref[pl.ds(s0, TK), :]
        s_idx = s0 + lax.broadcasted_iota(I32, (TK, TQ), 0)
        sel = (blk > thr) | ((blk == thr) & (s_idx <= mlim))
        kx_ref[pl.ds(s0, TK), 0:LANE] = jnp.where(sel, 0.0, NEG_BIG).astype(BF16)
        return carry

    lax.fori_loop(0, nt, bias_tile, 0)

    for g in range(ATTN_KV_HEADS):
        lanes = slice(LANE + g * HEAD_DIM, LANE + (g + 1) * HEAD_DIM)
        if wide:
            for r in range(ATTN_REP):
                hq = g * ATTN_REP + r
                qx_ref[g, r * nq:(r + 1) * nq, lanes] = (
                    q_ref[0, :, hq * HEAD_DIM:(hq + 1) * HEAD_DIM])
        else:
            qx_ref[g, :, lanes] = q_ref[0, g]
    m_ref[...] = jnp.full(m_ref.shape, -jnp.inf, F32)
    acc_ref[...] = jnp.zeros(acc_ref.shape, F32)

    def scores(t, slot):
        s0 = pl.multiple_of(jnp.minimum(t, nt - 1) * TK, TK)
        k_tile = kx_ref[pl.ds(s0, TK), :]
        for g in range(ATTN_KV_HEADS):
            s_ref[g, slot] = lax.dot_general(k_tile, qx_ref[g], NT_DIMS,
                                             preferred_element_type=F32)

    def accumulate(t, slot):
        s0 = pl.multiple_of(t * TK, TK)
        for g in range(ATTN_KV_HEADS):
            s = s_ref[g, slot]
            m_prev = m_ref[g]
            m_new = jnp.maximum(m_prev, jnp.max(s, axis=0, keepdims=True))
            p = jnp.exp2(s - m_new)
            acc_ref[g] = jnp.exp2(m_prev - m_new) * acc_ref[g] + jnp.dot(
                vx_ref[g, :, pl.ds(s0, TK)], p.astype(BF16), preferred_element_type=F32)
            m_ref[g] = m_new

    scores(0, 0)

    def attn_pair(u, carry):
        scores(2 * u + 1, 1)
        accumulate(2 * u, 0)
        scores(2 * u + 2, 0)
        accumulate(2 * u + 1, 1)
        return carry

    lax.fori_loop(0, nt // 2, attn_pair, 0)

    @pl.when(nt < ntp)
    def _():
        accumulate(nt - 1, 0)

    for g in range(ATTN_KV_HEADS):
        acc = acc_ref[g]
        o_t = acc[0:HEAD_DIM, :] * (1.0 / acc[HEAD_DIM:HEAD_DIM + 1, :])
        if wide:
            for r in range(0, ATTN_REP, 2):
                hq = g * ATTN_REP + r
                pair = jnp.concatenate([o_t[:, r * nq:(r + 1) * nq],
                                        o_t[:, (r + 1) * nq:(r + 2) * nq]], axis=0)
                o_ref[0, :, hq * HEAD_DIM:(hq + 2) * HEAD_DIM] = pair.T.astype(BF16)
        else:
            o_rq = jnp.concatenate([o_t, o_t], axis=0).T[:, 0:HEAD_DIM].astype(BF16)
            for r in range(ATTN_REP):
                hq = g * ATTN_REP + r
                o_ref[0, :, hq * HEAD_DIM:(hq + 1) * HEAD_DIM] = o_rq[r * nq:(r + 1) * nq, :]


def _dsa(q, qi, w, kidx_all, k_all, v_all, past_len, l_valid, t):
    b = kidx_all.shape[0]
    lp = kidx_all.shape[1]
    nq = Q_BLOCK if t % Q_BLOCK == 0 else SMALL_Q_BLOCK
    assert t % nq == 0 and lp % (2 * KEY_TILE) == 0 and lp < 1 << KEY_INDEX_BITS
    topk = min(TOPK_MAX, l_valid // 4)
    width = ATTN_REP * nq
    if nq == Q_BLOCK:
        qblk = lambda c: pl.BlockSpec((1, nq, c), lambda i, j: (i, j, 0))
        q_specs = [qblk(ATTN_WIDTH), qblk(IDX_HEADS * IDX_DIM), qblk(LANE)]
    else:
        assert t == nq
        q_specs = [pl.BlockSpec((1, ATTN_KV_HEADS, width, HEAD_DIM), lambda i, j: (i, 0, 0, 0)),
                   pl.BlockSpec((1, IDX_HEADS * nq, IDX_DIM), lambda i, j: (i, 0, 0)),
                   pl.BlockSpec((1, 1, LANE), lambda i, j: (i, 0, 0))]
    slot_onehot = jnp.asarray(
        (np.arange(width)[:, None] % nq == np.arange(LANE)[None, :]).astype(np.float32), BF16)
    per_stream = lambda c: pl.BlockSpec((1, lp, c), lambda i, j: (i, 0, 0))
    return pl.pallas_call(
        functools.partial(_dsa_kernel, past_len=past_len, l_valid=l_valid, topk=topk, nq=nq),
        out_shape=jax.ShapeDtypeStruct((b, t, ATTN_WIDTH), BF16),
        grid=(b, t // nq),
        in_specs=q_specs + [pl.BlockSpec((width, LANE), lambda i, j: (0, 0)),
                            per_stream(IDX_DIM), per_stream(LANE), per_stream(LANE)],
        out_specs=pl.BlockSpec((1, nq, ATTN_WIDTH), lambda i, j: (i, j, 0)),
        scratch_shapes=[pltpu.VMEM((width, IDX_DIM), BF16),
                        pltpu.VMEM((ATTN_KV_HEADS, width, 2 * LANE), BF16),
                        pltpu.VMEM((lp, LANE), I32),
                        pltpu.VMEM((lp, 2 * LANE), BF16),
                        pltpu.VMEM((ATTN_KV_HEADS, PV_ROWS, lp), BF16),
                        pltpu.VMEM((ATTN_KV_HEADS, 1, width), F32),
                        pltpu.VMEM((ATTN_KV_HEADS, PV_ROWS, width), F32),
                        pltpu.VMEM((1, LANE), I32),
                        pltpu.VMEM((ATTN_KV_HEADS, 2, KEY_TILE, width), F32)],
        compiler_params=pltpu.CompilerParams(
            dimension_semantics=("arbitrary", "arbitrary"), vmem_limit_bytes=VMEM_LIMIT),
        name="dsa",
    )(q, qi, w, slot_onehot, kidx_all, k_all, v_all)


def _merge_kernel(x_ref, yn_ref, o_ref, gate_ref, wa_ref, wb_ref, wo_ref, out_ref):
    br_a = jnp.dot(yn_ref[...], wa_ref[...], preferred_element_type=F32)
    br_b = jnp.dot(o_ref[...], wb_ref[...], preferred_element_type=F32)
    gates = gate_ref[...].astype(F32)
    merged = _sigmoid(gates[:, :D_MODEL]) * br_a + _sigmoid(gates[:, D_MODEL:]) * br_b
    out_ref[...] = x_ref[...] + jnp.dot(merged.astype(BF16), wo_ref[...],
                                        preferred_element_type=F32)


def _merge(x2d, yn, o, gate, w_ssm_out, w_attn_out, w_o, tm):
    n = x2d.shape[0]
    assert n % tm == 0
    row = lambda w: pl.BlockSpec((tm, w), lambda i: (i, 0))
    full = lambda r, c: pl.BlockSpec((r, c), lambda i: (0, 0))
    return pl.pallas_call(
        _merge_kernel,
        out_shape=jax.ShapeDtypeStruct((n, D_MODEL), F32),
        grid=(n // tm,),
        in_specs=[row(D_MODEL), row(SSM_INNER), row(ATTN_WIDTH), row(2 * D_MODEL),
                  full(SSM_INNER, D_MODEL), full(ATTN_WIDTH, D_MODEL), full(D_MODEL, D_MODEL)],
        out_specs=row(D_MODEL),
        compiler_params=pltpu.CompilerParams(
            dimension_semantics=("arbitrary",), vmem_limit_bytes=VMEM_LIMIT),
        name="merge",
    )(x2d, yn, o, gate, w_ssm_out.astype(BF16), w_attn_out.astype(BF16), w_o.astype(BF16))


def _moe_kernel(x_ref, ng_ref, wr_ref, br_ref, w1_ref, w3_ref, w2_ref, fg_ref, tri_ref, out_ref,
                hs_ref, combs_ref, acc_ref, pt_ref, seg_ref, *, n_steps, blk_rows):
    e_step = pl.program_id(1)
    tm = x_ref.shape[0]
    sr = hs_ref.shape[0]
    BLK = blk_rows

    @pl.when(e_step == 0)
    def _():
        x = x_ref[...]
        ms = jnp.mean(x * x, axis=-1, keepdims=True)
        h = ((x * lax.rsqrt(ms + EPS)) * ng_ref[...]).astype(BF16)
        logits = jnp.dot(h, wr_ref[...], preferred_element_type=F32) + br_ref[...]
        lane_i = lax.broadcasted_iota(I32, (tm, LANE), 1)
        lane = lane_i.astype(F32)
        first = lambda hit: jnp.min(jnp.where(hit, lane, float(LANE)), axis=1, keepdims=True)
        is_g = (lane_i >= N_EXPERTS) & (lane_i < N_EXPERTS + MOE_GROUPS)
        glog = jnp.where(is_g, logits, -jnp.inf)
        gmax = jnp.max(glog, axis=1, keepdims=True)
        gsel = first(glog == gmax) - float(N_EXPERTS)
        p_group = 1.0 / jnp.sum(jnp.exp(glog - gmax), axis=1, keepdims=True)
        grp = (lane_i >> EXPERT_GROUP_SHIFT).astype(F32)
        in_grp = (lane_i < N_EXPERTS) & (grp == gsel)
        el = jnp.where(in_grp, logits, -jnp.inf)
        v1 = jnp.max(el, axis=1, keepdims=True)
        i1 = first(el == v1)
        el2 = jnp.where(lane == i1, -jnp.inf, el)
        v2 = jnp.max(el2, axis=1, keepdims=True)
        i2 = first(el2 == v2)
        e21 = jnp.exp(v2 - v1)
        den = 1.0 + e21
        comb = (jnp.where(lane == i1, (1.0 / den) * p_group, 0.0)
                + jnp.where(lane == i2, (e21 / den) * p_group, 0.0))

        own = lane == gsel
        own_f = jnp.where(own, 1.0, 0.0)
        before = jnp.dot(tri_ref[...], own_f.astype(BF16), preferred_element_type=F32)
        rank = jnp.sum(jnp.where(own, before, 0.0), axis=1, keepdims=True)
        counts = jnp.sum(own_f, axis=0, keepdims=True)
        blocks = jnp.floor((counts + (BLK - 1)) * (1.0 / BLK))
        upper = (lax.broadcasted_iota(I32, (LANE, LANE), 0)
                 < lax.broadcasted_iota(I32, (LANE, LANE), 1))
        starts = jnp.dot(jnp.broadcast_to(blocks, (8, LANE)).astype(BF16),
                         jnp.where(upper, 1.0, 0.0).astype(BF16),
                         preferred_element_type=F32)[0:1, :] * BLK
        dest = jnp.sum(jnp.where(own, starts, 0.0), axis=1, keepdims=True) + rank
        for g in range(MOE_GROUPS):
            seg_ref[g] = jnp.sum(starts[:, g:g + 1]).astype(I32)
            seg_ref[MOE_GROUPS + g] = jnp.sum(blocks[:, g:g + 1]).astype(I32)

        dest_i = dest.astype(I32)
        pt = jnp.where(dest_i == lax.broadcasted_iota(I32, (tm, sr), 1), 1.0, 0.0).astype(BF16)
        pt_ref[...] = pt
        dest_row = jnp.broadcast_to(dest, (tm, LANE)).T[0:1, :].astype(I32)
        p_mat = jnp.where(lax.broadcasted_iota(I32, (sr, tm), 0) == dest_row, 1.0, 0.0
                          ).astype(BF16)
        hs_ref[...] = jnp.dot(p_mat, h, preferred_element_type=F32).astype(BF16)
        c3 = jnp.dot(p_mat, _split3(comb), preferred_element_type=F32)
        combs_ref[...] = c3[:, 0:LANE] + c3[:, LANE:2 * LANE] + c3[:, 2 * LANE:]
        acc_ref[...] = jnp.zeros(acc_ref.shape, F32)

    group = e_step // (EXPERTS_PER_GROUP // EXPERTS_PER_STEP)
    seg_start = seg_ref[group]

    def expert_rows(first_blk, n_rows):
        rows = pl.ds(pl.multiple_of(seg_start + first_blk * BLK, BLK), n_rows)
        h = hs_ref[rows, :]
        comb = combs_ref[rows, :]
        lane = lax.broadcasted_iota(I32, (n_rows, LANE), 1)
        for k in range(EXPERTS_PER_STEP):
            e = e_step * EXPERTS_PER_STEP + k
            a1 = jnp.dot(h, w1_ref[k], preferred_element_type=F32)
            a3 = jnp.dot(h, w3_ref[k], preferred_element_type=F32)
            he = (_silu(a1) * a3).astype(BF16)
            ye = jnp.dot(he, w2_ref[k], preferred_element_type=F32)
            wcol = jnp.sum(jnp.where(lane == e, comb, 0.0), axis=1, keepdims=True)
            acc_ref[rows, :] += wcol * ye

    n_blocks = seg_ref[MOE_GROUPS + group]
    odd_tail = (n_blocks % 2 == 1) & (n_blocks >= 3)
    n_pairs = jnp.where(odd_tail, (n_blocks - 3) // 2, n_blocks // 2)

    def block_pair(u, carry):
        expert_rows(2 * u, 2 * BLK)
        return carry

    lax.fori_loop(0, n_pairs, block_pair, 0)

    @pl.when(odd_tail)
    def _():
        expert_rows(n_blocks - 3, 3 * BLK)

    @pl.when(n_blocks == 1)
    def _():
        expert_rows(0, BLK)

    @pl.when(e_step == n_steps - 1)
    def _():
        y = x_ref[...] + jnp.dot(pt_ref[...], acc_ref[...].astype(BF16),
                                 preferred_element_type=F32)
        ms = jnp.mean(y * y, axis=-1, keepdims=True)
        out_ref[...] = (y * lax.rsqrt(ms + EPS)) * fg_ref[...]


def _moe(x2d, norm_g, w_router_group, b_router_group, w_router_expert, b_router_expert,
         w1, w3, w2, norm_final_g, tm):
    n = x2d.shape[0]
    blk_rows = min(MOE_ROW_BLOCK, tm)
    assert n % tm == 0 and tm % blk_rows == 0
    n_steps = N_EXPERTS // EXPERTS_PER_STEP
    sr = tm + (MOE_GROUPS - 1) * blk_rows
    padw = LANE - N_EXPERTS - MOE_GROUPS
    wr = jnp.concatenate([w_router_expert, w_router_group,
                          jnp.zeros((D_MODEL, padw), F32)], axis=1).astype(BF16)
    br = jnp.concatenate([b_router_expert, b_router_group,
                          jnp.zeros((padw,), F32)]).astype(F32).reshape(1, LANE)
    tri = jnp.asarray(np.tril(np.ones((tm, tm), np.float32), -1), BF16)
    row = pl.BlockSpec((tm, D_MODEL), lambda i, e: (i, 0))
    const = lambda r, c: pl.BlockSpec((r, c), lambda i, e: (0, 0))
    return pl.pallas_call(
        functools.partial(_moe_kernel, n_steps=n_steps, blk_rows=blk_rows),
        out_shape=jax.ShapeDtypeStruct((n, D_MODEL), F32),
        grid=(n // tm, n_steps),
        in_specs=[row, const(1, D_MODEL), const(D_MODEL, LANE), const(1, LANE),
                  pl.BlockSpec((EXPERTS_PER_STEP, D_MODEL, EXPERT_FF), lambda i, e: (e, 0, 0)),
                  pl.BlockSpec((EXPERTS_PER_STEP, D_MODEL, EXPERT_FF), lambda i, e: (e, 0, 0)),
                  pl.BlockSpec((EXPERTS_PER_STEP, EXPERT_FF, D_MODEL), lambda i, e: (e, 0, 0)),
                  const(1, D_MODEL), const(tm, tm)],
        out_specs=row,
        scratch_shapes=[pltpu.VMEM((sr, D_MODEL), BF16), pltpu.VMEM((sr, LANE), F32),
                        pltpu.VMEM((sr, D_MODEL), F32), pltpu.VMEM((tm, sr), BF16),
                        pltpu.SMEM((2 * MOE_GROUPS,), I32)],
        compiler_params=pltpu.CompilerParams(
            dimension_semantics=("arbitrary", "arbitrary"), vmem_limit_bytes=VMEM_LIMIT),
        name="moe",
    )(x2d, norm_g.reshape(1, D_MODEL), wr, br, w1.astype(BF16), w3.astype(BF16),
      w2.astype(BF16), norm_final_g.reshape(1, D_MODEL), tri)


def _token_tile(n):
    for tm in (1024, 512, 256, 128, 64, 32, 16, 8):
        if n % tm == 0:
            return tm
    raise ValueError(f"token count {n} is not a multiple of 8")


def _pad_axis(a, axis, size):
    if a.shape[axis] == size:
        return a
    widths = [(0, 0)] * a.ndim
    widths[axis] = (0, size - a.shape[axis])
    return jnp.pad(a, widths)


def _round_up(n, m):
    return (n + m - 1) // m * m


def _stream_step(x, conv_state, ssm_state, k_past, v_past, kidx_past, w_packed, p):
    b, t, d = x.shape
    n = b * t
    tm = _token_tile(n)
    x2d = x.reshape(n, d)
    z, xbc, q, gate, k, v, qi, ki, dtw, k_b, v_b, ki_b = _in_proj(
        x2d, p["norm_mix_g"], w_packed, min(tm, PROJ_ROWS))

    yn, conv_new, ssm_new = _ssd(
        xbc.reshape(b, t, CONV_DIM), z.reshape(b, t, SSM_INNER), dtw.reshape(b, t, LANE),
        conv_state, ssm_state, p["conv_w"], p["conv_b"], p["dt_bias"], p["A_log"],
        p["D_skip"], p["ssm_norm_g"])

    kv_w = ATTN_KV_HEADS * HEAD_DIM
    k_all, v_all, kidx_all = (k_b.reshape(b, t, kv_w), v_b.reshape(b, t, kv_w),
                              ki_b.reshape(b, t, IDX_DIM))
    past_len = 0
    if k_past is not None:
        past_len = k_past.shape[1]
        k_all = jnp.concatenate([k_past.astype(BF16).reshape(b, past_len, kv_w), k_all], axis=1)
        v_all = jnp.concatenate([v_past.astype(BF16).reshape(b, past_len, kv_w), v_all], axis=1)
        kidx_all = jnp.concatenate([kidx_past.astype(BF16), kidx_all], axis=1)
    l_valid = past_len + t
    lp = _round_up(l_valid, 2 * KEY_TILE)
    if t % Q_BLOCK == 0:
        q_in = (q.reshape(b, t, ATTN_WIDTH), qi.reshape(b, t, IDX_HEADS * IDX_DIM),
                dtw.reshape(b, t, LANE))
    else:
        q_in = (q.reshape(b, t, ATTN_KV_HEADS, ATTN_REP, HEAD_DIM).transpose(0, 2, 3, 1, 4)
                .reshape(b, ATTN_KV_HEADS, ATTN_REP * t, HEAD_DIM),
                qi.reshape(b, t, IDX_HEADS, IDX_DIM).transpose(0, 2, 1, 3)
                .reshape(b, IDX_HEADS * t, IDX_DIM),
                dtw.reshape(b, t, LANE)[:, :, SSM_HEADS:SSM_HEADS + IDX_HEADS]
                .transpose(0, 2, 1).reshape(b, 1, IDX_HEADS * t))
    o = _dsa(*q_in, _pad_axis(kidx_all, 1, lp), _pad_axis(k_all, 1, lp), _pad_axis(v_all, 1, lp),
             past_len, l_valid, t)
    o = o.reshape(n, ATTN_WIDTH)

    x1 = _merge(x2d, yn.reshape(n, SSM_INNER), o, gate, p["w_ssm_out"], p["w_attn_out"],
                p["w_o"], min(tm, MERGE_TILE))
    y = _moe(x1, p["norm_moe_g"], p["w_router_group"], p["b_router_group"],
             p["w_router_expert"], p["b_router_expert"], p["w1"], p["w3"], p["w2"],
             p["norm_final_g"], tm)
    return (y.reshape(b, t, d), k.reshape(b, t, ATTN_KV_HEADS, HEAD_DIM),
            v.reshape(b, t, ATTN_KV_HEADS, HEAD_DIM), ki.reshape(b, t, IDX_DIM), conv_new, ssm_new)


def kernel(x_prompt, x_sample, cache_k, cache_v, cache_kidx, state_conv, state_ssm, norm_mix_g, w_in, conv_w, conv_b, dt_bias, A_log, D_skip, ssm_norm_g, w_ssm_out, w_attn_out, w_o, norm_moe_g, w_router_group, b_router_group, w_router_expert, b_router_expert, w1, w3, w2, norm_final_g):
    p = dict(norm_mix_g=norm_mix_g, conv_w=conv_w, conv_b=conv_b, dt_bias=dt_bias, A_log=A_log,
             D_skip=D_skip, ssm_norm_g=ssm_norm_g, w_ssm_out=w_ssm_out, w_attn_out=w_attn_out,
             w_o=w_o, norm_moe_g=norm_moe_g, w_router_group=w_router_group,
             b_router_group=b_router_group, w_router_expert=w_router_expert,
             b_router_expert=b_router_expert, w1=w1, w3=w3, w2=w2, norm_final_g=norm_final_g)
    w_packed = _pack_w_in(w_in)
    yp, kp, vp, kip, cp, sp = _stream_step(x_prompt, None, None, None, None, None, w_packed, p)
    ys, ksn, vsn, kisn, csn, ssn = _stream_step(x_sample, state_conv, state_ssm, cache_k,
                                                cache_v, cache_kidx, w_packed, p)
    return (yp, ys, kp, vp, kip, cp, sp, ksn, vsn, kisn, csn, ssn)
```

```python
import functools

import numpy as np
import jax
import jax.numpy as jnp
from jax import lax
from jax.experimental import pallas as pl
from jax.experimental.pallas import tpu as pltpu

F32 = jnp.float32
BF16 = jnp.bfloat16
I32 = jnp.int32

D_MODEL = 1024
CHUNK = 64
CHUNK_SHIFT = 6
EPS = 1e-6
SSM_INNER = 2048
SSM_HEADDIM = 64
SSM_HEADS = 32
SSM_GROUPS = 4
SSM_HEADS_PER_GROUP = SSM_HEADS // SSM_GROUPS
SSM_STATE = 128
CONV_WIDTH = 4
CONV_DIM = SSM_INNER + 2 * SSM_GROUPS * SSM_STATE
CONV_CARRY = 16
SSD_CHUNKS_PER_STEP = 4
CONV_BLOCK = 128
ATTN_HEADS = 16
ATTN_KV_HEADS = 2
HEAD_DIM = 64
ATTN_REP = ATTN_HEADS // ATTN_KV_HEADS
ATTN_WIDTH = ATTN_HEADS * HEAD_DIM
IDX_HEADS = 8
IDX_DIM = 64
IDX_SCALE = (IDX_HEADS * IDX_DIM) ** -0.5
TOPK_MAX = 256
MOE_GROUPS = 4
EXPERTS_PER_GROUP = 8
EXPERT_GROUP_SHIFT = 3
N_EXPERTS = 32
EXPERT_FF = 256
N_BRANCHES = 2

LANE = 128
VMEM_LIMIT = 56 * 1024 * 1024
PROJ_TILE = 1024
PROJ_ROWS = 512
PROJ_TILES = 9
Q_BLOCK = 128
SMALL_Q_BLOCK = 16
KEY_TILE = 256
KEY_INDEX_BITS = 15
COUNT_CHAINS = 4
EXPERTS_PER_STEP = 4
MOE_ROW_BLOCK = 128
MERGE_TILE = 512
INT_MIN = np.int32(-2 ** 31)
NEG_BIG = -1e30
Q_SCALE = HEAD_DIM ** -0.5 * float(np.log2(np.e))
PV_ROWS = 80

NT_DIMS = (((1,), (1,)), ((), ()))
TN_DIMS = (((0,), (0,)), ((), ()))


def _sigmoid(x):
    return 1.0 / (1.0 + jnp.exp(-x))


def _silu(x):
    h = 0.5 * x
    return h + h * jnp.tanh(h)


def _split3(x):
    hi = x.astype(BF16)
    r1 = x - hi.astype(F32)
    mid = r1.astype(BF16)
    r2 = r1 - mid.astype(F32)
    lo = r2.astype(BF16)
    return jnp.concatenate([hi, mid, lo], axis=1)


def _inproj_kernel(x_ref, g_ref, w_ref, z_ref, xbc_ref, q_ref, gate_ref, k_ref, v_ref,
                   qi_ref, ki_ref, dtw_ref, kb_ref, vb_ref, kib_ref):
    x = x_ref[...]
    ms = jnp.mean(x * x, axis=-1, keepdims=True)
    h = ((x * lax.rsqrt(ms + EPS)) * g_ref[...]).astype(BF16)

    def cols(first, width):
        lo = first * PROJ_TILE
        return jnp.dot(h, w_ref[:, lo:lo + width], preferred_element_type=F32)

    for j in range(2):
        z_ref[:, j * PROJ_TILE:(j + 1) * PROJ_TILE] = cols(j, PROJ_TILE).astype(BF16)
    for j in range(3):
        xbc_ref[:, j * PROJ_TILE:(j + 1) * PROJ_TILE] = cols(2 + j, PROJ_TILE).astype(BF16)
    q_ref[...] = (cols(5, PROJ_TILE) * Q_SCALE).astype(BF16)
    for j in range(2):
        gate_ref[:, j * PROJ_TILE:(j + 1) * PROJ_TILE] = cols(6 + j, PROJ_TILE).astype(BF16)
    t = cols(8, PROJ_TILE)
    k_ref[...] = t[:, 0:128]
    v_ref[...] = t[:, 128:256]
    qi_ref[...] = t[:, 256:768].astype(BF16)
    ki_ref[...] = t[:, 768:832]
    dtw_ref[...] = t[:, 896:1024]
    kb_ref[...] = t[:, 0:128].astype(BF16)
    vb_ref[...] = t[:, 128:256].astype(BF16)
    kib_ref[...] = t[:, 768:832].astype(BF16)


def _pack_w_in(w_in):
    sizes = (SSM_INNER, CONV_DIM, SSM_HEADS, ATTN_WIDTH, ATTN_KV_HEADS * HEAD_DIM,
             ATTN_KV_HEADS * HEAD_DIM, IDX_HEADS * IDX_DIM, IDX_DIM, IDX_HEADS,
             N_BRANCHES * D_MODEL)
    offs = np.concatenate([[0], np.cumsum(sizes)])
    z, xbc, dt, q, k, v, qi, ki, wi, g = [w_in[:, offs[i]:offs[i + 1]] for i in range(10)]
    zeros = lambda n: jnp.zeros((D_MODEL, n), w_in.dtype)
    cols = [z, xbc, q, g, k, v, qi, ki, zeros(64), dt, wi, zeros(LANE - SSM_HEADS - IDX_HEADS)]
    return jnp.concatenate(cols, axis=1).astype(BF16)


def _in_proj(x2d, norm_g, w_packed, tm):
    n = x2d.shape[0]
    assert n % tm == 0
    bf = lambda c: jax.ShapeDtypeStruct((n, c), BF16)
    ff = lambda c: jax.ShapeDtypeStruct((n, c), F32)
    rows = lambda c: pl.BlockSpec((tm, c), lambda i: (i, 0))
    widths = (SSM_INNER, CONV_DIM, ATTN_WIDTH, 2 * D_MODEL, 128, 128, 512, IDX_DIM, 128,
              128, 128, IDX_DIM)
    return pl.pallas_call(
        _inproj_kernel,
        out_shape=(bf(SSM_INNER), bf(CONV_DIM), bf(ATTN_WIDTH), bf(2 * D_MODEL),
                   ff(128), ff(128), bf(512), ff(IDX_DIM), ff(128), bf(128), bf(128), bf(IDX_DIM)),
        grid=(n // tm,),
        in_specs=[
            rows(D_MODEL),
            pl.BlockSpec((1, D_MODEL), lambda i: (0, 0)),
            pl.BlockSpec((D_MODEL, PROJ_TILES * PROJ_TILE), lambda i: (0, 0),
                         pipeline_mode=pl.Buffered(1)),
        ],
        out_specs=tuple(rows(c) for c in widths),
        compiler_params=pltpu.CompilerParams(
            dimension_semantics=("arbitrary",), vmem_limit_bytes=VMEM_LIMIT),
        name="in_proj",
    )(x2d, norm_g.reshape(1, D_MODEL), w_packed)


def _ssd_kernel(*refs, L, cps, nsteps, has_state, first_shift):
    if has_state:
        (xbc_ref, z_ref, dtw_ref, cs_ref, s0_ref, shift_ref, convw_ref, convb_ref, dtb_ref, a_ref,
         dx_ref, ng_ref, e3p_ref, e3l_ref, tri_ref, bd_ref, yn_ref, cnew_ref, sout_ref,
         xe_ref, st, act_ref, ex_ref, yz_ref, acl_ref) = refs
    else:
        (xbc_ref, z_ref, dtw_ref, shift_ref, convw_ref, convb_ref, dtb_ref, a_ref,
         dx_ref, ng_ref, e3p_ref, e3l_ref, tri_ref, bd_ref, yn_ref, cnew_ref, sout_ref,
         xe_ref, st, act_ref, ex_ref, yz_ref, acl_ref) = refs
    c = pl.program_id(1)
    G, R, N, P = SSM_GROUPS, SSM_HEADS_PER_GROUP, SSM_STATE, SSM_HEADDIM
    GW = R * P
    RL = R * L
    LB = cps * L

    CB = min(LB, CONV_BLOCK)
    @pl.when(c == 0)
    def _():
        xe_ref[CB:, :] = jnp.zeros((xe_ref.shape[0] - CB, CONV_DIM), BF16)
        if has_state:
            xe_ref[CB:CB + CONV_CARRY, :] = cs_ref[0]
            st[...] = s0_ref[0].T
        else:
            st[...] = jnp.zeros(st.shape, F32)

    w = convw_ref[...]
    for cb in range(LB // CB):
        blk = slice(cb * CB, (cb + 1) * CB)
        xe_ref[0:CB, :] = xbc_ref[0, blk, :]
        which = jnp.where(c == 0, first_shift, 1) if cb == 0 else 1
        shifted = jnp.dot(shift_ref[which], xe_ref[...], preferred_element_type=F32)
        conv = convb_ref[...] + shifted[2 * CB:3 * CB, :] * w[0:1, :]
        conv = conv + shifted[CB:2 * CB, :] * w[1:2, :]
        conv = conv + shifted[0:CB, :] * w[2:3, :]
        conv = conv + xbc_ref[0, blk, :].astype(F32) * w[3:4, :]
        act_ref[blk, :] = _silu(conv)
        if nsteps * (LB // CB) > 1:
            xe_ref[CB:CB + CONV_CARRY, :] = xe_ref[CB - CONV_CARRY:CB, :]

    @pl.when(c == nsteps - 1)
    def _():
        cnew_ref[0] = xbc_ref[0, LB - CONV_CARRY:LB, :].astype(F32)[
            CONV_CARRY - (CONV_WIDTH - 1):CONV_CARRY, :]

    u = dtw_ref[0] + dtb_ref[...]
    dt = jnp.maximum(u, 0.0) + jnp.log1p(jnp.exp(-jnp.abs(u)))
    a = dt * a_ref[...]
    ac3 = jnp.dot(tri_ref[...], _split3(a), preferred_element_type=F32)
    a_cum = ac3[:, 0:LANE] + ac3[:, LANE:2 * LANE] + ac3[:, 2 * LANE:]

    ex_ref[...] = jnp.dot(_split3(jnp.concatenate([a_cum, dt], axis=0)), e3p_ref[...],
                          preferred_element_type=F32)
    if L != P:
        acl_ref[...] = jnp.dot(_split3(a_cum), e3l_ref[...], preferred_element_type=F32)
    row = lax.broadcasted_iota(I32, (L, RL), 0)
    scol = lax.broadcasted_iota(I32, (L, RL), 1) & (L - 1)

    for sub in range(cps):
        r0 = sub * L
        rows = slice(r0, r0 + L)
        ssq = jnp.zeros((L, 1), F32)
        for g in range(G):
            sl = slice(g * GW, (g + 1) * GW)
            xs = act_ref[rows, sl]
            b_g = act_ref[rows, SSM_INNER + g * N:SSM_INNER + (g + 1) * N].astype(BF16)
            c_g = act_ref[rows, SSM_INNER + (G + g) * N:SSM_INNER + (G + g + 1) * N].astype(BF16)
            acx = ex_ref[rows, sl]
            alast = ex_ref[r0 + L - 1:r0 + L, sl]
            xdt = xs * ex_ref[LB + r0:LB + r0 + L, sl]
            acl_g = acx if L == P else acl_ref[rows, g * RL:(g + 1) * RL]
            acs_g = jnp.sum(jnp.where(scol == row, acl_g, 0.0), axis=0, keepdims=True)
            decay = jnp.where(scol <= row, jnp.exp(acl_g - acs_g), 0.0)
            cb = lax.dot_general(c_g, jnp.concatenate([b_g] * R, axis=0), NT_DIMS,
                                 preferred_element_type=F32)
            m_g = (cb * decay).astype(BF16)
            xbd = jnp.concatenate([xdt.astype(BF16)] * R, axis=0) * bd_ref[...]
            y_in = jnp.dot(m_g, xbd, preferred_element_type=F32)
            st_g = st[:, sl]
            y_st = jnp.dot(c_g, st_g.astype(BF16), preferred_element_type=F32)
            xw = (xdt * jnp.exp(alast - acx)).astype(BF16)
            upd = lax.dot_general(b_g, xw, TN_DIMS, preferred_element_type=F32)
            st[:, sl] = st_g * jnp.exp(alast) + upd
            y = y_in + jnp.exp(acx) * y_st + dx_ref[:, sl] * xs
            yz = y * _silu(z_ref[0, rows, sl].astype(F32))
            ssq = ssq + jnp.sum(yz * yz, axis=-1, keepdims=True)
            yz_ref[rows, sl] = yz
        scale = lax.rsqrt(ssq * (1.0 / SSM_INNER) + EPS)
        yn_ref[0, rows, :] = ((yz_ref[rows, :] * scale) * ng_ref[...]).astype(BF16)

    @pl.when(c == nsteps - 1)
    def _():
        sout_ref[0] = st[...].T


def _shift_matrices(L, conv_rows):
    out = np.zeros((2, 3 * L, conv_rows), np.float32)
    for k in range(CONV_WIDTH - 1):
        for t in range(L):
            src = t - (k + 1)
            if src >= 0:
                out[:, k * L + t, src] = 1.0
            else:
                out[1, k * L + t, L + CONV_CARRY + src] = 1.0
                j = CONV_WIDTH - 1 + src
                for piece in range(3):
                    out[0, k * L + t, L + 3 * piece + j] = 1.0
    return jnp.asarray(out, BF16)


def _expand_matrix(width_per_head):
    r = np.arange(3 * LANE)[:, None] % LANE
    c = np.arange(SSM_HEADS * width_per_head)[None, :] // width_per_head
    return jnp.asarray((r == c).astype(np.float32), BF16)


def _ssd(xbc, z, dtw, conv_state, ssm_state, conv_w, conv_b, dt_bias, a_log, d_skip, ssm_norm_g):
    b, t, _ = xbc.shape
    L = min(CHUNK, t)
    nc = t // L
    assert t % L == 0 and L & (L - 1) == 0
    has_state = conv_state is not None
    pad = lambda v: jnp.pad(v.astype(F32), (0, LANE - SSM_HEADS)).reshape(1, LANE)
    a_neg = pad(-jnp.exp(a_log.astype(F32)))
    cps = SSD_CHUNKS_PER_STEP if nc % SSD_CHUNKS_PER_STEP == 0 else 1
    nsteps = nc // cps
    lb = cps * L
    cb_rows = min(lb, CONV_BLOCK)
    conv_rows = _round_up(cb_rows + CONV_CARRY, LANE)
    tri = jnp.asarray(np.kron(np.eye(cps), np.tril(np.ones((L, L)))).astype(np.float32), BF16)
    rl = SSM_HEADS_PER_GROUP * L
    gw = SSM_HEADS_PER_GROUP * SSM_HEADDIM
    bd = jnp.asarray((np.arange(rl)[:, None] // L == np.arange(gw)[None, :] // SSM_HEADDIM)
                     .astype(np.float32), BF16)
    const = lambda shape: pl.BlockSpec(shape, lambda i, j: (0,) * len(shape))
    tok = lambda w: pl.BlockSpec((1, lb, w), lambda i, j: (i, j, 0))
    per_b = lambda s: pl.BlockSpec((1,) + s, lambda i, j: (i, 0, 0))
    ins = [xbc, z, dtw]
    specs = [tok(CONV_DIM), tok(SSM_INNER), tok(LANE)]
    if has_state:
        triples = _split3(conv_state.astype(F32))
        ins += [_pad_axis(triples, 1, CONV_CARRY), ssm_state.reshape(b, SSM_INNER, SSM_STATE)]
        specs += [per_b((CONV_CARRY, CONV_DIM)), per_b((SSM_INNER, SSM_STATE))]
    first = 0 if has_state else 1
    ins += [_shift_matrices(cb_rows, conv_rows)]
    specs += [const((2, 3 * cb_rows, conv_rows))]
    ins += [conv_w, conv_b.reshape(1, CONV_DIM), pad(dt_bias), a_neg,
            jnp.repeat(d_skip.astype(F32), SSM_HEADDIM).reshape(1, SSM_INNER),
            ssm_norm_g.reshape(1, SSM_INNER), _expand_matrix(SSM_HEADDIM), _expand_matrix(L),
            tri, bd]
    specs += [const((CONV_WIDTH, CONV_DIM)), const((1, CONV_DIM)), const((1, LANE)),
              const((1, LANE)), const((1, SSM_INNER)), const((1, SSM_INNER)),
              const((3 * LANE, SSM_INNER)), const((3 * LANE, SSM_HEADS * L)),
              const((lb, lb)), const((rl, gw))]
    yn, cnew, sout = pl.pallas_call(
        functools.partial(_ssd_kernel, L=L, cps=cps, nsteps=nsteps, has_state=has_state,
                          first_shift=first),
        out_shape=(jax.ShapeDtypeStruct((b, t, SSM_INNER), BF16),
                   jax.ShapeDtypeStruct((b, CONV_WIDTH - 1, CONV_DIM), F32),
                   jax.ShapeDtypeStruct((b, SSM_INNER, SSM_STATE), F32)),
        grid=(b, nsteps),
        in_specs=specs,
        out_specs=(tok(SSM_INNER), per_b((CONV_WIDTH - 1, CONV_DIM)),
                   per_b((SSM_INNER, SSM_STATE))),
        scratch_shapes=[pltpu.VMEM((conv_rows, CONV_DIM), BF16),
                        pltpu.VMEM((SSM_STATE, SSM_INNER), F32),
                        pltpu.VMEM((lb, CONV_DIM), F32),
                        pltpu.VMEM((2 * lb, SSM_INNER), F32),
                        pltpu.VMEM((lb, SSM_INNER), F32),
                        pltpu.VMEM((lb, SSM_HEADS * L), F32)],
        compiler_params=pltpu.CompilerParams(
            dimension_semantics=("arbitrary", "arbitrary"), vmem_limit_bytes=VMEM_LIMIT),
        name="ssd",
    )(*ins)
    return yn, cnew, sout.reshape(b, SSM_HEADS, SSM_HEADDIM, SSM_STATE)


def _dsa_kernel(q_ref, qi_ref, w_ref, eq_ref, kidx_ref, k_ref, v_ref, o_ref,
                qs_ref, qx_ref, key_ref, kx_ref, vx_ref, m_ref, acc_ref, mlim_ref, s_ref,
                *, past_len, l_valid, topk, nq):
    qb = pl.program_id(1)
    TQ, TK = LANE, KEY_TILE
    W = ATTN_REP * nq
    wide = nq == LANE

    lane = lax.broadcasted_iota(I32, (1, TQ), 1)
    pos = past_len + qb * nq + (lane & (nq - 1))
    n_adm = jnp.minimum(((pos >> CHUNK_SHIFT) + 1) * CHUNK, l_valid)
    k_eff = jnp.minimum(n_adm, topk)
    last_pos = past_len + qb * nq + nq - 1
    n_max = jnp.minimum(((last_pos >> CHUNK_SHIFT) + 1) * CHUNK, l_valid)
    nt = (n_max + TK - 1) // TK
    ntp = 2 * ((nt + 1) // 2)
    lp = key_ref.shape[0]

    @pl.when(qb == 0)
    def _():
        kx_ref[:, LANE:] = k_ref[0]

        def v_tile(t, carry):
            s0 = pl.multiple_of(t * TK, TK)
            v_t = v_ref[0, pl.ds(s0, TK), :].astype(F32).T
            for g in range(ATTN_KV_HEADS):
                vx_ref[g, 0:HEAD_DIM, pl.ds(s0, TK)] = (
                    v_t[g * HEAD_DIM:(g + 1) * HEAD_DIM, :].astype(BF16))
            return carry

        lax.fori_loop(0, lp // TK, v_tile, 0)
        tail_row = lax.broadcasted_iota(I32, (PV_ROWS - HEAD_DIM, lp), 0)
        for g in range(ATTN_KV_HEADS):
            vx_ref[g, HEAD_DIM:, :] = jnp.where(tail_row == 0, 1.0, 0.0).astype(BF16)
            qx_ref[g, :, 0:LANE] = eq_ref[...]
            qx_ref[g, :, LANE:] = jnp.zeros((W, LANE), BF16)

    if wide:
        for h in range(IDX_HEADS):
            qs_ref[h * nq:(h + 1) * nq, :] = qi_ref[0, :, h * IDX_DIM:(h + 1) * IDX_DIM]
        w_t = w_ref[0].T[SSM_HEADS:SSM_HEADS + IDX_HEADS, :] * IDX_SCALE
    else:
        qs_ref[...] = qi_ref[0]
        w_row = w_ref[0] * IDX_SCALE

    def index_dots(t, slot):
        s0 = pl.multiple_of(jnp.minimum(t, nt - 1) * TK, TK)
        s_ref[0, slot] = lax.dot_general(kidx_ref[0, pl.ds(s0, TK), :], qs_ref[...], NT_DIMS,
                                         preferred_element_type=F32)

    def score_tile(t, slot):
        s0 = pl.multiple_of(t * TK, TK)
        d = s_ref[0, slot]
        if wide:
            sc = w_t[0:1, :] * jnp.maximum(d[:, 0:nq], 0.0)
            for h in range(1, IDX_HEADS):
                sc = sc + w_t[h:h + 1, :] * jnp.maximum(d[:, h * nq:(h + 1) * nq], 0.0)
        else:
            sc = w_row * jnp.maximum(d, 0.0)
            for shift in [nq << i for i in reversed(range(IDX_HEADS.bit_length() - 1))]:
                sc = sc + pltpu.roll(sc, shift, axis=1)
        sc = sc + 0.0
        bits = pltpu.bitcast(sc, I32)
        key = jnp.where(bits < 0, bits ^ np.int32(0x7FFFFFFF), bits)
        s_idx = s0 + lax.broadcasted_iota(I32, (TK, TQ), 0)
        key_ref[pl.ds(s0, TK), :] = jnp.where(s_idx < n_adm, key, INT_MIN)

    index_dots(0, 0)

    def score_pair(u, carry):
        index_dots(2 * u + 1, 1)
        score_tile(2 * u, 0)
        index_dots(2 * u + 2, 0)
        score_tile(2 * u + 1, 1)
        return carry

    lax.fori_loop(0, nt // 2, score_pair, 0)

    @pl.when(nt < ntp)
    def _():
        score_tile(nt - 1, 0)
        key_ref[pl.ds(pl.multiple_of(nt * TK, TK), TK), :] = jnp.full((TK, TQ), INT_MIN, I32)

    def count(pred):
        n_vregs = 2 * TK // 8

        def body(t, accs):
            s0 = pl.multiple_of(t * (2 * TK), 2 * TK)
            blk = key_ref[pl.ds(s0, 2 * TK), :].reshape(n_vregs, 8, TQ)
            s_idx = s0 + lax.broadcasted_iota(I32, (2 * TK, TQ), 0).reshape(n_vregs, 8, TQ)
            accs = list(accs)
            for r in range(n_vregs):
                a = accs[r % COUNT_CHAINS]
                accs[r % COUNT_CHAINS] = jnp.where(pred(blk[r], s_idx[r]), a + 1, a)
            return tuple(accs)

        zero = jnp.zeros((8, TQ), I32)
        accs = lax.fori_loop(0, ntp // 2, body, (zero,) * COUNT_CHAINS)
        acc = accs[0]
        for a in accs[1:]:
            acc = acc + a
        return jnp.sum(acc, axis=0, keepdims=True)

    def bit_step(i, prefix):
        cand = prefix | jnp.left_shift(jnp.int32(1), 31 - i)
        cand_s = cand ^ INT_MIN
        cnt = count(lambda blk, s_idx: blk >= cand_s)
        return jnp.where(cnt >= k_eff, cand, prefix)

    thr = lax.fori_loop(0, 32, bit_step, jnp.zeros((1, TQ), I32)) ^ INT_MIN

    n_gt = count(lambda blk, s_idx: blk > thr)
    n_eq = count(lambda blk, s_idx: blk == thr)
    need = k_eff - n_gt
    mlim_ref[...] = jnp.full((1, TQ), 1 << KEY_INDEX_BITS, I32)

    @pl.when(jnp.max(n_eq - need) > 0)
    def _():
        def idx_step(i, prefix):
            cand = prefix | jnp.left_shift(jnp.int32(1), KEY_INDEX_BITS - 1 - i)
            cnt = count(lambda blk, s_idx: (blk == thr) & (s_idx < cand))
            return jnp.where(cnt < need, cand, prefix)
        mlim_ref[...] = lax.fori_loop(0, KEY_INDEX_BITS, idx_step, jnp.zeros((1, TQ), I32))

    mlim = mlim_ref[...]

    def bias_tile(t, carry):
        s0 = pl.multiple_of(t * TK, TK)
        blk = key_ref[pl.ds(s0, TK), :]
        s_idx = s0 + lax.broadcasted_iota(I32, (TK, TQ), 0)
        sel = (blk > thr) | ((blk == thr) & (s_idx <= mlim))
        kx_ref[pl.ds(s0, TK), 0:LANE] = jnp.where(sel, 0.0, NEG_BIG).astype(BF16)
        return carry

    lax.fori_loop(0, nt, bias_tile, 0)

    for g in range(ATTN_KV_HEADS):
        lanes = slice(LANE + g * HEAD_DIM, LANE + (g + 1) * HEAD_DIM)
        if wide:
            for r in range(ATTN_REP):
                hq = g * ATTN_REP + r
                qx_ref[g, r * nq:(r + 1) * nq, lanes] = (
                    q_ref[0, :, hq * HEAD_DIM:(hq + 1) * HEAD_DIM])
        else:
            qx_ref[g, :, lanes] = q_ref[0, g]
    m_ref[...] = jnp.full(m_ref.shape, -jnp.inf, F32)
    acc_ref[...] = jnp.zeros(acc_ref.shape, F32)

    def scores(t, slot):
        s0 = pl.multiple_of(jnp.minimum(t, nt - 1) * TK, TK)
        k_tile = kx_ref[pl.ds(s0, TK), :]
        for g in range(ATTN_KV_HEADS):
            s_ref[g, slot] = lax.dot_general(k_tile, qx_ref[g], NT_DIMS,
                                             preferred_element_type=F32)

    def accumulate(t, slot):
        s0 = pl.multiple_of(t * TK, TK)
        for g in range(ATTN_KV_HEADS):
            s = s_ref[g, slot]
            m_prev = m_ref[g]
            m_new = jnp.maximum(m_prev, jnp.max(s, axis=0, keepdims=True))
            p = jnp.exp2(s - m_new)
            acc_ref[g] = jnp.exp2(m_prev - m_new) * acc_ref[g] + jnp.dot(
                vx_ref[g, :, pl.ds(s0, TK)], p.astype(BF16), preferred_element_type=F32)
            m_ref[g] = m_new

    scores(0, 0)

    def attn_pair(u, carry):
        scores(2 * u + 1, 1)
        accumulate(2 * u, 0)
        scores(2 * u + 2, 0)
        accumulate(2 * u + 1, 1)
        return carry

    lax.fori_loop(0, nt // 2, attn_pair, 0)

    @pl.when(nt < ntp)
    def _():
        accumulate(nt - 1, 0)

    for g in range(ATTN_KV_HEADS):
        acc = acc_ref[g]
        o_t = acc[0:HEAD_DIM, :] * (1.0 / acc[HEAD_DIM:HEAD_DIM + 1, :])
        if wide:
            for r in range(0, ATTN_REP, 2):
                hq = g * ATTN_REP + r
                pair = jnp.concatenate([o_t[:, r * nq:(r + 1) * nq],
                                        o_t[:, (r + 1) * nq:(r + 2) * nq]], axis=0)
                o_ref[0, :, hq * HEAD_DIM:(hq + 2) * HEAD_DIM] = pair.T.astype(BF16)
        else:
            o_rq = jnp.concatenate([o_t, o_t], axis=0).T[:, 0:HEAD_DIM].astype(BF16)
            for r in range(ATTN_REP):
                hq = g * ATTN_REP + r
                o_ref[0, :, hq * HEAD_DIM:(hq + 1) * HEAD_DIM] = o_rq[r * nq:(r + 1) * nq, :]


def _dsa(q, qi, w, kidx_all, k_all, v_all, past_len, l_valid, t):
    b = kidx_all.shape[0]
    lp = kidx_all.shape[1]
    nq = Q_BLOCK if t % Q_BLOCK == 0 else SMALL_Q_BLOCK
    assert t % nq == 0 and lp % (2 * KEY_TILE) == 0 and lp < 1 << KEY_INDEX_BITS
    topk = min(TOPK_MAX, l_valid // 4)
    width = ATTN_REP * nq
    if nq == Q_BLOCK:
        qblk = lambda c: pl.BlockSpec((1, nq, c), lambda i, j: (i, j, 0))
        q_specs = [qblk(ATTN_WIDTH), qblk(IDX_HEADS * IDX_DIM), qblk(LANE)]
    else:
        assert t == nq
        q_specs = [pl.BlockSpec((1, ATTN_KV_HEADS, width, HEAD_DIM), lambda i, j: (i, 0, 0, 0)),
                   pl.BlockSpec((1, IDX_HEADS * nq, IDX_DIM), lambda i, j: (i, 0, 0)),
                   pl.BlockSpec((1, 1, LANE), lambda i, j: (i, 0, 0))]
    slot_onehot = jnp.asarray(
        (np.arange(width)[:, None] % nq == np.arange(LANE)[None, :]).astype(np.float32), BF16)
    per_stream = lambda c: pl.BlockSpec((1, lp, c), lambda i, j: (i, 0, 0))
    return pl.pallas_call(
        functools.partial(_dsa_kernel, past_len=past_len, l_valid=l_valid, topk=topk, nq=nq),
        out_shape=jax.ShapeDtypeStruct((b, t, ATTN_WIDTH), BF16),
        grid=(b, t // nq),
        in_specs=q_specs + [pl.BlockSpec((width, LANE), lambda i, j: (0, 0)),
                            per_stream(IDX_DIM), per_stream(LANE), per_stream(LANE)],
        out_specs=pl.BlockSpec((1, nq, ATTN_WIDTH), lambda i, j: (i, j, 0)),
        scratch_shapes=[pltpu.VMEM((width, IDX_DIM), BF16),
                        pltpu.VMEM((ATTN_KV_HEADS, width, 2 * LANE), BF16),
                        pltpu.VMEM((lp, LANE), I32),
                        pltpu.VMEM((lp, 2 * LANE), BF16),
                        pltpu.VMEM((ATTN_KV_HEADS, PV_ROWS, lp), BF16),
                        pltpu.VMEM((ATTN_KV_HEADS, 1, width), F32),
                        pltpu.VMEM((ATTN_KV_HEADS, PV_ROWS, width), F32),
                        pltpu.VMEM((1, LANE), I32),
                        pltpu.VMEM((ATTN_KV_HEADS, 2, KEY_TILE, width), F32)],
        compiler_params=pltpu.CompilerParams(
            dimension_semantics=("arbitrary", "arbitrary"), vmem_limit_bytes=VMEM_LIMIT),
        name="dsa",
    )(q, qi, w, slot_onehot, kidx_all, k_all, v_all)


def _merge_kernel(x_ref, yn_ref, o_ref, gate_ref, wa_ref, wb_ref, wo_ref, out_ref):
    br_a = jnp.dot(yn_ref[...], wa_ref[...], preferred_element_type=F32)
    br_b = jnp.dot(o_ref[...], wb_ref[...], preferred_element_type=F32)
    gates = gate_ref[...].astype(F32)
    merged = _sigmoid(gates[:, :D_MODEL]) * br_a + _sigmoid(gates[:, D_MODEL:]) * br_b
    out_ref[...] = x_ref[...] + jnp.dot(merged.astype(BF16), wo_ref[...],
                                        preferred_element_type=F32)


def _merge(x2d, yn, o, gate, w_ssm_out, w_attn_out, w_o, tm):
    n = x2d.shape[0]
    assert n % tm == 0
    row = lambda w: pl.BlockSpec((tm, w), lambda i: (i, 0))
    full = lambda r, c: pl.BlockSpec((r, c), lambda i: (0, 0))
    return pl.pallas_call(
        _merge_kernel,
        out_shape=jax.ShapeDtypeStruct((n, D_MODEL), F32),
        grid=(n // tm,),
        in_specs=[row(D_MODEL), row(SSM_INNER), row(ATTN_WIDTH), row(2 * D_MODEL),
                  full(SSM_INNER, D_MODEL), full(ATTN_WIDTH, D_MODEL), full(D_MODEL, D_MODEL)],
        out_specs=row(D_MODEL),
        compiler_params=pltpu.CompilerParams(
            dimension_semantics=("arbitrary",), vmem_limit_bytes=VMEM_LIMIT),
        name="merge",
    )(x2d, yn, o, gate, w_ssm_out.astype(BF16), w_attn_out.astype(BF16), w_o.astype(BF16))


def _moe_kernel(x_ref, ng_ref, wr_ref, br_ref, w1_ref, w3_ref, w2_ref, fg_ref, tri_ref, out_ref,
                hs_ref, combs_ref, acc_ref, pt_ref, seg_ref, *, n_steps, blk_rows):
    e_step = pl.program_id(1)
    tm = x_ref.shape[0]
    sr = hs_ref.shape[0]
    BLK = blk_rows

    @pl.when(e_step == 0)
    def _():
        x = x_ref[...]
        ms = jnp.mean(x * x, axis=-1, keepdims=True)
        h = ((x * lax.rsqrt(ms + EPS)) * ng_ref[...]).astype(BF16)
        logits = jnp.dot(h, wr_ref[...], preferred_element_type=F32) + br_ref[...]
        lane_i = lax.broadcasted_iota(I32, (tm, LANE), 1)
        lane = lane_i.astype(F32)
        first = lambda hit: jnp.min(jnp.where(hit, lane, float(LANE)), axis=1, keepdims=True)
        is_g = (lane_i >= N_EXPERTS) & (lane_i < N_EXPERTS + MOE_GROUPS)
        glog = jnp.where(is_g, logits, -jnp.inf)
        gmax = jnp.max(glog, axis=1, keepdims=True)
        gsel = first(glog == gmax) - float(N_EXPERTS)
        p_group = 1.0 / jnp.sum(jnp.exp(glog - gmax), axis=1, keepdims=True)
        grp = (lane_i >> EXPERT_GROUP_SHIFT).astype(F32)
        in_grp = (lane_i < N_EXPERTS) & (grp == gsel)
        el = jnp.where(in_grp, logits, -jnp.inf)
        v1 = jnp.max(el, axis=1, keepdims=True)
        i1 = first(el == v1)
        el2 = jnp.where(lane == i1, -jnp.inf, el)
        v2 = jnp.max(el2, axis=1, keepdims=True)
        i2 = first(el2 == v2)
        e21 = jnp.exp(v2 - v1)
        den = 1.0 + e21
        comb = (jnp.where(lane == i1, (1.0 / den) * p_group, 0.0)
                + jnp.where(lane == i2, (e21 / den) * p_group, 0.0))

        own = lane == gsel
        own_f = jnp.where(own, 1.0, 0.0)
        before = jnp.dot(tri_ref[...], own_f.astype(BF16), preferred_element_type=F32)
        rank = jnp.sum(jnp.where(own, before, 0.0), axis=1, keepdims=True)
        counts = jnp.sum(own_f, axis=0, keepdims=True)
        blocks = jnp.floor((counts + (BLK - 1)) * (1.0 / BLK))
        upper = (lax.broadcasted_iota(I32, (LANE, LANE), 0)
                 < lax.broadcasted_iota(I32, (LANE, LANE), 1))
        starts = jnp.dot(jnp.broadcast_to(blocks, (8, LANE)).astype(BF16),
                         jnp.where(upper, 1.0, 0.0).astype(BF16),
                         preferred_element_type=F32)[0:1, :] * BLK
        dest = jnp.sum(jnp.where(own, starts, 0.0), axis=1, keepdims=True) + rank
        for g in range(MOE_GROUPS):
            seg_ref[g] = jnp.sum(starts[:, g:g + 1]).astype(I32)
            seg_ref[MOE_GROUPS + g] = jnp.sum(blocks[:, g:g + 1]).astype(I32)

        dest_i = dest.astype(I32)
        pt = jnp.where(dest_i == lax.broadcasted_iota(I32, (tm, sr), 1), 1.0, 0.0).astype(BF16)
        pt_ref[...] = pt
        dest_row = jnp.broadcast_to(dest, (tm, LANE)).T[0:1, :].astype(I32)
        p_mat = jnp.where(lax.broadcasted_iota(I32, (sr, tm), 0) == dest_row, 1.0, 0.0
                          ).astype(BF16)
        hs_ref[...] = jnp.dot(p_mat, h, preferred_element_type=F32).astype(BF16)
        c3 = jnp.dot(p_mat, _split3(comb), preferred_element_type=F32)
        combs_ref[...] = c3[:, 0:LANE] + c3[:, LANE:2 * LANE] + c3[:, 2 * LANE:]
        acc_ref[...] = jnp.zeros(acc_ref.shape, F32)

    group = e_step // (EXPERTS_PER_GROUP // EXPERTS_PER_STEP)
    seg_start = seg_ref[group]

    def expert_rows(first_blk, n_rows):
        rows = pl.ds(pl.multiple_of(seg_start + first_blk * BLK, BLK), n_rows)
        h = hs_ref[rows, :]
        comb = combs_ref[rows, :]
        lane = lax.broadcasted_iota(I32, (n_rows, LANE), 1)
        for k in range(EXPERTS_PER_STEP):
            e = e_step * EXPERTS_PER_STEP + k
            a1 = jnp.dot(h, w1_ref[k], preferred_element_type=F32)
            a3 = jnp.dot(h, w3_ref[k], preferred_element_type=F32)
            he = (_silu(a1) * a3).astype(BF16)
            ye = jnp.dot(he, w2_ref[k], preferred_element_type=F32)
            wcol = jnp.sum(jnp.where(lane == e, comb, 0.0), axis=1, keepdims=True)
            acc_ref[rows, :] += wcol * ye

    n_blocks = seg_ref[MOE_GROUPS + group]
    odd_tail = (n_blocks % 2 == 1) & (n_blocks >= 3)
    n_pairs = jnp.where(odd_tail, (n_blocks - 3) // 2, n_blocks // 2)

    def block_pair(u, carry):
        expert_rows(2 * u, 2 * BLK)
        return carry

    lax.fori_loop(0, n_pairs, block_pair, 0)

    @pl.when(odd_tail)
    def _():
        expert_rows(n_blocks - 3, 3 * BLK)

    @pl.when(n_blocks == 1)
    def _():
        expert_rows(0, BLK)

    @pl.when(e_step == n_steps - 1)
    def _():
        y = x_ref[...] + jnp.dot(pt_ref[...], acc_ref[...].astype(BF16),
                                 preferred_element_type=F32)
        ms = jnp.mean(y * y, axis=-1, keepdims=True)
        out_ref[...] = (y * lax.rsqrt(ms + EPS)) * fg_ref[...]


def _moe(x2d, norm_g, w_router_group, b_router_group, w_router_expert, b_router_expert,
         w1, w3, w2, norm_final_g, tm):
    n = x2d.shape[0]
    blk_rows = min(MOE_ROW_BLOCK, tm)
    assert n % tm == 0 and tm % blk_rows == 0
    n_steps = N_EXPERTS // EXPERTS_PER_STEP
    sr = tm + (MOE_GROUPS - 1) * blk_rows
    padw = LANE - N_EXPERTS - MOE_GROUPS
    wr = jnp.concatenate([w_router_expert, w_router_group,
                          jnp.zeros((D_MODEL, padw), F32)], axis=1).astype(BF16)
    br = jnp.concatenate([b_router_expert, b_router_group,
                          jnp.zeros((padw,), F32)]).astype(F32).reshape(1, LANE)
    tri = jnp.asarray(np.tril(np.ones((tm, tm), np.float32), -1), BF16)
    row = pl.BlockSpec((tm, D_MODEL), lambda i, e: (i, 0))
    const = lambda r, c: pl.BlockSpec((r, c), lambda i, e: (0, 0))
    return pl.pallas_call(
        functools.partial(_moe_kernel, n_steps=n_steps, blk_rows=blk_rows),
        out_shape=jax.ShapeDtypeStruct((n, D_MODEL), F32),
        grid=(n // tm, n_steps),
        in_specs=[row, const(1, D_MODEL), const(D_MODEL, LANE), const(1, LANE),
                  pl.BlockSpec((EXPERTS_PER_STEP, D_MODEL, EXPERT_FF), lambda i, e: (e, 0, 0)),
                  pl.BlockSpec((EXPERTS_PER_STEP, D_MODEL, EXPERT_FF), lambda i, e: (e, 0, 0)),
                  pl.BlockSpec((EXPERTS_PER_STEP, EXPERT_FF, D_MODEL), lambda i, e: (e, 0, 0)),
                  const(1, D_MODEL), const(tm, tm)],
        out_specs=row,
        scratch_shapes=[pltpu.VMEM((sr, D_MODEL), BF16), pltpu.VMEM((sr, LANE), F32),
                        pltpu.VMEM((sr, D_MODEL), F32), pltpu.VMEM((tm, sr), BF16),
                        pltpu.SMEM((2 * MOE_GROUPS,), I32)],
        compiler_params=pltpu.CompilerParams(
            dimension_semantics=("arbitrary", "arbitrary"), vmem_limit_bytes=VMEM_LIMIT),
        name="moe",
    )(x2d, norm_g.reshape(1, D_MODEL), wr, br, w1.astype(BF16), w3.astype(BF16),
      w2.astype(BF16), norm_final_g.reshape(1, D_MODEL), tri)


def _token_tile(n):
    for tm in (1024, 512, 256, 128, 64, 32, 16, 8):
        if n % tm == 0:
            return tm
    raise ValueError(f"token count {n} is not a multiple of 8")


def _pad_axis(a, axis, size):
    if a.shape[axis] == size:
        return a
    widths = [(0, 0)] * a.ndim
    widths[axis] = (0, size - a.shape[axis])
    return jnp.pad(a, widths)


def _round_up(n, m):
    return (n + m - 1) // m * m


def _stream_step(x, conv_state, ssm_state, k_past, v_past, kidx_past, w_packed, p):
    b, t, d = x.shape
    n = b * t
    tm = _token_tile(n)
    x2d = x.reshape(n, d)
    z, xbc, q, gate, k, v, qi, ki, dtw, k_b, v_b, ki_b = _in_proj(
        x2d, p["norm_mix_g"], w_packed, min(tm, PROJ_ROWS))

    yn, conv_new, ssm_new = _ssd(
        xbc.reshape(b, t, CONV_DIM), z.reshape(b, t, SSM_INNER), dtw.reshape(b, t, LANE),
        conv_state, ssm_state, p["conv_w"], p["conv_b"], p["dt_bias"], p["A_log"],
        p["D_skip"], p["ssm_norm_g"])

    kv_w = ATTN_KV_HEADS * HEAD_DIM
    k_all, v_all, kidx_all = (k_b.reshape(b, t, kv_w), v_b.reshape(b, t, kv_w),
                              ki_b.reshape(b, t, IDX_DIM))
    past_len = 0
    if k_past is not None:
        past_len = k_past.shape[1]
        k_all = jnp.concatenate([k_past.astype(BF16).reshape(b, past_len, kv_w), k_all], axis=1)
        v_all = jnp.concatenate([v_past.astype(BF16).reshape(b, past_len, kv_w), v_all], axis=1)
        kidx_all = jnp.concatenate([kidx_past.astype(BF16), kidx_all], axis=1)
    l_valid = past_len + t
    lp = _round_up(l_valid, 2 * KEY_TILE)
    if t % Q_BLOCK == 0:
        q_in = (q.reshape(b, t, ATTN_WIDTH), qi.reshape(b, t, IDX_HEADS * IDX_DIM),
                dtw.reshape(b, t, LANE))
    else:
        q_in = (q.reshape(b, t, ATTN_KV_HEADS, ATTN_REP, HEAD_DIM).transpose(0, 2, 3, 1, 4)
                .reshape(b, ATTN_KV_HEADS, ATTN_REP * t, HEAD_DIM),
                qi.reshape(b, t, IDX_HEADS, IDX_DIM).transpose(0, 2, 1, 3)
                .reshape(b, IDX_HEADS * t, IDX_DIM),
                dtw.reshape(b, t, LANE)[:, :, SSM_HEADS:SSM_HEADS + IDX_HEADS]
                .transpose(0, 2, 1).reshape(b, 1, IDX_HEADS * t))
    o = _dsa(*q_in, _pad_axis(kidx_all, 1, lp), _pad_axis(k_all, 1, lp), _pad_axis(v_all, 1, lp),
             past_len, l_valid, t)
    o = o.reshape(n, ATTN_WIDTH)

    x1 = _merge(x2d, yn.reshape(n, SSM_INNER), o, gate, p["w_ssm_out"], p["w_attn_out"],
                p["w_o"], min(tm, MERGE_TILE))
    y = _moe(x1, p["norm_moe_g"], p["w_router_group"], p["b_router_group"],
             p["w_router_expert"], p["b_router_expert"], p["w1"], p["w3"], p["w2"],
             p["norm_final_g"], tm)
    return (y.reshape(b, t, d), k.reshape(b, t, ATTN_KV_HEADS, HEAD_DIM),
            v.reshape(b, t, ATTN_KV_HEADS, HEAD_DIM), ki.reshape(b, t, IDX_DIM), conv_new, ssm_new)


def kernel(x_prompt, x_sample, cache_k, cache_v, cache_kidx, state_conv, state_ssm, norm_mix_g, w_in, conv_w, conv_b, dt_bias, A_log, D_skip, ssm_norm_g, w_ssm_out, w_attn_out, w_o, norm_moe_g, w_router_group, b_router_group, w_router_expert, b_router_expert, w1, w3, w2, norm_final_g):
    p = dict(norm_mix_g=norm_mix_g, conv_w=conv_w, conv_b=conv_b, dt_bias=dt_bias, A_log=A_log,
             D_skip=D_skip, ssm_norm_g=ssm_norm_g, w_ssm_out=w_ssm_out, w_attn_out=w_attn_out,
             w_o=w_o, norm_moe_g=norm_moe_g, w_router_group=w_router_group,
             b_router_group=b_router_group, w_router_expert=w_router_expert,
             b_router_expert=b_router_expert, w1=w1, w3=w3, w2=w2, norm_final_g=norm_final_g)
    w_packed = _pack_w_in(w_in)
    yp, kp, vp, kip, cp, sp = _stream_step(x_prompt, None, None, None, None, None, w_packed, p)
    ys, ksn, vsn, kisn, csn, ssn = _stream_step(x_sample, state_conv, state_ssm, cache_k,
                                                cache_v, cache_kidx, w_packed, p)
    return (yp, ys, kp, vp, kip, cp, sp, ksn, vsn, kisn, csn, ssn)
```

```python
import functools

import numpy as np
import jax
import jax.numpy as jnp
from jax import lax
from jax.experimental import pallas as pl
from jax.experimental.pallas import tpu as pltpu

F32 = jnp.float32
BF16 = jnp.bfloat16
I32 = jnp.int32

D_MODEL = 1024
CHUNK = 64
CHUNK_SHIFT = 6
EPS = 1e-6
SSM_INNER = 2048
SSM_HEADDIM = 64
SSM_HEADS = 32
SSM_GROUPS = 4
SSM_HEADS_PER_GROUP = SSM_HEADS // SSM_GROUPS
SSM_STATE = 128
CONV_WIDTH = 4
CONV_DIM = SSM_INNER + 2 * SSM_GROUPS * SSM_STATE
CONV_CARRY = 16
SSD_CHUNKS_PER_STEP = 8
CONV_BLOCK = 128
ATTN_HEADS = 16
ATTN_KV_HEADS = 2
HEAD_DIM = 64
ATTN_REP = ATTN_HEADS // ATTN_KV_HEADS
ATTN_WIDTH = ATTN_HEADS * HEAD_DIM
IDX_HEADS = 8
IDX_DIM = 64
IDX_SCALE = (IDX_HEADS * IDX_DIM) ** -0.5
TOPK_MAX = 256
MOE_GROUPS = 4
EXPERTS_PER_GROUP = 8
EXPERT_GROUP_SHIFT = 3
N_EXPERTS = 32
EXPERT_FF = 256
N_BRANCHES = 2

LANE = 128
VMEM_LIMIT = 56 * 1024 * 1024
PROJ_TILE = 1024
PROJ_ROWS = 512
PROJ_TILES = 9
Q_BLOCK = 128
SMALL_Q_BLOCK = 16
KEY_TILE = 256
KEY_INDEX_BITS = 15
COUNT_CHAINS = 4
EXPERTS_PER_STEP = 4
MOE_ROW_BLOCK = 128
MERGE_TILE = 512
INT_MIN = np.int32(-2 ** 31)
NEG_BIG = -1e30
Q_SCALE = HEAD_DIM ** -0.5 * float(np.log2(np.e))
PV_ROWS = 80

NT_DIMS = (((1,), (1,)), ((), ()))
TN_DIMS = (((0,), (0,)), ((), ()))


def _sigmoid(x):
    return 1.0 / (1.0 + jnp.exp(-x))


def _silu(x):
    h = 0.5 * x
    return h + h * jnp.tanh(h)


def _split3(x):
    hi = x.astype(BF16)
    r1 = x - hi.astype(F32)
    mid = r1.astype(BF16)
    r2 = r1 - mid.astype(F32)
    lo = r2.astype(BF16)
    return jnp.concatenate([hi, mid, lo], axis=1)


def _inproj_kernel(x_ref, g_ref, w_ref, z_ref, xbc_ref, q_ref, gate_ref, k_ref, v_ref,
                   qi_ref, ki_ref, dtw_ref, kb_ref, vb_ref, kib_ref):
    x = x_ref[...]
    ms = jnp.mean(x * x, axis=-1, keepdims=True)
    h = ((x * lax.rsqrt(ms + EPS)) * g_ref[...]).astype(BF16)

    def cols(first, width):
        lo = first * PROJ_TILE
        return jnp.dot(h, w_ref[:, lo:lo + width], preferred_element_type=F32)

    for j in range(2):
        z_ref[:, j * PROJ_TILE:(j + 1) * PROJ_TILE] = cols(j, PROJ_TILE).astype(BF16)
    for j in range(3):
        xbc_ref[:, j * PROJ_TILE:(j + 1) * PROJ_TILE] = cols(2 + j, PROJ_TILE).astype(BF16)
    q_ref[...] = (cols(5, PROJ_TILE) * Q_SCALE).astype(BF16)
    for j in range(2):
        gate_ref[:, j * PROJ_TILE:(j + 1) * PROJ_TILE] = cols(6 + j, PROJ_TILE).astype(BF16)
    t = cols(8, PROJ_TILE)
    k_ref[...] = t[:, 0:128]
    v_ref[...] = t[:, 128:256]
    qi_ref[...] = t[:, 256:768].astype(BF16)
    ki_ref[...] = t[:, 768:832]
    dtw_ref[...] = t[:, 896:1024]
    kb_ref[...] = t[:, 0:128].astype(BF16)
    vb_ref[...] = t[:, 128:256].astype(BF16)
    kib_ref[...] = t[:, 768:832].astype(BF16)


def _pack_w_in(w_in):
    sizes = (SSM_INNER, CONV_DIM, SSM_HEADS, ATTN_WIDTH, ATTN_KV_HEADS * HEAD_DIM,
             ATTN_KV_HEADS * HEAD_DIM, IDX_HEADS * IDX_DIM, IDX_DIM, IDX_HEADS,
             N_BRANCHES * D_MODEL)
    offs = np.concatenate([[0], np.cumsum(sizes)])
    z, xbc, dt, q, k, v, qi, ki, wi, g = [w_in[:, offs[i]:offs[i + 1]] for i in range(10)]
    zeros = lambda n: jnp.zeros((D_MODEL, n), w_in.dtype)
    cols = [z, xbc, q, g, k, v, qi, ki, zeros(64), dt, wi, zeros(LANE - SSM_HEADS - IDX_HEADS)]
    return jnp.concatenate(cols, axis=1).astype(BF16)


def _in_proj(x2d, norm_g, w_packed, tm):
    n = x2d.shape[0]
    assert n % tm == 0
    bf = lambda c: jax.ShapeDtypeStruct((n, c), BF16)
    ff = lambda c: jax.ShapeDtypeStruct((n, c), F32)
    rows = lambda c: pl.BlockSpec((tm, c), lambda i: (i, 0))
    widths = (SSM_INNER, CONV_DIM, ATTN_WIDTH, 2 * D_MODEL, 128, 128, 512, IDX_DIM, 128,
              128, 128, IDX_DIM)
    return pl.pallas_call(
        _inproj_kernel,
        out_shape=(bf(SSM_INNER), bf(CONV_DIM), bf(ATTN_WIDTH), bf(2 * D_MODEL),
                   ff(128), ff(128), bf(512), ff(IDX_DIM), ff(128), bf(128), bf(128), bf(IDX_DIM)),
        grid=(n // tm,),
        in_specs=[
            rows(D_MODEL),
            pl.BlockSpec((1, D_MODEL), lambda i: (0, 0)),
            pl.BlockSpec((D_MODEL, PROJ_TILES * PROJ_TILE), lambda i: (0, 0),
                         pipeline_mode=pl.Buffered(1)),
        ],
        out_specs=tuple(rows(c) for c in widths),
        compiler_params=pltpu.CompilerParams(
            dimension_semantics=("arbitrary",), vmem_limit_bytes=VMEM_LIMIT),
        name="in_proj",
    )(x2d, norm_g.reshape(1, D_MODEL), w_packed)


def _ssd_kernel(*refs, L, cps, nsteps, has_state, first_shift):
    if has_state:
        (xbc_ref, z_ref, dtw_ref, cs_ref, s0_ref, shift_ref, convw_ref, convb_ref, dtb_ref, a_ref,
         dx_ref, ng_ref, e3p_ref, e3l_ref, tri_ref, bd_ref, yn_ref, cnew_ref, sout_ref,
         xe_ref, st, act_ref, ex_ref, yz_ref, acl_ref) = refs
    else:
        (xbc_ref, z_ref, dtw_ref, shift_ref, convw_ref, convb_ref, dtb_ref, a_ref,
         dx_ref, ng_ref, e3p_ref, e3l_ref, tri_ref, bd_ref, yn_ref, cnew_ref, sout_ref,
         xe_ref, st, act_ref, ex_ref, yz_ref, acl_ref) = refs
    c = pl.program_id(1)
    G, R, N, P = SSM_GROUPS, SSM_HEADS_PER_GROUP, SSM_STATE, SSM_HEADDIM
    GW = R * P
    RL = R * L
    LB = cps * L

    CB = min(LB, CONV_BLOCK)
    @pl.when(c == 0)
    def _():
        xe_ref[CB:, :] = jnp.zeros((xe_ref.shape[0] - CB, CONV_DIM), BF16)
        if has_state:
            xe_ref[CB:CB + CONV_CARRY, :] = cs_ref[0]
            st[...] = s0_ref[0].T
        else:
            st[...] = jnp.zeros(st.shape, F32)

    w = convw_ref[...]
    for cb in range(LB // CB):
        blk = slice(cb * CB, (cb + 1) * CB)
        xe_ref[0:CB, :] = xbc_ref[0, blk, :]
        which = jnp.where(c == 0, first_shift, 1) if cb == 0 else 1
        shifted = jnp.dot(shift_ref[which], xe_ref[...], preferred_element_type=F32)
        conv = convb_ref[...] + shifted[2 * CB:3 * CB, :] * w[0:1, :]
        conv = conv + shifted[CB:2 * CB, :] * w[1:2, :]
        conv = conv + shifted[0:CB, :] * w[2:3, :]
        conv = conv + xbc_ref[0, blk, :].astype(F32) * w[3:4, :]
        act_ref[blk, :] = _silu(conv)
        if nsteps * (LB // CB) > 1:
            xe_ref[CB:CB + CONV_CARRY, :] = xe_ref[CB - CONV_CARRY:CB, :]

    @pl.when(c == nsteps - 1)
    def _():
        cnew_ref[0] = xbc_ref[0, LB - CONV_CARRY:LB, :].astype(F32)[
            CONV_CARRY - (CONV_WIDTH - 1):CONV_CARRY, :]

    u = dtw_ref[0] + dtb_ref[...]
    dt = jnp.maximum(u, 0.0) + jnp.log1p(jnp.exp(-jnp.abs(u)))
    a = dt * a_ref[...]
    ac3 = jnp.dot(tri_ref[...], _split3(a), preferred_element_type=F32)
    a_cum = ac3[:, 0:LANE] + ac3[:, LANE:2 * LANE] + ac3[:, 2 * LANE:]

    ex_ref[...] = jnp.dot(_split3(jnp.concatenate([a_cum, dt], axis=0)), e3p_ref[...],
                          preferred_element_type=F32)
    if L != P:
        acl_ref[...] = jnp.dot(_split3(a_cum), e3l_ref[...], preferred_element_type=F32)
    row = lax.broadcasted_iota(I32, (L, RL), 0)
    scol = lax.broadcasted_iota(I32, (L, RL), 1) & (L - 1)

    for sub in range(cps):
        r0 = sub * L
        rows = slice(r0, r0 + L)
        ssq = jnp.zeros((L, 1), F32)
        for g in range(G):
            sl = slice(g * GW, (g + 1) * GW)
            xs = act_ref[rows, sl]
            b_g = act_ref[rows, SSM_INNER + g * N:SSM_INNER + (g + 1) * N].astype(BF16)
            c_g = act_ref[rows, SSM_INNER + (G + g) * N:SSM_INNER + (G + g + 1) * N].astype(BF16)
            acx = ex_ref[rows, sl]
            alast = ex_ref[r0 + L - 1:r0 + L, sl]
            xdt = xs * ex_ref[LB + r0:LB + r0 + L, sl]
            acl_g = acx if L == P else acl_ref[rows, g * RL:(g + 1) * RL]
            acs_g = jnp.sum(jnp.where(scol == row, acl_g, 0.0), axis=0, keepdims=True)
            decay = jnp.where(scol <= row, jnp.exp(acl_g - acs_g), 0.0)
            cb = lax.dot_general(c_g, jnp.concatenate([b_g] * R, axis=0), NT_DIMS,
                                 preferred_element_type=F32)
            m_g = (cb * decay).astype(BF16)
            xbd = jnp.concatenate([xdt.astype(BF16)] * R, axis=0) * bd_ref[...]
            y_in = jnp.dot(m_g, xbd, preferred_element_type=F32)
            st_g = st[:, sl]
            y_st = jnp.dot(c_g, st_g.astype(BF16), preferred_element_type=F32)
            xw = (xdt * jnp.exp(alast - acx)).astype(BF16)
            upd = lax.dot_general(b_g, xw, TN_DIMS, preferred_element_type=F32)
            st[:, sl] = st_g * jnp.exp(alast) + upd
            y = y_in + jnp.exp(acx) * y_st + dx_ref[:, sl] * xs
            yz = y * _silu(z_ref[0, rows, sl].astype(F32))
            ssq = ssq + jnp.sum(yz * yz, axis=-1, keepdims=True)
            yz_ref[rows, sl] = yz
        scale = lax.rsqrt(ssq * (1.0 / SSM_INNER) + EPS)
        yn_ref[0, rows, :] = ((yz_ref[rows, :] * scale) * ng_ref[...]).astype(BF16)

    @pl.when(c == nsteps - 1)
    def _():
        sout_ref[0] = st[...].T


def _shift_matrices(L, conv_rows):
    out = np.zeros((2, 3 * L, conv_rows), np.float32)
    for k in range(CONV_WIDTH - 1):
        for t in range(L):
            src = t - (k + 1)
            if src >= 0:
                out[:, k * L + t, src] = 1.0
            else:
                out[1, k * L + t, L + CONV_CARRY + src] = 1.0
                j = CONV_WIDTH - 1 + src
                for piece in range(3):
                    out[0, k * L + t, L + 3 * piece + j] = 1.0
    return jnp.asarray(out, BF16)


def _expand_matrix(width_per_head):
    r = np.arange(3 * LANE)[:, None] % LANE
    c = np.arange(SSM_HEADS * width_per_head)[None, :] // width_per_head
    return jnp.asarray((r == c).astype(np.float32), BF16)


def _ssd(xbc, z, dtw, conv_state, ssm_state, conv_w, conv_b, dt_bias, a_log, d_skip, ssm_norm_g):
    b, t, _ = xbc.shape
    L = min(CHUNK, t)
    nc = t // L
    assert t % L == 0 and L & (L - 1) == 0
    has_state = conv_state is not None
    pad = lambda v: jnp.pad(v.astype(F32), (0, LANE - SSM_HEADS)).reshape(1, LANE)
    a_neg = pad(-jnp.exp(a_log.astype(F32)))
    cps = SSD_CHUNKS_PER_STEP if nc % SSD_CHUNKS_PER_STEP == 0 else 1
    nsteps = nc // cps
    lb = cps * L
    cb_rows = min(lb, CONV_BLOCK)
    conv_rows = _round_up(cb_rows + CONV_CARRY, LANE)
    tri = jnp.asarray(np.kron(np.eye(cps), np.tril(np.ones((L, L)))).astype(np.float32), BF16)
    rl = SSM_HEADS_PER_GROUP * L
    gw = SSM_HEADS_PER_GROUP * SSM_HEADDIM
    bd = jnp.asarray((np.arange(rl)[:, None] // L == np.arange(gw)[None, :] // SSM_HEADDIM)
                     .astype(np.float32), BF16)
    const = lambda shape: pl.BlockSpec(shape, lambda i, j: (0,) * len(shape))
    tok = lambda w: pl.BlockSpec((1, lb, w), lambda i, j: (i, j, 0))
    per_b = lambda s: pl.BlockSpec((1,) + s, lambda i, j: (i, 0, 0))
    ins = [xbc, z, dtw]
    specs = [tok(CONV_DIM), tok(SSM_INNER), tok(LANE)]
    if has_state:
        triples = _split3(conv_state.astype(F32))
        ins += [_pad_axis(triples, 1, CONV_CARRY), ssm_state.reshape(b, SSM_INNER, SSM_STATE)]
        specs += [per_b((CONV_CARRY, CONV_DIM)), per_b((SSM_INNER, SSM_STATE))]
    first = 0 if has_state else 1
    ins += [_shift_matrices(cb_rows, conv_rows)]
    specs += [const((2, 3 * cb_rows, conv_rows))]
    ins += [conv_w, conv_b.reshape(1, CONV_DIM), pad(dt_bias), a_neg,
            jnp.repeat(d_skip.astype(F32), SSM_HEADDIM).reshape(1, SSM_INNER),
            ssm_norm_g.reshape(1, SSM_INNER), _expand_matrix(SSM_HEADDIM), _expand_matrix(L),
            tri, bd]
    specs += [const((CONV_WIDTH, CONV_DIM)), const((1, CONV_DIM)), const((1, LANE)),
              const((1, LANE)), const((1, SSM_INNER)), const((1, SSM_INNER)),
              const((3 * LANE, SSM_INNER)), const((3 * LANE, SSM_HEADS * L)),
              const((lb, lb)), const((rl, gw))]
    yn, cnew, sout = pl.pallas_call(
        functools.partial(_ssd_kernel, L=L, cps=cps, nsteps=nsteps, has_state=has_state,
                          first_shift=first),
        out_shape=(jax.ShapeDtypeStruct((b, t, SSM_INNER), BF16),
                   jax.ShapeDtypeStruct((b, CONV_WIDTH - 1, CONV_DIM), F32),
                   jax.ShapeDtypeStruct((b, SSM_INNER, SSM_STATE), F32)),
        grid=(b, nsteps),
        in_specs=specs,
        out_specs=(tok(SSM_INNER), per_b((CONV_WIDTH - 1, CONV_DIM)),
                   per_b((SSM_INNER, SSM_STATE))),
        scratch_shapes=[pltpu.VMEM((conv_rows, CONV_DIM), BF16),
                        pltpu.VMEM((SSM_STATE, SSM_INNER), F32),
                        pltpu.VMEM((lb, CONV_DIM), F32),
                        pltpu.VMEM((2 * lb, SSM_INNER), F32),
                        pltpu.VMEM((lb, SSM_INNER), F32),
                        pltpu.VMEM((lb, SSM_HEADS * L), F32)],
        compiler_params=pltpu.CompilerParams(
            dimension_semantics=("arbitrary", "arbitrary"), vmem_limit_bytes=VMEM_LIMIT),
        name="ssd",
    )(*ins)
    return yn, cnew, sout.reshape(b, SSM_HEADS, SSM_HEADDIM, SSM_STATE)


def _dsa_kernel(q_ref, qi_ref, w_ref, eq_ref, kidx_ref, k_ref, v_ref, o_ref,
                qs_ref, qx_ref, key_ref, kx_ref, vx_ref, m_ref, acc_ref, mlim_ref, s_ref,
                *, past_len, l_valid, topk, nq):
    qb = pl.program_id(1)
    TQ, TK = LANE, KEY_TILE
    W = ATTN_REP * nq
    wide = nq == LANE

    lane = lax.broadcasted_iota(I32, (1, TQ), 1)
    pos = past_len + qb * nq + (lane & (nq - 1))
    n_adm = jnp.minimum(((pos >> CHUNK_SHIFT) + 1) * CHUNK, l_valid)
    k_eff = jnp.minimum(n_adm, topk)
    last_pos = past_len + qb * nq + nq - 1
    n_max = jnp.minimum(((last_pos >> CHUNK_SHIFT) + 1) * CHUNK, l_valid)
    nt = (n_max + TK - 1) // TK
    ntp = 2 * ((nt + 1) // 2)
    lp = key_ref.shape[0]

    @pl.when(qb == 0)
    def _():
        kx_ref[:, LANE:] = k_ref[0]

        def v_tile(t, carry):
            s0 = pl.multiple_of(t * TK, TK)
            v_t = v_ref[0, pl.ds(s0, TK), :].astype(F32).T
            for g in range(ATTN_KV_HEADS):
                vx_ref[g, 0:HEAD_DIM, pl.ds(s0, TK)] = (
                    v_t[g * HEAD_DIM:(g + 1) * HEAD_DIM, :].astype(BF16))
            return carry

        lax.fori_loop(0, lp // TK, v_tile, 0)
        tail_row = lax.broadcasted_iota(I32, (PV_ROWS - HEAD_DIM, lp), 0)
        for g in range(ATTN_KV_HEADS):
            vx_ref[g, HEAD_DIM:, :] = jnp.where(tail_row == 0, 1.0, 0.0).astype(BF16)
            qx_ref[g, :, 0:LANE] = eq_ref[...]
            qx_ref[g, :, LANE:] = jnp.zeros((W, LANE), BF16)

    if wide:
        for h in range(IDX_HEADS):
            qs_ref[h * nq:(h + 1) * nq, :] = qi_ref[0, :, h * IDX_DIM:(h + 1) * IDX_DIM]
        w_t = w_ref[0].T[SSM_HEADS:SSM_HEADS + IDX_HEADS, :] * IDX_SCALE
    else:
        qs_ref[...] = qi_ref[0]
        w_row = w_ref[0] * IDX_SCALE

    def index_dots(t, slot):
        s0 = pl.multiple_of(jnp.minimum(t, nt - 1) * TK, TK)
        s_ref[0, slot] = lax.dot_general(kidx_ref[0, pl.ds(s0, TK), :], qs_ref[...], NT_DIMS,
                                         preferred_element_type=F32)

    def score_tile(t, slot):
        s0 = pl.multiple_of(t * TK, TK)
        d = s_ref[0, slot]
        if wide:
            sc = w_t[0:1, :] * jnp.maximum(d[:, 0:nq], 0.0)
            for h in range(1, IDX_HEADS):
                sc = sc + w_t[h:h + 1, :] * jnp.maximum(d[:, h * nq:(h + 1) * nq], 0.0)
        else:
            sc = w_row * jnp.maximum(d, 0.0)
            for shift in [nq << i for i in reversed(range(IDX_HEADS.bit_length() - 1))]:
                sc = sc + pltpu.roll(sc, shift, axis=1)
        sc = sc + 0.0
        bits = pltpu.bitcast(sc, I32)
        key = jnp.where(bits < 0, bits ^ np.int32(0x7FFFFFFF), bits)
        s_idx = s0 + lax.broadcasted_iota(I32, (TK, TQ), 0)
        key_ref[pl.ds(s0, TK), :] = jnp.where(s_idx < n_adm, key, INT_MIN)

    index_dots(0, 0)

    def score_pair(u, carry):
        index_dots(2 * u + 1, 1)
        score_tile(2 * u, 0)
        index_dots(2 * u + 2, 0)
        score_tile(2 * u + 1, 1)
        return carry

    lax.fori_loop(0, nt // 2, score_pair, 0)

    @pl.when(nt < ntp)
    def _():
        score_tile(nt - 1, 0)
        key_ref[pl.ds(pl.multiple_of(nt * TK, TK), TK), :] = jnp.full((TK, TQ), INT_MIN, I32)

    def count(pred):
        n_vregs = 2 * TK // 8

        def body(t, accs):
            s0 = pl.multiple_of(t * (2 * TK), 2 * TK)
            blk = key_ref[pl.ds(s0, 2 * TK), :].reshape(n_vregs, 8, TQ)
            s_idx = s0 + lax.broadcasted_iota(I32, (2 * TK, TQ), 0).reshape(n_vregs, 8, TQ)
            accs = list(accs)
            for r in range(n_vregs):
                a = accs[r % COUNT_CHAINS]
                accs[r % COUNT_CHAINS] = jnp.where(pred(blk[r], s_idx[r]), a + 1, a)
            return tuple(accs)

        zero = jnp.zeros((8, TQ), I32)
        accs = lax.fori_loop(0, ntp // 2, body, (zero,) * COUNT_CHAINS)
        acc = accs[0]
        for a in accs[1:]:
            acc = acc + a
        return jnp.sum(acc, axis=0, keepdims=True)

    def bit_step(i, prefix):
        cand = prefix | jnp.left_shift(jnp.int32(1), 31 - i)
        cand_s = cand ^ INT_MIN
        cnt = count(lambda blk, s_idx: blk >= cand_s)
        return jnp.where(cnt >= k_eff, cand, prefix)

    thr = lax.fori_loop(0, 32, bit_step, jnp.zeros((1, TQ), I32)) ^ INT_MIN

    n_gt = count(lambda blk, s_idx: blk > thr)
    n_eq = count(lambda blk, s_idx: blk == thr)
    need = k_eff - n_gt
    mlim_ref[...] = jnp.full((1, TQ), 1 << KEY_INDEX_BITS, I32)

    @pl.when(jnp.max(n_eq - need) > 0)
    def _():
        def idx_step(i, prefix):
            cand = prefix | jnp.left_shift(jnp.int32(1), KEY_INDEX_BITS - 1 - i)
            cnt = count(lambda blk, s_idx: (blk == thr) & (s_idx < cand))
            return jnp.where(cnt < need, cand, prefix)
        mlim_ref[...] = lax.fori_loop(0, KEY_INDEX_BITS, idx_step, jnp.zeros((1, TQ), I32))

    mlim = mlim_ref[...]

    def bias_tile(t, carry):
        s0 = pl.multiple_of(t * TK, TK)
        blk = key_ref[pl.ds(s0, TK), :]
        s_idx = s0 + lax.broadcasted_iota(I32, (TK, TQ), 0)
        sel = (blk > thr) | ((blk == thr) & (s_idx <= mlim))
        kx_ref[pl.ds(s0, TK), 0:LANE] = jnp.where(sel, 0.0, NEG_BIG).astype(BF16)
        return carry

    lax.fori_loop(0, nt, bias_tile, 0)

    for g in range(ATTN_KV_HEADS):
        lanes = slice(LANE + g * HEAD_DIM, LANE + (g + 1) * HEAD_DIM)
        if wide:
            for r in range(ATTN_REP):
                hq = g * ATTN_REP + r
                qx_ref[g, r * nq:(r + 1) * nq, lanes] = (
                    q_ref[0, :, hq * HEAD_DIM:(hq + 1) * HEAD_DIM])
        else:
            qx_ref[g, :, lanes] = q_ref[0, g]
    m_ref[...] = jnp.full(m_ref.shape, -jnp.inf, F32)
    acc_ref[...] = jnp.zeros(acc_ref.shape, F32)

    def scores(t, slot):
        s0 = pl.multiple_of(jnp.minimum(t, nt - 1) * TK, TK)
        k_tile = kx_ref[pl.ds(s0, TK), :]
        for g in range(ATTN_KV_HEADS):
            s_ref[g, slot] = lax.dot_general(k_tile, qx_ref[g], NT_DIMS,
                                             preferred_element_type=F32)

    def accumulate(t, slot):
        s0 = pl.multiple_of(t * TK, TK)
        for g in range(ATTN_KV_HEADS):
            s = s_ref[g, slot]
            m_prev = m_ref[g]
            m_new = jnp.maximum(m_prev, jnp.max(s, axis=0, keepdims=True))
            p = jnp.exp2(s - m_new)
            acc_ref[g] = jnp.exp2(m_prev - m_new) * acc_ref[g] + jnp.dot(
                vx_ref[g, :, pl.ds(s0, TK)], p.astype(BF16), preferred_element_type=F32)
            m_ref[g] = m_new

    scores(0, 0)

    def attn_pair(u, carry):
        scores(2 * u + 1, 1)
        accumulate(2 * u, 0)
        scores(2 * u + 2, 0)
        accumulate(2 * u + 1, 1)
        return carry

    lax.fori_loop(0, nt // 2, attn_pair, 0)

    @pl.when(nt < ntp)
    def _():
        accumulate(nt - 1, 0)

    for g in range(ATTN_KV_HEADS):
        acc = acc_ref[g]
        o_t = acc[0:HEAD_DIM, :] * (1.0 / acc[HEAD_DIM:HEAD_DIM + 1, :])
        if wide:
            for r in range(0, ATTN_REP, 2):
                hq = g * ATTN_REP + r
                pair = jnp.concatenate([o_t[:, r * nq:(r + 1) * nq],
                                        o_t[:, (r + 1) * nq:(r + 2) * nq]], axis=0)
                o_ref[0, :, hq * HEAD_DIM:(hq + 2) * HEAD_DIM] = pair.T.astype(BF16)
        else:
            o_rq = jnp.concatenate([o_t, o_t], axis=0).T[:, 0:HEAD_DIM].astype(BF16)
            for r in range(ATTN_REP):
                hq = g * ATTN_REP + r
                o_ref[0, :, hq * HEAD_DIM:(hq + 1) * HEAD_DIM] = o_rq[r * nq:(r + 1) * nq, :]


def _dsa(q, qi, w, kidx_all, k_all, v_all, past_len, l_valid, t):
    b = kidx_all.shape[0]
    lp = kidx_all.shape[1]
    nq = Q_BLOCK if t % Q_BLOCK == 0 else SMALL_Q_BLOCK
    assert t % nq == 0 and lp % (2 * KEY_TILE) == 0 and lp < 1 << KEY_INDEX_BITS
    topk = min(TOPK_MAX, l_valid // 4)
    width = ATTN_REP * nq
    if nq == Q_BLOCK:
        qblk = lambda c: pl.BlockSpec((1, nq, c), lambda i, j: (i, j, 0))
        q_specs = [qblk(ATTN_WIDTH), qblk(IDX_HEADS * IDX_DIM), qblk(LANE)]
    else:
        assert t == nq
        q_specs = [pl.BlockSpec((1, ATTN_KV_HEADS, width, HEAD_DIM), lambda i, j: (i, 0, 0, 0)),
                   pl.BlockSpec((1, IDX_HEADS * nq, IDX_DIM), lambda i, j: (i, 0, 0)),
                   pl.BlockSpec((1, 1, LANE), lambda i, j: (i, 0, 0))]
    slot_onehot = jnp.asarray(
        (np.arange(width)[:, None] % nq == np.arange(LANE)[None, :]).astype(np.float32), BF16)
    per_stream = lambda c: pl.BlockSpec((1, lp, c), lambda i, j: (i, 0, 0))
    return pl.pallas_call(
        functools.partial(_dsa_kernel, past_len=past_len, l_valid=l_valid, topk=topk, nq=nq),
        out_shape=jax.ShapeDtypeStruct((b, t, ATTN_WIDTH), BF16),
        grid=(b, t // nq),
        in_specs=q_specs + [pl.BlockSpec((width, LANE), lambda i, j: (0, 0)),
                            per_stream(IDX_DIM), per_stream(LANE), per_stream(LANE)],
        out_specs=pl.BlockSpec((1, nq, ATTN_WIDTH), lambda i, j: (i, j, 0)),
        scratch_shapes=[pltpu.VMEM((width, IDX_DIM), BF16),
                        pltpu.VMEM((ATTN_KV_HEADS, width, 2 * LANE), BF16),
                        pltpu.VMEM((lp, LANE), I32),
                        pltpu.VMEM((lp, 2 * LANE), BF16),
                        pltpu.VMEM((ATTN_KV_HEADS, PV_ROWS, lp), BF16),
                        pltpu.VMEM((ATTN_KV_HEADS, 1, width), F32),
                        pltpu.VMEM((ATTN_KV_HEADS, PV_ROWS, width), F32),
                        pltpu.VMEM((1, LANE), I32),
                        pltpu.VMEM((ATTN_KV_HEADS, 2, KEY_TILE, width), F32)],
        compiler_params=pltpu.CompilerParams(
            dimension_semantics=("arbitrary", "arbitrary"), vmem_limit_bytes=VMEM_LIMIT),
        name="dsa",
    )(q, qi, w, slot_onehot, kidx_all, k_all, v_all)


def _merge_kernel(x_ref, yn_ref, o_ref, gate_ref, wa_ref, wb_ref, wo_ref, out_ref):
    br_a = jnp.dot(yn_ref[...], wa_ref[...], preferred_element_type=F32)
    br_b = jnp.dot(o_ref[...], wb_ref[...], preferred_element_type=F32)
    gates = gate_ref[...].astype(F32)
    merged = _sigmoid(gates[:, :D_MODEL]) * br_a + _sigmoid(gates[:, D_MODEL:]) * br_b
    out_ref[...] = x_ref[...] + jnp.dot(merged.astype(BF16), wo_ref[...],
                                        preferred_element_type=F32)


def _merge(x2d, yn, o, gate, w_ssm_out, w_attn_out, w_o, tm):
    n = x2d.shape[0]
    assert n % tm == 0
    row = lambda w: pl.BlockSpec((tm, w), lambda i: (i, 0))
    full = lambda r, c: pl.BlockSpec((r, c), lambda i: (0, 0))
    return pl.pallas_call(
        _merge_kernel,
        out_shape=jax.ShapeDtypeStruct((n, D_MODEL), F32),
        grid=(n // tm,),
        in_specs=[row(D_MODEL), row(SSM_INNER), row(ATTN_WIDTH), row(2 * D_MODEL),
                  full(SSM_INNER, D_MODEL), full(ATTN_WIDTH, D_MODEL), full(D_MODEL, D_MODEL)],
        out_specs=row(D_MODEL),
        compiler_params=pltpu.CompilerParams(
            dimension_semantics=("arbitrary",), vmem_limit_bytes=VMEM_LIMIT),
        name="merge",
    )(x2d, yn, o, gate, w_ssm_out.astype(BF16), w_attn_out.astype(BF16), w_o.astype(BF16))


def _moe_kernel(x_ref, ng_ref, wr_ref, br_ref, w1_ref, w3_ref, w2_ref, fg_ref, tri_ref, out_ref,
                hs_ref, combs_ref, acc_ref, pt_ref, seg_ref, *, n_steps, blk_rows):
    e_step = pl.program_id(1)
    tm = x_ref.shape[0]
    sr = hs_ref.shape[0]
    BLK = blk_rows

    @pl.when(e_step == 0)
    def _():
        x = x_ref[...]
        ms = jnp.mean(x * x, axis=-1, keepdims=True)
        h = ((x * lax.rsqrt(ms + EPS)) * ng_ref[...]).astype(BF16)
        logits = jnp.dot(h, wr_ref[...], preferred_element_type=F32) + br_ref[...]
        lane_i = lax.broadcasted_iota(I32, (tm, LANE), 1)
        lane = lane_i.astype(F32)
        first = lambda hit: jnp.min(jnp.where(hit, lane, float(LANE)), axis=1, keepdims=True)
        is_g = (lane_i >= N_EXPERTS) & (lane_i < N_EXPERTS + MOE_GROUPS)
        glog = jnp.where(is_g, logits, -jnp.inf)
        gmax = jnp.max(glog, axis=1, keepdims=True)
        gsel = first(glog == gmax) - float(N_EXPERTS)
        p_group = 1.0 / jnp.sum(jnp.exp(glog - gmax), axis=1, keepdims=True)
        grp = (lane_i >> EXPERT_GROUP_SHIFT).astype(F32)
        in_grp = (lane_i < N_EXPERTS) & (grp == gsel)
        el = jnp.where(in_grp, logits, -jnp.inf)
        v1 = jnp.max(el, axis=1, keepdims=True)
        i1 = first(el == v1)
        el2 = jnp.where(lane == i1, -jnp.inf, el)
        v2 = jnp.max(el2, axis=1, keepdims=True)
        i2 = first(el2 == v2)
        e21 = jnp.exp(v2 - v1)
        den = 1.0 + e21
        comb = (jnp.where(lane == i1, (1.0 / den) * p_group, 0.0)
                + jnp.where(lane == i2, (e21 / den) * p_group, 0.0))

        own = lane == gsel
        own_f = jnp.where(own, 1.0, 0.0)
        before = jnp.dot(tri_ref[...], own_f.astype(BF16), preferred_element_type=F32)
        rank = jnp.sum(jnp.where(own, before, 0.0), axis=1, keepdims=True)
        counts = jnp.sum(own_f, axis=0, keepdims=True)
        blocks = jnp.floor((counts + (BLK - 1)) * (1.0 / BLK))
        upper = (lax.broadcasted_iota(I32, (LANE, LANE), 0)
                 < lax.broadcasted_iota(I32, (LANE, LANE), 1))
        starts = jnp.dot(jnp.broadcast_to(blocks, (8, LANE)).astype(BF16),
                         jnp.where(upper, 1.0, 0.0).astype(BF16),
                         preferred_element_type=F32)[0:1, :] * BLK
        dest = jnp.sum(jnp.where(own, starts, 0.0), axis=1, keepdims=True) + rank
        for g in range(MOE_GROUPS):
            seg_ref[g] = jnp.sum(starts[:, g:g + 1]).astype(I32)
            seg_ref[MOE_GROUPS + g] = jnp.sum(blocks[:, g:g + 1]).astype(I32)

        dest_i = dest.astype(I32)
        pt = jnp.where(dest_i == lax.broadcasted_iota(I32, (tm, sr), 1), 1.0, 0.0).astype(BF16)
        pt_ref[...] = pt
        dest_row = jnp.broadcast_to(dest, (tm, LANE)).T[0:1, :].astype(I32)
        p_mat = jnp.where(lax.broadcasted_iota(I32, (sr, tm), 0) == dest_row, 1.0, 0.0
                          ).astype(BF16)
        hs_ref[...] = jnp.dot(p_mat, h, preferred_element_type=F32).astype(BF16)
        c3 = jnp.dot(p_mat, _split3(comb), preferred_element_type=F32)
        combs_ref[...] = c3[:, 0:LANE] + c3[:, LANE:2 * LANE] + c3[:, 2 * LANE:]
        acc_ref[...] = jnp.zeros(acc_ref.shape, F32)

    group = e_step // (EXPERTS_PER_GROUP // EXPERTS_PER_STEP)
    seg_start = seg_ref[group]

    def expert_rows(first_blk, n_rows):
        rows = pl.ds(pl.multiple_of(seg_start + first_blk * BLK, BLK), n_rows)
        h = hs_ref[rows, :]
        comb = combs_ref[rows, :]
        lane = lax.broadcasted_iota(I32, (n_rows, LANE), 1)
        for k in range(EXPERTS_PER_STEP):
            e = e_step * EXPERTS_PER_STEP + k
            a1 = jnp.dot(h, w1_ref[k], preferred_element_type=F32)
            a3 = jnp.dot(h, w3_ref[k], preferred_element_type=F32)
            he = (_silu(a1) * a3).astype(BF16)
            ye = jnp.dot(he, w2_ref[k], preferred_element_type=F32)
            wcol = jnp.sum(jnp.where(lane == e, comb, 0.0), axis=1, keepdims=True)
            acc_ref[rows, :] += wcol * ye

    n_blocks = seg_ref[MOE_GROUPS + group]
    odd_tail = (n_blocks % 2 == 1) & (n_blocks >= 3)
    n_pairs = jnp.where(odd_tail, (n_blocks - 3) // 2, n_blocks // 2)

    def block_pair(u, carry):
        expert_rows(2 * u, 2 * BLK)
        return carry

    lax.fori_loop(0, n_pairs, block_pair, 0)

    @pl.when(odd_tail)
    def _():
        expert_rows(n_blocks - 3, 3 * BLK)

    @pl.when(n_blocks == 1)
    def _():
        expert_rows(0, BLK)

    @pl.when(e_step == n_steps - 1)
    def _():
        y = x_ref[...] + jnp.dot(pt_ref[...], acc_ref[...].astype(BF16),
                                 preferred_element_type=F32)
        ms = jnp.mean(y * y, axis=-1, keepdims=True)
        out_ref[...] = (y * lax.rsqrt(ms + EPS)) * fg_ref[...]


def _moe(x2d, norm_g, w_router_group, b_router_group, w_router_expert, b_router_expert,
         w1, w3, w2, norm_final_g, tm):
    n = x2d.shape[0]
    blk_rows = min(MOE_ROW_BLOCK, tm)
    assert n % tm == 0 and tm % blk_rows == 0
    n_steps = N_EXPERTS // EXPERTS_PER_STEP
    sr = tm + (MOE_GROUPS - 1) * blk_rows
    padw = LANE - N_EXPERTS - MOE_GROUPS
    wr = jnp.concatenate([w_router_expert, w_router_group,
                          jnp.zeros((D_MODEL, padw), F32)], axis=1).astype(BF16)
    br = jnp.concatenate([b_router_expert, b_router_group,
                          jnp.zeros((padw,), F32)]).astype(F32).reshape(1, LANE)
    tri = jnp.asarray(np.tril(np.ones((tm, tm), np.float32), -1), BF16)
    row = pl.BlockSpec((tm, D_MODEL), lambda i, e: (i, 0))
    const = lambda r, c: pl.BlockSpec((r, c), lambda i, e: (0, 0))
    return pl.pallas_call(
        functools.partial(_moe_kernel, n_steps=n_steps, blk_rows=blk_rows),
        out_shape=jax.ShapeDtypeStruct((n, D_MODEL), F32),
        grid=(n // tm, n_steps),
        in_specs=[row, const(1, D_MODEL), const(D_MODEL, LANE), const(1, LANE),
                  pl.BlockSpec((EXPERTS_PER_STEP, D_MODEL, EXPERT_FF), lambda i, e: (e, 0, 0)),
                  pl.BlockSpec((EXPERTS_PER_STEP, D_MODEL, EXPERT_FF), lambda i, e: (e, 0, 0)),
                  pl.BlockSpec((EXPERTS_PER_STEP, EXPERT_FF, D_MODEL), lambda i, e: (e, 0, 0)),
                  const(1, D_MODEL), const(tm, tm)],
        out_specs=row,
        scratch_shapes=[pltpu.VMEM((sr, D_MODEL), BF16), pltpu.VMEM((sr, LANE), F32),
                        pltpu.VMEM((sr, D_MODEL), F32), pltpu.VMEM((tm, sr), BF16),
                        pltpu.SMEM((2 * MOE_GROUPS,), I32)],
        compiler_params=pltpu.CompilerParams(
            dimension_semantics=("arbitrary", "arbitrary"), vmem_limit_bytes=VMEM_LIMIT),
        name="moe",
    )(x2d, norm_g.reshape(1, D_MODEL), wr, br, w1.astype(BF16), w3.astype(BF16),
      w2.astype(BF16), norm_final_g.reshape(1, D_MODEL), tri)


def _token_tile(n):
    for tm in (1024, 512, 256, 128, 64, 32, 16, 8):
        if n % tm == 0:
            return tm
    raise ValueError(f"token count {n} is not a multiple of 8")


def _pad_axis(a, axis, size):
    if a.shape[axis] == size:
        return a
    widths = [(0, 0)] * a.ndim
    widths[axis] = (0, size - a.shape[axis])
    return jnp.pad(a, widths)


def _round_up(n, m):
    return (n + m - 1) // m * m


def _stream_step(x, conv_state, ssm_state, k_past, v_past, kidx_past, w_packed, p):
    b, t, d = x.shape
    n = b * t
    tm = _token_tile(n)
    x2d = x.reshape(n, d)
    z, xbc, q, gate, k, v, qi, ki, dtw, k_b, v_b, ki_b = _in_proj(
        x2d, p["norm_mix_g"], w_packed, min(tm, PROJ_ROWS))

    yn, conv_new, ssm_new = _ssd(
        xbc.reshape(b, t, CONV_DIM), z.reshape(b, t, SSM_INNER), dtw.reshape(b, t, LANE),
        conv_state, ssm_state, p["conv_w"], p["conv_b"], p["dt_bias"], p["A_log"],
        p["D_skip"], p["ssm_norm_g"])

    kv_w = ATTN_KV_HEADS * HEAD_DIM
    k_all, v_all, kidx_all = (k_b.reshape(b, t, kv_w), v_b.reshape(b, t, kv_w),
                              ki_b.reshape(b, t, IDX_DIM))
    past_len = 0
    if k_past is not None:
        past_len = k_past.shape[1]
        k_all = jnp.concatenate([k_past.astype(BF16).reshape(b, past_len, kv_w), k_all], axis=1)
        v_all = jnp.concatenate([v_past.astype(BF16).reshape(b, past_len, kv_w), v_all], axis=1)
        kidx_all = jnp.concatenate([kidx_past.astype(BF16), kidx_all], axis=1)
    l_valid = past_len + t
    lp = _round_up(l_valid, 2 * KEY_TILE)
    if t % Q_BLOCK == 0:
        q_in = (q.reshape(b, t, ATTN_WIDTH), qi.reshape(b, t, IDX_HEADS * IDX_DIM),
                dtw.reshape(b, t, LANE))
    else:
        q_in = (q.reshape(b, t, ATTN_KV_HEADS, ATTN_REP, HEAD_DIM).transpose(0, 2, 3, 1, 4)
                .reshape(b, ATTN_KV_HEADS, ATTN_REP * t, HEAD_DIM),
                qi.reshape(b, t, IDX_HEADS, IDX_DIM).transpose(0, 2, 1, 3)
                .reshape(b, IDX_HEADS * t, IDX_DIM),
                dtw.reshape(b, t, LANE)[:, :, SSM_HEADS:SSM_HEADS + IDX_HEADS]
                .transpose(0, 2, 1).reshape(b, 1, IDX_HEADS * t))
    o = _dsa(*q_in, _pad_axis(kidx_all, 1, lp), _pad_axis(k_all, 1, lp), _pad_axis(v_all, 1, lp),
             past_len, l_valid, t)
    o = o.reshape(n, ATTN_WIDTH)

    x1 = _merge(x2d, yn.reshape(n, SSM_INNER), o, gate, p["w_ssm_out"], p["w_attn_out"],
                p["w_o"], min(tm, MERGE_TILE))
    y = _moe(x1, p["norm_moe_g"], p["w_router_group"], p["b_router_group"],
             p["w_router_expert"], p["b_router_expert"], p["w1"], p["w3"], p["w2"],
             p["norm_final_g"], tm)
    return (y.reshape(b, t, d), k.reshape(b, t, ATTN_KV_HEADS, HEAD_DIM),
            v.reshape(b, t, ATTN_KV_HEADS, HEAD_DIM), ki.reshape(b, t, IDX_DIM), conv_new, ssm_new)


def kernel(x_prompt, x_sample, cache_k, cache_v, cache_kidx, state_conv, state_ssm, norm_mix_g, w_in, conv_w, conv_b, dt_bias, A_log, D_skip, ssm_norm_g, w_ssm_out, w_attn_out, w_o, norm_moe_g, w_router_group, b_router_group, w_router_expert, b_router_expert, w1, w3, w2, norm_final_g):
    p = dict(norm_mix_g=norm_mix_g, conv_w=conv_w, conv_b=conv_b, dt_bias=dt_bias, A_log=A_log,
             D_skip=D_skip, ssm_norm_g=ssm_norm_g, w_ssm_out=w_ssm_out, w_attn_out=w_attn_out,
             w_o=w_o, norm_moe_g=norm_moe_g, w_router_group=w_router_group,
             b_router_group=b_router_group, w_router_expert=w_router_expert,
             b_router_expert=b_router_expert, w1=w1, w3=w3, w2=w2, norm_final_g=norm_final_g)
    w_packed = _pack_w_in(w_in)
    yp, kp, vp, kip, cp, sp = _stream_step(x_prompt, None, None, None, None, None, w_packed, p)
    ys, ksn, vsn, kisn, csn, ssn = _stream_step(x_sample, state_conv, state_ssm, cache_k,
                                                cache_v, cache_kidx, w_packed, p)
    return (yp, ys, kp, vp, kip, cp, sp, ksn, vsn, kisn, csn, ssn)
```

```python
import functools

import numpy as np
import jax
import jax.numpy as jnp
from jax import lax
from jax.experimental import pallas as pl
from jax.experimental.pallas import tpu as pltpu

F32 = jnp.float32
BF16 = jnp.bfloat16
I32 = jnp.int32

D_MODEL = 1024
CHUNK = 64
CHUNK_SHIFT = 6
EPS = 1e-6
SSM_INNER = 2048
SSM_HEADDIM = 64
SSM_HEADS = 32
SSM_GROUPS = 4
SSM_HEADS_PER_GROUP = SSM_HEADS // SSM_GROUPS
SSM_STATE = 128
CONV_WIDTH = 4
CONV_DIM = SSM_INNER + 2 * SSM_GROUPS * SSM_STATE
CONV_CARRY = 16
SSD_CHUNKS_PER_STEP = 8
CONV_BLOCK = 128
ATTN_HEADS = 16
ATTN_KV_HEADS = 2
HEAD_DIM = 64
ATTN_REP = ATTN_HEADS // ATTN_KV_HEADS
ATTN_WIDTH = ATTN_HEADS * HEAD_DIM
IDX_HEADS = 8
IDX_DIM = 64
IDX_SCALE = (IDX_HEADS * IDX_DIM) ** -0.5
TOPK_MAX = 256
MOE_GROUPS = 4
EXPERTS_PER_GROUP = 8
EXPERT_GROUP_SHIFT = 3
N_EXPERTS = 32
EXPERT_FF = 256
N_BRANCHES = 2

LANE = 128
VMEM_LIMIT = 56 * 1024 * 1024
PROJ_TILE = 1024
PROJ_ROWS = 512
PROJ_TILES = 9
Q_BLOCK = 128
SMALL_Q_BLOCK = 16
KEY_TILE = 256
KEY_INDEX_BITS = 15
COUNT_CHAINS = 4
EXPERTS_PER_STEP = 4
MOE_ROW_BLOCK = 128
MERGE_TILE = 512
INT_MIN = np.int32(-2 ** 31)
NEG_BIG = -1e30
Q_SCALE = HEAD_DIM ** -0.5 * float(np.log2(np.e))
PV_ROWS = 80

NT_DIMS = (((1,), (1,)), ((), ()))
TN_DIMS = (((0,), (0,)), ((), ()))


def _sigmoid(x):
    return 1.0 / (1.0 + jnp.exp(-x))


def _silu(x):
    h = 0.5 * x
    return h + h * jnp.tanh(h)


def _split3(x):
    hi = x.astype(BF16)
    r1 = x - hi.astype(F32)
    mid = r1.astype(BF16)
    r2 = r1 - mid.astype(F32)
    lo = r2.astype(BF16)
    return jnp.concatenate([hi, mid, lo], axis=1)


def _inproj_kernel(x_ref, g_ref, w_ref, z_ref, xbc_ref, q_ref, gate_ref, k_ref, v_ref,
                   qi_ref, ki_ref, dtw_ref, kb_ref, vb_ref, kib_ref):
    x = x_ref[...]
    ms = jnp.mean(x * x, axis=-1, keepdims=True)
    h = ((x * lax.rsqrt(ms + EPS)) * g_ref[...]).astype(BF16)

    def cols(first, width):
        lo = first * PROJ_TILE
        return jnp.dot(h, w_ref[:, lo:lo + width], preferred_element_type=F32)

    for j in range(2):
        z_ref[:, j * PROJ_TILE:(j + 1) * PROJ_TILE] = cols(j, PROJ_TILE).astype(BF16)
    for j in range(3):
        xbc_ref[:, j * PROJ_TILE:(j + 1) * PROJ_TILE] = cols(2 + j, PROJ_TILE).astype(BF16)
    q_ref[...] = (cols(5, PROJ_TILE) * Q_SCALE).astype(BF16)
    for j in range(2):
        gate_ref[:, j * PROJ_TILE:(j + 1) * PROJ_TILE] = cols(6 + j, PROJ_TILE).astype(BF16)
    t = cols(8, PROJ_TILE)
    k_ref[...] = t[:, 0:128]
    v_ref[...] = t[:, 128:256]
    qi_ref[...] = t[:, 256:768].astype(BF16)
    ki_ref[...] = t[:, 768:832]
    dtw_ref[...] = t[:, 896:1024]
    kb_ref[...] = t[:, 0:128].astype(BF16)
    vb_ref[...] = t[:, 128:256].astype(BF16)
    kib_ref[...] = t[:, 768:832].astype(BF16)


def _pack_w_in(w_in):
    sizes = (SSM_INNER, CONV_DIM, SSM_HEADS, ATTN_WIDTH, ATTN_KV_HEADS * HEAD_DIM,
             ATTN_KV_HEADS * HEAD_DIM, IDX_HEADS * IDX_DIM, IDX_DIM, IDX_HEADS,
             N_BRANCHES * D_MODEL)
    offs = np.concatenate([[0], np.cumsum(sizes)])
    z, xbc, dt, q, k, v, qi, ki, wi, g = [w_in[:, offs[i]:offs[i + 1]] for i in range(10)]
    zeros = lambda n: jnp.zeros((D_MODEL, n), w_in.dtype)
    cols = [z, xbc, q, g, k, v, qi, ki, zeros(64), dt, wi, zeros(LANE - SSM_HEADS - IDX_HEADS)]
    return jnp.concatenate(cols, axis=1).astype(BF16)


def _in_proj(x2d, norm_g, w_packed, tm):
    n = x2d.shape[0]
    assert n % tm == 0
    bf = lambda c: jax.ShapeDtypeStruct((n, c), BF16)
    ff = lambda c: jax.ShapeDtypeStruct((n, c), F32)
    rows = lambda c: pl.BlockSpec((tm, c), lambda i: (i, 0))
    widths = (SSM_INNER, CONV_DIM, ATTN_WIDTH, 2 * D_MODEL, 128, 128, 512, IDX_DIM, 128,
              128, 128, IDX_DIM)
    return pl.pallas_call(
        _inproj_kernel,
        out_shape=(bf(SSM_INNER), bf(CONV_DIM), bf(ATTN_WIDTH), bf(2 * D_MODEL),
                   ff(128), ff(128), bf(512), ff(IDX_DIM), ff(128), bf(128), bf(128), bf(IDX_DIM)),
        grid=(n // tm,),
        in_specs=[
            rows(D_MODEL),
            pl.BlockSpec((1, D_MODEL), lambda i: (0, 0)),
            pl.BlockSpec((D_MODEL, PROJ_TILES * PROJ_TILE), lambda i: (0, 0),
                         pipeline_mode=pl.Buffered(1)),
        ],
        out_specs=tuple(rows(c) for c in widths),
        compiler_params=pltpu.CompilerParams(
            dimension_semantics=("arbitrary",), vmem_limit_bytes=VMEM_LIMIT),
        name="in_proj",
    )(x2d, norm_g.reshape(1, D_MODEL), w_packed)


def _ssd_kernel(*refs, L, cps, nsteps, has_state, first_shift):
    if has_state:
        (xbc_ref, z_ref, dtw_ref, cs_ref, s0_ref, shift_ref, convw_ref, convb_ref, dtb_ref, a_ref,
         dx_ref, ng_ref, e3p_ref, e3l_ref, tri_ref, bd_ref, yn_ref, cnew_ref, sout_ref,
         xe_ref, st, act_ref, ex_ref, yz_ref, acl_ref) = refs
    else:
        (xbc_ref, z_ref, dtw_ref, shift_ref, convw_ref, convb_ref, dtb_ref, a_ref,
         dx_ref, ng_ref, e3p_ref, e3l_ref, tri_ref, bd_ref, yn_ref, cnew_ref, sout_ref,
         xe_ref, st, act_ref, ex_ref, yz_ref, acl_ref) = refs
    c = pl.program_id(1)
    G, R, N, P = SSM_GROUPS, SSM_HEADS_PER_GROUP, SSM_STATE, SSM_HEADDIM
    GW = R * P
    RL = R * L
    LB = cps * L

    CB = min(LB, CONV_BLOCK)
    @pl.when(c == 0)
    def _():
        xe_ref[CB:, :] = jnp.zeros((xe_ref.shape[0] - CB, CONV_DIM), BF16)
        if has_state:
            xe_ref[CB:CB + CONV_CARRY, :] = cs_ref[0]
            st[...] = s0_ref[0].T
        else:
            st[...] = jnp.zeros(st.shape, F32)

    w = convw_ref[...]
    for cb in range(LB // CB):
        blk = slice(cb * CB, (cb + 1) * CB)
        xe_ref[0:CB, :] = xbc_ref[0, blk, :]
        which = jnp.where(c == 0, first_shift, 1) if cb == 0 else 1
        shifted = jnp.dot(shift_ref[which], xe_ref[...], preferred_element_type=F32)
        conv = convb_ref[...] + shifted[2 * CB:3 * CB, :] * w[0:1, :]
        conv = conv + shifted[CB:2 * CB, :] * w[1:2, :]
        conv = conv + shifted[0:CB, :] * w[2:3, :]
        conv = conv + xbc_ref[0, blk, :].astype(F32) * w[3:4, :]
        act_ref[blk, :] = _silu(conv)
        if nsteps * (LB // CB) > 1:
            xe_ref[CB:CB + CONV_CARRY, :] = xe_ref[CB - CONV_CARRY:CB, :]

    @pl.when(c == nsteps - 1)
    def _():
        cnew_ref[0] = xbc_ref[0, LB - CONV_CARRY:LB, :].astype(F32)[
            CONV_CARRY - (CONV_WIDTH - 1):CONV_CARRY, :]

    u = dtw_ref[0] + dtb_ref[...]
    dt = jnp.maximum(u, 0.0) + jnp.log1p(jnp.exp(-jnp.abs(u)))
    a = dt * a_ref[...]
    ac3 = jnp.dot(tri_ref[...], _split3(a), preferred_element_type=F32)
    a_cum = ac3[:, 0:LANE] + ac3[:, LANE:2 * LANE] + ac3[:, 2 * LANE:]

    ex_ref[...] = jnp.dot(_split3(jnp.concatenate([a_cum, dt], axis=0)), e3p_ref[...],
                          preferred_element_type=F32)
    if L != P:
        acl_ref[...] = jnp.dot(_split3(a_cum), e3l_ref[...], preferred_element_type=F32)
    row = lax.broadcasted_iota(I32, (L, RL), 0)
    scol = lax.broadcasted_iota(I32, (L, RL), 1) & (L - 1)

    for sub in range(cps):
        r0 = sub * L
        rows = slice(r0, r0 + L)
        ssq = jnp.zeros((L, 1), F32)
        for g in range(G):
            sl = slice(g * GW, (g + 1) * GW)
            xs = act_ref[rows, sl]
            b_g = act_ref[rows, SSM_INNER + g * N:SSM_INNER + (g + 1) * N].astype(BF16)
            c_g = act_ref[rows, SSM_INNER + (G + g) * N:SSM_INNER + (G + g + 1) * N].astype(BF16)
            acx = ex_ref[rows, sl]
            alast = ex_ref[r0 + L - 1:r0 + L, sl]
            xdt = xs * ex_ref[LB + r0:LB + r0 + L, sl]
            acl_g = acx if L == P else acl_ref[rows, g * RL:(g + 1) * RL]
            acs_g = jnp.sum(jnp.where(scol == row, acl_g, 0.0), axis=0, keepdims=True)
            decay = jnp.where(scol <= row, jnp.exp(acl_g - acs_g), 0.0)
            cb = lax.dot_general(c_g, jnp.concatenate([b_g] * R, axis=0), NT_DIMS,
                                 preferred_element_type=F32)
            m_g = (cb * decay).astype(BF16)
            xbd = jnp.concatenate([xdt.astype(BF16)] * R, axis=0) * bd_ref[...]
            y_in = jnp.dot(m_g, xbd, preferred_element_type=F32)
            st_g = st[:, sl]
            y_st = jnp.dot(c_g, st_g.astype(BF16), preferred_element_type=F32)
            xw = (xdt * jnp.exp(alast - acx)).astype(BF16)
            upd = lax.dot_general(b_g, xw, TN_DIMS, preferred_element_type=F32)
            st[:, sl] = st_g * jnp.exp(alast) + upd
            y = y_in + jnp.exp(acx) * y_st + dx_ref[:, sl] * xs
            yz = y * _silu(z_ref[0, rows, sl].astype(F32))
            ssq = ssq + jnp.sum(yz * yz, axis=-1, keepdims=True)
            yz_ref[rows, sl] = yz
        scale = lax.rsqrt(ssq * (1.0 / SSM_INNER) + EPS)
        yn_ref[0, rows, :] = ((yz_ref[rows, :] * scale) * ng_ref[...]).astype(BF16)

    @pl.when(c == nsteps - 1)
    def _():
        sout_ref[0] = st[...].T


def _shift_matrices(L, conv_rows):
    out = np.zeros((2, 3 * L, conv_rows), np.float32)
    for k in range(CONV_WIDTH - 1):
        for t in range(L):
            src = t - (k + 1)
            if src >= 0:
                out[:, k * L + t, src] = 1.0
            else:
                out[1, k * L + t, L + CONV_CARRY + src] = 1.0
                j = CONV_WIDTH - 1 + src
                for piece in range(3):
                    out[0, k * L + t, L + 3 * piece + j] = 1.0
    return jnp.asarray(out, BF16)


def _expand_matrix(width_per_head):
    r = np.arange(3 * LANE)[:, None] % LANE
    c = np.arange(SSM_HEADS * width_per_head)[None, :] // width_per_head
    return jnp.asarray((r == c).astype(np.float32), BF16)


def _ssd(xbc, z, dtw, conv_state, ssm_state, conv_w, conv_b, dt_bias, a_log, d_skip, ssm_norm_g):
    b, t, _ = xbc.shape
    L = min(CHUNK, t)
    nc = t // L
    assert t % L == 0 and L & (L - 1) == 0
    has_state = conv_state is not None
    pad = lambda v: jnp.pad(v.astype(F32), (0, LANE - SSM_HEADS)).reshape(1, LANE)
    a_neg = pad(-jnp.exp(a_log.astype(F32)))
    cps = SSD_CHUNKS_PER_STEP if nc % SSD_CHUNKS_PER_STEP == 0 else 1
    nsteps = nc // cps
    lb = cps * L
    cb_rows = min(lb, CONV_BLOCK)
    conv_rows = _round_up(cb_rows + CONV_CARRY, LANE)
    tri = jnp.asarray(np.kron(np.eye(cps), np.tril(np.ones((L, L)))).astype(np.float32), BF16)
    rl = SSM_HEADS_PER_GROUP * L
    gw = SSM_HEADS_PER_GROUP * SSM_HEADDIM
    bd = jnp.asarray((np.arange(rl)[:, None] // L == np.arange(gw)[None, :] // SSM_HEADDIM)
                     .astype(np.float32), BF16)
    const = lambda shape: pl.BlockSpec(shape, lambda i, j: (0,) * len(shape))
    tok = lambda w: pl.BlockSpec((1, lb, w), lambda i, j: (i, j, 0))
    per_b = lambda s: pl.BlockSpec((1,) + s, lambda i, j: (i, 0, 0))
    ins = [xbc, z, dtw]
    specs = [tok(CONV_DIM), tok(SSM_INNER), tok(LANE)]
    if has_state:
        triples = _split3(conv_state.astype(F32))
        ins += [_pad_axis(triples, 1, CONV_CARRY), ssm_state.reshape(b, SSM_INNER, SSM_STATE)]
        specs += [per_b((CONV_CARRY, CONV_DIM)), per_b((SSM_INNER, SSM_STATE))]
    first = 0 if has_state else 1
    ins += [_shift_matrices(cb_rows, conv_rows)]
    specs += [const((2, 3 * cb_rows, conv_rows))]
    ins += [conv_w, conv_b.reshape(1, CONV_DIM), pad(dt_bias), a_neg,
            jnp.repeat(d_skip.astype(F32), SSM_HEADDIM).reshape(1, SSM_INNER),
            ssm_norm_g.reshape(1, SSM_INNER), _expand_matrix(SSM_HEADDIM), _expand_matrix(L),
            tri, bd]
    specs += [const((CONV_WIDTH, CONV_DIM)), const((1, CONV_DIM)), const((1, LANE)),
              const((1, LANE)), const((1, SSM_INNER)), const((1, SSM_INNER)),
              const((3 * LANE, SSM_INNER)), const((3 * LANE, SSM_HEADS * L)),
              const((lb, lb)), const((rl, gw))]
    yn, cnew, sout = pl.pallas_call(
        functools.partial(_ssd_kernel, L=L, cps=cps, nsteps=nsteps, has_state=has_state,
                          first_shift=first),
        out_shape=(jax.ShapeDtypeStruct((b, t, SSM_INNER), BF16),
                   jax.ShapeDtypeStruct((b, CONV_WIDTH - 1, CONV_DIM), F32),
                   jax.ShapeDtypeStruct((b, SSM_INNER, SSM_STATE), F32)),
        grid=(b, nsteps),
        in_specs=specs,
        out_specs=(tok(SSM_INNER), per_b((CONV_WIDTH - 1, CONV_DIM)),
                   per_b((SSM_INNER, SSM_STATE))),
        scratch_shapes=[pltpu.VMEM((conv_rows, CONV_DIM), BF16),
                        pltpu.VMEM((SSM_STATE, SSM_INNER), F32),
                        pltpu.VMEM((lb, CONV_DIM), F32),
                        pltpu.VMEM((2 * lb, SSM_INNER), F32),
                        pltpu.VMEM((lb, SSM_INNER), F32),
                        pltpu.VMEM((lb, SSM_HEADS * L), F32)],
        compiler_params=pltpu.CompilerParams(
            dimension_semantics=("arbitrary", "arbitrary"), vmem_limit_bytes=VMEM_LIMIT),
        name="ssd",
    )(*ins)
    return yn, cnew, sout.reshape(b, SSM_HEADS, SSM_HEADDIM, SSM_STATE)


def _dsa_kernel(q_ref, qi_ref, w_ref, eq_ref, kidx_ref, k_ref, v_ref, o_ref,
                qs_ref, qx_ref, key_ref, kx_ref, vx_ref, m_ref, acc_ref, mlim_ref, s_ref,
                *, past_len, l_valid, topk, nq):
    qb = pl.program_id(1)
    TQ, TK = LANE, KEY_TILE
    W = ATTN_REP * nq
    wide = nq == LANE

    lane = lax.broadcasted_iota(I32, (1, TQ), 1)
    pos = past_len + qb * nq + (lane & (nq - 1))
    n_adm = jnp.minimum(((pos >> CHUNK_SHIFT) + 1) * CHUNK, l_valid)
    k_eff = jnp.minimum(n_adm, topk)
    last_pos = past_len + qb * nq + nq - 1
    n_max = jnp.minimum(((last_pos >> CHUNK_SHIFT) + 1) * CHUNK, l_valid)
    nt = (n_max + TK - 1) // TK
    ntp = 2 * ((nt + 1) // 2)
    lp = key_ref.shape[0]

    @pl.when(qb == 0)
    def _():
        kx_ref[:, LANE:] = k_ref[0]

        def v_tile(t, carry):
            s0 = pl.multiple_of(t * TK, TK)
            v_t = v_ref[0, pl.ds(s0, TK), :].astype(F32).T
            for g in range(ATTN_KV_HEADS):
                vx_ref[g, 0:HEAD_DIM, pl.ds(s0, TK)] = (
                    v_t[g * HEAD_DIM:(g + 1) * HEAD_DIM, :].astype(BF16))
            return carry

        lax.fori_loop(0, lp // TK, v_tile, 0)
        tail_row = lax.broadcasted_iota(I32, (PV_ROWS - HEAD_DIM, lp), 0)
        for g in range(ATTN_KV_HEADS):
            vx_ref[g, HEAD_DIM:, :] = jnp.where(tail_row == 0, 1.0, 0.0).astype(BF16)
            qx_ref[g, :, 0:LANE] = eq_ref[...]
            qx_ref[g, :, LANE:] = jnp.zeros((W, LANE), BF16)

    if wide:
        for h in range(IDX_HEADS):
            qs_ref[h * nq:(h + 1) * nq, :] = qi_ref[0, :, h * IDX_DIM:(h + 1) * IDX_DIM]
        w_t = w_ref[0].T[SSM_HEADS:SSM_HEADS + IDX_HEADS, :] * IDX_SCALE
    else:
        qs_ref[...] = qi_ref[0]
        w_row = w_ref[0] * IDX_SCALE

    def index_dots(t, slot):
        s0 = pl.multiple_of(jnp.minimum(t, nt - 1) * TK, TK)
        s_ref[0, slot] = lax.dot_general(kidx_ref[0, pl.ds(s0, TK), :], qs_ref[...], NT_DIMS,
                                         preferred_element_type=F32)

    def score_tile(t, slot):
        s0 = pl.multiple_of(t * TK, TK)
        d = s_ref[0, slot]
        if wide:
            sc = w_t[0:1, :] * jnp.maximum(d[:, 0:nq], 0.0)
            for h in range(1, IDX_HEADS):
                sc = sc + w_t[h:h + 1, :] * jnp.maximum(d[:, h * nq:(h + 1) * nq], 0.0)
        else:
            sc = w_row * jnp.maximum(d, 0.0)
            for shift in [nq << i for i in reversed(range(IDX_HEADS.bit_length() - 1))]:
                sc = sc + pltpu.roll(sc, shift, axis=1)
        sc = sc + 0.0
        bits = pltpu.bitcast(sc, I32)
        key = jnp.where(bits < 0, bits ^ np.int32(0x7FFFFFFF), bits)
        s_idx = s0 + lax.broadcasted_iota(I32, (TK, TQ), 0)
        key_ref[pl.ds(s0, TK), :] = jnp.where(s_idx < n_adm, key, INT_MIN)

    index_dots(0, 0)

    def score_pair(u, carry):
        index_dots(2 * u + 1, 1)
        score_tile(2 * u, 0)
        index_dots(2 * u + 2, 0)
        score_tile(2 * u + 1, 1)
        return carry

    lax.fori_loop(0, nt // 2, score_pair, 0)

    @pl.when(nt < ntp)
    def _():
        score_tile(nt - 1, 0)
        key_ref[pl.ds(pl.multiple_of(nt * TK, TK), TK), :] = jnp.full((TK, TQ), INT_MIN, I32)

    def count(pred):
        n_vregs = 2 * TK // 8

        def body(t, accs):
            s0 = pl.multiple_of(t * (2 * TK), 2 * TK)
            blk = key_ref[pl.ds(s0, 2 * TK), :].reshape(n_vregs, 8, TQ)
            s_idx = s0 + lax.broadcasted_iota(I32, (2 * TK, TQ), 0).reshape(n_vregs, 8, TQ)
            accs = list(accs)
            for r in range(n_vregs):
                a = accs[r % COUNT_CHAINS]
                accs[r % COUNT_CHAINS] = jnp.where(pred(blk[r], s_idx[r]), a + 1, a)
            return tuple(accs)

        zero = jnp.zeros((8, TQ), I32)
        accs = lax.fori_loop(0, ntp // 2, body, (zero,) * COUNT_CHAINS)
        acc = accs[0]
        for a in accs[1:]:
            acc = acc + a
        return jnp.sum(acc, axis=0, keepdims=True)

    def bit_step(i, state):
        prefix, n_ge = state
        cand = prefix | jnp.left_shift(jnp.int32(1), 31 - i)
        cand_s = cand ^ INT_MIN
        cnt = count(lambda blk, s_idx: blk >= cand_s)
        ok = cnt >= k_eff
        return jnp.where(ok, cand, prefix), jnp.where(ok, cnt, n_ge)

    prefix, n_ge = lax.fori_loop(
        0, 16, lambda i, state: bit_step(2 * i + 1, bit_step(2 * i, state)),
        (jnp.zeros((1, TQ), I32), jnp.full((1, TQ), lp, I32)))
    thr = prefix ^ INT_MIN

    n_gt = count(lambda blk, s_idx: blk > thr)
    need = k_eff - n_gt
    mlim_ref[...] = jnp.full((1, TQ), 1 << KEY_INDEX_BITS, I32)

    @pl.when(jnp.max(n_ge - k_eff) > 0)
    def _():
        def idx_step(i, prefix):
            cand = prefix | jnp.left_shift(jnp.int32(1), KEY_INDEX_BITS - 1 - i)
            cnt = count(lambda blk, s_idx: (blk == thr) & (s_idx < cand))
            return jnp.where(cnt < need, cand, prefix)
        mlim_ref[...] = lax.fori_loop(0, KEY_INDEX_BITS, idx_step, jnp.zeros((1, TQ), I32))

    mlim = mlim_ref[...]

    def bias_tile(t, carry):
        s0 = pl.multiple_of(t * TK, TK)
        blk = key_ref[pl.ds(s0, TK), :]
        s_idx = s0 + lax.broadcasted_iota(I32, (TK, TQ), 0)
        sel = (blk > thr) | ((blk == thr) & (s_idx <= mlim))
        kx_ref[pl.ds(s0, TK), 0:LANE] = jnp.where(sel, 0.0, NEG_BIG).astype(BF16)
        return carry

    lax.fori_loop(0, nt, bias_tile, 0)

    for g in range(ATTN_KV_HEADS):
        lanes = slice(LANE + g * HEAD_DIM, LANE + (g + 1) * HEAD_DIM)
        if wide:
            for r in range(ATTN_REP):
                hq = g * ATTN_REP + r
                qx_ref[g, r * nq:(r + 1) * nq, lanes] = (
                    q_ref[0, :, hq * HEAD_DIM:(hq + 1) * HEAD_DIM])
        else:
            qx_ref[g, :, lanes] = q_ref[0, g]
    m_ref[...] = jnp.full(m_ref.shape, -jnp.inf, F32)
    acc_ref[...] = jnp.zeros(acc_ref.shape, F32)

    def scores(t, slot):
        s0 = pl.multiple_of(jnp.minimum(t, nt - 1) * TK, TK)
        k_tile = kx_ref[pl.ds(s0, TK), :]
        for g in range(ATTN_KV_HEADS):
            s_ref[g, slot] = lax.dot_general(k_tile, qx_ref[g], NT_DIMS,
                                             preferred_element_type=F32)

    def accumulate(t, slot):
        s0 = pl.multiple_of(t * TK, TK)
        for g in range(ATTN_KV_HEADS):
            s = s_ref[g, slot]
            m_prev = m_ref[g]
            m_new = jnp.maximum(m_prev, jnp.max(s, axis=0, keepdims=True))
            p = jnp.exp2(s - m_new)
            acc_ref[g] = jnp.exp2(m_prev - m_new) * acc_ref[g] + jnp.dot(
                vx_ref[g, :, pl.ds(s0, TK)], p.astype(BF16), preferred_element_type=F32)
            m_ref[g] = m_new

    scores(0, 0)

    def attn_pair(u, carry):
        scores(2 * u + 1, 1)
        accumulate(2 * u, 0)
        scores(2 * u + 2, 0)
        accumulate(2 * u + 1, 1)
        return carry

    lax.fori_loop(0, nt // 2, attn_pair, 0)

    @pl.when(nt < ntp)
    def _():
        accumulate(nt - 1, 0)

    for g in range(ATTN_KV_HEADS):
        acc = acc_ref[g]
        o_t = acc[0:HEAD_DIM, :] * (1.0 / acc[HEAD_DIM:HEAD_DIM + 1, :])
        if wide:
            for r in range(0, ATTN_REP, 2):
                hq = g * ATTN_REP + r
                pair = jnp.concatenate([o_t[:, r * nq:(r + 1) * nq],
                                        o_t[:, (r + 1) * nq:(r + 2) * nq]], axis=0)
                o_ref[0, :, hq * HEAD_DIM:(hq + 2) * HEAD_DIM] = pair.T.astype(BF16)
        else:
            o_rq = jnp.concatenate([o_t, o_t], axis=0).T[:, 0:HEAD_DIM].astype(BF16)
            for r in range(ATTN_REP):
                hq = g * ATTN_REP + r
                o_ref[0, :, hq * HEAD_DIM:(hq + 1) * HEAD_DIM] = o_rq[r * nq:(r + 1) * nq, :]


def _dsa(q, qi, w, kidx_all, k_all, v_all, past_len, l_valid, t):
    b = kidx_all.shape[0]
    lp = kidx_all.shape[1]
    nq = Q_BLOCK if t % Q_BLOCK == 0 else SMALL_Q_BLOCK
    assert t % nq == 0 and lp % (2 * KEY_TILE) == 0 and lp < 1 << KEY_INDEX_BITS
    topk = min(TOPK_MAX, l_valid // 4)
    width = ATTN_REP * nq
    if nq == Q_BLOCK:
        qblk = lambda c: pl.BlockSpec((1, nq, c), lambda i, j: (i, j, 0))
        q_specs = [qblk(ATTN_WIDTH), qblk(IDX_HEADS * IDX_DIM), qblk(LANE)]
    else:
        assert t == nq
        q_specs = [pl.BlockSpec((1, ATTN_KV_HEADS, width, HEAD_DIM), lambda i, j: (i, 0, 0, 0)),
                   pl.BlockSpec((1, IDX_HEADS * nq, IDX_DIM), lambda i, j: (i, 0, 0)),
                   pl.BlockSpec((1, 1, LANE), lambda i, j: (i, 0, 0))]
    slot_onehot = jnp.asarray(
        (np.arange(width)[:, None] % nq == np.arange(LANE)[None, :]).astype(np.float32), BF16)
    per_stream = lambda c: pl.BlockSpec((1, lp, c), lambda i, j: (i, 0, 0))
    return pl.pallas_call(
        functools.partial(_dsa_kernel, past_len=past_len, l_valid=l_valid, topk=topk, nq=nq),
        out_shape=jax.ShapeDtypeStruct((b, t, ATTN_WIDTH), BF16),
        grid=(b, t // nq),
        in_specs=q_specs + [pl.BlockSpec((width, LANE), lambda i, j: (0, 0)),
                            per_stream(IDX_DIM), per_stream(LANE), per_stream(LANE)],
        out_specs=pl.BlockSpec((1, nq, ATTN_WIDTH), lambda i, j: (i, j, 0)),
        scratch_shapes=[pltpu.VMEM((width, IDX_DIM), BF16),
                        pltpu.VMEM((ATTN_KV_HEADS, width, 2 * LANE), BF16),
                        pltpu.VMEM((lp, LANE), I32),
                        pltpu.VMEM((lp, 2 * LANE), BF16),
                        pltpu.VMEM((ATTN_KV_HEADS, PV_ROWS, lp), BF16),
                        pltpu.VMEM((ATTN_KV_HEADS, 1, width), F32),
                        pltpu.VMEM((ATTN_KV_HEADS, PV_ROWS, width), F32),
                        pltpu.VMEM((1, LANE), I32),
                        pltpu.VMEM((ATTN_KV_HEADS, 2, KEY_TILE, width), F32)],
        compiler_params=pltpu.CompilerParams(
            dimension_semantics=("arbitrary", "arbitrary"), vmem_limit_bytes=VMEM_LIMIT),
        name="dsa",
    )(q, qi, w, slot_onehot, kidx_all, k_all, v_all)


def _merge_kernel(x_ref, yn_ref, o_ref, gate_ref, wa_ref, wb_ref, wo_ref, out_ref):
    br_a = jnp.dot(yn_ref[...], wa_ref[...], preferred_element_type=F32)
    br_b = jnp.dot(o_ref[...], wb_ref[...], preferred_element_type=F32)
    gates = gate_ref[...].astype(F32)
    merged = _sigmoid(gates[:, :D_MODEL]) * br_a + _sigmoid(gates[:, D_MODEL:]) * br_b
    out_ref[...] = x_ref[...] + jnp.dot(merged.astype(BF16), wo_ref[...],
                                        preferred_element_type=F32)


def _merge(x2d, yn, o, gate, w_ssm_out, w_attn_out, w_o, tm):
    n = x2d.shape[0]
    assert n % tm == 0
    row = lambda w: pl.BlockSpec((tm, w), lambda i: (i, 0))
    full = lambda r, c: pl.BlockSpec((r, c), lambda i: (0, 0))
    return pl.pallas_call(
        _merge_kernel,
        out_shape=jax.ShapeDtypeStruct((n, D_MODEL), F32),
        grid=(n // tm,),
        in_specs=[row(D_MODEL), row(SSM_INNER), row(ATTN_WIDTH), row(2 * D_MODEL),
                  full(SSM_INNER, D_MODEL), full(ATTN_WIDTH, D_MODEL), full(D_MODEL, D_MODEL)],
        out_specs=row(D_MODEL),
        compiler_params=pltpu.CompilerParams(
            dimension_semantics=("arbitrary",), vmem_limit_bytes=VMEM_LIMIT),
        name="merge",
    )(x2d, yn, o, gate, w_ssm_out.astype(BF16), w_attn_out.astype(BF16), w_o.astype(BF16))


def _moe_kernel(x_ref, ng_ref, wr_ref, br_ref, w1_ref, w3_ref, w2_ref, fg_ref, tri_ref, out_ref,
                hs_ref, combs_ref, acc_ref, pt_ref, seg_ref, *, n_steps, blk_rows):
    e_step = pl.program_id(1)
    tm = x_ref.shape[0]
    sr = hs_ref.shape[0]
    BLK = blk_rows

    @pl.when(e_step == 0)
    def _():
        x = x_ref[...]
        ms = jnp.mean(x * x, axis=-1, keepdims=True)
        h = ((x * lax.rsqrt(ms + EPS)) * ng_ref[...]).astype(BF16)
        logits = jnp.dot(h, wr_ref[...], preferred_element_type=F32) + br_ref[...]
        lane_i = lax.broadcasted_iota(I32, (tm, LANE), 1)
        lane = lane_i.astype(F32)
        first = lambda hit: jnp.min(jnp.where(hit, lane, float(LANE)), axis=1, keepdims=True)
        is_g = (lane_i >= N_EXPERTS) & (lane_i < N_EXPERTS + MOE_GROUPS)
        glog = jnp.where(is_g, logits, -jnp.inf)
        gmax = jnp.max(glog, axis=1, keepdims=True)
        gsel = first(glog == gmax) - float(N_EXPERTS)
        p_group = 1.0 / jnp.sum(jnp.exp(glog - gmax), axis=1, keepdims=True)
        grp = (lane_i >> EXPERT_GROUP_SHIFT).astype(F32)
        in_grp = (lane_i < N_EXPERTS) & (grp == gsel)
        el = jnp.where(in_grp, logits, -jnp.inf)
        v1 = jnp.max(el, axis=1, keepdims=True)
        i1 = first(el == v1)
        el2 = jnp.where(lane == i1, -jnp.inf, el)
        v2 = jnp.max(el2, axis=1, keepdims=True)
        i2 = first(el2 == v2)
        e21 = jnp.exp(v2 - v1)
        den = 1.0 + e21
        comb = (jnp.where(lane == i1, (1.0 / den) * p_group, 0.0)
                + jnp.where(lane == i2, (e21 / den) * p_group, 0.0))

        own = lane == gsel
        own_f = jnp.where(own, 1.0, 0.0)
        before = jnp.dot(tri_ref[...], own_f.astype(BF16), preferred_element_type=F32)
        rank = jnp.sum(jnp.where(own, before, 0.0), axis=1, keepdims=True)
        counts = jnp.sum(own_f, axis=0, keepdims=True)
        blocks = jnp.floor((counts + (BLK - 1)) * (1.0 / BLK))
        upper = (lax.broadcasted_iota(I32, (LANE, LANE), 0)
                 < lax.broadcasted_iota(I32, (LANE, LANE), 1))
        starts = jnp.dot(jnp.broadcast_to(blocks, (8, LANE)).astype(BF16),
                         jnp.where(upper, 1.0, 0.0).astype(BF16),
                         preferred_element_type=F32)[0:1, :] * BLK
        dest = jnp.sum(jnp.where(own, starts, 0.0), axis=1, keepdims=True) + rank
        for g in range(MOE_GROUPS):
            seg_ref[g] = jnp.sum(starts[:, g:g + 1]).astype(I32)
            seg_ref[MOE_GROUPS + g] = jnp.sum(blocks[:, g:g + 1]).astype(I32)

        dest_i = dest.astype(I32)
        pt = jnp.where(dest_i == lax.broadcasted_iota(I32, (tm, sr), 1), 1.0, 0.0).astype(BF16)
        pt_ref[...] = pt
        dest_row = jnp.broadcast_to(dest, (tm, LANE)).T[0:1, :].astype(I32)
        p_mat = jnp.where(lax.broadcasted_iota(I32, (sr, tm), 0) == dest_row, 1.0, 0.0
                          ).astype(BF16)
        hs_ref[...] = jnp.dot(p_mat, h, preferred_element_type=F32).astype(BF16)
        c3 = jnp.dot(p_mat, _split3(comb), preferred_element_type=F32)
        combs_ref[...] = c3[:, 0:LANE] + c3[:, LANE:2 * LANE] + c3[:, 2 * LANE:]
        acc_ref[...] = jnp.zeros(acc_ref.shape, F32)

    group = e_step // (EXPERTS_PER_GROUP // EXPERTS_PER_STEP)
    seg_start = seg_ref[group]

    def expert_rows(first_blk, n_rows):
        rows = pl.ds(pl.multiple_of(seg_start + first_blk * BLK, BLK), n_rows)
        h = hs_ref[rows, :]
        comb = combs_ref[rows, :]
        lane = lax.broadcasted_iota(I32, (n_rows, LANE), 1)
        for k in range(EXPERTS_PER_STEP):
            e = e_step * EXPERTS_PER_STEP + k
            a1 = jnp.dot(h, w1_ref[k], preferred_element_type=F32)
            a3 = jnp.dot(h, w3_ref[k], preferred_element_type=F32)
            he = (_silu(a1) * a3).astype(BF16)
            ye = jnp.dot(he, w2_ref[k], preferred_element_type=F32)
            wcol = jnp.sum(jnp.where(lane == e, comb, 0.0), axis=1, keepdims=True)
            acc_ref[rows, :] += wcol * ye

    n_blocks = seg_ref[MOE_GROUPS + group]
    odd_tail = (n_blocks % 2 == 1) & (n_blocks >= 3)
    n_pairs = jnp.where(odd_tail, (n_blocks - 3) // 2, n_blocks // 2)

    def block_pair(u, carry):
        expert_rows(2 * u, 2 * BLK)
        return carry

    lax.fori_loop(0, n_pairs, block_pair, 0)

    @pl.when(odd_tail)
    def _():
        expert_rows(n_blocks - 3, 3 * BLK)

    @pl.when(n_blocks == 1)
    def _():
        expert_rows(0, BLK)

    @pl.when(e_step == n_steps - 1)
    def _():
        y = x_ref[...] + jnp.dot(pt_ref[...], acc_ref[...].astype(BF16),
                                 preferred_element_type=F32)
        ms = jnp.mean(y * y, axis=-1, keepdims=True)
        out_ref[...] = (y * lax.rsqrt(ms + EPS)) * fg_ref[...]


def _moe(x2d, norm_g, w_router_group, b_router_group, w_router_expert, b_router_expert,
         w1, w3, w2, norm_final_g, tm):
    n = x2d.shape[0]
    blk_rows = min(MOE_ROW_BLOCK, tm)
    assert n % tm == 0 and tm % blk_rows == 0
    n_steps = N_EXPERTS // EXPERTS_PER_STEP
    sr = tm + (MOE_GROUPS - 1) * blk_rows
    padw = LANE - N_EXPERTS - MOE_GROUPS
    wr = jnp.concatenate([w_router_expert, w_router_group,
                          jnp.zeros((D_MODEL, padw), F32)], axis=1).astype(BF16)
    br = jnp.concatenate([b_router_expert, b_router_group,
                          jnp.zeros((padw,), F32)]).astype(F32).reshape(1, LANE)
    tri = jnp.asarray(np.tril(np.ones((tm, tm), np.float32), -1), BF16)
    row = pl.BlockSpec((tm, D_MODEL), lambda i, e: (i, 0))
    const = lambda r, c: pl.BlockSpec((r, c), lambda i, e: (0, 0))
    return pl.pallas_call(
        functools.partial(_moe_kernel, n_steps=n_steps, blk_rows=blk_rows),
        out_shape=jax.ShapeDtypeStruct((n, D_MODEL), F32),
        grid=(n // tm, n_steps),
        in_specs=[row, const(1, D_MODEL), const(D_MODEL, LANE), const(1, LANE),
                  pl.BlockSpec((EXPERTS_PER_STEP, D_MODEL, EXPERT_FF), lambda i, e: (e, 0, 0)),
                  pl.BlockSpec((EXPERTS_PER_STEP, D_MODEL, EXPERT_FF), lambda i, e: (e, 0, 0)),
                  pl.BlockSpec((EXPERTS_PER_STEP, EXPERT_FF, D_MODEL), lambda i, e: (e, 0, 0)),
                  const(1, D_MODEL), const(tm, tm)],
        out_specs=row,
        scratch_shapes=[pltpu.VMEM((sr, D_MODEL), BF16), pltpu.VMEM((sr, LANE), F32),
                        pltpu.VMEM((sr, D_MODEL), F32), pltpu.VMEM((tm, sr), BF16),
                        pltpu.SMEM((2 * MOE_GROUPS,), I32)],
        compiler_params=pltpu.CompilerParams(
            dimension_semantics=("arbitrary", "arbitrary"), vmem_limit_bytes=VMEM_LIMIT),
        name="moe",
    )(x2d, norm_g.reshape(1, D_MODEL), wr, br, w1.astype(BF16), w3.astype(BF16),
      w2.astype(BF16), norm_final_g.reshape(1, D_MODEL), tri)


def _token_tile(n):
    for tm in (1024, 512, 256, 128, 64, 32, 16, 8):
        if n % tm == 0:
            return tm
    raise ValueError(f"token count {n} is not a multiple of 8")


def _pad_axis(a, axis, size):
    if a.shape[axis] == size:
        return a
    widths = [(0, 0)] * a.ndim
    widths[axis] = (0, size - a.shape[axis])
    return jnp.pad(a, widths)


def _round_up(n, m):
    return (n + m - 1) // m * m


def _stream_step(x, conv_state, ssm_state, k_past, v_past, kidx_past, w_packed, p):
    b, t, d = x.shape
    n = b * t
    tm = _token_tile(n)
    x2d = x.reshape(n, d)
    z, xbc, q, gate, k, v, qi, ki, dtw, k_b, v_b, ki_b = _in_proj(
        x2d, p["norm_mix_g"], w_packed, min(tm, PROJ_ROWS))

    yn, conv_new, ssm_new = _ssd(
        xbc.reshape(b, t, CONV_DIM), z.reshape(b, t, SSM_INNER), dtw.reshape(b, t, LANE),
        conv_state, ssm_state, p["conv_w"], p["conv_b"], p["dt_bias"], p["A_log"],
        p["D_skip"], p["ssm_norm_g"])

    kv_w = ATTN_KV_HEADS * HEAD_DIM
    k_all, v_all, kidx_all = (k_b.reshape(b, t, kv_w), v_b.reshape(b, t, kv_w),
                              ki_b.reshape(b, t, IDX_DIM))
    past_len = 0
    if k_past is not None:
        past_len = k_past.shape[1]
        k_all = jnp.concatenate([k_past.astype(BF16).reshape(b, past_len, kv_w), k_all], axis=1)
        v_all = jnp.concatenate([v_past.astype(BF16).reshape(b, past_len, kv_w), v_all], axis=1)
        kidx_all = jnp.concatenate([kidx_past.astype(BF16), kidx_all], axis=1)
    l_valid = past_len + t
    lp = _round_up(l_valid, 2 * KEY_TILE)
    if t % Q_BLOCK == 0:
        q_in = (q.reshape(b, t, ATTN_WIDTH), qi.reshape(b, t, IDX_HEADS * IDX_DIM),
                dtw.reshape(b, t, LANE))
    else:
        q_in = (q.reshape(b, t, ATTN_KV_HEADS, ATTN_REP, HEAD_DIM).transpose(0, 2, 3, 1, 4)
                .reshape(b, ATTN_KV_HEADS, ATTN_REP * t, HEAD_DIM),
                qi.reshape(b, t, IDX_HEADS, IDX_DIM).transpose(0, 2, 1, 3)
                .reshape(b, IDX_HEADS * t, IDX_DIM),
                dtw.reshape(b, t, LANE)[:, :, SSM_HEADS:SSM_HEADS + IDX_HEADS]
                .transpose(0, 2, 1).reshape(b, 1, IDX_HEADS * t))
    o = _dsa(*q_in, _pad_axis(kidx_all, 1, lp), _pad_axis(k_all, 1, lp), _pad_axis(v_all, 1, lp),
             past_len, l_valid, t)
    o = o.reshape(n, ATTN_WIDTH)

    x1 = _merge(x2d, yn.reshape(n, SSM_INNER), o, gate, p["w_ssm_out"], p["w_attn_out"],
                p["w_o"], min(tm, MERGE_TILE))
    y = _moe(x1, p["norm_moe_g"], p["w_router_group"], p["b_router_group"],
             p["w_router_expert"], p["b_router_expert"], p["w1"], p["w3"], p["w2"],
             p["norm_final_g"], tm)
    return (y.reshape(b, t, d), k.reshape(b, t, ATTN_KV_HEADS, HEAD_DIM),
            v.reshape(b, t, ATTN_KV_HEADS, HEAD_DIM), ki.reshape(b, t, IDX_DIM), conv_new, ssm_new)


def kernel(x_prompt, x_sample, cache_k, cache_v, cache_kidx, state_conv, state_ssm, norm_mix_g, w_in, conv_w, conv_b, dt_bias, A_log, D_skip, ssm_norm_g, w_ssm_out, w_attn_out, w_o, norm_moe_g, w_router_group, b_router_group, w_router_expert, b_router_expert, w1, w3, w2, norm_final_g):
    p = dict(norm_mix_g=norm_mix_g, conv_w=conv_w, conv_b=conv_b, dt_bias=dt_bias, A_log=A_log,
             D_skip=D_skip, ssm_norm_g=ssm_norm_g, w_ssm_out=w_ssm_out, w_attn_out=w_attn_out,
             w_o=w_o, norm_moe_g=norm_moe_g, w_router_group=w_router_group,
             b_router_group=b_router_group, w_router_expert=w_router_expert,
             b_router_expert=b_router_expert, w1=w1, w3=w3, w2=w2, norm_final_g=norm_final_g)
    w_packed = _pack_w_in(w_in)
    yp, kp, vp, kip, cp, sp = _stream_step(x_prompt, None, None, None, None, None, w_packed, p)
    ys, ksn, vsn, kisn, csn, ssn = _stream_step(x_sample, state_conv, state_ssm, cache_k,
                                                cache_v, cache_kidx, w_packed, p)
    return (yp, ys, kp, vp, kip, cp, sp, ksn, vsn, kisn, csn, ssn)
```

```python
import functools

import numpy as np
import jax
import jax.numpy as jnp
from jax import lax
from jax.experimental import pallas as pl
from jax.experimental.pallas import tpu as pltpu

F32 = jnp.float32
BF16 = jnp.bfloat16
I32 = jnp.int32

D_MODEL = 1024
CHUNK = 64
CHUNK_SHIFT = 6
EPS = 1e-6
SSM_INNER = 2048
SSM_HEADDIM = 64
SSM_HEADS = 32
SSM_GROUPS = 4
SSM_HEADS_PER_GROUP = SSM_HEADS // SSM_GROUPS
SSM_STATE = 128
CONV_WIDTH = 4
CONV_DIM = SSM_INNER + 2 * SSM_GROUPS * SSM_STATE
CONV_CARRY = 16
SSD_CHUNKS_PER_STEP = 8
CONV_BLOCK = 128
ATTN_HEADS = 16
ATTN_KV_HEADS = 2
HEAD_DIM = 64
ATTN_REP = ATTN_HEADS // ATTN_KV_HEADS
ATTN_WIDTH = ATTN_HEADS * HEAD_DIM
IDX_HEADS = 8
IDX_DIM = 64
IDX_SCALE = (IDX_HEADS * IDX_DIM) ** -0.5
TOPK_MAX = 256
MOE_GROUPS = 4
EXPERTS_PER_GROUP = 8
EXPERT_GROUP_SHIFT = 3
N_EXPERTS = 32
EXPERT_FF = 256
N_BRANCHES = 2

LANE = 128
VMEM_LIMIT = 56 * 1024 * 1024
PROJ_TILE = 1024
PROJ_ROWS = 512
PROJ_TILES = 9
Q_BLOCK = 128
SMALL_Q_BLOCK = 16
KEY_TILE = 256
KEY_INDEX_BITS = 15
COUNT_CHAINS = 4
EXPERTS_PER_STEP = 4
MOE_ROW_BLOCK = 128
MERGE_TILE = 512
INT_MIN = np.int32(-2 ** 31)
NEG_BIG = -1e30
Q_SCALE = HEAD_DIM ** -0.5 * float(np.log2(np.e))
PV_ROWS = 80

NT_DIMS = (((1,), (1,)), ((), ()))
TN_DIMS = (((0,), (0,)), ((), ()))


def _sigmoid(x):
    return 1.0 / (1.0 + jnp.exp(-x))


def _silu(x):
    h = 0.5 * x
    return h + h * jnp.tanh(h)


def _split3(x):
    hi = x.astype(BF16)
    r1 = x - hi.astype(F32)
    mid = r1.astype(BF16)
    r2 = r1 - mid.astype(F32)
    lo = r2.astype(BF16)
    return jnp.concatenate([hi, mid, lo], axis=1)


def _inproj_kernel(x_ref, g_ref, w_ref, z_ref, xbc_ref, q_ref, gate_ref, k_ref, v_ref,
                   qi_ref, ki_ref, dtw_ref, kb_ref, vb_ref, kib_ref):
    x = x_ref[...]
    ms = jnp.mean(x * x, axis=-1, keepdims=True)
    h = ((x * lax.rsqrt(ms + EPS)) * g_ref[...]).astype(BF16)

    def cols(first, width):
        lo = first * PROJ_TILE
        return jnp.dot(h, w_ref[:, lo:lo + width], preferred_element_type=F32)

    for j in range(2):
        z_ref[:, j * PROJ_TILE:(j + 1) * PROJ_TILE] = cols(j, PROJ_TILE).astype(BF16)
    for j in range(3):
        xbc_ref[:, j * PROJ_TILE:(j + 1) * PROJ_TILE] = cols(2 + j, PROJ_TILE).astype(BF16)
    q_ref[...] = (cols(5, PROJ_TILE) * Q_SCALE).astype(BF16)
    for j in range(2):
        gate_ref[:, j * PROJ_TILE:(j + 1) * PROJ_TILE] = cols(6 + j, PROJ_TILE).astype(BF16)
    t = cols(8, PROJ_TILE)
    k_ref[...] = t[:, 0:128]
    v_ref[...] = t[:, 128:256]
    qi_ref[...] = t[:, 256:768].astype(BF16)
    ki_ref[...] = t[:, 768:832]
    dtw_ref[...] = t[:, 896:1024]
    kb_ref[...] = t[:, 0:128].astype(BF16)
    vb_ref[...] = t[:, 128:256].astype(BF16)
    kib_ref[...] = t[:, 768:832].astype(BF16)


def _pack_w_in(w_in):
    sizes = (SSM_INNER, CONV_DIM, SSM_HEADS, ATTN_WIDTH, ATTN_KV_HEADS * HEAD_DIM,
             ATTN_KV_HEADS * HEAD_DIM, IDX_HEADS * IDX_DIM, IDX_DIM, IDX_HEADS,
             N_BRANCHES * D_MODEL)
    offs = np.concatenate([[0], np.cumsum(sizes)])
    z, xbc, dt, q, k, v, qi, ki, wi, g = [w_in[:, offs[i]:offs[i + 1]] for i in range(10)]
    zeros = lambda n: jnp.zeros((D_MODEL, n), w_in.dtype)
    cols = [z, xbc, q, g, k, v, qi, ki, zeros(64), dt, wi, zeros(LANE - SSM_HEADS - IDX_HEADS)]
    return jnp.concatenate(cols, axis=1).astype(BF16)


def _in_proj(x2d, norm_g, w_packed, tm):
    n = x2d.shape[0]
    assert n % tm == 0
    bf = lambda c: jax.ShapeDtypeStruct((n, c), BF16)
    ff = lambda c: jax.ShapeDtypeStruct((n, c), F32)
    rows = lambda c: pl.BlockSpec((tm, c), lambda i: (i, 0))
    widths = (SSM_INNER, CONV_DIM, ATTN_WIDTH, 2 * D_MODEL, 128, 128, 512, IDX_DIM, 128,
              128, 128, IDX_DIM)
    return pl.pallas_call(
        _inproj_kernel,
        out_shape=(bf(SSM_INNER), bf(CONV_DIM), bf(ATTN_WIDTH), bf(2 * D_MODEL),
                   ff(128), ff(128), bf(512), ff(IDX_DIM), ff(128), bf(128), bf(128), bf(IDX_DIM)),
        grid=(n // tm,),
        in_specs=[
            rows(D_MODEL),
            pl.BlockSpec((1, D_MODEL), lambda i: (0, 0)),
            pl.BlockSpec((D_MODEL, PROJ_TILES * PROJ_TILE), lambda i: (0, 0),
                         pipeline_mode=pl.Buffered(1)),
        ],
        out_specs=tuple(rows(c) for c in widths),
        compiler_params=pltpu.CompilerParams(
            dimension_semantics=("arbitrary",), vmem_limit_bytes=VMEM_LIMIT),
        name="in_proj",
    )(x2d, norm_g.reshape(1, D_MODEL), w_packed)


def _ssd_kernel(*refs, L, cps, nsteps, has_state, first_shift):
    if has_state:
        (xbc_ref, z_ref, dtw_ref, cs_ref, s0_ref, shift_ref, convw_ref, convb_ref, dtb_ref, a_ref,
         dx_ref, ng_ref, e3p_ref, e3l_ref, tri_ref, bd_ref, yn_ref, cnew_ref, sout_ref,
         xe_ref, st, act_ref, ex_ref, yz_ref, acl_ref) = refs
    else:
        (xbc_ref, z_ref, dtw_ref, shift_ref, convw_ref, convb_ref, dtb_ref, a_ref,
         dx_ref, ng_ref, e3p_ref, e3l_ref, tri_ref, bd_ref, yn_ref, cnew_ref, sout_ref,
         xe_ref, st, act_ref, ex_ref, yz_ref, acl_ref) = refs
    c = pl.program_id(1)
    G, R, N, P = SSM_GROUPS, SSM_HEADS_PER_GROUP, SSM_STATE, SSM_HEADDIM
    GW = R * P
    RL = R * L
    LB = cps * L

    CB = min(LB, CONV_BLOCK)
    @pl.when(c == 0)
    def _():
        xe_ref[CB:, :] = jnp.zeros((xe_ref.shape[0] - CB, CONV_DIM), BF16)
        if has_state:
            xe_ref[CB:CB + CONV_CARRY, :] = cs_ref[0]
            st[...] = s0_ref[0].T
        else:
            st[...] = jnp.zeros(st.shape, F32)

    w = convw_ref[...]
    for cb in range(LB // CB):
        blk = slice(cb * CB, (cb + 1) * CB)
        xe_ref[0:CB, :] = xbc_ref[0, blk, :]
        which = jnp.where(c == 0, first_shift, 1) if cb == 0 else 1
        shifted = jnp.dot(shift_ref[which], xe_ref[...], preferred_element_type=F32)
        conv = convb_ref[...] + shifted[2 * CB:3 * CB, :] * w[0:1, :]
        conv = conv + shifted[CB:2 * CB, :] * w[1:2, :]
        conv = conv + shifted[0:CB, :] * w[2:3, :]
        conv = conv + xbc_ref[0, blk, :].astype(F32) * w[3:4, :]
        act_ref[blk, :] = _silu(conv)
        if nsteps * (LB // CB) > 1:
            xe_ref[CB:CB + CONV_CARRY, :] = xe_ref[CB - CONV_CARRY:CB, :]

    @pl.when(c == nsteps - 1)
    def _():
        cnew_ref[0] = xbc_ref[0, LB - CONV_CARRY:LB, :].astype(F32)[
            CONV_CARRY - (CONV_WIDTH - 1):CONV_CARRY, :]

    u = dtw_ref[0] + dtb_ref[...]
    dt = jnp.maximum(u, 0.0) + jnp.log1p(jnp.exp(-jnp.abs(u)))
    a = dt * a_ref[...]
    ac3 = jnp.dot(tri_ref[...], _split3(a), preferred_element_type=F32)
    a_cum = ac3[:, 0:LANE] + ac3[:, LANE:2 * LANE] + ac3[:, 2 * LANE:]

    ex_ref[...] = jnp.dot(_split3(jnp.concatenate([a_cum, dt], axis=0)), e3p_ref[...],
                          preferred_element_type=F32)
    if L != P:
        acl_ref[...] = jnp.dot(_split3(a_cum), e3l_ref[...], preferred_element_type=F32)
    row = lax.broadcasted_iota(I32, (L, RL), 0)
    scol = lax.broadcasted_iota(I32, (L, RL), 1) & (L - 1)

    for sub in range(cps):
        r0 = sub * L
        rows = slice(r0, r0 + L)
        ssq = jnp.zeros((L, 1), F32)
        for g in range(G):
            sl = slice(g * GW, (g + 1) * GW)
            xs = act_ref[rows, sl]
            b_g = act_ref[rows, SSM_INNER + g * N:SSM_INNER + (g + 1) * N].astype(BF16)
            c_g = act_ref[rows, SSM_INNER + (G + g) * N:SSM_INNER + (G + g + 1) * N].astype(BF16)
            acx = ex_ref[rows, sl]
            alast = ex_ref[r0 + L - 1:r0 + L, sl]
            xdt = xs * ex_ref[LB + r0:LB + r0 + L, sl]
            acl_g = acx if L == P else acl_ref[rows, g * RL:(g + 1) * RL]
            acs_g = jnp.sum(jnp.where(scol == row, acl_g, 0.0), axis=0, keepdims=True)
            decay = jnp.where(scol <= row, jnp.exp(acl_g - acs_g), 0.0)
            cb = lax.dot_general(c_g, jnp.concatenate([b_g] * R, axis=0), NT_DIMS,
                                 preferred_element_type=F32)
            m_g = (cb * decay).astype(BF16)
            xbd = jnp.concatenate([xdt.astype(BF16)] * R, axis=0) * bd_ref[...]
            y_in = jnp.dot(m_g, xbd, preferred_element_type=F32)
            st_g = st[:, sl]
            y_st = jnp.dot(c_g, st_g.astype(BF16), preferred_element_type=F32)
            xw = (xdt * jnp.exp(alast - acx)).astype(BF16)
            upd = lax.dot_general(b_g, xw, TN_DIMS, preferred_element_type=F32)
            st[:, sl] = st_g * jnp.exp(alast) + upd
            y = y_in + jnp.exp(acx) * y_st + dx_ref[:, sl] * xs
            yz = y * _silu(z_ref[0, rows, sl].astype(F32))
            ssq = ssq + jnp.sum(yz * yz, axis=-1, keepdims=True)
            yz_ref[rows, sl] = yz
        scale = lax.rsqrt(ssq * (1.0 / SSM_INNER) + EPS)
        yn_ref[0, rows, :] = ((yz_ref[rows, :] * scale) * ng_ref[...]).astype(BF16)

    @pl.when(c == nsteps - 1)
    def _():
        sout_ref[0] = st[...].T


def _shift_matrices(L, conv_rows):
    out = np.zeros((2, 3 * L, conv_rows), np.float32)
    for k in range(CONV_WIDTH - 1):
        for t in range(L):
            src = t - (k + 1)
            if src >= 0:
                out[:, k * L + t, src] = 1.0
            else:
                out[1, k * L + t, L + CONV_CARRY + src] = 1.0
                j = CONV_WIDTH - 1 + src
                for piece in range(3):
                    out[0, k * L + t, L + 3 * piece + j] = 1.0
    return jnp.asarray(out, BF16)


def _expand_matrix(width_per_head):
    r = np.arange(3 * LANE)[:, None] % LANE
    c = np.arange(SSM_HEADS * width_per_head)[None, :] // width_per_head
    return jnp.asarray((r == c).astype(np.float32), BF16)


def _ssd(xbc, z, dtw, conv_state, ssm_state, conv_w, conv_b, dt_bias, a_log, d_skip, ssm_norm_g):
    b, t, _ = xbc.shape
    L = min(CHUNK, t)
    nc = t // L
    assert t % L == 0 and L & (L - 1) == 0
    has_state = conv_state is not None
    pad = lambda v: jnp.pad(v.astype(F32), (0, LANE - SSM_HEADS)).reshape(1, LANE)
    a_neg = pad(-jnp.exp(a_log.astype(F32)))
    cps = SSD_CHUNKS_PER_STEP if nc % SSD_CHUNKS_PER_STEP == 0 else 1
    nsteps = nc // cps
    lb = cps * L
    cb_rows = min(lb, CONV_BLOCK)
    conv_rows = _round_up(cb_rows + CONV_CARRY, LANE)
    tri = jnp.asarray(np.kron(np.eye(cps), np.tril(np.ones((L, L)))).astype(np.float32), BF16)
    rl = SSM_HEADS_PER_GROUP * L
    gw = SSM_HEADS_PER_GROUP * SSM_HEADDIM
    bd = jnp.asarray((np.arange(rl)[:, None] // L == np.arange(gw)[None, :] // SSM_HEADDIM)
                     .astype(np.float32), BF16)
    const = lambda shape: pl.BlockSpec(shape, lambda i, j: (0,) * len(shape))
    tok = lambda w: pl.BlockSpec((1, lb, w), lambda i, j: (i, j, 0))
    per_b = lambda s: pl.BlockSpec((1,) + s, lambda i, j: (i, 0, 0))
    ins = [xbc, z, dtw]
    specs = [tok(CONV_DIM), tok(SSM_INNER), tok(LANE)]
    if has_state:
        triples = _split3(conv_state.astype(F32))
        ins += [_pad_axis(triples, 1, CONV_CARRY), ssm_state.reshape(b, SSM_INNER, SSM_STATE)]
        specs += [per_b((CONV_CARRY, CONV_DIM)), per_b((SSM_INNER, SSM_STATE))]
    first = 0 if has_state else 1
    ins += [_shift_matrices(cb_rows, conv_rows)]
    specs += [const((2, 3 * cb_rows, conv_rows))]
    ins += [conv_w, conv_b.reshape(1, CONV_DIM), pad(dt_bias), a_neg,
            jnp.repeat(d_skip.astype(F32), SSM_HEADDIM).reshape(1, SSM_INNER),
            ssm_norm_g.reshape(1, SSM_INNER), _expand_matrix(SSM_HEADDIM), _expand_matrix(L),
            tri, bd]
    specs += [const((CONV_WIDTH, CONV_DIM)), const((1, CONV_DIM)), const((1, LANE)),
              const((1, LANE)), const((1, SSM_INNER)), const((1, SSM_INNER)),
              const((3 * LANE, SSM_INNER)), const((3 * LANE, SSM_HEADS * L)),
              const((lb, lb)), const((rl, gw))]
    yn, cnew, sout = pl.pallas_call(
        functools.partial(_ssd_kernel, L=L, cps=cps, nsteps=nsteps, has_state=has_state,
                          first_shift=first),
        out_shape=(jax.ShapeDtypeStruct((b, t, SSM_INNER), BF16),
                   jax.ShapeDtypeStruct((b, CONV_WIDTH - 1, CONV_DIM), F32),
                   jax.ShapeDtypeStruct((b, SSM_INNER, SSM_STATE), F32)),
        grid=(b, nsteps),
        in_specs=specs,
        out_specs=(tok(SSM_INNER), per_b((CONV_WIDTH - 1, CONV_DIM)),
                   per_b((SSM_INNER, SSM_STATE))),
        scratch_shapes=[pltpu.VMEM((conv_rows, CONV_DIM), BF16),
                        pltpu.VMEM((SSM_STATE, SSM_INNER), F32),
                        pltpu.VMEM((lb, CONV_DIM), F32),
                        pltpu.VMEM((2 * lb, SSM_INNER), F32),
                        pltpu.VMEM((lb, SSM_INNER), F32),
                        pltpu.VMEM((lb, SSM_HEADS * L), F32)],
        compiler_params=pltpu.CompilerParams(
            dimension_semantics=("arbitrary", "arbitrary"), vmem_limit_bytes=VMEM_LIMIT),
        name="ssd",
    )(*ins)
    return yn, cnew, sout.reshape(b, SSM_HEADS, SSM_HEADDIM, SSM_STATE)


def _dsa_kernel(q_ref, qi_ref, w_ref, eq_ref, kidx_ref, k_ref, v_ref, o_ref,
                qs_ref, qx_ref, key_ref, kx_ref, vx_ref, m_ref, acc_ref, mlim_ref, s_ref,
                *, past_len, l_valid, topk, nq):
    qb = pl.program_id(1)
    TQ, TK = LANE, KEY_TILE
    W = ATTN_REP * nq
    wide = nq == LANE

    lane = lax.broadcasted_iota(I32, (1, TQ), 1)
    pos = past_len + qb * nq + (lane & (nq - 1))
    n_adm = jnp.minimum(((pos >> CHUNK_SHIFT) + 1) * CHUNK, l_valid)
    k_eff = jnp.minimum(n_adm, topk)
    last_pos = past_len + qb * nq + nq - 1
    n_max = jnp.minimum(((last_pos >> CHUNK_SHIFT) + 1) * CHUNK, l_valid)
    nt = (n_max + TK - 1) // TK
    ntp = 2 * ((nt + 1) // 2)
    lp = key_ref.shape[0]

    @pl.when(qb == 0)
    def _():
        kx_ref[:, LANE:] = k_ref[0]

        def v_tile(t, carry):
            s0 = pl.multiple_of(t * TK, TK)
            v_t = v_ref[0, pl.ds(s0, TK), :].astype(F32).T
            for g in range(ATTN_KV_HEADS):
                vx_ref[g, 0:HEAD_DIM, pl.ds(s0, TK)] = (
                    v_t[g * HEAD_DIM:(g + 1) * HEAD_DIM, :].astype(BF16))
            return carry

        lax.fori_loop(0, lp // TK, v_tile, 0)
        tail_row = lax.broadcasted_iota(I32, (PV_ROWS - HEAD_DIM, lp), 0)
        for g in range(ATTN_KV_HEADS):
            vx_ref[g, HEAD_DIM:, :] = jnp.where(tail_row == 0, 1.0, 0.0).astype(BF16)
            qx_ref[g, :, 0:LANE] = eq_ref[...]
            qx_ref[g, :, LANE:] = jnp.zeros((W, LANE), BF16)

    if wide:
        for h in range(IDX_HEADS):
            qs_ref[h * nq:(h + 1) * nq, :] = qi_ref[0, :, h * IDX_DIM:(h + 1) * IDX_DIM]
        w_t = w_ref[0].T[SSM_HEADS:SSM_HEADS + IDX_HEADS, :] * IDX_SCALE
    else:
        qs_ref[...] = qi_ref[0]
        w_row = w_ref[0] * IDX_SCALE

    def index_dots(t, slot):
        s0 = pl.multiple_of(jnp.minimum(t, nt - 1) * TK, TK)
        s_ref[0, slot] = lax.dot_general(kidx_ref[0, pl.ds(s0, TK), :], qs_ref[...], NT_DIMS,
                                         preferred_element_type=F32)

    def score_tile(t, slot):
        s0 = pl.multiple_of(t * TK, TK)
        d = s_ref[0, slot]
        if wide:
            sc = w_t[0:1, :] * jnp.maximum(d[:, 0:nq], 0.0)
            for h in range(1, IDX_HEADS):
                sc = sc + w_t[h:h + 1, :] * jnp.maximum(d[:, h * nq:(h + 1) * nq], 0.0)
        else:
            sc = w_row * jnp.maximum(d, 0.0)
            for shift in [nq << i for i in reversed(range(IDX_HEADS.bit_length() - 1))]:
                sc = sc + pltpu.roll(sc, shift, axis=1)
        sc = sc + 0.0
        bits = pltpu.bitcast(sc, I32)
        key = jnp.where(bits < 0, bits ^ np.int32(0x7FFFFFFF), bits)
        s_idx = s0 + lax.broadcasted_iota(I32, (TK, TQ), 0)
        key_ref[pl.ds(s0, TK), :] = jnp.where(s_idx < n_adm, key, INT_MIN)

    index_dots(0, 0)

    def score_pair(u, carry):
        index_dots(2 * u + 1, 1)
        score_tile(2 * u, 0)
        index_dots(2 * u + 2, 0)
        score_tile(2 * u + 1, 1)
        return carry

    lax.fori_loop(0, nt // 2, score_pair, 0)

    @pl.when(nt < ntp)
    def _():
        score_tile(nt - 1, 0)
        key_ref[pl.ds(pl.multiple_of(nt * TK, TK), TK), :] = jnp.full((TK, TQ), INT_MIN, I32)

    def count(pred):
        n_vregs = 2 * TK // 8

        def body(t, accs):
            s0 = pl.multiple_of(t * (2 * TK), 2 * TK)
            blk = key_ref[pl.ds(s0, 2 * TK), :].reshape(n_vregs, 8, TQ)
            s_idx = s0 + lax.broadcasted_iota(I32, (2 * TK, TQ), 0).reshape(n_vregs, 8, TQ)
            accs = list(accs)
            for r in range(n_vregs):
                a = accs[r % COUNT_CHAINS]
                accs[r % COUNT_CHAINS] = jnp.where(pred(blk[r], s_idx[r]), a + 1, a)
            return tuple(accs)

        zero = jnp.zeros((8, TQ), I32)
        accs = lax.fori_loop(0, ntp // 2, body, (zero,) * COUNT_CHAINS)
        acc = accs[0]
        for a in accs[1:]:
            acc = acc + a
        return jnp.sum(acc, axis=0, keepdims=True)

    def bit_step(i, state):
        prefix, n_ge, n_gt = state
        cand = prefix | jnp.left_shift(jnp.int32(1), 31 - i)
        cand_s = cand ^ INT_MIN
        cnt = count(lambda blk, s_idx: blk >= cand_s)
        ok = cnt >= k_eff
        return (jnp.where(ok, cand, prefix), jnp.where(ok, cnt, n_ge),
                jnp.where(ok, n_gt, cnt))

    prefix, n_ge, n_gt = lax.fori_loop(
        0, 16, lambda i, state: bit_step(2 * i + 1, bit_step(2 * i, state)),
        (jnp.zeros((1, TQ), I32), jnp.full((1, TQ), lp, I32), jnp.zeros((1, TQ), I32)))
    thr = prefix ^ INT_MIN

    need = k_eff - n_gt
    mlim_ref[...] = jnp.full((1, TQ), 1 << KEY_INDEX_BITS, I32)

    @pl.when(jnp.max(n_ge - k_eff) > 0)
    def _():
        def idx_step(i, prefix):
            cand = prefix | jnp.left_shift(jnp.int32(1), KEY_INDEX_BITS - 1 - i)
            cnt = count(lambda blk, s_idx: (blk == thr) & (s_idx < cand))
            return jnp.where(cnt < need, cand, prefix)
        mlim_ref[...] = lax.fori_loop(0, KEY_INDEX_BITS, idx_step, jnp.zeros((1, TQ), I32))

    mlim = mlim_ref[...]

    def bias_tile(t, carry):
        s0 = pl.multiple_of(t * TK, TK)
        blk = key_ref[pl.ds(s0, TK), :]
        s_idx = s0 + lax.broadcasted_iota(I32, (TK, TQ), 0)
        sel = (blk > thr) | ((blk == thr) & (s_idx <= mlim))
        kx_ref[pl.ds(s0, TK), 0:LANE] = jnp.where(sel, 0.0, NEG_BIG).astype(BF16)
        return carry

    lax.fori_loop(0, nt, bias_tile, 0)

    for g in range(ATTN_KV_HEADS):
        lanes = slice(LANE + g * HEAD_DIM, LANE + (g + 1) * HEAD_DIM)
        if wide:
            for r in range(ATTN_REP):
                hq = g * ATTN_REP + r
                qx_ref[g, r * nq:(r + 1) * nq, lanes] = (
                    q_ref[0, :, hq * HEAD_DIM:(hq + 1) * HEAD_DIM])
        else:
            qx_ref[g, :, lanes] = q_ref[0, g]
    m_ref[...] = jnp.full(m_ref.shape, -jnp.inf, F32)
    acc_ref[...] = jnp.zeros(acc_ref.shape, F32)

    def scores(t, slot):
        s0 = pl.multiple_of(jnp.minimum(t, nt - 1) * TK, TK)
        k_tile = kx_ref[pl.ds(s0, TK), :]
        for g in range(ATTN_KV_HEADS):
            s_ref[g, slot] = lax.dot_general(k_tile, qx_ref[g], NT_DIMS,
                                             preferred_element_type=F32)

    def accumulate(t, slot):
        s0 = pl.multiple_of(t * TK, TK)
        for g in range(ATTN_KV_HEADS):
            s = s_ref[g, slot]
            m_prev = m_ref[g]
            m_new = jnp.maximum(m_prev, jnp.max(s, axis=0, keepdims=True))
            p = jnp.exp2(s - m_new)
            acc_ref[g] = jnp.exp2(m_prev - m_new) * acc_ref[g] + jnp.dot(
                vx_ref[g, :, pl.ds(s0, TK)], p.astype(BF16), preferred_element_type=F32)
            m_ref[g] = m_new

    scores(0, 0)

    def attn_pair(u, carry):
        scores(2 * u + 1, 1)
        accumulate(2 * u, 0)
        scores(2 * u + 2, 0)
        accumulate(2 * u + 1, 1)
        return carry

    lax.fori_loop(0, nt // 2, attn_pair, 0)

    @pl.when(nt < ntp)
    def _():
        accumulate(nt - 1, 0)

    for g in range(ATTN_KV_HEADS):
        acc = acc_ref[g]
        o_t = acc[0:HEAD_DIM, :] * (1.0 / acc[HEAD_DIM:HEAD_DIM + 1, :])
        if wide:
            for r in range(0, ATTN_REP, 2):
                hq = g * ATTN_REP + r
                pair = jnp.concatenate([o_t[:, r * nq:(r + 1) * nq],
                                        o_t[:, (r + 1) * nq:(r + 2) * nq]], axis=0)
                o_ref[0, :, hq * HEAD_DIM:(hq + 2) * HEAD_DIM] = pair.T.astype(BF16)
        else:
            o_rq = jnp.concatenate([o_t, o_t], axis=0).T[:, 0:HEAD_DIM].astype(BF16)
            for r in range(ATTN_REP):
                hq = g * ATTN_REP + r
                o_ref[0, :, hq * HEAD_DIM:(hq + 1) * HEAD_DIM] = o_rq[r * nq:(r + 1) * nq, :]


def _dsa(q, qi, w, kidx_all, k_all, v_all, past_len, l_valid, t):
    b = kidx_all.shape[0]
    lp = kidx_all.shape[1]
    nq = Q_BLOCK if t % Q_BLOCK == 0 else SMALL_Q_BLOCK
    assert t % nq == 0 and lp % (2 * KEY_TILE) == 0 and lp < 1 << KEY_INDEX_BITS
    topk = min(TOPK_MAX, l_valid // 4)
    width = ATTN_REP * nq
    if nq == Q_BLOCK:
        qblk = lambda c: pl.BlockSpec((1, nq, c), lambda i, j: (i, j, 0))
        q_specs = [qblk(ATTN_WIDTH), qblk(IDX_HEADS * IDX_DIM), qblk(LANE)]
    else:
        assert t == nq
        q_specs = [pl.BlockSpec((1, ATTN_KV_HEADS, width, HEAD_DIM), lambda i, j: (i, 0, 0, 0)),
                   pl.BlockSpec((1, IDX_HEADS * nq, IDX_DIM), lambda i, j: (i, 0, 0)),
                   pl.BlockSpec((1, 1, LANE), lambda i, j: (i, 0, 0))]
    slot_onehot = jnp.asarray(
        (np.arange(width)[:, None] % nq == np.arange(LANE)[None, :]).astype(np.float32), BF16)
    per_stream = lambda c: pl.BlockSpec((1, lp, c), lambda i, j: (i, 0, 0))
    return pl.pallas_call(
        functools.partial(_dsa_kernel, past_len=past_len, l_valid=l_valid, topk=topk, nq=nq),
        out_shape=jax.ShapeDtypeStruct((b, t, ATTN_WIDTH), BF16),
        grid=(b, t // nq),
        in_specs=q_specs + [pl.BlockSpec((width, LANE), lambda i, j: (0, 0)),
                            per_stream(IDX_DIM), per_stream(LANE), per_stream(LANE)],
        out_specs=pl.BlockSpec((1, nq, ATTN_WIDTH), lambda i, j: (i, j, 0)),
        scratch_shapes=[pltpu.VMEM((width, IDX_DIM), BF16),
                        pltpu.VMEM((ATTN_KV_HEADS, width, 2 * LANE), BF16),
                        pltpu.VMEM((lp, LANE), I32),
                        pltpu.VMEM((lp, 2 * LANE), BF16),
                        pltpu.VMEM((ATTN_KV_HEADS, PV_ROWS, lp), BF16),
                        pltpu.VMEM((ATTN_KV_HEADS, 1, width), F32),
                        pltpu.VMEM((ATTN_KV_HEADS, PV_ROWS, width), F32),
                        pltpu.VMEM((1, LANE), I32),
                        pltpu.VMEM((ATTN_KV_HEADS, 2, KEY_TILE, width), F32)],
        compiler_params=pltpu.CompilerParams(
            dimension_semantics=("arbitrary", "arbitrary"), vmem_limit_bytes=VMEM_LIMIT),
        name="dsa",
    )(q, qi, w, slot_onehot, kidx_all, k_all, v_all)


def _merge_kernel(x_ref, yn_ref, o_ref, gate_ref, wa_ref, wb_ref, wo_ref, out_ref):
    br_a = jnp.dot(yn_ref[...], wa_ref[...], preferred_element_type=F32)
    br_b = jnp.dot(o_ref[...], wb_ref[...], preferred_element_type=F32)
    gates = gate_ref[...].astype(F32)
    merged = _sigmoid(gates[:, :D_MODEL]) * br_a + _sigmoid(gates[:, D_MODEL:]) * br_b
    out_ref[...] = x_ref[...] + jnp.dot(merged.astype(BF16), wo_ref[...],
                                        preferred_element_type=F32)


def _merge(x2d, yn, o, gate, w_ssm_out, w_attn_out, w_o, tm):
    n = x2d.shape[0]
    assert n % tm == 0
    row = lambda w: pl.BlockSpec((tm, w), lambda i: (i, 0))
    full = lambda r, c: pl.BlockSpec((r, c), lambda i: (0, 0))
    return pl.pallas_call(
        _merge_kernel,
        out_shape=jax.ShapeDtypeStruct((n, D_MODEL), F32),
        grid=(n // tm,),
        in_specs=[row(D_MODEL), row(SSM_INNER), row(ATTN_WIDTH), row(2 * D_MODEL),
                  full(SSM_INNER, D_MODEL), full(ATTN_WIDTH, D_MODEL), full(D_MODEL, D_MODEL)],
        out_specs=row(D_MODEL),
        compiler_params=pltpu.CompilerParams(
            dimension_semantics=("arbitrary",), vmem_limit_bytes=VMEM_LIMIT),
        name="merge",
    )(x2d, yn, o, gate, w_ssm_out.astype(BF16), w_attn_out.astype(BF16), w_o.astype(BF16))


def _moe_kernel(x_ref, ng_ref, wr_ref, br_ref, w1_ref, w3_ref, w2_ref, fg_ref, tri_ref, out_ref,
                hs_ref, combs_ref, acc_ref, pt_ref, seg_ref, *, n_steps, blk_rows):
    e_step = pl.program_id(1)
    tm = x_ref.shape[0]
    sr = hs_ref.shape[0]
    BLK = blk_rows

    @pl.when(e_step == 0)
    def _():
        x = x_ref[...]
        ms = jnp.mean(x * x, axis=-1, keepdims=True)
        h = ((x * lax.rsqrt(ms + EPS)) * ng_ref[...]).astype(BF16)
        logits = jnp.dot(h, wr_ref[...], preferred_element_type=F32) + br_ref[...]
        lane_i = lax.broadcasted_iota(I32, (tm, LANE), 1)
        lane = lane_i.astype(F32)
        first = lambda hit: jnp.min(jnp.where(hit, lane, float(LANE)), axis=1, keepdims=True)
        is_g = (lane_i >= N_EXPERTS) & (lane_i < N_EXPERTS + MOE_GROUPS)
        glog = jnp.where(is_g, logits, -jnp.inf)
        gmax = jnp.max(glog, axis=1, keepdims=True)
        gsel = first(glog == gmax) - float(N_EXPERTS)
        p_group = 1.0 / jnp.sum(jnp.exp(glog - gmax), axis=1, keepdims=True)
        grp = (lane_i >> EXPERT_GROUP_SHIFT).astype(F32)
        in_grp = (lane_i < N_EXPERTS) & (grp == gsel)
        el = jnp.where(in_grp, logits, -jnp.inf)
        v1 = jnp.max(el, axis=1, keepdims=True)
        i1 = first(el == v1)
        el2 = jnp.where(lane == i1, -jnp.inf, el)
        v2 = jnp.max(el2, axis=1, keepdims=True)
        i2 = first(el2 == v2)
        e21 = jnp.exp(v2 - v1)
        den = 1.0 + e21
        comb = (jnp.where(lane == i1, (1.0 / den) * p_group, 0.0)
                + jnp.where(lane == i2, (e21 / den) * p_group, 0.0))

        own = lane == gsel
        own_f = jnp.where(own, 1.0, 0.0)
        before = jnp.dot(tri_ref[...], own_f.astype(BF16), preferred_element_type=F32)
        rank = jnp.sum(jnp.where(own, before, 0.0), axis=1, keepdims=True)
        counts = jnp.sum(own_f, axis=0, keepdims=True)
        blocks = jnp.floor((counts + (BLK - 1)) * (1.0 / BLK))
        upper = (lax.broadcasted_iota(I32, (LANE, LANE), 0)
                 < lax.broadcasted_iota(I32, (LANE, LANE), 1))
        starts = jnp.dot(jnp.broadcast_to(blocks, (8, LANE)).astype(BF16),
                         jnp.where(upper, 1.0, 0.0).astype(BF16),
                         preferred_element_type=F32)[0:1, :] * BLK
        dest = jnp.sum(jnp.where(own, starts, 0.0), axis=1, keepdims=True) + rank
        for g in range(MOE_GROUPS):
            seg_ref[g] = jnp.sum(starts[:, g:g + 1]).astype(I32)
            seg_ref[MOE_GROUPS + g] = jnp.sum(blocks[:, g:g + 1]).astype(I32)

        dest_i = dest.astype(I32)
        pt = jnp.where(dest_i == lax.broadcasted_iota(I32, (tm, sr), 1), 1.0, 0.0).astype(BF16)
        pt_ref[...] = pt
        dest_row = jnp.broadcast_to(dest, (tm, LANE)).T[0:1, :].astype(I32)
        p_mat = jnp.where(lax.broadcasted_iota(I32, (sr, tm), 0) == dest_row, 1.0, 0.0
                          ).astype(BF16)
        hs_ref[...] = jnp.dot(p_mat, h, preferred_element_type=F32).astype(BF16)
        c3 = jnp.dot(p_mat, _split3(comb), preferred_element_type=F32)
        combs_ref[...] = c3[:, 0:LANE] + c3[:, LANE:2 * LANE] + c3[:, 2 * LANE:]
        acc_ref[...] = jnp.zeros(acc_ref.shape, F32)

    group = e_step // (EXPERTS_PER_GROUP // EXPERTS_PER_STEP)
    seg_start = seg_ref[group]

    def expert_rows(first_blk, n_rows):
        rows = pl.ds(pl.multiple_of(seg_start + first_blk * BLK, BLK), n_rows)
        h = hs_ref[rows, :]
        comb = combs_ref[rows, :]
        lane = lax.broadcasted_iota(I32, (n_rows, LANE), 1)
        for k in range(EXPERTS_PER_STEP):
            e = e_step * EXPERTS_PER_STEP + k
            a1 = jnp.dot(h, w1_ref[k], preferred_element_type=F32)
            a3 = jnp.dot(h, w3_ref[k], preferred_element_type=F32)
            he = (_silu(a1) * a3).astype(BF16)
            ye = jnp.dot(he, w2_ref[k], preferred_element_type=F32)
            wcol = jnp.sum(jnp.where(lane == e, comb, 0.0), axis=1, keepdims=True)
            acc_ref[rows, :] += wcol * ye

    n_blocks = seg_ref[MOE_GROUPS + group]
    odd_tail = (n_blocks % 2 == 1) & (n_blocks >= 3)
    n_pairs = jnp.where(odd_tail, (n_blocks - 3) // 2, n_blocks // 2)

    def block_pair(u, carry):
        expert_rows(2 * u, 2 * BLK)
        return carry

    lax.fori_loop(0, n_pairs, block_pair, 0)

    @pl.when(odd_tail)
    def _():
        expert_rows(n_blocks - 3, 3 * BLK)

    @pl.when(n_blocks == 1)
    def _():
        expert_rows(0, BLK)

    @pl.when(e_step == n_steps - 1)
    def _():
        y = x_ref[...] + jnp.dot(pt_ref[...], acc_ref[...].astype(BF16),
                                 preferred_element_type=F32)
        ms = jnp.mean(y * y, axis=-1, keepdims=True)
        out_ref[...] = (y * lax.rsqrt(ms + EPS)) * fg_ref[...]


def _moe(x2d, norm_g, w_router_group, b_router_group, w_router_expert, b_router_expert,
         w1, w3, w2, norm_final_g, tm):
    n = x2d.shape[0]
    blk_rows = min(MOE_ROW_BLOCK, tm)
    assert n % tm == 0 and tm % blk_rows == 0
    n_steps = N_EXPERTS // EXPERTS_PER_STEP
    sr = tm + (MOE_GROUPS - 1) * blk_rows
    padw = LANE - N_EXPERTS - MOE_GROUPS
    wr = jnp.concatenate([w_router_expert, w_router_group,
                          jnp.zeros((D_MODEL, padw), F32)], axis=1).astype(BF16)
    br = jnp.concatenate([b_router_expert, b_router_group,
                          jnp.zeros((padw,), F32)]).astype(F32).reshape(1, LANE)
    tri = jnp.asarray(np.tril(np.ones((tm, tm), np.float32), -1), BF16)
    row = pl.BlockSpec((tm, D_MODEL), lambda i, e: (i, 0))
    const = lambda r, c: pl.BlockSpec((r, c), lambda i, e: (0, 0))
    return pl.pallas_call(
        functools.partial(_moe_kernel, n_steps=n_steps, blk_rows=blk_rows),
        out_shape=jax.ShapeDtypeStruct((n, D_MODEL), F32),
        grid=(n // tm, n_steps),
        in_specs=[row, const(1, D_MODEL), const(D_MODEL, LANE), const(1, LANE),
                  pl.BlockSpec((EXPERTS_PER_STEP, D_MODEL, EXPERT_FF), lambda i, e: (e, 0, 0)),
                  pl.BlockSpec((EXPERTS_PER_STEP, D_MODEL, EXPERT_FF), lambda i, e: (e, 0, 0)),
                  pl.BlockSpec((EXPERTS_PER_STEP, EXPERT_FF, D_MODEL), lambda i, e: (e, 0, 0)),
                  const(1, D_MODEL), const(tm, tm)],
        out_specs=row,
        scratch_shapes=[pltpu.VMEM((sr, D_MODEL), BF16), pltpu.VMEM((sr, LANE), F32),
                        pltpu.VMEM((sr, D_MODEL), F32), pltpu.VMEM((tm, sr), BF16),
                        pltpu.SMEM((2 * MOE_GROUPS,), I32)],
        compiler_params=pltpu.CompilerParams(
            dimension_semantics=("arbitrary", "arbitrary"), vmem_limit_bytes=VMEM_LIMIT),
        name="moe",
    )(x2d, norm_g.reshape(1, D_MODEL), wr, br, w1.astype(BF16), w3.astype(BF16),
      w2.astype(BF16), norm_final_g.reshape(1, D_MODEL), tri)


def _token_tile(n):
    for tm in (1024, 512, 256, 128, 64, 32, 16, 8):
        if n % tm == 0:
            return tm
    raise ValueError(f"token count {n} is not a multiple of 8")


def _pad_axis(a, axis, size):
    if a.shape[axis] == size:
        return a
    widths = [(0, 0)] * a.ndim
    widths[axis] = (0, size - a.shape[axis])
    return jnp.pad(a, widths)


def _round_up(n, m):
    return (n + m - 1) // m * m


def _stream_step(x, conv_state, ssm_state, k_past, v_past, kidx_past, w_packed, p):
    b, t, d = x.shape
    n = b * t
    tm = _token_tile(n)
    x2d = x.reshape(n, d)
    z, xbc, q, gate, k, v, qi, ki, dtw, k_b, v_b, ki_b = _in_proj(
        x2d, p["norm_mix_g"], w_packed, min(tm, PROJ_ROWS))

    yn, conv_new, ssm_new = _ssd(
        xbc.reshape(b, t, CONV_DIM), z.reshape(b, t, SSM_INNER), dtw.reshape(b, t, LANE),
        conv_state, ssm_state, p["conv_w"], p["conv_b"], p["dt_bias"], p["A_log"],
        p["D_skip"], p["ssm_norm_g"])

    kv_w = ATTN_KV_HEADS * HEAD_DIM
    k_all, v_all, kidx_all = (k_b.reshape(b, t, kv_w), v_b.reshape(b, t, kv_w),
                              ki_b.reshape(b, t, IDX_DIM))
    past_len = 0
    if k_past is not None:
        past_len = k_past.shape[1]
        k_all = jnp.concatenate([k_past.astype(BF16).reshape(b, past_len, kv_w), k_all], axis=1)
        v_all = jnp.concatenate([v_past.astype(BF16).reshape(b, past_len, kv_w), v_all], axis=1)
        kidx_all = jnp.concatenate([kidx_past.astype(BF16), kidx_all], axis=1)
    l_valid = past_len + t
    lp = _round_up(l_valid, 2 * KEY_TILE)
    if t % Q_BLOCK == 0:
        q_in = (q.reshape(b, t, ATTN_WIDTH), qi.reshape(b, t, IDX_HEADS * IDX_DIM),
                dtw.reshape(b, t, LANE))
    else:
        q_in = (q.reshape(b, t, ATTN_KV_HEADS, ATTN_REP, HEAD_DIM).transpose(0, 2, 3, 1, 4)
                .reshape(b, ATTN_KV_HEADS, ATTN_REP * t, HEAD_DIM),
                qi.reshape(b, t, IDX_HEADS, IDX_DIM).transpose(0, 2, 1, 3)
                .reshape(b, IDX_HEADS * t, IDX_DIM),
                dtw.reshape(b, t, LANE)[:, :, SSM_HEADS:SSM_HEADS + IDX_HEADS]
                .transpose(0, 2, 1).reshape(b, 1, IDX_HEADS * t))
    o = _dsa(*q_in, _pad_axis(kidx_all, 1, lp), _pad_axis(k_all, 1, lp), _pad_axis(v_all, 1, lp),
             past_len, l_valid, t)
    o = o.reshape(n, ATTN_WIDTH)

    x1 = _merge(x2d, yn.reshape(n, SSM_INNER), o, gate, p["w_ssm_out"], p["w_attn_out"],
                p["w_o"], min(tm, MERGE_TILE))
    y = _moe(x1, p["norm_moe_g"], p["w_router_group"], p["b_router_group"],
             p["w_router_expert"], p["b_router_expert"], p["w1"], p["w3"], p["w2"],
             p["norm_final_g"], tm)
    return (y.reshape(b, t, d), k.reshape(b, t, ATTN_KV_HEADS, HEAD_DIM),
            v.reshape(b, t, ATTN_KV_HEADS, HEAD_DIM), ki.reshape(b, t, IDX_DIM), conv_new, ssm_new)


def kernel(x_prompt, x_sample, cache_k, cache_v, cache_kidx, state_conv, state_ssm, norm_mix_g, w_in, conv_w, conv_b, dt_bias, A_log, D_skip, ssm_norm_g, w_ssm_out, w_attn_out, w_o, norm_moe_g, w_router_group, b_router_group, w_router_expert, b_router_expert, w1, w3, w2, norm_final_g):
    p = dict(norm_mix_g=norm_mix_g, conv_w=conv_w, conv_b=conv_b, dt_bias=dt_bias, A_log=A_log,
             D_skip=D_skip, ssm_norm_g=ssm_norm_g, w_ssm_out=w_ssm_out, w_attn_out=w_attn_out,
             w_o=w_o, norm_moe_g=norm_moe_g, w_router_group=w_router_group,
             b_router_group=b_router_group, w_router_expert=w_router_expert,
             b_router_expert=b_router_expert, w1=w1, w3=w3, w2=w2, norm_final_g=norm_final_g)
    w_packed = _pack_w_in(w_in)
    yp, kp, vp, kip, cp, sp = _stream_step(x_prompt, None, None, None, None, None, w_packed, p)
    ys, ksn, vsn, kisn, csn, ssn = _stream_step(x_sample, state_conv, state_ssm, cache_k,
                                                cache_v, cache_kidx, w_packed, p)
    return (yp, ys, kp, vp, kip, cp, sp, ksn, vsn, kisn, csn, ssn)
```

```python
import functools

import numpy as np
import jax
import jax.numpy as jnp
from jax import lax
from jax.experimental import pallas as pl
from jax.experimental.pallas import tpu as pltpu

F32 = jnp.float32
BF16 = jnp.bfloat16
I32 = jnp.int32

D_MODEL = 1024
CHUNK = 64
CHUNK_SHIFT = 6
EPS = 1e-6
SSM_INNER = 2048
SSM_HEADDIM = 64
SSM_HEADS = 32
SSM_GROUPS = 4
SSM_HEADS_PER_GROUP = SSM_HEADS // SSM_GROUPS
SSM_STATE = 128
CONV_WIDTH = 4
CONV_DIM = SSM_INNER + 2 * SSM_GROUPS * SSM_STATE
CONV_CARRY = 16
SSD_CHUNKS_PER_STEP = 8
CONV_BLOCK = 128
ATTN_HEADS = 16
ATTN_KV_HEADS = 2
HEAD_DIM = 64
ATTN_REP = ATTN_HEADS // ATTN_KV_HEADS
ATTN_WIDTH = ATTN_HEADS * HEAD_DIM
IDX_HEADS = 8
IDX_DIM = 64
IDX_SCALE = (IDX_HEADS * IDX_DIM) ** -0.5
TOPK_MAX = 256
MOE_GROUPS = 4
EXPERTS_PER_GROUP = 8
EXPERT_GROUP_SHIFT = 3
N_EXPERTS = 32
EXPERT_FF = 256
N_BRANCHES = 2

LANE = 128
VMEM_LIMIT = 56 * 1024 * 1024
PROJ_TILE = 1024
PROJ_ROWS = 512
PROJ_TILES = 9
Q_BLOCK = 128
SMALL_Q_BLOCK = 16
KEY_TILE = 256
KEY_INDEX_BITS = 15
COUNT_CHAINS = 4
EXPERTS_PER_STEP = 4
MOE_ROW_BLOCK = 128
MERGE_TILE = 512
INT_MIN = np.int32(-2 ** 31)
NEG_BIG = -1e30
Q_SCALE = HEAD_DIM ** -0.5 * float(np.log2(np.e))
PV_ROWS = 80

NT_DIMS = (((1,), (1,)), ((), ()))
TN_DIMS = (((0,), (0,)), ((), ()))


def _sigmoid(x):
    return 1.0 / (1.0 + jnp.exp(-x))


def _silu(x):
    h = 0.5 * x
    return h + h * jnp.tanh(h)


def _split3(x):
    hi = x.astype(BF16)
    r1 = x - hi.astype(F32)
    mid = r1.astype(BF16)
    r2 = r1 - mid.astype(F32)
    lo = r2.astype(BF16)
    return jnp.concatenate([hi, mid, lo], axis=1)


def _inproj_kernel(x_ref, g_ref, w_ref, z_ref, xbc_ref, q_ref, gate_ref, k_ref, v_ref,
                   qi_ref, ki_ref, dtw_ref, kb_ref, vb_ref, kib_ref):
    x = x_ref[...]
    ms = jnp.mean(x * x, axis=-1, keepdims=True)
    h = ((x * lax.rsqrt(ms + EPS)) * g_ref[...]).astype(BF16)

    def cols(first, width):
        lo = first * PROJ_TILE
        return jnp.dot(h, w_ref[:, lo:lo + width], preferred_element_type=F32)

    for j in range(2):
        z_ref[:, j * PROJ_TILE:(j + 1) * PROJ_TILE] = cols(j, PROJ_TILE).astype(BF16)
    for j in range(3):
        xbc_ref[:, j * PROJ_TILE:(j + 1) * PROJ_TILE] = cols(2 + j, PROJ_TILE).astype(BF16)
    q_ref[...] = (cols(5, PROJ_TILE) * Q_SCALE).astype(BF16)
    for j in range(2):
        gate_ref[:, j * PROJ_TILE:(j + 1) * PROJ_TILE] = cols(6 + j, PROJ_TILE).astype(BF16)
    t = cols(8, PROJ_TILE)
    k_ref[...] = t[:, 0:128]
    v_ref[...] = t[:, 128:256]
    qi_ref[...] = t[:, 256:768].astype(BF16)
    ki_ref[...] = t[:, 768:832]
    dtw_ref[...] = t[:, 896:1024]
    kb_ref[...] = t[:, 0:128].astype(BF16)
    vb_ref[...] = t[:, 128:256].astype(BF16)
    kib_ref[...] = t[:, 768:832].astype(BF16)


def _pack_w_in(w_in):
    sizes = (SSM_INNER, CONV_DIM, SSM_HEADS, ATTN_WIDTH, ATTN_KV_HEADS * HEAD_DIM,
             ATTN_KV_HEADS * HEAD_DIM, IDX_HEADS * IDX_DIM, IDX_DIM, IDX_HEADS,
             N_BRANCHES * D_MODEL)
    offs = np.concatenate([[0], np.cumsum(sizes)])
    z, xbc, dt, q, k, v, qi, ki, wi, g = [w_in[:, offs[i]:offs[i + 1]] for i in range(10)]
    zeros = lambda n: jnp.zeros((D_MODEL, n), w_in.dtype)
    cols = [z, xbc, q, g, k, v, qi, ki, zeros(64), dt, wi, zeros(LANE - SSM_HEADS - IDX_HEADS)]
    return jnp.concatenate(cols, axis=1).astype(BF16)


def _in_proj(x2d, norm_g, w_packed, tm):
    n = x2d.shape[0]
    assert n % tm == 0
    bf = lambda c: jax.ShapeDtypeStruct((n, c), BF16)
    ff = lambda c: jax.ShapeDtypeStruct((n, c), F32)
    rows = lambda c: pl.BlockSpec((tm, c), lambda i: (i, 0))
    widths = (SSM_INNER, CONV_DIM, ATTN_WIDTH, 2 * D_MODEL, 128, 128, 512, IDX_DIM, 128,
              128, 128, IDX_DIM)
    return pl.pallas_call(
        _inproj_kernel,
        out_shape=(bf(SSM_INNER), bf(CONV_DIM), bf(ATTN_WIDTH), bf(2 * D_MODEL),
                   ff(128), ff(128), bf(512), ff(IDX_DIM), ff(128), bf(128), bf(128), bf(IDX_DIM)),
        grid=(n // tm,),
        in_specs=[
            rows(D_MODEL),
            pl.BlockSpec((1, D_MODEL), lambda i: (0, 0)),
            pl.BlockSpec((D_MODEL, PROJ_TILES * PROJ_TILE), lambda i: (0, 0),
                         pipeline_mode=pl.Buffered(1)),
        ],
        out_specs=tuple(rows(c) for c in widths),
        compiler_params=pltpu.CompilerParams(
            dimension_semantics=("arbitrary",), vmem_limit_bytes=VMEM_LIMIT),
        name="in_proj",
    )(x2d, norm_g.reshape(1, D_MODEL), w_packed)


def _ssd_kernel(*refs, L, cps, nsteps, has_state, first_shift):
    if has_state:
        (xbc_ref, z_ref, dtw_ref, cs_ref, s0_ref, shift_ref, convw_ref, convb_ref, dtb_ref, a_ref,
         dx_ref, ng_ref, e3p_ref, e3l_ref, tri_ref, bd_ref, yn_ref, cnew_ref, sout_ref,
         xe_ref, st, act_ref, ex_ref, yz_ref, acl_ref) = refs
    else:
        (xbc_ref, z_ref, dtw_ref, shift_ref, convw_ref, convb_ref, dtb_ref, a_ref,
         dx_ref, ng_ref, e3p_ref, e3l_ref, tri_ref, bd_ref, yn_ref, cnew_ref, sout_ref,
         xe_ref, st, act_ref, ex_ref, yz_ref, acl_ref) = refs
    c = pl.program_id(1)
    G, R, N, P = SSM_GROUPS, SSM_HEADS_PER_GROUP, SSM_STATE, SSM_HEADDIM
    GW = R * P
    RL = R * L
    LB = cps * L

    CB = min(LB, CONV_BLOCK)
    @pl.when(c == 0)
    def _():
        xe_ref[CB:, :] = jnp.zeros((xe_ref.shape[0] - CB, CONV_DIM), BF16)
        if has_state:
            xe_ref[CB:CB + CONV_CARRY, :] = cs_ref[0]
            st[...] = s0_ref[0].T
        else:
            st[...] = jnp.zeros(st.shape, F32)

    w = convw_ref[...]
    for cb in range(LB // CB):
        blk = slice(cb * CB, (cb + 1) * CB)
        xe_ref[0:CB, :] = xbc_ref[0, blk, :]
        which = jnp.where(c == 0, first_shift, 1) if cb == 0 else 1
        shifted = jnp.dot(shift_ref[which], xe_ref[...], preferred_element_type=F32)
        conv = convb_ref[...] + shifted[2 * CB:3 * CB, :] * w[0:1, :]
        conv = conv + shifted[CB:2 * CB, :] * w[1:2, :]
        conv = conv + shifted[0:CB, :] * w[2:3, :]
        conv = conv + xbc_ref[0, blk, :].astype(F32) * w[3:4, :]
        act_ref[blk, :] = _silu(conv)
        if nsteps * (LB // CB) > 1:
            xe_ref[CB:CB + CONV_CARRY, :] = xe_ref[CB - CONV_CARRY:CB, :]

    @pl.when(c == nsteps - 1)
    def _():
        cnew_ref[0] = xbc_ref[0, LB - CONV_CARRY:LB, :].astype(F32)[
            CONV_CARRY - (CONV_WIDTH - 1):CONV_CARRY, :]

    u = dtw_ref[0] + dtb_ref[...]
    dt = jnp.maximum(u, 0.0) + jnp.log1p(jnp.exp(-jnp.abs(u)))
    a = dt * a_ref[...]
    ac3 = jnp.dot(tri_ref[...], _split3(a), preferred_element_type=F32)
    a_cum = ac3[:, 0:LANE] + ac3[:, LANE:2 * LANE] + ac3[:, 2 * LANE:]

    ex_ref[...] = jnp.dot(_split3(jnp.concatenate([a_cum, dt], axis=0)), e3p_ref[...],
                          preferred_element_type=F32)
    if L != P:
        acl_ref[...] = jnp.dot(_split3(a_cum), e3l_ref[...], preferred_element_type=F32)
    row = lax.broadcasted_iota(I32, (L, RL), 0)
    scol = lax.broadcasted_iota(I32, (L, RL), 1) & (L - 1)

    for sub in range(cps):
        r0 = sub * L
        rows = slice(r0, r0 + L)
        ssq = jnp.zeros((L, 1), F32)
        for g in range(G):
            sl = slice(g * GW, (g + 1) * GW)
            xs = act_ref[rows, sl]
            b_g = act_ref[rows, SSM_INNER + g * N:SSM_INNER + (g + 1) * N].astype(BF16)
            c_g = act_ref[rows, SSM_INNER + (G + g) * N:SSM_INNER + (G + g + 1) * N].astype(BF16)
            acx = ex_ref[rows, sl]
            alast = ex_ref[r0 + L - 1:r0 + L, sl]
            xdt = xs * ex_ref[LB + r0:LB + r0 + L, sl]
            acl_g = acx if L == P else acl_ref[rows, g * RL:(g + 1) * RL]
            acs_g = jnp.sum(jnp.where(scol == row, acl_g, 0.0), axis=0, keepdims=True)
            decay = jnp.where(scol <= row, jnp.exp(acl_g - acs_g), 0.0)
            cb = lax.dot_general(c_g, jnp.concatenate([b_g] * R, axis=0), NT_DIMS,
                                 preferred_element_type=F32)
            m_g = (cb * decay).astype(BF16)
            xbd = jnp.concatenate([xdt.astype(BF16)] * R, axis=0) * bd_ref[...]
            y_in = jnp.dot(m_g, xbd, preferred_element_type=F32)
            st_g = st[:, sl]
            y_st = jnp.dot(c_g, st_g.astype(BF16), preferred_element_type=F32)
            xw = (xdt * jnp.exp(alast - acx)).astype(BF16)
            upd = lax.dot_general(b_g, xw, TN_DIMS, preferred_element_type=F32)
            st[:, sl] = st_g * jnp.exp(alast) + upd
            y = y_in + jnp.exp(acx) * y_st + dx_ref[:, sl] * xs
            yz = y * _silu(z_ref[0, rows, sl].astype(F32))
            ssq = ssq + jnp.sum(yz * yz, axis=-1, keepdims=True)
            yz_ref[rows, sl] = yz
        scale = lax.rsqrt(ssq * (1.0 / SSM_INNER) + EPS)
        yn_ref[0, rows, :] = ((yz_ref[rows, :] * scale) * ng_ref[...]).astype(BF16)

    @pl.when(c == nsteps - 1)
    def _():
        sout_ref[0] = st[...].T


def _shift_matrices(L, conv_rows):
    out = np.zeros((2, 3 * L, conv_rows), np.float32)
    for k in range(CONV_WIDTH - 1):
        for t in range(L):
            src = t - (k + 1)
            if src >= 0:
                out[:, k * L + t, src] = 1.0
            else:
                out[1, k * L + t, L + CONV_CARRY + src] = 1.0
                j = CONV_WIDTH - 1 + src
                for piece in range(3):
                    out[0, k * L + t, L + 3 * piece + j] = 1.0
    return jnp.asarray(out, BF16)


def _expand_matrix(width_per_head):
    r = np.arange(3 * LANE)[:, None] % LANE
    c = np.arange(SSM_HEADS * width_per_head)[None, :] // width_per_head
    return jnp.asarray((r == c).astype(np.float32), BF16)


def _ssd(xbc, z, dtw, conv_state, ssm_state, conv_w, conv_b, dt_bias, a_log, d_skip, ssm_norm_g):
    b, t, _ = xbc.shape
    L = min(CHUNK, t)
    nc = t // L
    assert t % L == 0 and L & (L - 1) == 0
    has_state = conv_state is not None
    pad = lambda v: jnp.pad(v.astype(F32), (0, LANE - SSM_HEADS)).reshape(1, LANE)
    a_neg = pad(-jnp.exp(a_log.astype(F32)))
    cps = SSD_CHUNKS_PER_STEP if nc % SSD_CHUNKS_PER_STEP == 0 else 1
    nsteps = nc // cps
    lb = cps * L
    cb_rows = min(lb, CONV_BLOCK)
    conv_rows = _round_up(cb_rows + CONV_CARRY, LANE)
    tri = jnp.asarray(np.kron(np.eye(cps), np.tril(np.ones((L, L)))).astype(np.float32), BF16)
    rl = SSM_HEADS_PER_GROUP * L
    gw = SSM_HEADS_PER_GROUP * SSM_HEADDIM
    bd = jnp.asarray((np.arange(rl)[:, None] // L == np.arange(gw)[None, :] // SSM_HEADDIM)
                     .astype(np.float32), BF16)
    const = lambda shape: pl.BlockSpec(shape, lambda i, j: (0,) * len(shape))
    tok = lambda w: pl.BlockSpec((1, lb, w), lambda i, j: (i, j, 0))
    per_b = lambda s: pl.BlockSpec((1,) + s, lambda i, j: (i, 0, 0))
    ins = [xbc, z, dtw]
    specs = [tok(CONV_DIM), tok(SSM_INNER), tok(LANE)]
    if has_state:
        triples = _split3(conv_state.astype(F32))
        ins += [_pad_axis(triples, 1, CONV_CARRY), ssm_state.reshape(b, SSM_INNER, SSM_STATE)]
        specs += [per_b((CONV_CARRY, CONV_DIM)), per_b((SSM_INNER, SSM_STATE))]
    first = 0 if has_state else 1
    ins += [_shift_matrices(cb_rows, conv_rows)]
    specs += [const((2, 3 * cb_rows, conv_rows))]
    ins += [conv_w, conv_b.reshape(1, CONV_DIM), pad(dt_bias), a_neg,
            jnp.repeat(d_skip.astype(F32), SSM_HEADDIM).reshape(1, SSM_INNER),
            ssm_norm_g.reshape(1, SSM_INNER), _expand_matrix(SSM_HEADDIM), _expand_matrix(L),
            tri, bd]
    specs += [const((CONV_WIDTH, CONV_DIM)), const((1, CONV_DIM)), const((1, LANE)),
              const((1, LANE)), const((1, SSM_INNER)), const((1, SSM_INNER)),
              const((3 * LANE, SSM_INNER)), const((3 * LANE, SSM_HEADS * L)),
              const((lb, lb)), const((rl, gw))]
    yn, cnew, sout = pl.pallas_call(
        functools.partial(_ssd_kernel, L=L, cps=cps, nsteps=nsteps, has_state=has_state,
                          first_shift=first),
        out_shape=(jax.ShapeDtypeStruct((b, t, SSM_INNER), BF16),
                   jax.ShapeDtypeStruct((b, CONV_WIDTH - 1, CONV_DIM), F32),
                   jax.ShapeDtypeStruct((b, SSM_INNER, SSM_STATE), F32)),
        grid=(b, nsteps),
        in_specs=specs,
        out_specs=(tok(SSM_INNER), per_b((CONV_WIDTH - 1, CONV_DIM)),
                   per_b((SSM_INNER, SSM_STATE))),
        scratch_shapes=[pltpu.VMEM((conv_rows, CONV_DIM), BF16),
                        pltpu.VMEM((SSM_STATE, SSM_INNER), F32),
                        pltpu.VMEM((lb, CONV_DIM), F32),
                        pltpu.VMEM((2 * lb, SSM_INNER), F32),
                        pltpu.VMEM((lb, SSM_INNER), F32),
                        pltpu.VMEM((lb, SSM_HEADS * L), F32)],
        compiler_params=pltpu.CompilerParams(
            dimension_semantics=("arbitrary", "arbitrary"), vmem_limit_bytes=VMEM_LIMIT),
        name="ssd",
    )(*ins)
    return yn, cnew, sout.reshape(b, SSM_HEADS, SSM_HEADDIM, SSM_STATE)


def _dsa_kernel(q_ref, qi_ref, w_ref, eq_ref, kidx_ref, k_ref, v_ref, o_ref,
                qs_ref, qx_ref, key_ref, kx_ref, vx_ref, m_ref, acc_ref, mlim_ref, s_ref,
                *, past_len, l_valid, topk, nq):
    qb = pl.program_id(1)
    TQ, TK = LANE, KEY_TILE
    W = ATTN_REP * nq
    wide = nq == LANE

    lane = lax.broadcasted_iota(I32, (1, TQ), 1)
    pos = past_len + qb * nq + (lane & (nq - 1))
    n_adm = jnp.minimum(((pos >> CHUNK_SHIFT) + 1) * CHUNK, l_valid)
    k_eff = jnp.minimum(n_adm, topk)
    last_pos = past_len + qb * nq + nq - 1
    n_max = jnp.minimum(((last_pos >> CHUNK_SHIFT) + 1) * CHUNK, l_valid)
    nt = (n_max + TK - 1) // TK
    ntp = 2 * ((nt + 1) // 2)
    lp = key_ref.shape[0]

    @pl.when(qb == 0)
    def _():
        kx_ref[:, LANE:] = k_ref[0]

        def v_tile(t, carry):
            s0 = pl.multiple_of(t * TK, TK)
            v_t = v_ref[0, pl.ds(s0, TK), :].astype(F32).T
            for g in range(ATTN_KV_HEADS):
                vx_ref[g, 0:HEAD_DIM, pl.ds(s0, TK)] = (
                    v_t[g * HEAD_DIM:(g + 1) * HEAD_DIM, :].astype(BF16))
            return carry

        lax.fori_loop(0, lp // TK, v_tile, 0)
        tail_row = lax.broadcasted_iota(I32, (PV_ROWS - HEAD_DIM, lp), 0)
        for g in range(ATTN_KV_HEADS):
            vx_ref[g, HEAD_DIM:, :] = jnp.where(tail_row == 0, 1.0, 0.0).astype(BF16)
            qx_ref[g, :, 0:LANE] = eq_ref[...]
            qx_ref[g, :, LANE:] = jnp.zeros((W, LANE), BF16)

    if wide:
        for h in range(IDX_HEADS):
            qs_ref[h * nq:(h + 1) * nq, :] = qi_ref[0, :, h * IDX_DIM:(h + 1) * IDX_DIM]
        w_t = w_ref[0].T[SSM_HEADS:SSM_HEADS + IDX_HEADS, :] * IDX_SCALE
    else:
        qs_ref[...] = qi_ref[0]
        w_row = w_ref[0] * IDX_SCALE

    def index_dots(t, slot):
        s0 = pl.multiple_of(jnp.minimum(t, nt - 1) * TK, TK)
        s_ref[0, slot] = lax.dot_general(kidx_ref[0, pl.ds(s0, TK), :], qs_ref[...], NT_DIMS,
                                         preferred_element_type=F32)

    def score_tile(t, slot):
        s0 = pl.multiple_of(t * TK, TK)
        d = s_ref[0, slot]
        if wide:
            sc = w_t[0:1, :] * jnp.maximum(d[:, 0:nq], 0.0)
            for h in range(1, IDX_HEADS):
                sc = sc + w_t[h:h + 1, :] * jnp.maximum(d[:, h * nq:(h + 1) * nq], 0.0)
        else:
            sc = w_row * jnp.maximum(d, 0.0)
            for shift in [nq << i for i in reversed(range(IDX_HEADS.bit_length() - 1))]:
                sc = sc + pltpu.roll(sc, shift, axis=1)
        sc = sc + 0.0
        bits = pltpu.bitcast(sc, I32)
        key = jnp.where(bits < 0, bits ^ np.int32(0x7FFFFFFF), bits)
        s_idx = s0 + lax.broadcasted_iota(I32, (TK, TQ), 0)
        key_ref[pl.ds(s0, TK), :] = jnp.where(s_idx < n_adm, key, INT_MIN)

    index_dots(0, 0)

    def score_pair(u, carry):
        index_dots(2 * u + 1, 1)
        score_tile(2 * u, 0)
        index_dots(2 * u + 2, 0)
        score_tile(2 * u + 1, 1)
        return carry

    lax.fori_loop(0, nt // 2, score_pair, 0)

    @pl.when(nt < ntp)
    def _():
        score_tile(nt - 1, 0)
        key_ref[pl.ds(pl.multiple_of(nt * TK, TK), TK), :] = jnp.full((TK, TQ), INT_MIN, I32)

    def count(pred):
        n_vregs = 2 * TK // 8

        def body(t, accs):
            s0 = pl.multiple_of(t * (2 * TK), 2 * TK)
            blk = key_ref[pl.ds(s0, 2 * TK), :].reshape(n_vregs, 8, TQ)
            s_idx = s0 + lax.broadcasted_iota(I32, (2 * TK, TQ), 0).reshape(n_vregs, 8, TQ)
            accs = list(accs)
            for r in range(n_vregs):
                a = accs[r % COUNT_CHAINS]
                accs[r % COUNT_CHAINS] = jnp.where(pred(blk[r], s_idx[r]), a + 1, a)
            return tuple(accs)

        zero = jnp.zeros((8, TQ), I32)
        accs = lax.fori_loop(0, ntp // 2, body, (zero,) * COUNT_CHAINS)
        acc = accs[0]
        for a in accs[1:]:
            acc = acc + a
        return jnp.sum(acc, axis=0, keepdims=True)

    def bit_step(i, state):
        prefix, n_ge, n_gt = state
        cand = prefix | jnp.left_shift(jnp.int32(1), 31 - i)
        cand_s = cand ^ INT_MIN
        cnt = count(lambda blk, s_idx: blk >= cand_s)
        ok = cnt >= k_eff
        return (jnp.where(ok, cand, prefix), jnp.where(ok, cnt, n_ge),
                jnp.where(ok, n_gt, cnt))

    def search():
        return lax.fori_loop(
            0, 16, lambda i, state: bit_step(2 * i + 1, bit_step(2 * i, state)),
            (jnp.zeros((1, TQ), I32), jnp.full((1, TQ), lp, I32), jnp.zeros((1, TQ), I32)))

    def take_all():
        return jnp.ones((1, TQ), I32), k_eff, k_eff

    prefix, n_ge, n_gt = lax.cond(n_max > topk, search, take_all)
    thr = prefix ^ INT_MIN

    need = k_eff - n_gt
    mlim_ref[...] = jnp.full((1, TQ), 1 << KEY_INDEX_BITS, I32)

    @pl.when(jnp.max(n_ge - k_eff) > 0)
    def _():
        def idx_step(i, prefix):
            cand = prefix | jnp.left_shift(jnp.int32(1), KEY_INDEX_BITS - 1 - i)
            cnt = count(lambda blk, s_idx: (blk == thr) & (s_idx < cand))
            return jnp.where(cnt < need, cand, prefix)
        mlim_ref[...] = lax.fori_loop(0, KEY_INDEX_BITS, idx_step, jnp.zeros((1, TQ), I32))

    mlim = mlim_ref[...]

    def bias_tile(t, carry):
        s0 = pl.multiple_of(t * TK, TK)
        blk = key_ref[pl.ds(s0, TK), :]
        s_idx = s0 + lax.broadcasted_iota(I32, (TK, TQ), 0)
        sel = (blk > thr) | ((blk == thr) & (s_idx <= mlim))
        kx_ref[pl.ds(s0, TK), 0:LANE] = jnp.where(sel, 0.0, NEG_BIG).astype(BF16)
        return carry

    lax.fori_loop(0, nt, bias_tile, 0)

    for g in range(ATTN_KV_HEADS):
        lanes = slice(LANE + g * HEAD_DIM, LANE + (g + 1) * HEAD_DIM)
        if wide:
            for r in range(ATTN_REP):
                hq = g * ATTN_REP + r
                qx_ref[g, r * nq:(r + 1) * nq, lanes] = (
                    q_ref[0, :, hq * HEAD_DIM:(hq + 1) * HEAD_DIM])
        else:
            qx_ref[g, :, lanes] = q_ref[0, g]
    m_ref[...] = jnp.full(m_ref.shape, -jnp.inf, F32)
    acc_ref[...] = jnp.zeros(acc_ref.shape, F32)

    def scores(t, slot):
        s0 = pl.multiple_of(jnp.minimum(t, nt - 1) * TK, TK)
        k_tile = kx_ref[pl.ds(s0, TK), :]
        for g in range(ATTN_KV_HEADS):
            s_ref[g, slot] = lax.dot_general(k_tile, qx_ref[g], NT_DIMS,
                                             preferred_element_type=F32)

    def accumulate(t, slot):
        s0 = pl.multiple_of(t * TK, TK)
        for g in range(ATTN_KV_HEADS):
            s = s_ref[g, slot]
            m_prev = m_ref[g]
            m_new = jnp.maximum(m_prev, jnp.max(s, axis=0, keepdims=True))
            p = jnp.exp2(s - m_new)
            acc_ref[g] = jnp.exp2(m_prev - m_new) * acc_ref[g] + jnp.dot(
                vx_ref[g, :, pl.ds(s0, TK)], p.astype(BF16), preferred_element_type=F32)
            m_ref[g] = m_new

    scores(0, 0)

    def attn_pair(u, carry):
        scores(2 * u + 1, 1)
        accumulate(2 * u, 0)
        scores(2 * u + 2, 0)
        accumulate(2 * u + 1, 1)
        return carry

    lax.fori_loop(0, nt // 2, attn_pair, 0)

    @pl.when(nt < ntp)
    def _():
        accumulate(nt - 1, 0)

    for g in range(ATTN_KV_HEADS):
        acc = acc_ref[g]
        o_t = acc[0:HEAD_DIM, :] * (1.0 / acc[HEAD_DIM:HEAD_DIM + 1, :])
        if wide:
            for r in range(0, ATTN_REP, 2):
                hq = g * ATTN_REP + r
                pair = jnp.concatenate([o_t[:, r * nq:(r + 1) * nq],
                                        o_t[:, (r + 1) * nq:(r + 2) * nq]], axis=0)
                o_ref[0, :, hq * HEAD_DIM:(hq + 2) * HEAD_DIM] = pair.T.astype(BF16)
        else:
            o_rq = jnp.concatenate([o_t, o_t], axis=0).T[:, 0:HEAD_DIM].astype(BF16)
            for r in range(ATTN_REP):
                hq = g * ATTN_REP + r
                o_ref[0, :, hq * HEAD_DIM:(hq + 1) * HEAD_DIM] = o_rq[r * nq:(r + 1) * nq, :]


def _dsa(q, qi, w, kidx_all, k_all, v_all, past_len, l_valid, t):
    b = kidx_all.shape[0]
    lp = kidx_all.shape[1]
    nq = Q_BLOCK if t % Q_BLOCK == 0 else SMALL_Q_BLOCK
    assert t % nq == 0 and lp % (2 * KEY_TILE) == 0 and lp < 1 << KEY_INDEX_BITS
    topk = min(TOPK_MAX, l_valid // 4)
    width = ATTN_REP * nq
    if nq == Q_BLOCK:
        qblk = lambda c: pl.BlockSpec((1, nq, c), lambda i, j: (i, j, 0))
        q_specs = [qblk(ATTN_WIDTH), qblk(IDX_HEADS * IDX_DIM), qblk(LANE)]
    else:
        assert t == nq
        q_specs = [pl.BlockSpec((1, ATTN_KV_HEADS, width, HEAD_DIM), lambda i, j: (i, 0, 0, 0)),
                   pl.BlockSpec((1, IDX_HEADS * nq, IDX_DIM), lambda i, j: (i, 0, 0)),
                   pl.BlockSpec((1, 1, LANE), lambda i, j: (i, 0, 0))]
    slot_onehot = jnp.asarray(
        (np.arange(width)[:, None] % nq == np.arange(LANE)[None, :]).astype(np.float32), BF16)
    per_stream = lambda c: pl.BlockSpec((1, lp, c), lambda i, j: (i, 0, 0))
    return pl.pallas_call(
        functools.partial(_dsa_kernel, past_len=past_len, l_valid=l_valid, topk=topk, nq=nq),
        out_shape=jax.ShapeDtypeStruct((b, t, ATTN_WIDTH), BF16),
        grid=(b, t // nq),
        in_specs=q_specs + [pl.BlockSpec((width, LANE), lambda i, j: (0, 0)),
                            per_stream(IDX_DIM), per_stream(LANE), per_stream(LANE)],
        out_specs=pl.BlockSpec((1, nq, ATTN_WIDTH), lambda i, j: (i, j, 0)),
        scratch_shapes=[pltpu.VMEM((width, IDX_DIM), BF16),
                        pltpu.VMEM((ATTN_KV_HEADS, width, 2 * LANE), BF16),
                        pltpu.VMEM((lp, LANE), I32),
                        pltpu.VMEM((lp, 2 * LANE), BF16),
                        pltpu.VMEM((ATTN_KV_HEADS, PV_ROWS, lp), BF16),
                        pltpu.VMEM((ATTN_KV_HEADS, 1, width), F32),
                        pltpu.VMEM((ATTN_KV_HEADS, PV_ROWS, width), F32),
                        pltpu.VMEM((1, LANE), I32),
                        pltpu.VMEM((ATTN_KV_HEADS, 2, KEY_TILE, width), F32)],
        compiler_params=pltpu.CompilerParams(
            dimension_semantics=("arbitrary", "arbitrary"), vmem_limit_bytes=VMEM_LIMIT),
        name="dsa",
    )(q, qi, w, slot_onehot, kidx_all, k_all, v_all)


def _merge_kernel(x_ref, yn_ref, o_ref, gate_ref, wa_ref, wb_ref, wo_ref, out_ref):
    br_a = jnp.dot(yn_ref[...], wa_ref[...], preferred_element_type=F32)
    br_b = jnp.dot(o_ref[...], wb_ref[...], preferred_element_type=F32)
    gates = gate_ref[...].astype(F32)
    merged = _sigmoid(gates[:, :D_MODEL]) * br_a + _sigmoid(gates[:, D_MODEL:]) * br_b
    out_ref[...] = x_ref[...] + jnp.dot(merged.astype(BF16), wo_ref[...],
                                        preferred_element_type=F32)


def _merge(x2d, yn, o, gate, w_ssm_out, w_attn_out, w_o, tm):
    n = x2d.shape[0]
    assert n % tm == 0
    row = lambda w: pl.BlockSpec((tm, w), lambda i: (i, 0))
    full = lambda r, c: pl.BlockSpec((r, c), lambda i: (0, 0))
    return pl.pallas_call(
        _merge_kernel,
        out_shape=jax.ShapeDtypeStruct((n, D_MODEL), F32),
        grid=(n // tm,),
        in_specs=[row(D_MODEL), row(SSM_INNER), row(ATTN_WIDTH), row(2 * D_MODEL),
                  full(SSM_INNER, D_MODEL), full(ATTN_WIDTH, D_MODEL), full(D_MODEL, D_MODEL)],
        out_specs=row(D_MODEL),
        compiler_params=pltpu.CompilerParams(
            dimension_semantics=("arbitrary",), vmem_limit_bytes=VMEM_LIMIT),
        name="merge",
    )(x2d, yn, o, gate, w_ssm_out.astype(BF16), w_attn_out.astype(BF16), w_o.astype(BF16))


def _moe_kernel(x_ref, ng_ref, wr_ref, br_ref, w1_ref, w3_ref, w2_ref, fg_ref, tri_ref, out_ref,
                hs_ref, combs_ref, acc_ref, pt_ref, seg_ref, *, n_steps, blk_rows):
    e_step = pl.program_id(1)
    tm = x_ref.shape[0]
    sr = hs_ref.shape[0]
    BLK = blk_rows

    @pl.when(e_step == 0)
    def _():
        x = x_ref[...]
        ms = jnp.mean(x * x, axis=-1, keepdims=True)
        h = ((x * lax.rsqrt(ms + EPS)) * ng_ref[...]).astype(BF16)
        logits = jnp.dot(h, wr_ref[...], preferred_element_type=F32) + br_ref[...]
        lane_i = lax.broadcasted_iota(I32, (tm, LANE), 1)
        lane = lane_i.astype(F32)
        first = lambda hit: jnp.min(jnp.where(hit, lane, float(LANE)), axis=1, keepdims=True)
        is_g = (lane_i >= N_EXPERTS) & (lane_i < N_EXPERTS + MOE_GROUPS)
        glog = jnp.where(is_g, logits, -jnp.inf)
        gmax = jnp.max(glog, axis=1, keepdims=True)
        gsel = first(glog == gmax) - float(N_EXPERTS)
        p_group = 1.0 / jnp.sum(jnp.exp(glog - gmax), axis=1, keepdims=True)
        grp = (lane_i >> EXPERT_GROUP_SHIFT).astype(F32)
        in_grp = (lane_i < N_EXPERTS) & (grp == gsel)
        el = jnp.where(in_grp, logits, -jnp.inf)
        v1 = jnp.max(el, axis=1, keepdims=True)
        i1 = first(el == v1)
        el2 = jnp.where(lane == i1, -jnp.inf, el)
        v2 = jnp.max(el2, axis=1, keepdims=True)
        i2 = first(el2 == v2)
        e21 = jnp.exp(v2 - v1)
        den = 1.0 + e21
        comb = (jnp.where(lane == i1, (1.0 / den) * p_group, 0.0)
                + jnp.where(lane == i2, (e21 / den) * p_group, 0.0))

        own = lane == gsel
        own_f = jnp.where(own, 1.0, 0.0)
        before = jnp.dot(tri_ref[...], own_f.astype(BF16), preferred_element_type=F32)
        rank = jnp.sum(jnp.where(own, before, 0.0), axis=1, keepdims=True)
        counts = jnp.sum(own_f, axis=0, keepdims=True)
        blocks = jnp.floor((counts + (BLK - 1)) * (1.0 / BLK))
        upper = (lax.broadcasted_iota(I32, (LANE, LANE), 0)
                 < lax.broadcasted_iota(I32, (LANE, LANE), 1))
        starts = jnp.dot(jnp.broadcast_to(blocks, (8, LANE)).astype(BF16),
                         jnp.where(upper, 1.0, 0.0).astype(BF16),
                         preferred_element_type=F32)[0:1, :] * BLK
        dest = jnp.sum(jnp.where(own, starts, 0.0), axis=1, keepdims=True) + rank
        for g in range(MOE_GROUPS):
            seg_ref[g] = jnp.sum(starts[:, g:g + 1]).astype(I32)
            seg_ref[MOE_GROUPS + g] = jnp.sum(blocks[:, g:g + 1]).astype(I32)

        dest_i = dest.astype(I32)
        pt = jnp.where(dest_i == lax.broadcasted_iota(I32, (tm, sr), 1), 1.0, 0.0).astype(BF16)
        pt_ref[...] = pt
        dest_row = jnp.broadcast_to(dest, (tm, LANE)).T[0:1, :].astype(I32)
        p_mat = jnp.where(lax.broadcasted_iota(I32, (sr, tm), 0) == dest_row, 1.0, 0.0
                          ).astype(BF16)
        hs_ref[...] = jnp.dot(p_mat, h, preferred_element_type=F32).astype(BF16)
        c3 = jnp.dot(p_mat, _split3(comb), preferred_element_type=F32)
        combs_ref[...] = c3[:, 0:LANE] + c3[:, LANE:2 * LANE] + c3[:, 2 * LANE:]
        acc_ref[...] = jnp.zeros(acc_ref.shape, F32)

    group = e_step // (EXPERTS_PER_GROUP // EXPERTS_PER_STEP)
    seg_start = seg_ref[group]

    def expert_rows(first_blk, n_rows):
        rows = pl.ds(pl.multiple_of(seg_start + first_blk * BLK, BLK), n_rows)
        h = hs_ref[rows, :]
        comb = combs_ref[rows, :]
        lane = lax.broadcasted_iota(I32, (n_rows, LANE), 1)
        for k in range(EXPERTS_PER_STEP):
            e = e_step * EXPERTS_PER_STEP + k
            a1 = jnp.dot(h, w1_ref[k], preferred_element_type=F32)
            a3 = jnp.dot(h, w3_ref[k], preferred_element_type=F32)
            he = (_silu(a1) * a3).astype(BF16)
            ye = jnp.dot(he, w2_ref[k], preferred_element_type=F32)
            wcol = jnp.sum(jnp.where(lane == e, comb, 0.0), axis=1, keepdims=True)
            acc_ref[rows, :] += wcol * ye

    n_blocks = seg_ref[MOE_GROUPS + group]
    odd_tail = (n_blocks % 2 == 1) & (n_blocks >= 3)
    n_pairs = jnp.where(odd_tail, (n_blocks - 3) // 2, n_blocks // 2)

    def block_pair(u, carry):
        expert_rows(2 * u, 2 * BLK)
        return carry

    lax.fori_loop(0, n_pairs, block_pair, 0)

    @pl.when(odd_tail)
    def _():
        expert_rows(n_blocks - 3, 3 * BLK)

    @pl.when(n_blocks == 1)
    def _():
        expert_rows(0, BLK)

    @pl.when(e_step == n_steps - 1)
    def _():
        y = x_ref[...] + jnp.dot(pt_ref[...], acc_ref[...].astype(BF16),
                                 preferred_element_type=F32)
        ms = jnp.mean(y * y, axis=-1, keepdims=True)
        out_ref[...] = (y * lax.rsqrt(ms + EPS)) * fg_ref[...]


def _moe(x2d, norm_g, w_router_group, b_router_group, w_router_expert, b_router_expert,
         w1, w3, w2, norm_final_g, tm):
    n = x2d.shape[0]
    blk_rows = min(MOE_ROW_BLOCK, tm)
    assert n % tm == 0 and tm % blk_rows == 0
    n_steps = N_EXPERTS // EXPERTS_PER_STEP
    sr = tm + (MOE_GROUPS - 1) * blk_rows
    padw = LANE - N_EXPERTS - MOE_GROUPS
    wr = jnp.concatenate([w_router_expert, w_router_group,
                          jnp.zeros((D_MODEL, padw), F32)], axis=1).astype(BF16)
    br = jnp.concatenate([b_router_expert, b_router_group,
                          jnp.zeros((padw,), F32)]).astype(F32).reshape(1, LANE)
    tri = jnp.asarray(np.tril(np.ones((tm, tm), np.float32), -1), BF16)
    row = pl.BlockSpec((tm, D_MODEL), lambda i, e: (i, 0))
    const = lambda r, c: pl.BlockSpec((r, c), lambda i, e: (0, 0))
    return pl.pallas_call(
        functools.partial(_moe_kernel, n_steps=n_steps, blk_rows=blk_rows),
        out_shape=jax.ShapeDtypeStruct((n, D_MODEL), F32),
        grid=(n // tm, n_steps),
        in_specs=[row, const(1, D_MODEL), const(D_MODEL, LANE), const(1, LANE),
                  pl.BlockSpec((EXPERTS_PER_STEP, D_MODEL, EXPERT_FF), lambda i, e: (e, 0, 0)),
                  pl.BlockSpec((EXPERTS_PER_STEP, D_MODEL, EXPERT_FF), lambda i, e: (e, 0, 0)),
                  pl.BlockSpec((EXPERTS_PER_STEP, EXPERT_FF, D_MODEL), lambda i, e: (e, 0, 0)),
                  const(1, D_MODEL), const(tm, tm)],
        out_specs=row,
        scratch_shapes=[pltpu.VMEM((sr, D_MODEL), BF16), pltpu.VMEM((sr, LANE), F32),
                        pltpu.VMEM((sr, D_MODEL), F32), pltpu.VMEM((tm, sr), BF16),
                        pltpu.SMEM((2 * MOE_GROUPS,), I32)],
        compiler_params=pltpu.CompilerParams(
            dimension_semantics=("arbitrary", "arbitrary"), vmem_limit_bytes=VMEM_LIMIT),
        name="moe",
    )(x2d, norm_g.reshape(1, D_MODEL), wr, br, w1.astype(BF16), w3.astype(BF16),
      w2.astype(BF16), norm_final_g.reshape(1, D_MODEL), tri)


def _token_tile(n):
    for tm in (1024, 512, 256, 128, 64, 32, 16, 8):
        if n % tm == 0:
            return tm
    raise ValueError(f"token count {n} is not a multiple of 8")


def _pad_axis(a, axis, size):
    if a.shape[axis] == size:
        return a
    widths = [(0, 0)] * a.ndim
    widths[axis] = (0, size - a.shape[axis])
    return jnp.pad(a, widths)


def _round_up(n, m):
    return (n + m - 1) // m * m


def _stream_step(x, conv_state, ssm_state, k_past, v_past, kidx_past, w_packed, p):
    b, t, d = x.shape
    n = b * t
    tm = _token_tile(n)
    x2d = x.reshape(n, d)
    z, xbc, q, gate, k, v, qi, ki, dtw, k_b, v_b, ki_b = _in_proj(
        x2d, p["norm_mix_g"], w_packed, min(tm, PROJ_ROWS))

    yn, conv_new, ssm_new = _ssd(
        xbc.reshape(b, t, CONV_DIM), z.reshape(b, t, SSM_INNER), dtw.reshape(b, t, LANE),
        conv_state, ssm_state, p["conv_w"], p["conv_b"], p["dt_bias"], p["A_log"],
        p["D_skip"], p["ssm_norm_g"])

    kv_w = ATTN_KV_HEADS * HEAD_DIM
    k_all, v_all, kidx_all = (k_b.reshape(b, t, kv_w), v_b.reshape(b, t, kv_w),
                              ki_b.reshape(b, t, IDX_DIM))
    past_len = 0
    if k_past is not None:
        past_len = k_past.shape[1]
        k_all = jnp.concatenate([k_past.astype(BF16).reshape(b, past_len, kv_w), k_all], axis=1)
        v_all = jnp.concatenate([v_past.astype(BF16).reshape(b, past_len, kv_w), v_all], axis=1)
        kidx_all = jnp.concatenate([kidx_past.astype(BF16), kidx_all], axis=1)
    l_valid = past_len + t
    lp = _round_up(l_valid, 2 * KEY_TILE)
    if t % Q_BLOCK == 0:
        q_in = (q.reshape(b, t, ATTN_WIDTH), qi.reshape(b, t, IDX_HEADS * IDX_DIM),
                dtw.reshape(b, t, LANE))
    else:
        q_in = (q.reshape(b, t, ATTN_KV_HEADS, ATTN_REP, HEAD_DIM).transpose(0, 2, 3, 1, 4)
                .reshape(b, ATTN_KV_HEADS, ATTN_REP * t, HEAD_DIM),
                qi.reshape(b, t, IDX_HEADS, IDX_DIM).transpose(0, 2, 1, 3)
                .reshape(b, IDX_HEADS * t, IDX_DIM),
                dtw.reshape(b, t, LANE)[:, :, SSM_HEADS:SSM_HEADS + IDX_HEADS]
                .transpose(0, 2, 1).reshape(b, 1, IDX_HEADS * t))
    o = _dsa(*q_in, _pad_axis(kidx_all, 1, lp), _pad_axis(k_all, 1, lp), _pad_axis(v_all, 1, lp),
             past_len, l_valid, t)
    o = o.reshape(n, ATTN_WIDTH)

    x1 = _merge(x2d, yn.reshape(n, SSM_INNER), o, gate, p["w_ssm_out"], p["w_attn_out"],
                p["w_o"], min(tm, MERGE_TILE))
    y = _moe(x1, p["norm_moe_g"], p["w_router_group"], p["b_router_group"],
             p["w_router_expert"], p["b_router_expert"], p["w1"], p["w3"], p["w2"],
             p["norm_final_g"], tm)
    return (y.reshape(b, t, d), k.reshape(b, t, ATTN_KV_HEADS, HEAD_DIM),
            v.reshape(b, t, ATTN_KV_HEADS, HEAD_DIM), ki.reshape(b, t, IDX_DIM), conv_new, ssm_new)


def kernel(x_prompt, x_sample, cache_k, cache_v, cache_kidx, state_conv, state_ssm, norm_mix_g, w_in, conv_w, conv_b, dt_bias, A_log, D_skip, ssm_norm_g, w_ssm_out, w_attn_out, w_o, norm_moe_g, w_router_group, b_router_group, w_router_expert, b_router_expert, w1, w3, w2, norm_final_g):
    p = dict(norm_mix_g=norm_mix_g, conv_w=conv_w, conv_b=conv_b, dt_bias=dt_bias, A_log=A_log,
             D_skip=D_skip, ssm_norm_g=ssm_norm_g, w_ssm_out=w_ssm_out, w_attn_out=w_attn_out,
             w_o=w_o, norm_moe_g=norm_moe_g, w_router_group=w_router_group,
             b_router_group=b_router_group, w_router_expert=w_router_expert,
             b_router_expert=b_router_expert, w1=w1, w3=w3, w2=w2, norm_final_g=norm_final_g)
    w_packed = _pack_w_in(w_in)
    yp, kp, vp, kip, cp, sp = _stream_step(x_prompt, None, None, None, None, None, w_packed, p)
    ys, ksn, vsn, kisn, csn, ssn = _stream_step(x_sample, state_conv, state_ssm, cache_k,
                                                cache_v, cache_kidx, w_packed, p)
    return (yp, ys, kp, vp, kip, cp, sp, ksn, vsn, kisn, csn, ssn)
```
